```python
import jax
import jax.numpy as jnp
from jax import lax
import numpy as np

D_MODEL = 1024
BATCH = 16
SEQ = 2048
DEPTH = 2

HEAD_DIM = 64
BLOCK_Q = 128
A_HEADS = 8
A_KV_HEADS = 2
A_WINDOW = 128
B_HEADS = 8
B_KV_HEADS = 2
CMP_LEN = 32
CMP_STRIDE = 16
CMP_HIDDEN = 256
SLC_BLOCK = 64
SLC_TOPN = 8
B_WINDOW = 256
N_ATT_HEADS = A_HEADS + B_HEADS
A_Q_W = A_HEADS * HEAD_DIM
A_KV_W = A_KV_HEADS * HEAD_DIM
B_Q_W = B_HEADS * HEAD_DIM
B_KV_W = B_KV_HEADS * HEAD_DIM
N_GATES = 3 * B_HEADS
IN_SPLITS = (A_Q_W, A_KV_W, A_KV_W, B_Q_W, B_KV_W, B_KV_W, B_KV_W, B_KV_W, B_KV_W, B_KV_W, N_GATES)
IN_PROJ_W = sum(IN_SPLITS)
ATT_W = N_ATT_HEADS * HEAD_DIM
GMLP_W = D_MODEL
GMLP_GROUPS = 8
GMLP_CHUNK = 128
D_FF = 2816
N_EXPERTS = 8
TOP_K = 2
D_FF_EXPERT = 3584
MOE_BLOCK = 256
DN_ALPHA = (2.0 * DEPTH) ** 0.25
DN_BETA = (8.0 * DEPTH) ** -0.25
LN_EPS = 1e-5
N_EVEN = (DEPTH + 1) // 2
N_ODD = DEPTH // 2

kernel_name = 'hybrid_swa_nsa_gmlp_moe_deepnorm'


def layer_norm(x, g, b):
    xf = x.astype(jnp.float32)
    mu = jnp.mean(xf, axis=-1, keepdims=True)
    var = jnp.mean(jnp.square(xf - mu), axis=-1, keepdims=True)
    return ((xf - mu) * lax.rsqrt(var + LN_EPS)).astype(x.dtype) * g + b


def masked_softmax(s, mask, sink=None):
    s = jnp.where(mask, s, -jnp.inf)
    m = jnp.max(s, axis=-1, keepdims=True)
    if sink is not None:
        m = jnp.maximum(m, sink)
    m = jnp.where(jnp.isfinite(m), m, 0.0)
    e = jnp.where(mask, jnp.exp(s - m), 0.0)
    den = jnp.sum(e, axis=-1, keepdims=True)
    if sink is not None:
        den = den + jnp.exp(sink - m)
    return e / jnp.where(den > 0.0, den, 1.0)


def alibi_slopes():
    return 2.0 ** (-8.0 * jnp.arange(1, N_ATT_HEADS + 1, dtype=jnp.float32) / N_ATT_HEADS)


def banded_gqa(q, k, v, slopes, window, sink=None):
    bsz, seq, hkv, grp, hd = q.shape
    nblk = seq // BLOCK_Q
    n_prev = -(-window // BLOCK_Q)
    span = (n_prev + 1) * BLOCK_Q
    pad = ((0, 0), (n_prev * BLOCK_Q, 0), (0, 0), (0, 0))
    kb = jnp.pad(k, pad).reshape(bsz, nblk + n_prev, BLOCK_Q, hkv, hd)
    vb = jnp.pad(v, pad).reshape(bsz, nblk + n_prev, BLOCK_Q, hkv, hd)
    k_band = jnp.concatenate([kb[:, j:j + nblk] for j in range(n_prev + 1)], axis=2)
    v_band = jnp.concatenate([vb[:, j:j + nblk] for j in range(n_prev + 1)], axis=2)
    qb = q.reshape(bsz, nblk, BLOCK_Q, hkv, grp, hd)
    s = jnp.einsum('bnqhgd,bnkhd->bnhgqk', qb, k_band).astype(jnp.float32) * (hd ** -0.5)
    q_pos = jnp.arange(nblk)[:, None] * BLOCK_Q + jnp.arange(BLOCK_Q)[None, :]
    k_pos = jnp.arange(nblk)[:, None] * BLOCK_Q - n_prev * BLOCK_Q + jnp.arange(span)[None, :]
    dist = q_pos[:, :, None] - k_pos[:, None, :]
    mask = (dist >= 0) & (dist < window) & (k_pos[:, None, :] >= 0)
    s = s - slopes[None, None, :, :, None, None] * dist[None, :, None, None].astype(jnp.float32)
    sink_b = None if sink is None else sink.astype(jnp.float32)[None, None, :, :, None, None]
    p = masked_softmax(s, mask[None, :, None, None], sink_b)
    o = jnp.einsum('bnhgqk,bnkhd->bnqhgd', p.astype(v.dtype), v_band)
    return o.reshape(bsz, seq, hkv, grp, hd)


def nsa_compress(k, pe, w1, w2):
    bsz, seq, hkv, hd = k.shape
    r = CMP_LEN // CMP_STRIDE
    n_str = seq // CMP_STRIDE
    n_cmp = n_str - r + 1
    ks = k.reshape(bsz, n_str, CMP_STRIDE, hkv, hd)
    win = jnp.concatenate([ks[:, j:j + n_cmp] for j in range(r)], axis=2)
    win = win + pe[None, None, :, None, :]
    flat = jnp.moveaxis(win, 3, 2).reshape(bsz, n_cmp, hkv, CMP_LEN * hd)
    return jax.nn.gelu(flat @ w1) @ w2


def nsa_attention(q, k_cmp, v_cmp, k_slc, v_slc, k_win, v_win, gates, slopes, pe_k, wk1, wk2, pe_v, wv1, wv2):
    bsz, seq, hkv, grp, hd = q.shape
    scale = hd ** -0.5
    t = jnp.arange(seq)
    kc = nsa_compress(k_cmp, pe_k, wk1, wk2)
    vc = nsa_compress(v_cmp, pe_v, wv1, wv2)
    n_cmp = kc.shape[1]
    c_start = jnp.arange(n_cmp) * CMP_STRIDE
    c_end = c_start + CMP_LEN - 1
    dist_c = t[:, None] - c_end[None, :]
    s_c = jnp.einsum('bthgd,bchd->bhgtc', q, kc).astype(jnp.float32) * scale
    s_c = s_c - slopes[None, :, :, None, None] * dist_c.astype(jnp.float32)
    p_c = masked_softmax(s_c, (dist_c >= 0)[None, None, None])
    o_cmp = jnp.einsum('bhgtc,bchd->bthgd', p_c.astype(vc.dtype), vc)
    n_sel = seq // SLC_BLOCK
    s_start = jnp.arange(n_sel) * SLC_BLOCK
    overlap = ((c_start[:, None] < s_start[None, :] + SLC_BLOCK) & (c_start[:, None] + CMP_LEN > s_start[None, :])).astype(jnp.float32)
    imp = jnp.einsum('bhgtc,cj->bhtj', p_c, overlap)
    t_blk = t // SLC_BLOCK
    jb = jnp.arange(n_sel)
    valid = jb[None, :] <= t_blk[:, None]
    forced = (jb[None, :] == 0) | (jb[None, :] == t_blk[:, None]) | (jb[None, :] == t_blk[:, None] - 1)
    rank = jnp.where(forced, jnp.inf, jnp.where(valid, imp, -jnp.inf))
    top_n = min(SLC_TOPN, n_sel)
    _, sel_idx = lax.top_k(rank, top_n)
    nq = seq // BLOCK_Q
    kb = k_slc.reshape(bsz, n_sel, SLC_BLOCK, hkv, hd).transpose(0, 3, 1, 2, 4)
    vb = v_slc.reshape(bsz, n_sel, SLC_BLOCK, hkv, hd).transpose(0, 3, 1, 2, 4)
    q_blocks = jnp.moveaxis(q.reshape(bsz, nq, BLOCK_Q, hkv, grp, hd), 1, 0)
    i_blocks = jnp.moveaxis(sel_idx.reshape(bsz, hkv, nq, BLOCK_Q, top_n), 2, 0)
    b_ix = jnp.arange(bsz)[:, None, None, None]
    h_ix = jnp.arange(hkv)[None, :, None, None]
    span = top_n * SLC_BLOCK

    def selected_block(args):
        qi, ii, t0 = args
        kg = kb[b_ix, h_ix, ii].reshape(bsz, hkv, BLOCK_Q, span, hd)
        vg = vb[b_ix, h_ix, ii].reshape(bsz, hkv, BLOCK_Q, span, hd)
        s = jnp.einsum('bqhgd,bhqmd->bhgqm', qi, kg).astype(jnp.float32) * scale
        k_pos = (ii[..., None] * SLC_BLOCK + jnp.arange(SLC_BLOCK)).reshape(bsz, hkv, BLOCK_Q, span)
        dist = (t0 + jnp.arange(BLOCK_Q))[None, None, :, None] - k_pos
        s = s - slopes[None, :, :, None, None] * dist[:, :, None].astype(jnp.float32)
        p = masked_softmax(s, (dist >= 0)[:, :, None])
        return jnp.einsum('bhgqm,bhqmd->bqhgd', p.astype(vg.dtype), vg)

    o_slc = lax.map(selected_block, (q_blocks, i_blocks, jnp.arange(nq) * BLOCK_Q))
    o_slc = jnp.moveaxis(o_slc, 0, 1).reshape(bsz, seq, hkv, grp, hd)
    o_win = banded_gqa(q, k_win, v_win, slopes, B_WINDOW)
    g = jax.nn.sigmoid(gates.astype(jnp.float32)).astype(q.dtype).reshape(bsz, seq, hkv, grp, 3)
    return g[..., 0:1] * o_cmp + g[..., 1:2] * o_slc + g[..., 2:3] * o_win


def hybrid_attention(h, w_in, sinks, pe_k, wk1, wk2, pe_v, wv1, wv2, w_o):
    bsz, seq, _ = h.shape
    proj = h @ w_in
    offs = np.cumsum(IN_SPLITS)[:-1].tolist()
    qa, ka, va, qb, kc, vc, ks, vs, kw, vw, gates = jnp.split(proj, offs, axis=-1)
    ga = A_HEADS // A_KV_HEADS
    gb = B_HEADS // B_KV_HEADS
    slopes = alibi_slopes()

    def kv_heads(z, n):
        return z.reshape(bsz, seq, n, HEAD_DIM)

    o_a = banded_gqa(qa.reshape(bsz, seq, A_KV_HEADS, ga, HEAD_DIM), kv_heads(ka, A_KV_HEADS), kv_heads(va, A_KV_HEADS),
                     slopes[:A_HEADS].reshape(A_KV_HEADS, ga), A_WINDOW, sinks.reshape(A_KV_HEADS, ga))
    o_b = nsa_attention(qb.reshape(bsz, seq, B_KV_HEADS, gb, HEAD_DIM),
                        kv_heads(kc, B_KV_HEADS), kv_heads(vc, B_KV_HEADS),
                        kv_heads(ks, B_KV_HEADS), kv_heads(vs, B_KV_HEADS),
                        kv_heads(kw, B_KV_HEADS), kv_heads(vw, B_KV_HEADS),
                        gates, slopes[A_HEADS:].reshape(B_KV_HEADS, gb), pe_k, wk1, wk2, pe_v, wv1, wv2)
    o = jnp.concatenate([o_a.reshape(bsz, seq, A_Q_W), o_b.reshape(bsz, seq, B_Q_W)], axis=-1)
    return o @ w_o


def chunked_gmlp(h, w_in, ln_g, ln_b, w_s, b_s, w_out):
    bsz, seq, _ = h.shape
    u, v = jnp.split(jax.nn.gelu(h @ w_in), 2, axis=-1)
    v = layer_norm(v, ln_g, ln_b)
    gd = GMLP_W // GMLP_GROUPS
    vc = v.reshape(bsz, seq // GMLP_CHUNK, GMLP_CHUNK, GMLP_GROUPS, gd)
    mixed = jnp.einsum('gts,bcsgd->bctgd', jnp.tril(w_s), vc) + b_s.T[None, None, :, :, None]
    return (u * mixed.reshape(bsz, seq, GMLP_W)) @ w_out


def swiglu(h, w_gate, w_up, w_down):
    return (jax.nn.silu(h @ w_gate) * (h @ w_up)) @ w_down


def moe_swiglu(h, w_router, w_gate, w_up, w_down):
    bsz, seq, dm = h.shape
    xf = h.reshape(-1, dm)
    n_tok = xf.shape[0]
    logits = (xf @ w_router).astype(jnp.float32)
    top_val, top_idx = lax.top_k(logits, TOP_K)
    gate = jax.nn.softmax(top_val, axis=-1)
    n_asg = n_tok * TOP_K
    flat_e = top_idx.reshape(-1)
    flat_tok = jnp.repeat(jnp.arange(n_tok, dtype=jnp.int32), TOP_K)
    flat_g = gate.reshape(-1)
    order = jnp.argsort(flat_e)
    e_sorted = flat_e[order]
    tok_sorted = flat_tok[order]
    g_sorted = flat_g[order]
    counts = jnp.bincount(flat_e, length=N_EXPERTS)
    starts = jnp.cumsum(counts) - counts
    padded = (counts + MOE_BLOCK - 1) // MOE_BLOCK * MOE_BLOCK
    pad_ends = jnp.cumsum(padded)
    pad_starts = pad_ends - padded
    dest = pad_starts[e_sorted] + (jnp.arange(n_asg) - starts[e_sorted])
    n_blocks = -(-n_asg // MOE_BLOCK) + N_EXPERTS
    n_rows = n_blocks * MOE_BLOCK
    row_tok = jnp.full((n_rows,), n_tok, dtype=jnp.int32).at[dest].set(tok_sorted)
    x_pad = jnp.concatenate([xf, jnp.zeros((1, dm), xf.dtype)], axis=0)
    xin = x_pad[row_tok].reshape(n_blocks, MOE_BLOCK, dm)
    blk_e = jnp.minimum(jnp.searchsorted(pad_ends, jnp.arange(n_blocks) * MOE_BLOCK, side='right'), N_EXPERTS - 1)

    def expert_block(args):
        xb, e = args
        return (jax.nn.silu(xb @ w_gate[e]) * (xb @ w_up[e])) @ w_down[e]

    yb = lax.map(expert_block, (xin, blk_e)).reshape(n_rows, dm)
    contrib = yb[dest] * g_sorted[:, None].astype(yb.dtype)
    y = jnp.zeros((n_tok, dm), yb.dtype).at[tok_sorted].add(contrib)
    return y.reshape(bsz, seq, dm)


def setup_inputs(seed: int = 0) -> dict:
    key = jax.random.key(seed)
    ks = jax.random.split(key, 26)
    f32 = jnp.float32

    def nrm(k, shape, scale):
        return jax.random.normal(k, shape, f32) * scale

    dm = D_MODEL
    cmp_in = CMP_LEN * HEAD_DIM
    return {
        'x': nrm(ks[0], (BATCH, SEQ, dm), 1.0),
        'att_w_in': nrm(ks[1], (N_EVEN, dm, IN_PROJ_W), dm ** -0.5),
        'att_sinks': nrm(ks[2], (N_EVEN, A_HEADS), 0.5),
        'cmp_pe_k': nrm(ks[3], (N_EVEN, CMP_LEN, HEAD_DIM), 0.1),
        'cmp_wk1': nrm(ks[4], (N_EVEN, cmp_in, CMP_HIDDEN), cmp_in ** -0.5),
        'cmp_wk2': nrm(ks[5], (N_EVEN, CMP_HIDDEN, HEAD_DIM), CMP_HIDDEN ** -0.5),
        'cmp_pe_v': nrm(ks[6], (N_EVEN, CMP_LEN, HEAD_DIM), 0.1),
        'cmp_wv1': nrm(ks[7], (N_EVEN, cmp_in, CMP_HIDDEN), cmp_in ** -0.5),
        'cmp_wv2': nrm(ks[8], (N_EVEN, CMP_HIDDEN, HEAD_DIM), CMP_HIDDEN ** -0.5),
        'att_w_o': nrm(ks[9], (N_EVEN, ATT_W, dm), DN_BETA * ATT_W ** -0.5),
        'ffn_w_gate': nrm(ks[10], (N_EVEN, dm, D_FF), dm ** -0.5),
        'ffn_w_up': nrm(ks[11], (N_EVEN, dm, D_FF), dm ** -0.5),
        'ffn_w_down': nrm(ks[12], (N_EVEN, D_FF, dm), DN_BETA * D_FF ** -0.5),
        'gmlp_w_in': nrm(ks[13], (N_ODD, dm, 2 * GMLP_W), dm ** -0.5),
        'gmlp_ln_g': 1.0 + nrm(ks[14], (N_ODD, GMLP_W), 0.02),
        'gmlp_ln_b': nrm(ks[15], (N_ODD, GMLP_W), 0.02),
        'gmlp_w_s': nrm(ks[16], (N_ODD, GMLP_GROUPS, GMLP_CHUNK, GMLP_CHUNK), GMLP_CHUNK ** -0.5),
        'gmlp_b_s': 1.0 + nrm(ks[17], (N_ODD, GMLP_GROUPS, GMLP_CHUNK), 0.1),
        'gmlp_w_out': nrm(ks[18], (N_ODD, GMLP_W, dm), DN_BETA * GMLP_W ** -0.5),
        'moe_w_router': nrm(ks[19], (N_ODD, dm, N_EXPERTS), dm ** -0.5),
        'moe_w_gate': nrm(ks[20], (N_ODD, N_EXPERTS, dm, D_FF_EXPERT), dm ** -0.5),
        'moe_w_up': nrm(ks[21], (N_ODD, N_EXPERTS, dm, D_FF_EXPERT), dm ** -0.5),
        'moe_w_down': nrm(ks[22], (N_ODD, N_EXPERTS, D_FF_EXPERT, dm), DN_BETA * D_FF_EXPERT ** -0.5),
        'ln_g': 1.0 + nrm(ks[23], (DEPTH, 2, dm), 0.02),
        'ln_b': nrm(ks[24], (DEPTH, 2, dm), 0.02),
    }


def reference(x, att_w_in, att_sinks, cmp_pe_k, cmp_wk1, cmp_wk2, cmp_pe_v, cmp_wv1, cmp_wv2, att_w_o,
              ffn_w_gate, ffn_w_up, ffn_w_down, gmlp_w_in, gmlp_ln_g, gmlp_ln_b, gmlp_w_s, gmlp_b_s, gmlp_w_out,
              moe_w_router, moe_w_gate, moe_w_up, moe_w_down, ln_g, ln_b):
    for i in range(DEPTH):
        j = i // 2
        if i % 2 == 0:
            mix = hybrid_attention(x, att_w_in[j], att_sinks[j], cmp_pe_k[j], cmp_wk1[j], cmp_wk2[j],
                                   cmp_pe_v[j], cmp_wv1[j], cmp_wv2[j], att_w_o[j])
            x = layer_norm(DN_ALPHA * x + mix, ln_g[i, 0], ln_b[i, 0])
            ff = swiglu(x, ffn_w_gate[j], ffn_w_up[j], ffn_w_down[j])
        else:
            mix = chunked_gmlp(x, gmlp_w_in[j], gmlp_ln_g[j], gmlp_ln_b[j], gmlp_w_s[j], gmlp_b_s[j], gmlp_w_out[j])
            x = layer_norm(DN_ALPHA * x + mix, ln_g[i, 0], ln_b[i, 0])
            ff = moe_swiglu(x, moe_w_router[j], moe_w_gate[j], moe_w_up[j], moe_w_down[j])
        x = layer_norm(DN_ALPHA * x + ff, ln_g[i, 1], ln_b[i, 1])
    return x
```

```python
import functools

import jax
import jax.numpy as jnp
from jax import lax
from jax.experimental import pallas as pl
from jax.experimental.pallas import tpu as pltpu

F32 = jnp.float32
BF16 = jnp.bfloat16

D_MODEL = 1024
HEAD_DIM = 64
BLOCK_Q = 128
A_HEADS = 8
A_KV_HEADS = 2
A_WINDOW = 128
B_HEADS = 8
B_KV_HEADS = 2
GROUP = 4
CMP_LEN = 32
CMP_STRIDE = 16
CMP_HIDDEN = 256
SLC_BLOCK = 64
SLC_TOPN = 8
B_WINDOW = 256
N_ATT_HEADS = A_HEADS + B_HEADS
GMLP_GROUPS = 8
GMLP_CHUNK = 128
N_EXPERTS = 8
TOP_K = 2
LN_EPS = 1e-5
LANES = 128

PROJ_W = 2048
GATE_W = LANES
SLC_CHUNK = 256

ROW_TILE = 512
MOE_ROWS = 512
GATHER_ROWS = 256
FFN_TILE = 1408
EXPERT_TILE = 896

ALIBI_SLOPES = tuple(2.0 ** (-8.0 * h / N_ATT_HEADS) for h in range(1, N_ATT_HEADS + 1))
SCALE = HEAD_DIM ** -0.5
NEG_INF = float("-inf")
RANK_FORCED = 1e30
RANK_INVALID = -1.0


def _params(n_grid, vmem_mb):
    return pltpu.CompilerParams(
        dimension_semantics=("arbitrary",) * n_grid, vmem_limit_bytes=vmem_mb * 1024 * 1024
    )


def _dot(a, b):
    return jnp.dot(a, b, preferred_element_type=F32)


def _dot_nt(a, b):
    return lax.dot_general(a, b, (((1,), (1,)), ((), ())), preferred_element_type=F32)


def _layer_norm(z, g, b):
    mu = jnp.mean(z, axis=-1, keepdims=True)
    zc = z - mu
    var = jnp.mean(zc * zc, axis=-1, keepdims=True)
    return zc * lax.rsqrt(var + LN_EPS) * g + b


def _gelu(x):
    return 0.5 * x * (1.0 + jnp.tanh(0.7978845608028654 * (x + 0.044715 * (x * x * x))))


def _silu(x):
    return x / (1.0 + jnp.exp(-x))


def _inproj_kernel(x_ref, w_ref, p_ref, g_ref):
    x = x_ref[...].astype(BF16)
    for c in range(PROJ_W // 512):
        p_ref[:, c * 512:(c + 1) * 512] = _dot(x, w_ref[:, c * 512:(c + 1) * 512]).astype(BF16)
    g_ref[...] = _dot(x, w_ref[:, PROJ_W:])


def _inproj(x2d, w):
    n = x2d.shape[0]
    return pl.pallas_call(
        _inproj_kernel,
        grid=(n // ROW_TILE,),
        in_specs=[
            pl.BlockSpec((ROW_TILE, D_MODEL), lambda i: (i, 0)),
            pl.BlockSpec((D_MODEL, PROJ_W + GATE_W), lambda i: (0, 0)),
        ],
        out_specs=[
            pl.BlockSpec((ROW_TILE, PROJ_W), lambda i: (i, 0)),
            pl.BlockSpec((ROW_TILE, GATE_W), lambda i: (i, 0)),
        ],
        out_shape=[
            jax.ShapeDtypeStruct((n, PROJ_W), BF16),
            jax.ShapeDtypeStruct((n, GATE_W), F32),
        ],
        compiler_params=_params(1, 32),
        name="inproj",
    )(x2d, w)


def _compress_kernel(x_ref, pe_ref, w1_ref, w2_ref, o_ref):
    x = x_ref[0, 0]
    w1 = w1_ref[0]
    half = CMP_STRIDE * HEAD_DIM
    a = _dot(x, w1[:half])
    b = _dot(x, w1[half:])
    n_str = x.shape[0]
    b_next = pltpu.roll(b, shift=n_str - 1, axis=0)
    bias = _dot(pe_ref[0].astype(BF16), w1)[0:1]
    hid = _gelu(a + b_next + bias)
    o_ref[0, 0] = _dot(hid.astype(BF16), w2_ref[0])


def _compress(xs, pe, w1, w2):
    bsz, _, n_str, width = xs.shape
    return pl.pallas_call(
        _compress_kernel,
        grid=(bsz, 4),
        in_specs=[
            pl.BlockSpec((1, 1, n_str, width), lambda b, j: (b, j, 0, 0)),
            pl.BlockSpec((1, 8, 2 * width), lambda b, j: (j // 2, 0, 0)),
            pl.BlockSpec((1, 2 * width, CMP_HIDDEN), lambda b, j: (j // 2, 0, 0)),
            pl.BlockSpec((1, CMP_HIDDEN, HEAD_DIM), lambda b, j: (j // 2, 0, 0)),
        ],
        out_specs=pl.BlockSpec((1, 1, n_str, HEAD_DIM), lambda b, j: (b, j, 0, 0)),
        out_shape=jax.ShapeDtypeStruct((bsz, 4, n_str, HEAD_DIM), F32),
        compiler_params=_params(2, 16),
        name="nsa_compress",
    )(xs, pe, w1, w2)


def _band_head(q, k, v, slope, distf, mask, sink):
    s = _dot_nt(q, k) - slope * distf
    s = jnp.where(mask, s, NEG_INF)
    m = jnp.max(s, axis=-1, keepdims=True)
    if sink is not None:
        m = jnp.maximum(m, sink)
    e = jnp.exp(s - m)
    den = jnp.sum(e, axis=-1, keepdims=True)
    if sink is not None:
        den = den + jnp.exp(sink - m)
    return _dot(e.astype(BF16), v) / den


def _attn_kernel(sink_ref, q_ref, kva_ref, kvs_ref, kvw_ref, kcv_ref, g_ref, o_ref, selexp_ref, *, seq):
    n = pl.program_id(1)
    t0 = n * BLOCK_Q
    n_str = seq // CMP_STRIDE
    n_cmp = n_str - CMP_LEN // CMP_STRIDE + 1
    n_sel = seq // SLC_BLOCK
    top_n = min(SLC_TOPN, n_sel)
    hd = HEAD_DIM

    qall = q_ref[0]
    gates = jax.nn.sigmoid(g_ref[0])
    outs = []

    def band_inputs(ref, kvh, span, window):
        ks = pl.multiple_of(jnp.maximum(t0 - (span - BLOCK_Q), 0), BLOCK_Q)
        k = ref[0, pl.ds(ks, span), kvh * hd:(kvh + 1) * hd]
        v = ref[0, pl.ds(ks, span), 2 * hd + kvh * hd:2 * hd + (kvh + 1) * hd]
        iq = lax.broadcasted_iota(jnp.int32, (BLOCK_Q, span), 0)
        ik = lax.broadcasted_iota(jnp.int32, (BLOCK_Q, span), 1)
        dist = (t0 - ks) + iq - ik
        return k, v, dist.astype(F32), (dist >= 0) & (dist < window)

    def q_head(h):
        return (qall[:, h * hd:(h + 1) * hd].astype(F32) * SCALE).astype(BF16)

    for kvh in range(A_KV_HEADS):
        k, v, distf, mask = band_inputs(kva_ref, kvh, 2 * BLOCK_Q, A_WINDOW)
        for g in range(GROUP):
            h = kvh * GROUP + g
            outs.append(_band_head(q_head(h), k, v, ALIBI_SLOPES[h], distf, mask, sink_ref[h]))

    iq_col = lax.broadcasted_iota(jnp.int32, (BLOCK_Q, 1), 0)
    t_col = t0 + iq_col
    c_idx = lax.broadcasted_iota(jnp.int32, (BLOCK_Q, n_str), 1)
    dist_c = t_col - (c_idx * CMP_STRIDE + CMP_LEN - 1)
    valid_c = (dist_c >= 0) & (c_idx < n_cmp)
    dist_cf = dist_c.astype(F32)
    oc = lax.broadcasted_iota(jnp.int32, (n_str, n_sel), 0) * CMP_STRIDE
    oj = lax.broadcasted_iota(jnp.int32, (n_str, n_sel), 1) * SLC_BLOCK
    overlap = ((oc < oj + SLC_BLOCK) & (oc + CMP_LEN > oj)).astype(BF16)
    j_idx = lax.broadcasted_iota(jnp.int32, (BLOCK_Q, n_sel), 1)
    t_blk = t_col // SLC_BLOCK
    valid_j = j_idx <= t_blk
    forced_j = (j_idx == 0) | (j_idx == t_blk) | (j_idx == t_blk - 1)
    ej = lax.broadcasted_iota(jnp.int32, (n_sel, seq), 0)
    ek = lax.broadcasted_iota(jnp.int32, (n_sel, seq), 1) // SLC_BLOCK
    expand = (ej == ek).astype(BF16)
    ch = SLC_CHUNK
    dloc = lax.broadcasted_iota(jnp.int32, (BLOCK_Q, ch), 0) - lax.broadcasted_iota(jnp.int32, (BLOCK_Q, ch), 1)
    n_chunks = (t0 + BLOCK_Q + ch - 1) // ch

    for kvh in range(B_KV_HEADS):
        heads = [A_HEADS + kvh * GROUP + g for g in range(GROUP)]
        qs = [q_head(h) for h in heads]

        kc = kcv_ref[0, kvh].astype(BF16)
        vc = kcv_ref[0, B_KV_HEADS + kvh].astype(BF16)
        o_cmp = []
        psum = jnp.zeros((BLOCK_Q, n_str), F32)
        for g, h in enumerate(heads):
            s = _dot_nt(qs[g], kc) - ALIBI_SLOPES[h] * dist_cf
            s = jnp.where(valid_c, s, NEG_INF)
            m = jnp.max(s, axis=-1, keepdims=True)
            m = jnp.where(m == NEG_INF, 0.0, m)
            e = jnp.exp(s - m)
            den = jnp.sum(e, axis=-1, keepdims=True)
            den = jnp.where(den > 0.0, den, 1.0)
            o_cmp.append(_dot(e.astype(BF16), vc) / den)
            psum = psum + e / den
        p_hi = psum.astype(BF16)
        p_lo = (psum - p_hi.astype(F32)).astype(BF16)
        imp = _dot(p_hi, overlap) + _dot(p_lo, overlap)

        rank = jnp.where(forced_j, RANK_FORCED, jnp.where(valid_j, imp, RANK_INVALID))
        cnt = jnp.zeros((BLOCK_Q, n_sel), jnp.int32)
        for jp in range(n_sel):
            col = rank[:, jp:jp + 1]
            before = (col > rank) | ((col == rank) & (j_idx > jp))
            cnt = cnt + before.astype(jnp.int32)
        sel = (cnt < top_n).astype(BF16)
        selexp_ref[...] = _dot(sel, expand)

        kw, vw, distf_w, mask_w = band_inputs(kvw_ref, kvh, B_WINDOW + BLOCK_Q, B_WINDOW)

        for g, h in enumerate(heads):
            slope = ALIBI_SLOPES[h]
            q = qs[g]

            def slc_step(c, carry, q=q, slope=slope, kvh=kvh):
                m, l, acc = carry
                k0 = pl.multiple_of(c * ch, ch)
                k = kvs_ref[0, pl.ds(k0, ch), kvh * hd:(kvh + 1) * hd]
                v = kvs_ref[0, pl.ds(k0, ch), 2 * hd + kvh * hd:2 * hd + (kvh + 1) * hd]
                dist = (t0 - k0) + dloc
                keep = (selexp_ref[:, pl.ds(k0, ch)] > 0.5) & (dist >= 0)
                s = _dot_nt(q, k) - slope * dist.astype(F32)
                s = jnp.where(keep, s, NEG_INF)
                m_new = jnp.maximum(m, jnp.max(s, axis=-1, keepdims=True))
                alpha = jnp.exp(m - m_new)
                e = jnp.exp(s - m_new)
                l_new = alpha * l + jnp.sum(e, axis=-1, keepdims=True)
                acc_new = alpha * acc + _dot(e.astype(BF16), v)
                return m_new, l_new, acc_new

            init = (
                jnp.full((BLOCK_Q, 1), NEG_INF, F32),
                jnp.zeros((BLOCK_Q, 1), F32),
                jnp.zeros((BLOCK_Q, hd), F32),
            )
            _, l_fin, acc_fin = lax.fori_loop(0, n_chunks, slc_step, init)
            o_slc = acc_fin / l_fin
            o_win = _band_head(q, kw, vw, slope, distf_w, mask_w, None)

            hb = kvh * GROUP + g
            g_cmp = gates[:, hb:hb + 1]
            g_slc = gates[:, B_HEADS + hb:B_HEADS + hb + 1]
            g_win = gates[:, 2 * B_HEADS + hb:2 * B_HEADS + hb + 1]
            outs.append(g_cmp * o_cmp[g] + g_slc * o_slc + g_win * o_win)

    o_ref[0] = jnp.concatenate(outs, axis=-1).astype(BF16)


def _attention(sinks, proj, kcv, gates, seq):
    bsz = proj.shape[0]
    n_str = seq // CMP_STRIDE
    kv_w = 2 * A_KV_HEADS * HEAD_DIM

    def kv_spec(col_block):
        return pl.BlockSpec((1, seq, kv_w), lambda b, n: (b, 0, col_block))

    return pl.pallas_call(
        functools.partial(_attn_kernel, seq=seq),
        grid=(bsz, seq // BLOCK_Q),
        in_specs=[
            pl.BlockSpec(memory_space=pltpu.SMEM),
            pl.BlockSpec((1, BLOCK_Q, 2 * A_HEADS * HEAD_DIM), lambda b, n: (b, n, 0)),
            kv_spec(4),
            kv_spec(6),
            kv_spec(7),
            pl.BlockSpec((1, 4, n_str, HEAD_DIM), lambda b, n: (b, 0, 0, 0)),
            pl.BlockSpec((1, BLOCK_Q, GATE_W), lambda b, n: (b, n, 0)),
        ],
        out_specs=pl.BlockSpec((1, BLOCK_Q, N_ATT_HEADS * HEAD_DIM), lambda b, n: (b, n, 0)),
        out_shape=jax.ShapeDtypeStruct((bsz, seq, N_ATT_HEADS * HEAD_DIM), BF16),
        scratch_shapes=[pltpu.VMEM((BLOCK_Q, seq), F32)],
        compiler_params=_params(2, 32),
        name="hybrid_attention",
    )(sinks, proj, proj, proj, proj, kcv, gates)


def _proj_ln_kernel(o_ref, w_ref, x_ref, g_ref, b_ref, y_ref, *, alpha):
    y = _dot(o_ref[...], w_ref[...])
    y_ref[...] = _layer_norm(alpha * x_ref[...] + y, g_ref[...], b_ref[...])


def _proj_ln(o2d, w, x2d, g, b, alpha):
    n, k = o2d.shape
    row = pl.BlockSpec((ROW_TILE, D_MODEL), lambda i: (i, 0))
    vec = pl.BlockSpec((1, D_MODEL), lambda i: (0, 0))
    return pl.pallas_call(
        functools.partial(_proj_ln_kernel, alpha=alpha),
        grid=(n // ROW_TILE,),
        in_specs=[
            pl.BlockSpec((ROW_TILE, k), lambda i: (i, 0)),
            pl.BlockSpec((k, D_MODEL), lambda i: (0, 0)),
            row, vec, vec,
        ],
        out_specs=row,
        out_shape=jax.ShapeDtypeStruct((n, D_MODEL), F32),
        compiler_params=_params(1, 32),
        name="out_proj_ln",
    )(o2d, w, x2d, g, b)


def _swiglu_acc(x, wg, wu, wd, acc_ref, f):
    h = (_silu(_dot(x, wg)) * _dot(x, wu)).astype(BF16)
    y = _dot(h, wd)

    @pl.when(f == 0)
    def _():
        acc_ref[...] = y

    @pl.when(f != 0)
    def _():
        acc_ref[...] += y


def _ffn_ln_kernel(x_ref, wg_ref, wu_ref, wd_ref, g_ref, b_ref, y_ref, acc_ref, *, alpha):
    f = pl.program_id(1)
    x = x_ref[...]
    _swiglu_acc(x.astype(BF16), wg_ref[...], wu_ref[...], wd_ref[...], acc_ref, f)

    @pl.when(f == pl.num_programs(1) - 1)
    def _():
        y_ref[...] = _layer_norm(alpha * x + acc_ref[...], g_ref[...], b_ref[...])


def _ffn_ln(x2d, wg, wu, wd, g, b, alpha):
    n = x2d.shape[0]
    d_ff = wg.shape[1]
    row = pl.BlockSpec((ROW_TILE, D_MODEL), lambda i, f: (i, 0))
    vec = pl.BlockSpec((1, D_MODEL), lambda i, f: (0, 0))
    return pl.pallas_call(
        functools.partial(_ffn_ln_kernel, alpha=alpha),
        grid=(n // ROW_TILE, d_ff // FFN_TILE),
        in_specs=[
            row,
            pl.BlockSpec((D_MODEL, FFN_TILE), lambda i, f: (0, f)),
            pl.BlockSpec((D_MODEL, FFN_TILE), lambda i, f: (0, f)),
            pl.BlockSpec((FFN_TILE, D_MODEL), lambda i, f: (f, 0)),
            vec, vec,
        ],
        out_specs=row,
        out_shape=jax.ShapeDtypeStruct((n, D_MODEL), F32),
        scratch_shapes=[pltpu.VMEM((ROW_TILE, D_MODEL), F32)],
        compiler_params=_params(2, 48),
        name="swiglu_ln",
    )(x2d, wg, wu, wd, g, b)


def _expert_kernel(be_ref, x_ref, wg_ref, wu_ref, wd_ref, y_ref):
    del be_ref
    _swiglu_acc(x_ref[...], wg_ref[0], wu_ref[0], wd_ref[0], y_ref, pl.program_id(1))


def _experts(blk_e, xin, wg, wu, wd):
    n_rows = xin.shape[0]
    d_ff = wg.shape[2]
    return pl.pallas_call(
        _expert_kernel,
        grid_spec=pltpu.PrefetchScalarGridSpec(
            num_scalar_prefetch=1,
            grid=(n_rows // MOE_ROWS, d_ff // EXPERT_TILE),
            in_specs=[
                pl.BlockSpec((MOE_ROWS, D_MODEL), lambda i, f, be: (i, 0)),
                pl.BlockSpec((1, D_MODEL, EXPERT_TILE), lambda i, f, be: (be[i], 0, f)),
                pl.BlockSpec((1, D_MODEL, EXPERT_TILE), lambda i, f, be: (be[i], 0, f)),
                pl.BlockSpec((1, EXPERT_TILE, D_MODEL), lambda i, f, be: (be[i], f, 0)),
            ],
            out_specs=pl.BlockSpec((MOE_ROWS, D_MODEL), lambda i, f, be: (i, 0)),
        ),
        out_shape=jax.ShapeDtypeStruct((n_rows, D_MODEL), F32),
        compiler_params=_params(2, 40),
        name="expert_swiglu",
    )(blk_e, xin, wg, wu, wd)


def _gmlp_kernel(x_ref, win_ref, lng_ref, lnb_ref, ws_ref, bs_ref, wout_ref, g_ref, b_ref, y_ref,
                 u_ref, vn_ref, gated_ref, *, alpha):
    x = x_ref[...]
    xb = x.astype(BF16)
    u_ref[...] = _gelu(_dot(xb, win_ref[:, :D_MODEL]))
    v = _gelu(_dot(xb, win_ref[:, D_MODEL:]))
    vn_ref[...] = _layer_norm(v, lng_ref[...], lnb_ref[...]).astype(BF16)
    cs = GMLP_CHUNK
    lower = lax.broadcasted_iota(jnp.int32, (cs, cs), 0) >= lax.broadcasted_iota(jnp.int32, (cs, cs), 1)
    for grp in range(GMLP_GROUPS):
        w = jnp.where(lower, ws_ref[grp], 0.0).astype(BF16)
        bias = bs_ref[grp]
        cols = slice(grp * cs, (grp + 1) * cs)
        for c in range(x.shape[0] // cs):
            rows = slice(c * cs, (c + 1) * cs)
            mixed = _dot(w, vn_ref[rows, cols]) + bias
            gated_ref[rows, cols] = (u_ref[rows, cols] * mixed).astype(BF16)
    y = _dot(gated_ref[...], wout_ref[...])
    y_ref[...] = _layer_norm(alpha * x + y, g_ref[...], b_ref[...])


def _gmlp_ln(x2d, w_in, ln_g, ln_b, w_s, b_s, w_out, g, b, alpha):
    n = x2d.shape[0]
    row = pl.BlockSpec((ROW_TILE, D_MODEL), lambda i: (i, 0))
    vec = pl.BlockSpec((1, D_MODEL), lambda i: (0, 0))
    grp = pl.BlockSpec((GMLP_GROUPS, GMLP_CHUNK, GMLP_CHUNK), lambda i: (0, 0, 0))
    return pl.pallas_call(
        functools.partial(_gmlp_kernel, alpha=alpha),
        grid=(n // ROW_TILE,),
        in_specs=[
            row,
            pl.BlockSpec((D_MODEL, 2 * D_MODEL), lambda i: (0, 0)),
            vec, vec, grp, grp,
            pl.BlockSpec((D_MODEL, D_MODEL), lambda i: (0, 0)),
            vec, vec,
        ],
        out_specs=row,
        out_shape=jax.ShapeDtypeStruct((n, D_MODEL), F32),
        scratch_shapes=[
            pltpu.VMEM((ROW_TILE, D_MODEL), F32),
            pltpu.VMEM((ROW_TILE, D_MODEL), BF16),
            pltpu.VMEM((ROW_TILE, D_MODEL), BF16),
        ],
        compiler_params=_params(1, 40),
        name="gmlp_ln",
    )(x2d, w_in, ln_g, ln_b, w_s, b_s, w_out, g, b)


def _router_kernel(x_ref, w_ref, e_ref, g_ref, pos_ref, cnt_ref, run_ref):
    @pl.when(pl.program_id(0) == 0)
    def _():
        run_ref[...] = jnp.zeros_like(run_ref)

    tm = x_ref.shape[0]
    logits = jnp.dot(x_ref[...], w_ref[...], preferred_element_type=F32, precision=lax.Precision.HIGHEST)
    lane = lax.broadcasted_iota(jnp.int32, (tm, LANES), 1)
    lg = jnp.where(lane < N_EXPERTS, logits, NEG_INF)
    v1 = jnp.max(lg, axis=-1, keepdims=True)
    i1 = jnp.min(jnp.where(lg == v1, lane, LANES), axis=-1, keepdims=True)
    lg2 = jnp.where(lane == i1, NEG_INF, lg)
    v2 = jnp.max(lg2, axis=-1, keepdims=True)
    i2 = jnp.min(jnp.where(lg2 == v2, lane, LANES), axis=-1, keepdims=True)
    d = jnp.exp(v2 - v1)
    g1 = 1.0 / (1.0 + d)
    g2 = d / (1.0 + d)

    hit1 = lane == i1
    hit2 = lane == i2
    onehot = (hit1 | hit2).astype(BF16)
    strictly_lower = (
        lax.broadcasted_iota(jnp.int32, (tm, tm), 0) > lax.broadcasted_iota(jnp.int32, (tm, tm), 1)
    ).astype(BF16)
    before = _dot(strictly_lower, onehot) + run_ref[...]
    pos1 = jnp.sum(jnp.where(hit1, before, 0.0), axis=-1, keepdims=True)
    pos2 = jnp.sum(jnp.where(hit2, before, 0.0), axis=-1, keepdims=True)
    run_ref[...] += jnp.sum(onehot.astype(F32), axis=0, keepdims=True)

    e_ref[...] = jnp.where(lane == 0, i1, jnp.where(lane == 1, i2, 0))
    g_ref[...] = jnp.where(lane == 0, g1, jnp.where(lane == 1, g2, 0.0))
    pos_ref[...] = jnp.where(lane == 0, pos1, jnp.where(lane == 1, pos2, 0.0)).astype(jnp.int32)
    cnt_ref[...] = run_ref[...]


def _router(x2d, w_router):
    n = x2d.shape[0]
    tok = pl.BlockSpec((ROW_TILE, LANES), lambda i: (i, 0))
    return pl.pallas_call(
        _router_kernel,
        grid=(n // ROW_TILE,),
        in_specs=[
            pl.BlockSpec((ROW_TILE, D_MODEL), lambda i: (i, 0)),
            pl.BlockSpec((D_MODEL, LANES), lambda i: (0, 0)),
        ],
        out_specs=[tok, tok, tok, pl.BlockSpec((1, LANES), lambda i: (0, 0))],
        out_shape=[
            jax.ShapeDtypeStruct((n, LANES), jnp.int32),
            jax.ShapeDtypeStruct((n, LANES), F32),
            jax.ShapeDtypeStruct((n, LANES), jnp.int32),
            jax.ShapeDtypeStruct((1, LANES), F32),
        ],
        scratch_shapes=[pltpu.VMEM((1, LANES), F32)],
        compiler_params=_params(1, 16),
        name="moe_router",
    )(x2d, w_router)


def _row_copy(src_hbm, row, dst, r, sem):
    return pltpu.make_async_copy(src_hbm.at[pl.ds(row, 1)], dst.at[pl.ds(r, 1)], sem)


def _gather_kernel(idx_ref, x_hbm, o_ref, buf, sem):
    base = pl.program_id(0) * GATHER_ROWS

    def issue(r, carry):
        _row_copy(x_hbm, idx_ref[base + r], buf, r, sem).start()
        return carry

    def drain(r, carry):
        _row_copy(x_hbm, idx_ref[base + r], buf, r, sem).wait()
        return carry

    lax.fori_loop(0, GATHER_ROWS, issue, 0)
    lax.fori_loop(0, GATHER_ROWS, drain, 0)
    o_ref[...] = buf[...].astype(BF16)


def _gather_rows(row_tok, x2d):
    n_rows = row_tok.shape[0]
    return pl.pallas_call(
        _gather_kernel,
        grid_spec=pltpu.PrefetchScalarGridSpec(
            num_scalar_prefetch=1,
            grid=(n_rows // GATHER_ROWS,),
            in_specs=[pl.BlockSpec(memory_space=pl.ANY)],
            out_specs=pl.BlockSpec((GATHER_ROWS, D_MODEL), lambda i, idx: (i, 0)),
            scratch_shapes=[pltpu.VMEM((GATHER_ROWS, D_MODEL), F32), pltpu.SemaphoreType.DMA(())],
        ),
        out_shape=jax.ShapeDtypeStruct((n_rows, D_MODEL), BF16),
        compiler_params=_params(1, 16),
        name="moe_gather",
    )(row_tok, x2d)


def _combine_kernel(d0_ref, d1_ref, y_hbm, gate_ref, x_ref, g_ref, b_ref, o_ref, buf0, buf1, sem, *, alpha):
    tm = x_ref.shape[0]
    base = pl.program_id(0) * tm

    def issue(r, carry):
        _row_copy(y_hbm, d0_ref[base + r], buf0, r, sem.at[0]).start()
        _row_copy(y_hbm, d1_ref[base + r], buf1, r, sem.at[1]).start()
        return carry

    def drain(r, carry):
        _row_copy(y_hbm, d0_ref[base + r], buf0, r, sem.at[0]).wait()
        _row_copy(y_hbm, d1_ref[base + r], buf1, r, sem.at[1]).wait()
        return carry

    lax.fori_loop(0, tm, issue, 0)
    lax.fori_loop(0, tm, drain, 0)
    gate = gate_ref[...]
    y = buf0[...] * gate[:, 0:1] + buf1[...] * gate[:, 1:2]
    o_ref[...] = _layer_norm(alpha * x_ref[...] + y, g_ref[...], b_ref[...])


def _combine_ln(d0, d1, yb, gate, x2d, g, b, alpha):
    n = x2d.shape[0]
    tm = GATHER_ROWS
    row = pl.BlockSpec((tm, D_MODEL), lambda i, a, c: (i, 0))
    vec = pl.BlockSpec((1, D_MODEL), lambda i, a, c: (0, 0))
    return pl.pallas_call(
        functools.partial(_combine_kernel, alpha=alpha),
        grid_spec=pltpu.PrefetchScalarGridSpec(
            num_scalar_prefetch=2,
            grid=(n // tm,),
            in_specs=[
                pl.BlockSpec(memory_space=pl.ANY),
                pl.BlockSpec((tm, LANES), lambda i, a, c: (i, 0)),
                row, vec, vec,
            ],
            out_specs=row,
            scratch_shapes=[
                pltpu.VMEM((tm, D_MODEL), F32),
                pltpu.VMEM((tm, D_MODEL), F32),
                pltpu.SemaphoreType.DMA((2,)),
            ],
        ),
        out_shape=jax.ShapeDtypeStruct((n, D_MODEL), F32),
        compiler_params=_params(1, 16),
        name="moe_combine_ln",
    )(d0, d1, yb, gate, x2d, g, b)


def _attention_layer(x2d, bsz, seq, w_in, sinks, pe_k, wk1, wk2, pe_v, wv1, wv2, w_o, g, b, alpha):
    aq, akv = A_HEADS * HEAD_DIM, A_KV_HEADS * HEAD_DIM
    bq, bkv = B_HEADS * HEAD_DIM, B_KV_HEADS * HEAD_DIM
    bounds = [0]
    for width in (aq, akv, akv, bq, bkv, bkv, bkv, bkv, bkv, bkv, 3 * B_HEADS):
        bounds.append(bounds[-1] + width)
    qa, ka, va, qb, kc, vc, ks, vs, kw, vw, wg = [w_in[:, bounds[i]:bounds[i + 1]] for i in range(11)]
    wg = wg.reshape(D_MODEL, B_HEADS, 3).transpose(0, 2, 1).reshape(D_MODEL, 3 * B_HEADS)
    wg = jnp.pad(wg, ((0, 0), (0, GATE_W - 3 * B_HEADS)))
    w_cat = jnp.concatenate([qa, qb, ka, va, kc, vc, ks, vs, kw, vw, wg], axis=1).astype(BF16)
    proj, gates = _inproj(x2d, w_cat)

    n_str = seq // CMP_STRIDE
    kvc = proj[:, aq + bq + 2 * akv:aq + bq + 2 * akv + 2 * bkv]
    xs = kvc.reshape(bsz, n_str, CMP_STRIDE, 2 * B_KV_HEADS, HEAD_DIM).transpose(0, 3, 1, 2, 4)
    xs = xs.reshape(bsz, 2 * B_KV_HEADS, n_str, CMP_STRIDE * HEAD_DIM)
    pe = jnp.stack([pe_k, pe_v]).reshape(2, 1, CMP_LEN * HEAD_DIM)
    pe = jnp.broadcast_to(pe, (2, 8, CMP_LEN * HEAD_DIM))
    kcv = _compress(xs, pe, jnp.stack([wk1, wv1]).astype(BF16), jnp.stack([wk2, wv2]).astype(BF16))

    o = _attention(sinks, proj.reshape(bsz, seq, PROJ_W), kcv, gates.reshape(bsz, seq, GATE_W), seq)
    return _proj_ln(o.reshape(bsz * seq, N_ATT_HEADS * HEAD_DIM), w_o.astype(BF16), x2d, g, b, alpha)


def _moe_layer(x2d, w_router, w_gate, w_up, w_down, g, b, alpha):
    n_tok = x2d.shape[0]
    w_r = jnp.pad(w_router, ((0, 0), (0, LANES - N_EXPERTS)))
    e_out, gate, pos_out, cnt = _router(x2d, w_r)
    expert = e_out[:, :TOP_K]
    pos = pos_out[:, :TOP_K]
    counts = cnt[0, :N_EXPERTS].astype(jnp.int32)
    padded = (counts + MOE_ROWS - 1) // MOE_ROWS * MOE_ROWS
    pad_ends = jnp.cumsum(padded)
    pad_starts = pad_ends - padded
    dest = pad_starts[expert] + pos
    n_blocks = n_tok * TOP_K // MOE_ROWS + N_EXPERTS
    n_rows = n_blocks * MOE_ROWS
    tok = jnp.broadcast_to(jnp.arange(n_tok, dtype=jnp.int32)[:, None], (n_tok, TOP_K))
    row_tok = jnp.zeros((n_rows,), jnp.int32).at[dest.reshape(-1)].set(tok.reshape(-1))
    blk_e = jnp.minimum(
        jnp.searchsorted(pad_ends, jnp.arange(n_blocks, dtype=jnp.int32) * MOE_ROWS, side="right"),
        N_EXPERTS - 1,
    ).astype(jnp.int32)

    xin = _gather_rows(row_tok, x2d)
    yb = _experts(blk_e, xin, w_gate.astype(BF16), w_up.astype(BF16), w_down.astype(BF16))
    return _combine_ln(dest[:, 0], dest[:, 1], yb, gate, x2d, g, b, alpha)


def kernel(x, att_w_in, att_sinks, cmp_pe_k, cmp_wk1, cmp_wk2, cmp_pe_v, cmp_wv1, cmp_wv2, att_w_o,
           ffn_w_gate, ffn_w_up, ffn_w_down, gmlp_w_in, gmlp_ln_g, gmlp_ln_b, gmlp_w_s, gmlp_b_s, gmlp_w_out,
           moe_w_router, moe_w_gate, moe_w_up, moe_w_down, ln_g, ln_b):
    bsz, seq, dm = x.shape
    depth = ln_g.shape[0]
    alpha = (2.0 * depth) ** 0.25
    assert dm == D_MODEL and seq % SLC_CHUNK == 0 and (bsz * seq) % ROW_TILE == 0
    x2d = x.reshape(bsz * seq, dm)

    def vec(p):
        return p.reshape(1, -1)

    for i in range(depth):
        j = i // 2
        g0, b0, g1, b1 = vec(ln_g[i, 0]), vec(ln_b[i, 0]), vec(ln_g[i, 1]), vec(ln_b[i, 1])
        if i % 2 == 0:
            x2d = _attention_layer(x2d, bsz, seq, att_w_in[j], att_sinks[j], cmp_pe_k[j], cmp_wk1[j], cmp_wk2[j],
                                   cmp_pe_v[j], cmp_wv1[j], cmp_wv2[j], att_w_o[j], g0, b0, alpha)
            x2d = _ffn_ln(x2d, ffn_w_gate[j].astype(BF16), ffn_w_up[j].astype(BF16), ffn_w_down[j].astype(BF16),
                          g1, b1, alpha)
        else:
            b_s = jnp.broadcast_to(gmlp_b_s[j][:, :, None], (GMLP_GROUPS, GMLP_CHUNK, GMLP_CHUNK))
            x2d = _gmlp_ln(x2d, gmlp_w_in[j].astype(BF16), vec(gmlp_ln_g[j]), vec(gmlp_ln_b[j]), gmlp_w_s[j], b_s,
                           gmlp_w_out[j].astype(BF16), g0, b0, alpha)
            x2d = _moe_layer(x2d, moe_w_router[j], moe_w_gate[j], moe_w_up[j], moe_w_down[j], g1, b1, alpha)
    return x2d.reshape(bsz, seq, dm)
```

```python
import functools

import jax
import jax.numpy as jnp
from jax import lax
from jax.experimental import pallas as pl
from jax.experimental.pallas import tpu as pltpu

F32 = jnp.float32
BF16 = jnp.bfloat16

D_MODEL = 1024
HEAD_DIM = 64
BLOCK_Q = 128
A_HEADS = 8
A_KV_HEADS = 2
A_WINDOW = 128
B_HEADS = 8
B_KV_HEADS = 2
GROUP = 4
CMP_LEN = 32
CMP_STRIDE = 16
CMP_HIDDEN = 256
SLC_BLOCK = 64
SLC_TOPN = 8
B_WINDOW = 256
N_ATT_HEADS = A_HEADS + B_HEADS
GMLP_GROUPS = 8
GMLP_CHUNK = 128
N_EXPERTS = 8
TOP_K = 2
LN_EPS = 1e-5
LANES = 128

TOK_W = 640
FEAT_W = 1408
GATE_ROWS = 32
SLC_CHUNK = 256

ROW_TILE = 512
MOE_ROWS = 512
GATHER_ROWS = 256
FFN_TILE = 1408
EXPERT_TILE = 896

ALIBI_SLOPES = tuple(2.0 ** (-8.0 * h / N_ATT_HEADS) for h in range(1, N_ATT_HEADS + 1))
SCALE = HEAD_DIM ** -0.5
NEG_INF = float("-inf")
RANK_FORCED = 1e30
RANK_INVALID = -1.0


def _params(n_grid, vmem_mb):
    return pltpu.CompilerParams(
        dimension_semantics=("arbitrary",) * n_grid, vmem_limit_bytes=vmem_mb * 1024 * 1024
    )


def _dot(a, b):
    return jnp.dot(a, b, preferred_element_type=F32)


def _dot_nt(a, b):
    return lax.dot_general(a, b, (((1,), (1,)), ((), ())), preferred_element_type=F32)


def _layer_norm(z, g, b):
    mu = jnp.mean(z, axis=-1, keepdims=True)
    zc = z - mu
    var = jnp.mean(zc * zc, axis=-1, keepdims=True)
    return zc * lax.rsqrt(var + LN_EPS) * g + b


def _gelu(x):
    return 0.5 * x * (1.0 + jnp.tanh(0.7978845608028654 * (x + 0.044715 * (x * x * x))))


def _silu(x):
    return x / (1.0 + jnp.exp(-x))


def _inproj_kernel(x_ref, wt_ref, wf_ref, wg_ref, tok_ref, feat_ref, gate_ref):
    x = x_ref[...].astype(BF16)
    tok_ref[...] = _dot(x, wt_ref[...]).astype(BF16)
    rows = FEAT_W // 4
    for c in range(4):
        feat_ref[0, c * rows:(c + 1) * rows, :] = _dot_nt(wf_ref[c * rows:(c + 1) * rows, :], x).astype(BF16)
    gate_ref[0] = _dot_nt(wg_ref[...], x)


def _inproj(x2d, w_tok, w_feat, w_gate, bsz, seq):
    n = x2d.shape[0]
    per_seq = seq // ROW_TILE
    return pl.pallas_call(
        _inproj_kernel,
        grid=(n // ROW_TILE,),
        in_specs=[
            pl.BlockSpec((ROW_TILE, D_MODEL), lambda i: (i, 0)),
            pl.BlockSpec((D_MODEL, TOK_W), lambda i: (0, 0)),
            pl.BlockSpec((FEAT_W, D_MODEL), lambda i: (0, 0)),
            pl.BlockSpec((GATE_ROWS, D_MODEL), lambda i: (0, 0)),
        ],
        out_specs=[
            pl.BlockSpec((ROW_TILE, TOK_W), lambda i: (i, 0)),
            pl.BlockSpec((1, FEAT_W, ROW_TILE), lambda i: (i // per_seq, 0, i % per_seq)),
            pl.BlockSpec((1, GATE_ROWS, ROW_TILE), lambda i: (i // per_seq, 0, i % per_seq)),
        ],
        out_shape=[
            jax.ShapeDtypeStruct((n, TOK_W), BF16),
            jax.ShapeDtypeStruct((bsz, FEAT_W, seq), BF16),
            jax.ShapeDtypeStruct((bsz, GATE_ROWS, seq), F32),
        ],
        compiler_params=_params(1, 32),
        name="inproj",
    )(x2d, w_tok, w_feat, w_gate)


def _compress_hidden(x, pe, w1):
    half = CMP_STRIDE * HEAD_DIM
    a = _dot(x, w1[:half])
    b = _dot(x, w1[half:])
    b_next = pltpu.roll(b, shift=x.shape[0] - 1, axis=0)
    bias = _dot(pe.astype(BF16), w1)[0:1]
    return _gelu(a + b_next + bias).astype(BF16)


def _compress_kernel(x_ref, pe_ref, w1_ref, w2k_ref, w2vt_ref, kc_ref, vct_ref):
    hk = _compress_hidden(x_ref[0, 0, 0], pe_ref[0], w1_ref[0])
    kc_ref[0, 0] = _dot(hk, w2k_ref[...]).astype(BF16)
    hv = _compress_hidden(x_ref[0, 0, 1], pe_ref[1], w1_ref[1])
    vct_ref[0, 0] = _dot_nt(w2vt_ref[...], hv).astype(BF16)


def _compress(xs, pe, w1, w2k, w2vt):
    bsz, _, _, n_str, width = xs.shape

    def full(shape):
        return pl.BlockSpec(shape, lambda b, h: (0,) * len(shape))

    return pl.pallas_call(
        _compress_kernel,
        grid=(bsz, B_KV_HEADS),
        in_specs=[
            pl.BlockSpec((1, 1, 2, n_str, width), lambda b, h: (b, h, 0, 0, 0)),
            full((2, 8, 2 * width)),
            full((2, 2 * width, CMP_HIDDEN)),
            full((CMP_HIDDEN, HEAD_DIM)),
            full((HEAD_DIM, CMP_HIDDEN)),
        ],
        out_specs=[
            pl.BlockSpec((1, 1, n_str, HEAD_DIM), lambda b, h: (b, h, 0, 0)),
            pl.BlockSpec((1, 1, HEAD_DIM, n_str), lambda b, h: (b, h, 0, 0)),
        ],
        out_shape=[
            jax.ShapeDtypeStruct((bsz, B_KV_HEADS, n_str, HEAD_DIM), BF16),
            jax.ShapeDtypeStruct((bsz, B_KV_HEADS, HEAD_DIM, n_str), BF16),
        ],
        compiler_params=_params(2, 24),
        name="nsa_compress",
    )(xs, pe, w1, w2k, w2vt)


def _attn_kernel(sink_ref, qt_ref, ka_ref, ks_ref, kw_ref, vat_ref, vst_ref, vwt_ref, kc_ref, vct_ref, gt_ref,
                 o_ref, ot_ref, *, seq):
    n = pl.program_id(1)
    t0 = n * BLOCK_Q
    n_str = seq // CMP_STRIDE
    n_cmp = n_str - CMP_LEN // CMP_STRIDE + 1
    n_sel = seq // SLC_BLOCK
    top_n = min(SLC_TOPN, n_sel)
    hd = HEAD_DIM
    gq = GROUP * BLOCK_Q
    ch = SLC_CHUNK

    lane = lax.broadcasted_iota(jnp.int32, (1, gq), 1)
    q_loc = lane & (BLOCK_Q - 1)
    lane_head = lane >> (BLOCK_Q.bit_length() - 1)

    def head_row(vals):
        return jnp.where(lane_head == 0, vals[0], jnp.where(lane_head == 1, vals[1],
                                                           jnp.where(lane_head == 2, vals[2], vals[3])))

    def q_group(first_head):
        return jnp.concatenate(
            [qt_ref[0, (first_head + g) * hd:(first_head + g + 1) * hd, :] for g in range(GROUP)], axis=1)

    def band(k_ref, vt_ref, kvh, span, window, qg, slopes, sinks):
        ks = pl.multiple_of(jnp.maximum(t0 - (span - BLOCK_Q), 0), BLOCK_Q)
        k = k_ref[0, pl.ds(ks, span), kvh * hd:(kvh + 1) * hd]
        vt = vt_ref[0, kvh * hd:(kvh + 1) * hd, pl.ds(ks, span)]
        ik = lax.broadcasted_iota(jnp.int32, (span, gq), 0)
        dist = (t0 - ks) + q_loc - ik
        mask = (dist >= 0) & (dist < window)
        s = _dot(k, qg) - slopes * dist.astype(F32)
        s = jnp.where(mask, s, NEG_INF)
        m = jnp.max(s, axis=0, keepdims=True)
        if sinks is not None:
            m = jnp.maximum(m, sinks)
        e = jnp.exp(s - m)
        den = jnp.sum(e, axis=0, keepdims=True)
        if sinks is not None:
            den = den + jnp.exp(sinks - m)
        return _dot(vt, e.astype(BF16)) / den

    for kvh in range(A_KV_HEADS):
        heads = [kvh * GROUP + g for g in range(GROUP)]
        slopes = head_row([ALIBI_SLOPES[h] for h in heads])
        sinks = head_row([sink_ref[h] for h in heads])
        ot = band(ka_ref, vat_ref, kvh, 2 * BLOCK_Q, A_WINDOW, q_group(heads[0]), slopes, sinks)
        for g, h in enumerate(heads):
            ot_ref[h * hd:(h + 1) * hd, :] = ot[:, g * BLOCK_Q:(g + 1) * BLOCK_Q]

    gt = jax.nn.sigmoid(gt_ref[0])
    t_q = t0 + q_loc
    c_idx = lax.broadcasted_iota(jnp.int32, (n_str, gq), 0)
    dist_c = t_q - (c_idx * CMP_STRIDE + CMP_LEN - 1)
    valid_c = (dist_c >= 0) & (c_idx < n_cmp)
    dist_cf = dist_c.astype(F32)
    oj = lax.broadcasted_iota(jnp.int32, (n_sel, n_str), 0) * SLC_BLOCK
    oc = lax.broadcasted_iota(jnp.int32, (n_sel, n_str), 1) * CMP_STRIDE
    overlap_t = ((oc < oj + SLC_BLOCK) & (oc + CMP_LEN > oj)).astype(BF16)
    j_idx = lax.broadcasted_iota(jnp.int32, (n_sel, BLOCK_Q), 0)
    t_blk = (t0 + lax.broadcasted_iota(jnp.int32, (1, BLOCK_Q), 1)) >> (SLC_BLOCK.bit_length() - 1)
    valid_j = j_idx <= t_blk
    forced_j = (j_idx == 0) | (j_idx == t_blk) | (j_idx == t_blk - 1)
    ik_ch = lax.broadcasted_iota(jnp.int32, (ch, gq), 0)
    dloc = q_loc - ik_ch
    ek = lax.broadcasted_iota(jnp.int32, (ch, n_sel), 0)
    ej = lax.broadcasted_iota(jnp.int32, (ch, n_sel), 1)
    n_chunks = (t0 + BLOCK_Q + ch - 1) // ch

    for kvh in range(B_KV_HEADS):
        heads = [A_HEADS + kvh * GROUP + g for g in range(GROUP)]
        slopes = head_row([ALIBI_SLOPES[h] for h in heads])
        qg = q_group(heads[0])

        s = _dot(kc_ref[0, kvh], qg) - slopes * dist_cf
        s = jnp.where(valid_c, s, NEG_INF)
        m = jnp.max(s, axis=0, keepdims=True)
        m = jnp.where(m == NEG_INF, 0.0, m)
        e = jnp.exp(s - m)
        den = jnp.sum(e, axis=0, keepdims=True)
        den = jnp.where(den > 0.0, den, 1.0)
        o_cmp = _dot(vct_ref[0, kvh], e.astype(BF16)) / den
        p = e / den
        psum = p[:, 0:BLOCK_Q]
        for g in range(1, GROUP):
            psum = psum + p[:, g * BLOCK_Q:(g + 1) * BLOCK_Q]
        p_hi = psum.astype(BF16)
        p_lo = (psum - p_hi.astype(F32)).astype(BF16)
        imp = _dot(overlap_t, p_hi) + _dot(overlap_t, p_lo)

        rank = jnp.where(forced_j, RANK_FORCED, jnp.where(valid_j, imp, RANK_INVALID))
        cnt = jnp.zeros((n_sel, BLOCK_Q), jnp.int32)
        for jp in range(n_sel):
            row = rank[jp:jp + 1, :]
            before = (row > rank) | ((row == rank) & (j_idx > jp))
            cnt = cnt + before.astype(jnp.int32)
        sel = (cnt < top_n).astype(BF16)
        sel4 = jnp.concatenate([sel] * GROUP, axis=1)

        def slc_step(c, carry, kvh=kvh, qg=qg, slopes=slopes, sel4=sel4):
            m, l, acc = carry
            k0 = pl.multiple_of(c * ch, ch)
            k = ks_ref[0, pl.ds(k0, ch), kvh * hd:(kvh + 1) * hd]
            vt = vst_ref[0, kvh * hd:(kvh + 1) * hd, pl.ds(k0, ch)]
            picked = _dot((((k0 + ek) >> (SLC_BLOCK.bit_length() - 1)) == ej).astype(BF16), sel4)
            dist = (t0 - k0) + dloc
            keep = (picked > 0.5) & (dist >= 0)
            s = _dot(k, qg) - slopes * dist.astype(F32)
            s = jnp.where(keep, s, NEG_INF)
            m_new = jnp.maximum(m, jnp.max(s, axis=0, keepdims=True))
            alpha = jnp.exp(m - m_new)
            e = jnp.exp(s - m_new)
            l_new = alpha * l + jnp.sum(e, axis=0, keepdims=True)
            acc_new = alpha * acc + _dot(vt, e.astype(BF16))
            return m_new, l_new, acc_new

        init = (jnp.full((1, gq), NEG_INF, F32), jnp.zeros((1, gq), F32), jnp.zeros((hd, gq), F32))
        _, l_fin, acc_fin = lax.fori_loop(0, n_chunks, slc_step, init)
        o_slc = acc_fin / l_fin

        o_win = band(kw_ref, vwt_ref, kvh, B_WINDOW + BLOCK_Q, B_WINDOW, qg, slopes, None)

        for g, h in enumerate(heads):
            hb = kvh * GROUP + g
            cols = slice(g * BLOCK_Q, (g + 1) * BLOCK_Q)
            ot_ref[h * hd:(h + 1) * hd, :] = (
                gt[hb:hb + 1] * o_cmp[:, cols]
                + gt[B_HEADS + hb:B_HEADS + hb + 1] * o_slc[:, cols]
                + gt[2 * B_HEADS + hb:2 * B_HEADS + hb + 1] * o_win[:, cols]
            )

    o_ref[0] = ot_ref[...].T.astype(BF16)


def _attention(sinks, tok, feat, kc, vct, gates_t, seq):
    bsz = tok.shape[0]
    n_str = seq // CMP_STRIDE
    kv_w = A_KV_HEADS * HEAD_DIM
    q_rows = N_ATT_HEADS * HEAD_DIM

    def k_spec(col_block):
        return pl.BlockSpec((1, seq, kv_w), lambda b, n: (b, 0, col_block))

    def vt_spec(row_block):
        return pl.BlockSpec((1, kv_w, seq), lambda b, n: (b, q_rows // kv_w + row_block, 0))

    return pl.pallas_call(
        functools.partial(_attn_kernel, seq=seq),
        grid=(bsz, seq // BLOCK_Q),
        in_specs=[
            pl.BlockSpec(memory_space=pltpu.SMEM),
            pl.BlockSpec((1, q_rows, BLOCK_Q), lambda b, n: (b, 0, n)),
            k_spec(0), k_spec(1), k_spec(2),
            vt_spec(0), vt_spec(1), vt_spec(2),
            pl.BlockSpec((1, B_KV_HEADS, n_str, HEAD_DIM), lambda b, n: (b, 0, 0, 0)),
            pl.BlockSpec((1, B_KV_HEADS, HEAD_DIM, n_str), lambda b, n: (b, 0, 0, 0)),
            pl.BlockSpec((1, GATE_ROWS, BLOCK_Q), lambda b, n: (b, 0, n)),
        ],
        out_specs=pl.BlockSpec((1, BLOCK_Q, q_rows), lambda b, n: (b, n, 0)),
        out_shape=jax.ShapeDtypeStruct((bsz, seq, q_rows), BF16),
        scratch_shapes=[pltpu.VMEM((q_rows, BLOCK_Q), F32)],
        compiler_params=_params(2, 32),
        name="hybrid_attention",
    )(sinks, feat, tok, tok, tok, feat, feat, feat, kc, vct, gates_t)


def _proj_ln_kernel(o_ref, w_ref, x_ref, g_ref, b_ref, y_ref, *, alpha):
    y = _dot(o_ref[...], w_ref[...])
    y_ref[...] = _layer_norm(alpha * x_ref[...] + y, g_ref[...], b_ref[...])


def _proj_ln(o2d, w, x2d, g, b, alpha):
    n, k = o2d.shape
    row = pl.BlockSpec((ROW_TILE, D_MODEL), lambda i: (i, 0))
    vec = pl.BlockSpec((1, D_MODEL), lambda i: (0, 0))
    return pl.pallas_call(
        functools.partial(_proj_ln_kernel, alpha=alpha),
        grid=(n // ROW_TILE,),
        in_specs=[
            pl.BlockSpec((ROW_TILE, k), lambda i: (i, 0)),
            pl.BlockSpec((k, D_MODEL), lambda i: (0, 0)),
            row, vec, vec,
        ],
        out_specs=row,
        out_shape=jax.ShapeDtypeStruct((n, D_MODEL), F32),
        compiler_params=_params(1, 32),
        name="out_proj_ln",
    )(o2d, w, x2d, g, b)


def _swiglu_acc(x, wg, wu, wd, acc_ref, f):
    h = (_silu(_dot(x, wg)) * _dot(x, wu)).astype(BF16)
    y = _dot(h, wd)

    @pl.when(f == 0)
    def _():
        acc_ref[...] = y

    @pl.when(f != 0)
    def _():
        acc_ref[...] += y


def _ffn_ln_kernel(x_ref, wg_ref, wu_ref, wd_ref, g_ref, b_ref, y_ref, acc_ref, *, alpha):
    f = pl.program_id(1)
    x = x_ref[...]
    _swiglu_acc(x.astype(BF16), wg_ref[...], wu_ref[...], wd_ref[...], acc_ref, f)

    @pl.when(f == pl.num_programs(1) - 1)
    def _():
        y_ref[...] = _layer_norm(alpha * x + acc_ref[...], g_ref[...], b_ref[...])


def _ffn_ln(x2d, wg, wu, wd, g, b, alpha):
    n = x2d.shape[0]
    d_ff = wg.shape[1]
    row = pl.BlockSpec((ROW_TILE, D_MODEL), lambda i, f: (i, 0))
    vec = pl.BlockSpec((1, D_MODEL), lambda i, f: (0, 0))
    return pl.pallas_call(
        functools.partial(_ffn_ln_kernel, alpha=alpha),
        grid=(n // ROW_TILE, d_ff // FFN_TILE),
        in_specs=[
            row,
            pl.BlockSpec((D_MODEL, FFN_TILE), lambda i, f: (0, f)),
            pl.BlockSpec((D_MODEL, FFN_TILE), lambda i, f: (0, f)),
            pl.BlockSpec((FFN_TILE, D_MODEL), lambda i, f: (f, 0)),
            vec, vec,
        ],
        out_specs=row,
        out_shape=jax.ShapeDtypeStruct((n, D_MODEL), F32),
        scratch_shapes=[pltpu.VMEM((ROW_TILE, D_MODEL), F32)],
        compiler_params=_params(2, 48),
        name="swiglu_ln",
    )(x2d, wg, wu, wd, g, b)


def _expert_kernel(be_ref, x_ref, wg_ref, wu_ref, wd_ref, y_ref):
    del be_ref
    _swiglu_acc(x_ref[...], wg_ref[0], wu_ref[0], wd_ref[0], y_ref, pl.program_id(1))


def _experts(blk_e, xin, wg, wu, wd):
    n_rows = xin.shape[0]
    d_ff = wg.shape[2]
    return pl.pallas_call(
        _expert_kernel,
        grid_spec=pltpu.PrefetchScalarGridSpec(
            num_scalar_prefetch=1,
            grid=(n_rows // MOE_ROWS, d_ff // EXPERT_TILE),
            in_specs=[
                pl.BlockSpec((MOE_ROWS, D_MODEL), lambda i, f, be: (i, 0)),
                pl.BlockSpec((1, D_MODEL, EXPERT_TILE), lambda i, f, be: (be[i], 0, f)),
                pl.BlockSpec((1, D_MODEL, EXPERT_TILE), lambda i, f, be: (be[i], 0, f)),
                pl.BlockSpec((1, EXPERT_TILE, D_MODEL), lambda i, f, be: (be[i], f, 0)),
            ],
            out_specs=pl.BlockSpec((MOE_ROWS, D_MODEL), lambda i, f, be: (i, 0)),
        ),
        out_shape=jax.ShapeDtypeStruct((n_rows, D_MODEL), F32),
        compiler_params=_params(2, 40),
        name="expert_swiglu",
    )(blk_e, xin, wg, wu, wd)


def _gmlp_kernel(x_ref, win_ref, lng_ref, lnb_ref, ws_ref, bs_ref, wout_ref, g_ref, b_ref, y_ref,
                 u_ref, vn_ref, gated_ref, *, alpha):
    x = x_ref[...]
    xb = x.astype(BF16)
    u_ref[...] = _gelu(_dot(xb, win_ref[:, :D_MODEL]))
    v = _gelu(_dot(xb, win_ref[:, D_MODEL:]))
    vn_ref[...] = _layer_norm(v, lng_ref[...], lnb_ref[...]).astype(BF16)
    cs = GMLP_CHUNK
    lower = lax.broadcasted_iota(jnp.int32, (cs, cs), 0) >= lax.broadcasted_iota(jnp.int32, (cs, cs), 1)
    for grp in range(GMLP_GROUPS):
        w = jnp.where(lower, ws_ref[grp], 0.0).astype(BF16)
        bias = bs_ref[grp]
        cols = slice(grp * cs, (grp + 1) * cs)
        for c in range(x.shape[0] // cs):
            rows = slice(c * cs, (c + 1) * cs)
            mixed = _dot(w, vn_ref[rows, cols]) + bias
            gated_ref[rows, cols] = (u_ref[rows, cols] * mixed).astype(BF16)
    y = _dot(gated_ref[...], wout_ref[...])
    y_ref[...] = _layer_norm(alpha * x + y, g_ref[...], b_ref[...])


def _gmlp_ln(x2d, w_in, ln_g, ln_b, w_s, b_s, w_out, g, b, alpha):
    n = x2d.shape[0]
    row = pl.BlockSpec((ROW_TILE, D_MODEL), lambda i: (i, 0))
    vec = pl.BlockSpec((1, D_MODEL), lambda i: (0, 0))
    grp = pl.BlockSpec((GMLP_GROUPS, GMLP_CHUNK, GMLP_CHUNK), lambda i: (0, 0, 0))
    return pl.pallas_call(
        functools.partial(_gmlp_kernel, alpha=alpha),
        grid=(n // ROW_TILE,),
        in_specs=[
            row,
            pl.BlockSpec((D_MODEL, 2 * D_MODEL), lambda i: (0, 0)),
            vec, vec, grp, grp,
            pl.BlockSpec((D_MODEL, D_MODEL), lambda i: (0, 0)),
            vec, vec,
        ],
        out_specs=row,
        out_shape=jax.ShapeDtypeStruct((n, D_MODEL), F32),
        scratch_shapes=[
            pltpu.VMEM((ROW_TILE, D_MODEL), F32),
            pltpu.VMEM((ROW_TILE, D_MODEL), BF16),
            pltpu.VMEM((ROW_TILE, D_MODEL), BF16),
        ],
        compiler_params=_params(1, 40),
        name="gmlp_ln",
    )(x2d, w_in, ln_g, ln_b, w_s, b_s, w_out, g, b)


def _router_kernel(x_ref, w_ref, e_ref, g_ref, pos_ref, cnt_ref, run_ref):
    @pl.when(pl.program_id(0) == 0)
    def _():
        run_ref[...] = jnp.zeros_like(run_ref)

    tm = x_ref.shape[0]
    logits = jnp.dot(x_ref[...], w_ref[...], preferred_element_type=F32, precision=lax.Precision.HIGHEST)
    lane = lax.broadcasted_iota(jnp.int32, (tm, LANES), 1)
    lg = jnp.where(lane < N_EXPERTS, logits, NEG_INF)
    v1 = jnp.max(lg, axis=-1, keepdims=True)
    i1 = jnp.min(jnp.where(lg == v1, lane, LANES), axis=-1, keepdims=True)
    lg2 = jnp.where(lane == i1, NEG_INF, lg)
    v2 = jnp.max(lg2, axis=-1, keepdims=True)
    i2 = jnp.min(jnp.where(lg2 == v2, lane, LANES), axis=-1, keepdims=True)
    d = jnp.exp(v2 - v1)
    g1 = 1.0 / (1.0 + d)
    g2 = d / (1.0 + d)

    hit1 = lane == i1
    hit2 = lane == i2
    onehot = (hit1 | hit2).astype(BF16)
    strictly_lower = (
        lax.broadcasted_iota(jnp.int32, (tm, tm), 0) > lax.broadcasted_iota(jnp.int32, (tm, tm), 1)
    ).astype(BF16)
    before = _dot(strictly_lower, onehot) + run_ref[...]
    pos1 = jnp.sum(jnp.where(hit1, before, 0.0), axis=-1, keepdims=True)
    pos2 = jnp.sum(jnp.where(hit2, before, 0.0), axis=-1, keepdims=True)
    run_ref[...] += jnp.sum(onehot.astype(F32), axis=0, keepdims=True)

    e_ref[...] = jnp.where(lane == 0, i1, jnp.where(lane == 1, i2, 0))
    g_ref[...] = jnp.where(lane == 0, g1, jnp.where(lane == 1, g2, 0.0))
    pos_ref[...] = jnp.where(lane == 0, pos1, jnp.where(lane == 1, pos2, 0.0)).astype(jnp.int32)
    cnt_ref[...] = run_ref[...]


def _router(x2d, w_router):
    n = x2d.shape[0]
    tok = pl.BlockSpec((ROW_TILE, LANES), lambda i: (i, 0))
    return pl.pallas_call(
        _router_kernel,
        grid=(n // ROW_TILE,),
        in_specs=[
            pl.BlockSpec((ROW_TILE, D_MODEL), lambda i: (i, 0)),
            pl.BlockSpec((D_MODEL, LANES), lambda i: (0, 0)),
        ],
        out_specs=[tok, tok, tok, pl.BlockSpec((1, LANES), lambda i: (0, 0))],
        out_shape=[
            jax.ShapeDtypeStruct((n, LANES), jnp.int32),
            jax.ShapeDtypeStruct((n, LANES), F32),
            jax.ShapeDtypeStruct((n, LANES), jnp.int32),
            jax.ShapeDtypeStruct((1, LANES), F32),
        ],
        scratch_shapes=[pltpu.VMEM((1, LANES), F32)],
        compiler_params=_params(1, 16),
        name="moe_router",
    )(x2d, w_router)


def _row_copy(src_hbm, row, dst, r, sem):
    return pltpu.make_async_copy(src_hbm.at[pl.ds(row, 1)], dst.at[pl.ds(r, 1)], sem)


def _gather_kernel(idx_ref, x_hbm, o_ref, buf, sem):
    base = pl.program_id(0) * GATHER_ROWS

    def issue(r, carry):
        _row_copy(x_hbm, idx_ref[base + r], buf, r, sem).start()
        return carry

    def drain(r, carry):
        _row_copy(x_hbm, idx_ref[base + r], buf, r, sem).wait()
        return carry

    lax.fori_loop(0, GATHER_ROWS, issue, 0)
    lax.fori_loop(0, GATHER_ROWS, drain, 0)
    o_ref[...] = buf[...].astype(BF16)


def _gather_rows(row_tok, x2d):
    n_rows = row_tok.shape[0]
    return pl.pallas_call(
        _gather_kernel,
        grid_spec=pltpu.PrefetchScalarGridSpec(
            num_scalar_prefetch=1,
            grid=(n_rows // GATHER_ROWS,),
            in_specs=[pl.BlockSpec(memory_space=pl.ANY)],
            out_specs=pl.BlockSpec((GATHER_ROWS, D_MODEL), lambda i, idx: (i, 0)),
            scratch_shapes=[pltpu.VMEM((GATHER_ROWS, D_MODEL), F32), pltpu.SemaphoreType.DMA(())],
        ),
        out_shape=jax.ShapeDtypeStruct((n_rows, D_MODEL), BF16),
        compiler_params=_params(1, 16),
        name="moe_gather",
    )(row_tok, x2d)


def _combine_kernel(d0_ref, d1_ref, y_hbm, gate_ref, x_ref, g_ref, b_ref, o_ref, buf0, buf1, sem, *, alpha):
    tm = x_ref.shape[0]
    base = pl.program_id(0) * tm

    def issue(r, carry):
        _row_copy(y_hbm, d0_ref[base + r], buf0, r, sem.at[0]).start()
        _row_copy(y_hbm, d1_ref[base + r], buf1, r, sem.at[1]).start()
        return carry

    def drain(r, carry):
        _row_copy(y_hbm, d0_ref[base + r], buf0, r, sem.at[0]).wait()
        _row_copy(y_hbm, d1_ref[base + r], buf1, r, sem.at[1]).wait()
        return carry

    lax.fori_loop(0, tm, issue, 0)
    lax.fori_loop(0, tm, drain, 0)
    gate = gate_ref[...]
    y = buf0[...] * gate[:, 0:1] + buf1[...] * gate[:, 1:2]
    o_ref[...] = _layer_norm(alpha * x_ref[...] + y, g_ref[...], b_ref[...])


def _combine_ln(d0, d1, yb, gate, x2d, g, b, alpha):
    n = x2d.shape[0]
    tm = GATHER_ROWS
    row = pl.BlockSpec((tm, D_MODEL), lambda i, a, c: (i, 0))
    vec = pl.BlockSpec((1, D_MODEL), lambda i, a, c: (0, 0))
    return pl.pallas_call(
        functools.partial(_combine_kernel, alpha=alpha),
        grid_spec=pltpu.PrefetchScalarGridSpec(
            num_scalar_prefetch=2,
            grid=(n // tm,),
            in_specs=[
                pl.BlockSpec(memory_space=pl.ANY),
                pl.BlockSpec((tm, LANES), lambda i, a, c: (i, 0)),
                row, vec, vec,
            ],
            out_specs=row,
            scratch_shapes=[
                pltpu.VMEM((tm, D_MODEL), F32),
                pltpu.VMEM((tm, D_MODEL), F32),
                pltpu.SemaphoreType.DMA((2,)),
            ],
        ),
        out_shape=jax.ShapeDtypeStruct((n, D_MODEL), F32),
        compiler_params=_params(1, 16),
        name="moe_combine_ln",
    )(d0, d1, yb, gate, x2d, g, b)


def _attention_layer(x2d, bsz, seq, w_in, sinks, pe_k, wk1, wk2, pe_v, wv1, wv2, w_o, g, b, alpha):
    aq, akv = A_HEADS * HEAD_DIM, A_KV_HEADS * HEAD_DIM
    bq, bkv = B_HEADS * HEAD_DIM, B_KV_HEADS * HEAD_DIM
    bounds = [0]
    for width in (aq, akv, akv, bq, bkv, bkv, bkv, bkv, bkv, bkv, 3 * B_HEADS):
        bounds.append(bounds[-1] + width)
    qa, ka, va, qb, kc, vc, ks, vs, kw, vw, wg = [w_in[:, bounds[i]:bounds[i + 1]] for i in range(11)]
    w_tok = jnp.concatenate([ka, ks, kw, kc, vc], axis=1).astype(BF16)
    w_feat = jnp.concatenate([qa * SCALE, qb * SCALE, va, vs, vw], axis=1).T.astype(BF16)
    wg = wg.reshape(D_MODEL, B_HEADS, 3).transpose(2, 1, 0).reshape(3 * B_HEADS, D_MODEL)
    wg = jnp.pad(wg, ((0, GATE_ROWS - 3 * B_HEADS), (0, 0))).astype(BF16)
    tok, feat, gates_t = _inproj(x2d, w_tok, w_feat, wg, bsz, seq)

    n_str = seq // CMP_STRIDE
    kvc = tok[:, 3 * akv:]
    xs = kvc.reshape(bsz, n_str, CMP_STRIDE, 2, B_KV_HEADS, HEAD_DIM).transpose(0, 4, 3, 1, 2, 5)
    xs = xs.reshape(bsz, B_KV_HEADS, 2, n_str, CMP_STRIDE * HEAD_DIM)
    pe = jnp.stack([pe_k, pe_v]).reshape(2, 1, CMP_LEN * HEAD_DIM)
    pe = jnp.broadcast_to(pe, (2, 8, CMP_LEN * HEAD_DIM))
    kcmp, vcmp_t = _compress(xs, pe, jnp.stack([wk1, wv1]).astype(BF16), wk2.astype(BF16), wv2.T.astype(BF16))

    o = _attention(sinks, tok.reshape(bsz, seq, TOK_W), feat, kcmp, vcmp_t, gates_t, seq)
    return _proj_ln(o.reshape(bsz * seq, N_ATT_HEADS * HEAD_DIM), w_o.astype(BF16), x2d, g, b, alpha)


def _moe_layer(x2d, w_router, w_gate, w_up, w_down, g, b, alpha):
    n_tok = x2d.shape[0]
    w_r = jnp.pad(w_router, ((0, 0), (0, LANES - N_EXPERTS)))
    e_out, gate, pos_out, cnt = _router(x2d, w_r)
    expert = e_out[:, :TOP_K]
    pos = pos_out[:, :TOP_K]
    counts = cnt[0, :N_EXPERTS].astype(jnp.int32)
    padded = (counts + MOE_ROWS - 1) // MOE_ROWS * MOE_ROWS
    pad_ends = jnp.cumsum(padded)
    pad_starts = pad_ends - padded
    dest = pad_starts[expert] + pos
    n_blocks = n_tok * TOP_K // MOE_ROWS + N_EXPERTS
    n_rows = n_blocks * MOE_ROWS
    tok = jnp.broadcast_to(jnp.arange(n_tok, dtype=jnp.int32)[:, None], (n_tok, TOP_K))
    row_tok = jnp.zeros((n_rows,), jnp.int32).at[dest.reshape(-1)].set(tok.reshape(-1))
    blk_start = jnp.arange(n_blocks, dtype=jnp.int32) * MOE_ROWS
    blk_e = jnp.sum((pad_ends[None, :] <= blk_start[:, None]).astype(jnp.int32), axis=1)
    blk_e = jnp.minimum(blk_e, N_EXPERTS - 1)

    xin = _gather_rows(row_tok, x2d)
    yb = _experts(blk_e, xin, w_gate.astype(BF16), w_up.astype(BF16), w_down.astype(BF16))
    return _combine_ln(dest[:, 0], dest[:, 1], yb, gate, x2d, g, b, alpha)


def kernel(x, att_w_in, att_sinks, cmp_pe_k, cmp_wk1, cmp_wk2, cmp_pe_v, cmp_wv1, cmp_wv2, att_w_o,
           ffn_w_gate, ffn_w_up, ffn_w_down, gmlp_w_in, gmlp_ln_g, gmlp_ln_b, gmlp_w_s, gmlp_b_s, gmlp_w_out,
           moe_w_router, moe_w_gate, moe_w_up, moe_w_down, ln_g, ln_b):
    bsz, seq, dm = x.shape
    depth = ln_g.shape[0]
    alpha = (2.0 * depth) ** 0.25
    assert dm == D_MODEL and seq % ROW_TILE == 0
    x2d = x.reshape(bsz * seq, dm)

    def vec(p):
        return p.reshape(1, -1)

    for i in range(depth):
        j = i // 2
        g0, b0, g1, b1 = vec(ln_g[i, 0]), vec(ln_b[i, 0]), vec(ln_g[i, 1]), vec(ln_b[i, 1])
        if i % 2 == 0:
            x2d = _attention_layer(x2d, bsz, seq, att_w_in[j], att_sinks[j], cmp_pe_k[j], cmp_wk1[j], cmp_wk2[j],
                                   cmp_pe_v[j], cmp_wv1[j], cmp_wv2[j], att_w_o[j], g0, b0, alpha)
            x2d = _ffn_ln(x2d, ffn_w_gate[j].astype(BF16), ffn_w_up[j].astype(BF16), ffn_w_down[j].astype(BF16),
                          g1, b1, alpha)
        else:
            b_s = jnp.broadcast_to(gmlp_b_s[j][:, :, None], (GMLP_GROUPS, GMLP_CHUNK, GMLP_CHUNK))
            x2d = _gmlp_ln(x2d, gmlp_w_in[j].astype(BF16), vec(gmlp_ln_g[j]), vec(gmlp_ln_b[j]), gmlp_w_s[j], b_s,
                           gmlp_w_out[j].astype(BF16), g0, b0, alpha)
            x2d = _moe_layer(x2d, moe_w_router[j], moe_w_gate[j], moe_w_up[j], moe_w_down[j], g1, b1, alpha)
    return x2d.reshape(bsz, seq, dm)
```

```python
import functools

import jax
import jax.numpy as jnp
from jax import lax
from jax.experimental import pallas as pl
from jax.experimental.pallas import tpu as pltpu

F32 = jnp.float32
BF16 = jnp.bfloat16

D_MODEL = 1024
HEAD_DIM = 64
BLOCK_Q = 128
A_HEADS = 8
A_KV_HEADS = 2
A_WINDOW = 128
B_HEADS = 8
B_KV_HEADS = 2
GROUP = 4
CMP_LEN = 32
CMP_STRIDE = 16
CMP_HIDDEN = 256
SLC_BLOCK = 64
SLC_TOPN = 8
B_WINDOW = 256
N_ATT_HEADS = A_HEADS + B_HEADS
GMLP_GROUPS = 8
GMLP_CHUNK = 128
N_EXPERTS = 8
TOP_K = 2
LN_EPS = 1e-5
LANES = 128
ROW_TILES = (D_MODEL // LANES, LANES)

TOK_W = 640
FEAT_W = 1408
GATE_ROWS = 32
SLC_CHUNK = 256

ROW_TILE = 512
MOE_ROWS = 512
GATHER_ROWS = 256
FFN_TILE = 1408
EXPERT_TILE = 896

ALIBI_SLOPES = tuple(2.0 ** (-8.0 * h / N_ATT_HEADS) for h in range(1, N_ATT_HEADS + 1))
SCALE = HEAD_DIM ** -0.5
NEG_INF = float("-inf")
RANK_FORCED = 1e30
RANK_INVALID = -1.0


def _params(n_grid, vmem_mb):
    return pltpu.CompilerParams(
        dimension_semantics=("arbitrary",) * n_grid, vmem_limit_bytes=vmem_mb * 1024 * 1024
    )


def _dot(a, b):
    return jnp.dot(a, b, preferred_element_type=F32)


def _dot_nt(a, b):
    return lax.dot_general(a, b, (((1,), (1,)), ((), ())), preferred_element_type=F32)


def _layer_norm(z, g, b):
    mu = jnp.mean(z, axis=-1, keepdims=True)
    zc = z - mu
    var = jnp.mean(zc * zc, axis=-1, keepdims=True)
    return zc * lax.rsqrt(var + LN_EPS) * g + b


def _gelu(x):
    return 0.5 * x * (1.0 + jnp.tanh(0.7978845608028654 * (x + 0.044715 * (x * x * x))))


def _silu(x):
    return x / (1.0 + jnp.exp(-x))


def _to_row_tiles(o_ref, y):
    for c in range(D_MODEL // LANES):
        o_ref[:, c, :] = y[:, c * LANES:(c + 1) * LANES]


def _from_row_tiles(ref):
    return jnp.concatenate([ref[:, c, :] for c in range(D_MODEL // LANES)], axis=1)


def _inproj_kernel(x_ref, wt_ref, wf_ref, wg_ref, tok_ref, feat_ref, gate_ref):
    x = x_ref[...].astype(BF16)
    tok_ref[...] = _dot(x, wt_ref[...]).astype(BF16)
    rows = FEAT_W // 4
    for c in range(4):
        feat_ref[0, c * rows:(c + 1) * rows, :] = _dot_nt(wf_ref[c * rows:(c + 1) * rows, :], x).astype(BF16)
    gate_ref[0] = _dot_nt(wg_ref[...], x)


def _inproj(x2d, w_tok, w_feat, w_gate, bsz, seq):
    n = x2d.shape[0]
    per_seq = seq // ROW_TILE
    return pl.pallas_call(
        _inproj_kernel,
        grid=(n // ROW_TILE,),
        in_specs=[
            pl.BlockSpec((ROW_TILE, D_MODEL), lambda i: (i, 0)),
            pl.BlockSpec((D_MODEL, TOK_W), lambda i: (0, 0)),
            pl.BlockSpec((FEAT_W, D_MODEL), lambda i: (0, 0)),
            pl.BlockSpec((GATE_ROWS, D_MODEL), lambda i: (0, 0)),
        ],
        out_specs=[
            pl.BlockSpec((ROW_TILE, TOK_W), lambda i: (i, 0)),
            pl.BlockSpec((1, FEAT_W, ROW_TILE), lambda i: (i // per_seq, 0, i % per_seq)),
            pl.BlockSpec((1, GATE_ROWS, ROW_TILE), lambda i: (i // per_seq, 0, i % per_seq)),
        ],
        out_shape=[
            jax.ShapeDtypeStruct((n, TOK_W), BF16),
            jax.ShapeDtypeStruct((bsz, FEAT_W, seq), BF16),
            jax.ShapeDtypeStruct((bsz, GATE_ROWS, seq), F32),
        ],
        compiler_params=_params(1, 32),
        name="inproj",
    )(x2d, w_tok, w_feat, w_gate)


def _compress_hidden(x, pe, w1):
    half = CMP_STRIDE * HEAD_DIM
    a = _dot(x, w1[:half])
    b = _dot(x, w1[half:])
    b_next = pltpu.roll(b, shift=x.shape[0] - 1, axis=0)
    bias = _dot(pe.astype(BF16), w1)[0:1]
    return _gelu(a + b_next + bias).astype(BF16)


def _compress_kernel(x_ref, pe_ref, w1_ref, w2k_ref, w2vt_ref, kc_ref, vct_ref):
    hk = _compress_hidden(x_ref[0, 0, 0], pe_ref[0], w1_ref[0])
    kc_ref[0, 0] = _dot(hk, w2k_ref[...]).astype(BF16)
    hv = _compress_hidden(x_ref[0, 0, 1], pe_ref[1], w1_ref[1])
    vct_ref[0, 0] = _dot_nt(w2vt_ref[...], hv).astype(BF16)


def _compress(xs, pe, w1, w2k, w2vt):
    bsz, _, _, n_str, width = xs.shape

    def full(shape):
        return pl.BlockSpec(shape, lambda b, h: (0,) * len(shape))

    return pl.pallas_call(
        _compress_kernel,
        grid=(bsz, B_KV_HEADS),
        in_specs=[
            pl.BlockSpec((1, 1, 2, n_str, width), lambda b, h: (b, h, 0, 0, 0)),
            full((2, 8, 2 * width)),
            full((2, 2 * width, CMP_HIDDEN)),
            full((CMP_HIDDEN, HEAD_DIM)),
            full((HEAD_DIM, CMP_HIDDEN)),
        ],
        out_specs=[
            pl.BlockSpec((1, 1, n_str, HEAD_DIM), lambda b, h: (b, h, 0, 0)),
            pl.BlockSpec((1, 1, HEAD_DIM, n_str), lambda b, h: (b, h, 0, 0)),
        ],
        out_shape=[
            jax.ShapeDtypeStruct((bsz, B_KV_HEADS, n_str, HEAD_DIM), BF16),
            jax.ShapeDtypeStruct((bsz, B_KV_HEADS, HEAD_DIM, n_str), BF16),
        ],
        compiler_params=_params(2, 24),
        name="nsa_compress",
    )(xs, pe, w1, w2k, w2vt)


def _attn_kernel(sink_ref, qt_ref, ka_ref, ks_ref, kw_ref, vat_ref, vst_ref, vwt_ref, kc_ref, vct_ref, gt_ref,
                 o_ref, ot_ref, *, seq):
    n = pl.program_id(1)
    t0 = n * BLOCK_Q
    n_str = seq // CMP_STRIDE
    n_cmp = n_str - CMP_LEN // CMP_STRIDE + 1
    n_sel = seq // SLC_BLOCK
    top_n = min(SLC_TOPN, n_sel)
    hd = HEAD_DIM
    gq = GROUP * BLOCK_Q
    ch = SLC_CHUNK

    lane = lax.broadcasted_iota(jnp.int32, (1, gq), 1)
    q_loc = lane & (BLOCK_Q - 1)
    lane_head = lane >> (BLOCK_Q.bit_length() - 1)

    def head_row(vals):
        return jnp.where(lane_head == 0, vals[0], jnp.where(lane_head == 1, vals[1],
                                                           jnp.where(lane_head == 2, vals[2], vals[3])))

    def q_group(first_head):
        return jnp.concatenate(
            [qt_ref[0, (first_head + g) * hd:(first_head + g + 1) * hd, :] for g in range(GROUP)], axis=1)

    def band(k_ref, vt_ref, kvh, span, window, qg, slopes, sinks):
        ks = pl.multiple_of(jnp.maximum(t0 - (span - BLOCK_Q), 0), BLOCK_Q)
        k = k_ref[0, pl.ds(ks, span), kvh * hd:(kvh + 1) * hd]
        vt = vt_ref[0, kvh * hd:(kvh + 1) * hd, pl.ds(ks, span)]
        ik = lax.broadcasted_iota(jnp.int32, (span, gq), 0)
        dist = (t0 - ks) + q_loc - ik
        mask = (dist >= 0) & (dist < window)
        s = _dot(k, qg) - slopes * dist.astype(F32)
        s = jnp.where(mask, s, NEG_INF)
        m = jnp.max(s, axis=0, keepdims=True)
        if sinks is not None:
            m = jnp.maximum(m, sinks)
        e = jnp.exp(s - m)
        den = jnp.sum(e, axis=0, keepdims=True)
        if sinks is not None:
            den = den + jnp.exp(sinks - m)
        return _dot(vt, e.astype(BF16)) / den

    for kvh in range(A_KV_HEADS):
        heads = [kvh * GROUP + g for g in range(GROUP)]
        slopes = head_row([ALIBI_SLOPES[h] for h in heads])
        sinks = head_row([sink_ref[h] for h in heads])
        ot = band(ka_ref, vat_ref, kvh, 2 * BLOCK_Q, A_WINDOW, q_group(heads[0]), slopes, sinks)
        for g, h in enumerate(heads):
            ot_ref[h * hd:(h + 1) * hd, :] = ot[:, g * BLOCK_Q:(g + 1) * BLOCK_Q]

    gt = jax.nn.sigmoid(gt_ref[0])
    t_q = t0 + q_loc
    c_idx = lax.broadcasted_iota(jnp.int32, (n_str, gq), 0)
    dist_c = t_q - (c_idx * CMP_STRIDE + CMP_LEN - 1)
    valid_c = (dist_c >= 0) & (c_idx < n_cmp)
    dist_cf = dist_c.astype(F32)
    oj = lax.broadcasted_iota(jnp.int32, (n_sel, n_str), 0) * SLC_BLOCK
    oc = lax.broadcasted_iota(jnp.int32, (n_sel, n_str), 1) * CMP_STRIDE
    overlap_t = ((oc < oj + SLC_BLOCK) & (oc + CMP_LEN > oj)).astype(BF16)
    j_idx = lax.broadcasted_iota(jnp.int32, (n_sel, BLOCK_Q), 0)
    t_blk = (t0 + lax.broadcasted_iota(jnp.int32, (1, BLOCK_Q), 1)) >> (SLC_BLOCK.bit_length() - 1)
    valid_j = j_idx <= t_blk
    forced_j = (j_idx == 0) | (j_idx == t_blk) | (j_idx == t_blk - 1)
    ik_ch = lax.broadcasted_iota(jnp.int32, (ch, gq), 0)
    dloc = q_loc - ik_ch
    ek = lax.broadcasted_iota(jnp.int32, (ch, n_sel), 0)
    ej = lax.broadcasted_iota(jnp.int32, (ch, n_sel), 1)
    n_chunks = (t0 + BLOCK_Q + ch - 1) // ch

    for kvh in range(B_KV_HEADS):
        heads = [A_HEADS + kvh * GROUP + g for g in range(GROUP)]
        slopes = head_row([ALIBI_SLOPES[h] for h in heads])
        qg = q_group(heads[0])

        s = _dot(kc_ref[0, kvh], qg) - slopes * dist_cf
        s = jnp.where(valid_c, s, NEG_INF)
        m = jnp.max(s, axis=0, keepdims=True)
        m = jnp.where(m == NEG_INF, 0.0, m)
        e = jnp.exp(s - m)
        den = jnp.sum(e, axis=0, keepdims=True)
        den = jnp.where(den > 0.0, den, 1.0)
        o_cmp = _dot(vct_ref[0, kvh], e.astype(BF16)) / den
        p = e / den
        psum = p[:, 0:BLOCK_Q]
        for g in range(1, GROUP):
            psum = psum + p[:, g * BLOCK_Q:(g + 1) * BLOCK_Q]
        p_hi = psum.astype(BF16)
        p_lo = (psum - p_hi.astype(F32)).astype(BF16)
        imp = _dot(overlap_t, p_hi) + _dot(overlap_t, p_lo)

        rank = jnp.where(forced_j, RANK_FORCED, jnp.where(valid_j, imp, RANK_INVALID))
        cnt = jnp.zeros((n_sel, BLOCK_Q), jnp.int32)
        for jp in range(n_sel):
            row = rank[jp:jp + 1, :]
            before = (row > rank) | ((row == rank) & (j_idx > jp))
            cnt = cnt + before.astype(jnp.int32)
        sel = (cnt < top_n).astype(BF16)
        sel4 = jnp.concatenate([sel] * GROUP, axis=1)

        def slc_step(c, carry, kvh=kvh, qg=qg, slopes=slopes, sel4=sel4):
            m, l, acc = carry
            k0 = pl.multiple_of(c * ch, ch)
            k = ks_ref[0, pl.ds(k0, ch), kvh * hd:(kvh + 1) * hd]
            vt = vst_ref[0, kvh * hd:(kvh + 1) * hd, pl.ds(k0, ch)]
            picked = _dot((((k0 + ek) >> (SLC_BLOCK.bit_length() - 1)) == ej).astype(BF16), sel4)
            dist = (t0 - k0) + dloc
            keep = (picked > 0.5) & (dist >= 0)
            s = _dot(k, qg) - slopes * dist.astype(F32)
            s = jnp.where(keep, s, NEG_INF)
            m_new = jnp.maximum(m, jnp.max(s, axis=0, keepdims=True))
            alpha = jnp.exp(m - m_new)
            e = jnp.exp(s - m_new)
            l_new = alpha * l + jnp.sum(e, axis=0, keepdims=True)
            acc_new = alpha * acc + _dot(vt, e.astype(BF16))
            return m_new, l_new, acc_new

        init = (jnp.full((1, gq), NEG_INF, F32), jnp.zeros((1, gq), F32), jnp.zeros((hd, gq), F32))
        _, l_fin, acc_fin = lax.fori_loop(0, n_chunks, slc_step, init)
        o_slc = acc_fin / l_fin

        o_win = band(kw_ref, vwt_ref, kvh, B_WINDOW + BLOCK_Q, B_WINDOW, qg, slopes, None)

        for g, h in enumerate(heads):
            hb = kvh * GROUP + g
            cols = slice(g * BLOCK_Q, (g + 1) * BLOCK_Q)
            ot_ref[h * hd:(h + 1) * hd, :] = (
                gt[hb:hb + 1] * o_cmp[:, cols]
                + gt[B_HEADS + hb:B_HEADS + hb + 1] * o_slc[:, cols]
                + gt[2 * B_HEADS + hb:2 * B_HEADS + hb + 1] * o_win[:, cols]
            )

    o_ref[0] = ot_ref[...].T.astype(BF16)


def _attention(sinks, tok, feat, kc, vct, gates_t, seq):
    bsz = tok.shape[0]
    n_str = seq // CMP_STRIDE
    kv_w = A_KV_HEADS * HEAD_DIM
    q_rows = N_ATT_HEADS * HEAD_DIM

    def k_spec(col_block):
        return pl.BlockSpec((1, seq, kv_w), lambda b, n: (b, 0, col_block))

    def vt_spec(row_block):
        return pl.BlockSpec((1, kv_w, seq), lambda b, n: (b, q_rows // kv_w + row_block, 0))

    return pl.pallas_call(
        functools.partial(_attn_kernel, seq=seq),
        grid=(bsz, seq // BLOCK_Q),
        in_specs=[
            pl.BlockSpec(memory_space=pltpu.SMEM),
            pl.BlockSpec((1, q_rows, BLOCK_Q), lambda b, n: (b, 0, n)),
            k_spec(0), k_spec(1), k_spec(2),
            vt_spec(0), vt_spec(1), vt_spec(2),
            pl.BlockSpec((1, B_KV_HEADS, n_str, HEAD_DIM), lambda b, n: (b, 0, 0, 0)),
            pl.BlockSpec((1, B_KV_HEADS, HEAD_DIM, n_str), lambda b, n: (b, 0, 0, 0)),
            pl.BlockSpec((1, GATE_ROWS, BLOCK_Q), lambda b, n: (b, 0, n)),
        ],
        out_specs=pl.BlockSpec((1, BLOCK_Q, q_rows), lambda b, n: (b, n, 0)),
        out_shape=jax.ShapeDtypeStruct((bsz, seq, q_rows), BF16),
        scratch_shapes=[pltpu.VMEM((q_rows, BLOCK_Q), F32)],
        compiler_params=_params(2, 32),
        name="hybrid_attention",
    )(sinks, feat, tok, tok, tok, feat, feat, feat, kc, vct, gates_t)


def _proj_ln_kernel(o_ref, w_ref, x_ref, g_ref, b_ref, y_ref, *, alpha):
    y = _dot(o_ref[...], w_ref[...])
    y_ref[...] = _layer_norm(alpha * x_ref[...] + y, g_ref[...], b_ref[...])


def _proj_ln(o2d, w, x2d, g, b, alpha):
    n, k = o2d.shape
    row = pl.BlockSpec((ROW_TILE, D_MODEL), lambda i: (i, 0))
    vec = pl.BlockSpec((1, D_MODEL), lambda i: (0, 0))
    return pl.pallas_call(
        functools.partial(_proj_ln_kernel, alpha=alpha),
        grid=(n // ROW_TILE,),
        in_specs=[
            pl.BlockSpec((ROW_TILE, k), lambda i: (i, 0)),
            pl.BlockSpec((k, D_MODEL), lambda i: (0, 0)),
            row, vec, vec,
        ],
        out_specs=row,
        out_shape=jax.ShapeDtypeStruct((n, D_MODEL), F32),
        compiler_params=_params(1, 32),
        name="out_proj_ln",
    )(o2d, w, x2d, g, b)


def _swiglu_acc(x, wg, wu, wd, acc_ref, f):
    h = (_silu(_dot(x, wg)) * _dot(x, wu)).astype(BF16)
    y = _dot(h, wd)

    @pl.when(f == 0)
    def _():
        acc_ref[...] = y

    @pl.when(f != 0)
    def _():
        acc_ref[...] += y


def _ffn_ln_kernel(x_ref, wg_ref, wu_ref, wd_ref, g_ref, b_ref, y_ref, acc_ref, *, alpha):
    f = pl.program_id(1)
    x = x_ref[...]
    _swiglu_acc(x.astype(BF16), wg_ref[...], wu_ref[...], wd_ref[...], acc_ref, f)

    @pl.when(f == pl.num_programs(1) - 1)
    def _():
        y_ref[...] = _layer_norm(alpha * x + acc_ref[...], g_ref[...], b_ref[...])


def _ffn_ln(x2d, wg, wu, wd, g, b, alpha):
    n = x2d.shape[0]
    d_ff = wg.shape[1]
    row = pl.BlockSpec((ROW_TILE, D_MODEL), lambda i, f: (i, 0))
    vec = pl.BlockSpec((1, D_MODEL), lambda i, f: (0, 0))
    return pl.pallas_call(
        functools.partial(_ffn_ln_kernel, alpha=alpha),
        grid=(n // ROW_TILE, d_ff // FFN_TILE),
        in_specs=[
            row,
            pl.BlockSpec((D_MODEL, FFN_TILE), lambda i, f: (0, f)),
            pl.BlockSpec((D_MODEL, FFN_TILE), lambda i, f: (0, f)),
            pl.BlockSpec((FFN_TILE, D_MODEL), lambda i, f: (f, 0)),
            vec, vec,
        ],
        out_specs=row,
        out_shape=jax.ShapeDtypeStruct((n, D_MODEL), F32),
        scratch_shapes=[pltpu.VMEM((ROW_TILE, D_MODEL), F32)],
        compiler_params=_params(2, 48),
        name="swiglu_ln",
    )(x2d, wg, wu, wd, g, b)


def _expert_kernel(be_ref, x_ref, wg_ref, wu_ref, wd_ref, y_ref, acc_ref):
    del be_ref
    f = pl.program_id(1)
    _swiglu_acc(x_ref[...], wg_ref[0], wu_ref[0], wd_ref[0], acc_ref, f)

    @pl.when(f == pl.num_programs(1) - 1)
    def _():
        _to_row_tiles(y_ref, acc_ref[...])


def _experts(blk_e, xin, wg, wu, wd):
    n_rows = xin.shape[0]
    d_ff = wg.shape[2]
    return pl.pallas_call(
        _expert_kernel,
        grid_spec=pltpu.PrefetchScalarGridSpec(
            num_scalar_prefetch=1,
            grid=(n_rows // MOE_ROWS, d_ff // EXPERT_TILE),
            in_specs=[
                pl.BlockSpec((MOE_ROWS, D_MODEL), lambda i, f, be: (i, 0)),
                pl.BlockSpec((1, D_MODEL, EXPERT_TILE), lambda i, f, be: (be[i], 0, f)),
                pl.BlockSpec((1, D_MODEL, EXPERT_TILE), lambda i, f, be: (be[i], 0, f)),
                pl.BlockSpec((1, EXPERT_TILE, D_MODEL), lambda i, f, be: (be[i], f, 0)),
            ],
            out_specs=pl.BlockSpec((MOE_ROWS,) + ROW_TILES, lambda i, f, be: (i, 0, 0)),
            scratch_shapes=[pltpu.VMEM((MOE_ROWS, D_MODEL), F32)],
        ),
        out_shape=jax.ShapeDtypeStruct((n_rows,) + ROW_TILES, F32),
        compiler_params=_params(2, 40),
        name="expert_swiglu",
    )(blk_e, xin, wg, wu, wd)


def _gmlp_kernel(x_ref, win_ref, lng_ref, lnb_ref, ws_ref, bs_ref, wout_ref, g_ref, b_ref, y_ref, yt_ref,
                 u_ref, vn_ref, gated_ref, *, alpha):
    x = x_ref[...]
    xb = x.astype(BF16)
    u_ref[...] = _gelu(_dot(xb, win_ref[:, :D_MODEL]))
    v = _gelu(_dot(xb, win_ref[:, D_MODEL:]))
    vn_ref[...] = _layer_norm(v, lng_ref[...], lnb_ref[...]).astype(BF16)
    cs = GMLP_CHUNK
    lower = lax.broadcasted_iota(jnp.int32, (cs, cs), 0) >= lax.broadcasted_iota(jnp.int32, (cs, cs), 1)
    for grp in range(GMLP_GROUPS):
        w = jnp.where(lower, ws_ref[grp], 0.0).astype(BF16)
        bias = bs_ref[grp]
        cols = slice(grp * cs, (grp + 1) * cs)
        for c in range(x.shape[0] // cs):
            rows = slice(c * cs, (c + 1) * cs)
            mixed = _dot(w, vn_ref[rows, cols]) + bias
            gated_ref[rows, cols] = (u_ref[rows, cols] * mixed).astype(BF16)
    y = _dot(gated_ref[...], wout_ref[...])
    out = _layer_norm(alpha * x + y, g_ref[...], b_ref[...])
    y_ref[...] = out
    _to_row_tiles(yt_ref, out)


def _gmlp_ln(x2d, w_in, ln_g, ln_b, w_s, b_s, w_out, g, b, alpha):
    n = x2d.shape[0]
    row = pl.BlockSpec((ROW_TILE, D_MODEL), lambda i: (i, 0))
    vec = pl.BlockSpec((1, D_MODEL), lambda i: (0, 0))
    grp = pl.BlockSpec((GMLP_GROUPS, GMLP_CHUNK, GMLP_CHUNK), lambda i: (0, 0, 0))
    return pl.pallas_call(
        functools.partial(_gmlp_kernel, alpha=alpha),
        grid=(n // ROW_TILE,),
        in_specs=[
            row,
            pl.BlockSpec((D_MODEL, 2 * D_MODEL), lambda i: (0, 0)),
            vec, vec, grp, grp,
            pl.BlockSpec((D_MODEL, D_MODEL), lambda i: (0, 0)),
            vec, vec,
        ],
        out_specs=[row, pl.BlockSpec((ROW_TILE,) + ROW_TILES, lambda i: (i, 0, 0))],
        out_shape=[
            jax.ShapeDtypeStruct((n, D_MODEL), F32),
            jax.ShapeDtypeStruct((n,) + ROW_TILES, F32),
        ],
        scratch_shapes=[
            pltpu.VMEM((ROW_TILE, D_MODEL), F32),
            pltpu.VMEM((ROW_TILE, D_MODEL), BF16),
            pltpu.VMEM((ROW_TILE, D_MODEL), BF16),
        ],
        compiler_params=_params(1, 40),
        name="gmlp_ln",
    )(x2d, w_in, ln_g, ln_b, w_s, b_s, w_out, g, b)


def _router_kernel(x_ref, w_ref, e_ref, g_ref, pos_ref, cnt_ref, run_ref):
    @pl.when(pl.program_id(0) == 0)
    def _():
        run_ref[...] = jnp.zeros_like(run_ref)

    tm = x_ref.shape[0]
    logits = jnp.dot(x_ref[...], w_ref[...], preferred_element_type=F32, precision=lax.Precision.HIGHEST)
    lane = lax.broadcasted_iota(jnp.int32, (tm, LANES), 1)
    lg = jnp.where(lane < N_EXPERTS, logits, NEG_INF)
    v1 = jnp.max(lg, axis=-1, keepdims=True)
    i1 = jnp.min(jnp.where(lg == v1, lane, LANES), axis=-1, keepdims=True)
    lg2 = jnp.where(lane == i1, NEG_INF, lg)
    v2 = jnp.max(lg2, axis=-1, keepdims=True)
    i2 = jnp.min(jnp.where(lg2 == v2, lane, LANES), axis=-1, keepdims=True)
    d = jnp.exp(v2 - v1)
    g1 = 1.0 / (1.0 + d)
    g2 = d / (1.0 + d)

    hit1 = lane == i1
    hit2 = lane == i2
    onehot = (hit1 | hit2).astype(BF16)
    strictly_lower = (
        lax.broadcasted_iota(jnp.int32, (tm, tm), 0) > lax.broadcasted_iota(jnp.int32, (tm, tm), 1)
    ).astype(BF16)
    before = _dot(strictly_lower, onehot) + run_ref[...]
    pos1 = jnp.sum(jnp.where(hit1, before, 0.0), axis=-1, keepdims=True)
    pos2 = jnp.sum(jnp.where(hit2, before, 0.0), axis=-1, keepdims=True)
    run_ref[...] += jnp.sum(onehot.astype(F32), axis=0, keepdims=True)

    e_ref[...] = jnp.where(lane == 0, i1, jnp.where(lane == 1, i2, 0))
    g_ref[...] = jnp.where(lane == 0, g1, jnp.where(lane == 1, g2, 0.0))
    pos_ref[...] = jnp.where(lane == 0, pos1, jnp.where(lane == 1, pos2, 0.0)).astype(jnp.int32)
    cnt_ref[...] = run_ref[...]


def _router(x2d, w_router):
    n = x2d.shape[0]
    tok = pl.BlockSpec((ROW_TILE, LANES), lambda i: (i, 0))
    return pl.pallas_call(
        _router_kernel,
        grid=(n // ROW_TILE,),
        in_specs=[
            pl.BlockSpec((ROW_TILE, D_MODEL), lambda i: (i, 0)),
            pl.BlockSpec((D_MODEL, LANES), lambda i: (0, 0)),
        ],
        out_specs=[tok, tok, tok, pl.BlockSpec((1, LANES), lambda i: (0, 0))],
        out_shape=[
            jax.ShapeDtypeStruct((n, LANES), jnp.int32),
            jax.ShapeDtypeStruct((n, LANES), F32),
            jax.ShapeDtypeStruct((n, LANES), jnp.int32),
            jax.ShapeDtypeStruct((1, LANES), F32),
        ],
        scratch_shapes=[pltpu.VMEM((1, LANES), F32)],
        compiler_params=_params(1, 16),
        name="moe_router",
    )(x2d, w_router)


def _gather_kernel(idx_ref, x_hbm, o_ref, buf, sem):
    base = pl.program_id(0) * GATHER_ROWS

    def issue(r, carry):
        pltpu.make_async_copy(x_hbm.at[idx_ref[base + r]], buf.at[r], sem).start()
        return carry

    lax.fori_loop(0, GATHER_ROWS, issue, 0, unroll=8)
    pltpu.make_async_copy(x_hbm.at[pl.ds(0, GATHER_ROWS)], buf, sem).wait()
    o_ref[...] = _from_row_tiles(buf).astype(BF16)


def _gather_rows(row_tok, x_tiles):
    n_rows = row_tok.shape[0]
    tile = x_tiles.shape[1:]
    return pl.pallas_call(
        _gather_kernel,
        grid_spec=pltpu.PrefetchScalarGridSpec(
            num_scalar_prefetch=1,
            grid=(n_rows // GATHER_ROWS,),
            in_specs=[pl.BlockSpec(memory_space=pl.ANY)],
            out_specs=pl.BlockSpec((GATHER_ROWS, D_MODEL), lambda i, idx: (i, 0)),
            scratch_shapes=[pltpu.VMEM((GATHER_ROWS,) + tile, F32), pltpu.SemaphoreType.DMA(())],
        ),
        out_shape=jax.ShapeDtypeStruct((n_rows, D_MODEL), BF16),
        compiler_params=_params(1, 16),
        name="moe_gather",
    )(row_tok, x_tiles)


def _combine_kernel(d0_ref, d1_ref, y_hbm, gate_ref, x_ref, g_ref, b_ref, o_ref, buf0, buf1, sem, *, alpha):
    tm = x_ref.shape[0]
    base = pl.program_id(0) * tm

    def issue(r, carry):
        pltpu.make_async_copy(y_hbm.at[d0_ref[base + r]], buf0.at[r], sem.at[0]).start()
        pltpu.make_async_copy(y_hbm.at[d1_ref[base + r]], buf1.at[r], sem.at[1]).start()
        return carry

    lax.fori_loop(0, tm, issue, 0, unroll=8)
    pltpu.make_async_copy(y_hbm.at[pl.ds(0, tm)], buf0, sem.at[0]).wait()
    pltpu.make_async_copy(y_hbm.at[pl.ds(0, tm)], buf1, sem.at[1]).wait()
    gate = gate_ref[...]
    y = _from_row_tiles(buf0) * gate[:, 0:1] + _from_row_tiles(buf1) * gate[:, 1:2]
    o_ref[...] = _layer_norm(alpha * x_ref[...] + y, g_ref[...], b_ref[...])


def _combine_ln(d0, d1, yb, gate, x2d, g, b, alpha):
    n = x2d.shape[0]
    tm = GATHER_ROWS
    row = pl.BlockSpec((tm, D_MODEL), lambda i, a, c: (i, 0))
    vec = pl.BlockSpec((1, D_MODEL), lambda i, a, c: (0, 0))
    return pl.pallas_call(
        functools.partial(_combine_kernel, alpha=alpha),
        grid_spec=pltpu.PrefetchScalarGridSpec(
            num_scalar_prefetch=2,
            grid=(n // tm,),
            in_specs=[
                pl.BlockSpec(memory_space=pl.ANY),
                pl.BlockSpec((tm, LANES), lambda i, a, c: (i, 0)),
                row, vec, vec,
            ],
            out_specs=row,
            scratch_shapes=[
                pltpu.VMEM((tm,) + ROW_TILES, F32),
                pltpu.VMEM((tm,) + ROW_TILES, F32),
                pltpu.SemaphoreType.DMA((2,)),
            ],
        ),
        out_shape=jax.ShapeDtypeStruct((n, D_MODEL), F32),
        compiler_params=_params(1, 16),
        name="moe_combine_ln",
    )(d0, d1, yb, gate, x2d, g, b)


def _attention_layer(x2d, bsz, seq, w_in, sinks, pe_k, wk1, wk2, pe_v, wv1, wv2, w_o, g, b, alpha):
    aq, akv = A_HEADS * HEAD_DIM, A_KV_HEADS * HEAD_DIM
    bq, bkv = B_HEADS * HEAD_DIM, B_KV_HEADS * HEAD_DIM
    bounds = [0]
    for width in (aq, akv, akv, bq, bkv, bkv, bkv, bkv, bkv, bkv, 3 * B_HEADS):
        bounds.append(bounds[-1] + width)
    qa, ka, va, qb, kc, vc, ks, vs, kw, vw, wg = [w_in[:, bounds[i]:bounds[i + 1]] for i in range(11)]
    w_tok = jnp.concatenate([ka, ks, kw, kc, vc], axis=1).astype(BF16)
    w_feat = jnp.concatenate([qa * SCALE, qb * SCALE, va, vs, vw], axis=1).T.astype(BF16)
    wg = wg.reshape(D_MODEL, B_HEADS, 3).transpose(2, 1, 0).reshape(3 * B_HEADS, D_MODEL)
    wg = jnp.pad(wg, ((0, GATE_ROWS - 3 * B_HEADS), (0, 0))).astype(BF16)
    tok, feat, gates_t = _inproj(x2d, w_tok, w_feat, wg, bsz, seq)

    n_str = seq // CMP_STRIDE
    kvc = tok[:, 3 * akv:]
    xs = kvc.reshape(bsz, n_str, CMP_STRIDE, 2, B_KV_HEADS, HEAD_DIM).transpose(0, 4, 3, 1, 2, 5)
    xs = xs.reshape(bsz, B_KV_HEADS, 2, n_str, CMP_STRIDE * HEAD_DIM)
    pe = jnp.stack([pe_k, pe_v]).reshape(2, 1, CMP_LEN * HEAD_DIM)
    pe = jnp.broadcast_to(pe, (2, 8, CMP_LEN * HEAD_DIM))
    kcmp, vcmp_t = _compress(xs, pe, jnp.stack([wk1, wv1]).astype(BF16), wk2.astype(BF16), wv2.T.astype(BF16))

    o = _attention(sinks, tok.reshape(bsz, seq, TOK_W), feat, kcmp, vcmp_t, gates_t, seq)
    return _proj_ln(o.reshape(bsz * seq, N_ATT_HEADS * HEAD_DIM), w_o.astype(BF16), x2d, g, b, alpha)


def _moe_layer(x2d, x_tiles, w_router, w_gate, w_up, w_down, g, b, alpha):
    n_tok = x2d.shape[0]
    w_r = jnp.pad(w_router, ((0, 0), (0, LANES - N_EXPERTS)))
    e_out, gate, pos_out, cnt = _router(x2d, w_r)
    expert = e_out[:, :TOP_K]
    pos = pos_out[:, :TOP_K]
    counts = cnt[0, :N_EXPERTS].astype(jnp.int32)
    padded = (counts + MOE_ROWS - 1) // MOE_ROWS * MOE_ROWS
    pad_ends = jnp.cumsum(padded)
    pad_starts = pad_ends - padded
    dest = pad_starts[expert] + pos
    n_blocks = n_tok * TOP_K // MOE_ROWS + N_EXPERTS
    n_rows = n_blocks * MOE_ROWS
    tok = jnp.broadcast_to(jnp.arange(n_tok, dtype=jnp.int32)[:, None], (n_tok, TOP_K))
    row_tok = jnp.zeros((n_rows,), jnp.int32).at[dest.reshape(-1)].set(tok.reshape(-1))
    blk_start = jnp.arange(n_blocks, dtype=jnp.int32) * MOE_ROWS
    blk_e = jnp.sum((pad_ends[None, :] <= blk_start[:, None]).astype(jnp.int32), axis=1)
    blk_e = jnp.minimum(blk_e, N_EXPERTS - 1)

    xin = _gather_rows(row_tok, x_tiles)
    yb = _experts(blk_e, xin, w_gate.astype(BF16), w_up.astype(BF16), w_down.astype(BF16))
    return _combine_ln(dest[:, 0], dest[:, 1], yb, gate, x2d, g, b, alpha)


def kernel(x, att_w_in, att_sinks, cmp_pe_k, cmp_wk1, cmp_wk2, cmp_pe_v, cmp_wv1, cmp_wv2, att_w_o,
           ffn_w_gate, ffn_w_up, ffn_w_down, gmlp_w_in, gmlp_ln_g, gmlp_ln_b, gmlp_w_s, gmlp_b_s, gmlp_w_out,
           moe_w_router, moe_w_gate, moe_w_up, moe_w_down, ln_g, ln_b):
    bsz, seq, dm = x.shape
    depth = ln_g.shape[0]
    alpha = (2.0 * depth) ** 0.25
    assert dm == D_MODEL and seq % ROW_TILE == 0
    x2d = x.reshape(bsz * seq, dm)

    def vec(p):
        return p.reshape(1, -1)

    for i in range(depth):
        j = i // 2
        g0, b0, g1, b1 = vec(ln_g[i, 0]), vec(ln_b[i, 0]), vec(ln_g[i, 1]), vec(ln_b[i, 1])
        if i % 2 == 0:
            x2d = _attention_layer(x2d, bsz, seq, att_w_in[j], att_sinks[j], cmp_pe_k[j], cmp_wk1[j], cmp_wk2[j],
                                   cmp_pe_v[j], cmp_wv1[j], cmp_wv2[j], att_w_o[j], g0, b0, alpha)
            x2d = _ffn_ln(x2d, ffn_w_gate[j].astype(BF16), ffn_w_up[j].astype(BF16), ffn_w_down[j].astype(BF16),
                          g1, b1, alpha)
        else:
            b_s = jnp.broadcast_to(gmlp_b_s[j][:, :, None], (GMLP_GROUPS, GMLP_CHUNK, GMLP_CHUNK))
            x2d, x_tiles = _gmlp_ln(x2d, gmlp_w_in[j].astype(BF16), vec(gmlp_ln_g[j]), vec(gmlp_ln_b[j]), gmlp_w_s[j],
                                    b_s, gmlp_w_out[j].astype(BF16), g0, b0, alpha)
            x2d = _moe_layer(x2d, x_tiles, moe_w_router[j], moe_w_gate[j], moe_w_up[j], moe_w_down[j], g1, b1, alpha)
    return x2d.reshape(bsz, seq, dm)
```

```python
import functools

import jax
import jax.numpy as jnp
import numpy as np
from jax import lax
from jax.experimental import pallas as pl
from jax.experimental.pallas import tpu as pltpu

F32 = jnp.float32
BF16 = jnp.bfloat16

D_MODEL = 1024
HEAD_DIM = 64
BLOCK_Q = 128
A_HEADS = 8
A_KV_HEADS = 2
A_WINDOW = 128
B_HEADS = 8
B_KV_HEADS = 2
GROUP = 4
CMP_LEN = 32
CMP_STRIDE = 16
CMP_HIDDEN = 256
SLC_BLOCK = 64
SLC_TOPN = 8
B_WINDOW = 256
N_ATT_HEADS = A_HEADS + B_HEADS
GMLP_GROUPS = 8
GMLP_CHUNK = 128
N_EXPERTS = 8
TOP_K = 2
LN_EPS = 1e-5
LANES = 128
ROW_TILES = (D_MODEL // LANES, LANES)

TOK_W = 640
FEAT_W = 1408
GATE_ROWS = 32
SLC_CHUNK = 256

ROW_TILE = 512
MOE_ROWS = 512
GATHER_ROWS = 256
FFN_CHUNK = 1024
EXPERT_TILE = 1792

ALIBI_SLOPES = tuple(2.0 ** (-8.0 * h / N_ATT_HEADS) for h in range(1, N_ATT_HEADS + 1))
SCALE = HEAD_DIM ** -0.5
NEG_INF = float("-inf")
RANK_FORCED = 1e30
RANK_INVALID = -1.0
MASK_BIG = 1e30
AUX_SEL = 32


def _params(n_grid, vmem_mb):
    return pltpu.CompilerParams(
        dimension_semantics=("arbitrary",) * n_grid, vmem_limit_bytes=vmem_mb * 1024 * 1024
    )


def _dot(a, b):
    return jnp.dot(a, b, preferred_element_type=F32)


def _dot_nt(a, b):
    return lax.dot_general(a, b, (((1,), (1,)), ((), ())), preferred_element_type=F32)


def _layer_norm(z, g, b):
    mu = jnp.mean(z, axis=-1, keepdims=True)
    zc = z - mu
    var = jnp.mean(zc * zc, axis=-1, keepdims=True)
    return zc * lax.rsqrt(var + LN_EPS) * g + b


def _gelu(x):
    return 0.5 * x * (1.0 + jnp.tanh(0.7978845608028654 * (x + 0.044715 * (x * x * x))))


def _silu(x):
    return x / (1.0 + jnp.exp(-x))


def _to_row_tiles(o_ref, y):
    for c in range(D_MODEL // LANES):
        o_ref[:, c, :] = y[:, c * LANES:(c + 1) * LANES]


def _from_row_tiles(ref):
    return jnp.concatenate([ref[:, c, :] for c in range(D_MODEL // LANES)], axis=1)


def _inproj_kernel(x_ref, wt_ref, wf_ref, wg_ref, tok_ref, feat_ref, gate_ref):
    x = x_ref[...].astype(BF16)
    tok_ref[...] = _dot(x, wt_ref[...]).astype(BF16)
    rows = FEAT_W // 4
    for c in range(4):
        feat_ref[0, c * rows:(c + 1) * rows, :] = _dot_nt(wf_ref[c * rows:(c + 1) * rows, :], x).astype(BF16)
    gate_ref[0] = _dot_nt(wg_ref[...], x)


def _inproj(x2d, w_tok, w_feat, w_gate, bsz, seq):
    n = x2d.shape[0]
    per_seq = seq // ROW_TILE
    return pl.pallas_call(
        _inproj_kernel,
        grid=(n // ROW_TILE,),
        in_specs=[
            pl.BlockSpec((ROW_TILE, D_MODEL), lambda i: (i, 0)),
            pl.BlockSpec((D_MODEL, TOK_W), lambda i: (0, 0)),
            pl.BlockSpec((FEAT_W, D_MODEL), lambda i: (0, 0)),
            pl.BlockSpec((GATE_ROWS, D_MODEL), lambda i: (0, 0)),
        ],
        out_specs=[
            pl.BlockSpec((ROW_TILE, TOK_W), lambda i: (i, 0)),
            pl.BlockSpec((1, FEAT_W, ROW_TILE), lambda i: (i // per_seq, 0, i % per_seq)),
            pl.BlockSpec((1, GATE_ROWS, ROW_TILE), lambda i: (i // per_seq, 0, i % per_seq)),
        ],
        out_shape=[
            jax.ShapeDtypeStruct((n, TOK_W), BF16),
            jax.ShapeDtypeStruct((bsz, FEAT_W, seq), BF16),
            jax.ShapeDtypeStruct((bsz, GATE_ROWS, seq), F32),
        ],
        compiler_params=_params(1, 32),
        name="inproj",
    )(x2d, w_tok, w_feat, w_gate)


def _compress_hidden(x, pe, w1):
    half = CMP_STRIDE * HEAD_DIM
    a = _dot(x, w1[:half])
    b = _dot(x, w1[half:])
    b_next = pltpu.roll(b, shift=x.shape[0] - 1, axis=0)
    bias = _dot(pe.astype(BF16), w1)[0:1]
    return _gelu(a + b_next + bias).astype(BF16)


def _compress_kernel(x_ref, pe_ref, w1_ref, w2k_ref, w2vt_ref, kc_ref, vct_ref):
    hk = _compress_hidden(x_ref[0, 0, 0], pe_ref[0], w1_ref[0])
    kc_ref[0, 0] = _dot(hk, w2k_ref[...]).astype(BF16)
    hv = _compress_hidden(x_ref[0, 0, 1], pe_ref[1], w1_ref[1])
    vct_ref[0, 0] = _dot_nt(w2vt_ref[...], hv).astype(BF16)


def _compress(xs, pe, w1, w2k, w2vt):
    bsz, _, _, n_str, width = xs.shape

    def full(shape):
        return pl.BlockSpec(shape, lambda b, h: (0,) * len(shape))

    return pl.pallas_call(
        _compress_kernel,
        grid=(bsz, B_KV_HEADS),
        in_specs=[
            pl.BlockSpec((1, 1, 2, n_str, width), lambda b, h: (b, h, 0, 0, 0)),
            full((2, 8, 2 * width)),
            full((2, 2 * width, CMP_HIDDEN)),
            full((CMP_HIDDEN, HEAD_DIM)),
            full((HEAD_DIM, CMP_HIDDEN)),
        ],
        out_specs=[
            pl.BlockSpec((1, 1, n_str, HEAD_DIM), lambda b, h: (b, h, 0, 0)),
            pl.BlockSpec((1, 1, HEAD_DIM, n_str), lambda b, h: (b, h, 0, 0)),
        ],
        out_shape=[
            jax.ShapeDtypeStruct((bsz, B_KV_HEADS, n_str, HEAD_DIM), BF16),
            jax.ShapeDtypeStruct((bsz, B_KV_HEADS, HEAD_DIM, n_str), BF16),
        ],
        compiler_params=_params(2, 24),
        name="nsa_compress",
    )(xs, pe, w1, w2k, w2vt)


def _bf16_parts(x, parts=3):
    out, rest = [], np.asarray(x, np.float32)
    for _ in range(parts):
        piece = rest.astype(jnp.bfloat16).astype(np.float32)
        out.append(piece)
        rest = rest - piece
    return out


def _aux_key_lanes(seq):
    t = np.arange(seq)
    aux = np.zeros((seq, LANES), np.float32)
    aux[t, t // SLC_BLOCK] = 1.0
    aux[:, AUX_SEL:AUX_SEL + 3] = (t >> 7)[:, None]
    aux[:, AUX_SEL + 3:AUX_SEL + 6] = (t & 127)[:, None]
    return jnp.asarray(aux, BF16)


def _aux_slope_rows():
    rows = np.zeros((N_ATT_HEADS // GROUP, LANES - AUX_SEL, GROUP * BLOCK_Q), np.float32)
    for grp in range(N_ATT_HEADS // GROUP):
        for g in range(GROUP):
            slope = ALIBI_SLOPES[grp * GROUP + g]
            cols = slice(g * BLOCK_Q, (g + 1) * BLOCK_Q)
            for i, piece in enumerate(_bf16_parts(128.0 * slope) + _bf16_parts(slope)):
                rows[grp, i, cols] = piece
    return jnp.asarray(rows, BF16)


def _attn_kernel(sink_ref, qt_ref, ka_ref, ks_ref, kw_ref, vat_ref, vst_ref, vwt_ref, kc_ref, vct_ref, gt_ref,
                 aux_ref, slope_ref, o_ref, ot_ref, mask_a_ref, mask_w_ref, *, seq):
    n = pl.program_id(1)
    t0 = n * BLOCK_Q
    n_str = seq // CMP_STRIDE
    n_cmp = n_str - CMP_LEN // CMP_STRIDE + 1
    n_sel = seq // SLC_BLOCK
    top_n = min(SLC_TOPN, n_sel)
    hd = HEAD_DIM
    gq = GROUP * BLOCK_Q
    ch = SLC_CHUNK
    span_a = A_WINDOW + BLOCK_Q
    span_w = B_WINDOW + BLOCK_Q

    lane = lax.broadcasted_iota(jnp.int32, (1, gq), 1)
    q_loc = lane & (BLOCK_Q - 1)
    lane_head = lane >> (BLOCK_Q.bit_length() - 1)
    t_q = t0 + q_loc

    def head_row(vals):
        return jnp.where(lane_head == 0, vals[0], jnp.where(lane_head == 1, vals[1],
                                                           jnp.where(lane_head == 2, vals[2], vals[3])))

    def q_group(first_head):
        return jnp.concatenate(
            [qt_ref[0, (first_head + g) * hd:(first_head + g + 1) * hd, :] for g in range(GROUP)], axis=1)

    def score_rhs(grp, kvh, qg, sel_rows):
        zero = jnp.zeros((hd, gq), BF16)
        q_rows = [qg, zero] if kvh == 0 else [zero, qg]
        return jnp.concatenate(q_rows + [sel_rows, slope_ref[grp]], axis=0)

    def score_lhs(k_ref, start, span):
        return jnp.concatenate([k_ref[0, pl.ds(start, span), :], aux_ref[pl.ds(start, span), :]], axis=1)

    def band_start(span):
        return pl.multiple_of(jnp.maximum(t0 - (span - BLOCK_Q), 0), BLOCK_Q)

    def band_mask(span, window):
        ik = lax.broadcasted_iota(jnp.int32, (span, gq), 0)
        dist = (t0 - band_start(span)) + q_loc - ik
        return jnp.where((dist >= 0) & (dist < window), 0.0, -MASK_BIG)

    @pl.when(n <= span_w // BLOCK_Q - 1)
    def _():
        mask_a_ref[...] = band_mask(span_a, A_WINDOW)
        mask_w_ref[...] = band_mask(span_w, B_WINDOW)

    def band(k_ref, vt_ref, mask_ref, kvh, span, rhs, sinks):
        ks = band_start(span)
        s = _dot(score_lhs(k_ref, ks, span), rhs) + mask_ref[...]
        m = jnp.max(s, axis=0, keepdims=True)
        if sinks is not None:
            m = jnp.maximum(m, sinks)
        e = jnp.exp(s - m)
        den = jnp.sum(e, axis=0, keepdims=True)
        if sinks is not None:
            den = den + jnp.exp(sinks - m)
        return _dot(vt_ref[0, kvh * hd:(kvh + 1) * hd, pl.ds(ks, span)], e.astype(BF16)) / den

    no_sel = jnp.zeros((AUX_SEL, gq), BF16)

    for kvh in range(A_KV_HEADS):
        heads = [kvh * GROUP + g for g in range(GROUP)]
        slopes = head_row([ALIBI_SLOPES[h] for h in heads])
        sinks = head_row([sink_ref[h] for h in heads]) + slopes * t_q.astype(F32)
        rhs = score_rhs(kvh, kvh, q_group(heads[0]), no_sel)
        ot = band(ka_ref, vat_ref, mask_a_ref, kvh, span_a, rhs, sinks)
        for g, h in enumerate(heads):
            ot_ref[h * hd:(h + 1) * hd, :] = ot[:, g * BLOCK_Q:(g + 1) * BLOCK_Q]

    gt = jax.nn.sigmoid(gt_ref[0])
    c_idx = lax.broadcasted_iota(jnp.int32, (n_str, gq), 0)
    dist_c = t_q - (c_idx * CMP_STRIDE + CMP_LEN - 1)
    valid_c = (dist_c >= 0) & (c_idx < n_cmp)
    dist_cf = dist_c.astype(F32)
    oj = lax.broadcasted_iota(jnp.int32, (n_sel, n_str), 0) * SLC_BLOCK
    oc = lax.broadcasted_iota(jnp.int32, (n_sel, n_str), 1) * CMP_STRIDE
    overlap_t = ((oc < oj + SLC_BLOCK) & (oc + CMP_LEN > oj)).astype(BF16)
    j_idx = lax.broadcasted_iota(jnp.int32, (n_sel, BLOCK_Q), 0)
    t_blk = (t0 + lax.broadcasted_iota(jnp.int32, (1, BLOCK_Q), 1)) >> (SLC_BLOCK.bit_length() - 1)
    valid_j = j_idx <= t_blk
    forced_j = (j_idx == 0) | (j_idx == t_blk) | (j_idx == t_blk - 1)
    dloc = q_loc - lax.broadcasted_iota(jnp.int32, (ch, gq), 0)
    n_full = t0 // ch

    for kvh in range(B_KV_HEADS):
        grp = A_KV_HEADS + kvh
        heads = [grp * GROUP + g for g in range(GROUP)]
        slopes = head_row([ALIBI_SLOPES[h] for h in heads])
        qg = q_group(heads[0])

        s = _dot(kc_ref[0, kvh], qg) - slopes * dist_cf
        s = jnp.where(valid_c, s, NEG_INF)
        m = jnp.max(s, axis=0, keepdims=True)
        m = jnp.where(m == NEG_INF, 0.0, m)
        e = jnp.exp(s - m)
        den = jnp.sum(e, axis=0, keepdims=True)
        den = jnp.where(den > 0.0, den, 1.0)
        o_cmp = _dot(vct_ref[0, kvh], e.astype(BF16)) / den
        p = e / den
        psum = p[:, 0:BLOCK_Q]
        for g in range(1, GROUP):
            psum = psum + p[:, g * BLOCK_Q:(g + 1) * BLOCK_Q]
        p_hi = psum.astype(BF16)
        p_lo = (psum - p_hi.astype(F32)).astype(BF16)
        imp = _dot(overlap_t, p_hi) + _dot(overlap_t, p_lo)

        rank = jnp.where(forced_j, RANK_FORCED, jnp.where(valid_j, imp, RANK_INVALID))
        cnt = jnp.zeros((n_sel, BLOCK_Q), jnp.int32)
        for jp in range(n_sel):
            row = rank[jp:jp + 1, :]
            before = (row > rank) | ((row == rank) & (j_idx > jp))
            cnt = cnt + before.astype(jnp.int32)
        sel_bias = jnp.where(cnt < top_n, 0.0, -MASK_BIG)
        if n_sel < AUX_SEL:
            sel_bias = jnp.concatenate([sel_bias, jnp.zeros((AUX_SEL - n_sel, BLOCK_Q), F32)], axis=0)
        sel_rows = jnp.concatenate([sel_bias.astype(BF16)] * GROUP, axis=1)
        rhs = score_rhs(grp, kvh, qg, sel_rows)

        def slc_scores(c, rhs=rhs):
            return _dot(score_lhs(ks_ref, pl.multiple_of(c * ch, ch), ch), rhs)

        def slc_update(c, m, l, acc, s, kvh=kvh):
            m_new = jnp.maximum(m, jnp.max(s, axis=0, keepdims=True))
            alpha = jnp.exp(m - m_new)
            e = jnp.exp(s - m_new)
            l_new = alpha * l + jnp.sum(e, axis=0, keepdims=True)
            vt = vst_ref[0, kvh * hd:(kvh + 1) * hd, pl.ds(pl.multiple_of(c * ch, ch), ch)]
            return m_new, l_new, alpha * acc + _dot(vt, e.astype(BF16))

        def slc_step(c, carry, slc_scores=slc_scores, slc_update=slc_update):
            m, l, acc, s = carry
            s_next = slc_scores(c + 1)
            return slc_update(c, m, l, acc, s) + (s_next,)

        init = (jnp.full((1, gq), NEG_INF, F32), jnp.zeros((1, gq), F32), jnp.zeros((hd, gq), F32))
        m, l, acc, s = lax.fori_loop(0, n_full, slc_step, init + (slc_scores(0),))
        s = jnp.where(dloc >= n_full * ch - t0, s, -MASK_BIG)
        _, l_fin, acc_fin = slc_update(n_full, m, l, acc, s)
        o_slc = acc_fin / l_fin

        o_win = band(kw_ref, vwt_ref, mask_w_ref, kvh, span_w, score_rhs(grp, kvh, qg, no_sel), None)

        for g, h in enumerate(heads):
            hb = kvh * GROUP + g
            cols = slice(g * BLOCK_Q, (g + 1) * BLOCK_Q)
            ot_ref[h * hd:(h + 1) * hd, :] = (
                gt[hb:hb + 1] * o_cmp[:, cols]
                + gt[B_HEADS + hb:B_HEADS + hb + 1] * o_slc[:, cols]
                + gt[2 * B_HEADS + hb:2 * B_HEADS + hb + 1] * o_win[:, cols]
            )

    o_ref[0] = ot_ref[...].T.astype(BF16)


def _attention(sinks, tok, feat, kc, vct, gates_t, seq):
    bsz = tok.shape[0]
    n_str = seq // CMP_STRIDE
    kv_w = A_KV_HEADS * HEAD_DIM
    q_rows = N_ATT_HEADS * HEAD_DIM
    gq = GROUP * BLOCK_Q
    assert seq // SLC_BLOCK <= AUX_SEL and seq <= 128 * 256

    def k_spec(col_block):
        return pl.BlockSpec((1, seq, kv_w), lambda b, n: (b, 0, col_block))

    def vt_spec(row_block):
        return pl.BlockSpec((1, kv_w, seq), lambda b, n: (b, q_rows // kv_w + row_block, 0))

    return pl.pallas_call(
        functools.partial(_attn_kernel, seq=seq),
        grid=(bsz, seq // BLOCK_Q),
        in_specs=[
            pl.BlockSpec(memory_space=pltpu.SMEM),
            pl.BlockSpec((1, q_rows, BLOCK_Q), lambda b, n: (b, 0, n)),
            k_spec(0), k_spec(1), k_spec(2),
            vt_spec(0), vt_spec(1), vt_spec(2),
            pl.BlockSpec((1, B_KV_HEADS, n_str, HEAD_DIM), lambda b, n: (b, 0, 0, 0)),
            pl.BlockSpec((1, B_KV_HEADS, HEAD_DIM, n_str), lambda b, n: (b, 0, 0, 0)),
            pl.BlockSpec((1, GATE_ROWS, BLOCK_Q), lambda b, n: (b, 0, n)),
            pl.BlockSpec((seq, LANES), lambda b, n: (0, 0)),
            pl.BlockSpec((N_ATT_HEADS // GROUP, LANES - AUX_SEL, gq), lambda b, n: (0, 0, 0)),
        ],
        out_specs=pl.BlockSpec((1, BLOCK_Q, q_rows), lambda b, n: (b, n, 0)),
        out_shape=jax.ShapeDtypeStruct((bsz, seq, q_rows), BF16),
        scratch_shapes=[
            pltpu.VMEM((q_rows, BLOCK_Q), F32),
            pltpu.VMEM((A_WINDOW + BLOCK_Q, gq), F32),
            pltpu.VMEM((B_WINDOW + BLOCK_Q, gq), F32),
        ],
        compiler_params=_params(2, 32),
        name="hybrid_attention",
    )(sinks, feat, tok, tok, tok, feat, feat, feat, kc, vct, gates_t, _aux_key_lanes(seq), _aux_slope_rows())


def _proj_ln_kernel(o_ref, w_ref, x_ref, g_ref, b_ref, y_ref, *, alpha):
    y = _dot(o_ref[...], w_ref[...])
    y_ref[...] = _layer_norm(alpha * x_ref[...] + y, g_ref[...], b_ref[...])


def _proj_ln(o2d, w, x2d, g, b, alpha):
    n, k = o2d.shape
    row = pl.BlockSpec((ROW_TILE, D_MODEL), lambda i: (i, 0))
    vec = pl.BlockSpec((1, D_MODEL), lambda i: (0, 0))
    return pl.pallas_call(
        functools.partial(_proj_ln_kernel, alpha=alpha),
        grid=(n // ROW_TILE,),
        in_specs=[
            pl.BlockSpec((ROW_TILE, k), lambda i: (i, 0)),
            pl.BlockSpec((k, D_MODEL), lambda i: (0, 0)),
            row, vec, vec,
        ],
        out_specs=row,
        out_shape=jax.ShapeDtypeStruct((n, D_MODEL), F32),
        compiler_params=_params(1, 32),
        name="out_proj_ln",
    )(o2d, w, x2d, g, b)


def _swiglu_acc(x, wg, wu, wd, acc_ref, f):
    h = (_silu(_dot(x, wg)) * _dot(x, wu)).astype(BF16)
    y = _dot(h, wd)

    @pl.when(f == 0)
    def _():
        acc_ref[...] = y

    @pl.when(f != 0)
    def _():
        acc_ref[...] += y


def _ffn_ln_kernel(x_ref, wg_ref, wu_ref, wd_ref, g_ref, b_ref, y_ref, *, alpha):
    x = x_ref[...]
    xb = x.astype(BF16)
    d_ff = wg_ref.shape[1]
    y = None
    for lo in range(0, d_ff, FFN_CHUNK):
        hi = min(lo + FFN_CHUNK, d_ff)
        h = (_silu(_dot(xb, wg_ref[:, lo:hi])) * _dot(xb, wu_ref[:, lo:hi])).astype(BF16)
        part = _dot(h, wd_ref[lo:hi, :])
        y = part if y is None else y + part
    y_ref[...] = _layer_norm(alpha * x + y, g_ref[...], b_ref[...])


def _ffn_ln(x2d, wg, wu, wd, g, b, alpha):
    n = x2d.shape[0]
    d_ff = wg.shape[1]
    row = pl.BlockSpec((ROW_TILE, D_MODEL), lambda i: (i, 0))
    vec = pl.BlockSpec((1, D_MODEL), lambda i: (0, 0))
    once = pl.Buffered(1)
    return pl.pallas_call(
        functools.partial(_ffn_ln_kernel, alpha=alpha),
        grid=(n // ROW_TILE,),
        in_specs=[
            row,
            pl.BlockSpec((D_MODEL, d_ff), lambda i: (0, 0), pipeline_mode=once),
            pl.BlockSpec((D_MODEL, d_ff), lambda i: (0, 0), pipeline_mode=once),
            pl.BlockSpec((d_ff, D_MODEL), lambda i: (0, 0), pipeline_mode=once),
            vec, vec,
        ],
        out_specs=row,
        out_shape=jax.ShapeDtypeStruct((n, D_MODEL), F32),
        compiler_params=_params(1, 48),
        name="swiglu_ln",
    )(x2d, wg, wu, wd, g, b)


def _expert_kernel(be_ref, x_ref, wg_ref, wu_ref, wd_ref, y_ref, acc_ref):
    del be_ref
    f = pl.program_id(1)
    _swiglu_acc(x_ref[...], wg_ref[0], wu_ref[0], wd_ref[0], acc_ref, f)

    @pl.when(f == pl.num_programs(1) - 1)
    def _():
        _to_row_tiles(y_ref, acc_ref[...])


def _experts(blk_e, xin, wg, wu, wd):
    n_rows = xin.shape[0]
    d_ff = wg.shape[2]
    return pl.pallas_call(
        _expert_kernel,
        grid_spec=pltpu.PrefetchScalarGridSpec(
            num_scalar_prefetch=1,
            grid=(n_rows // MOE_ROWS, d_ff // EXPERT_TILE),
            in_specs=[
                pl.BlockSpec((MOE_ROWS, D_MODEL), lambda i, f, be: (i, 0)),
                pl.BlockSpec((1, D_MODEL, EXPERT_TILE), lambda i, f, be: (be[i], 0, f)),
                pl.BlockSpec((1, D_MODEL, EXPERT_TILE), lambda i, f, be: (be[i], 0, f)),
                pl.BlockSpec((1, EXPERT_TILE, D_MODEL), lambda i, f, be: (be[i], f, 0)),
            ],
            out_specs=pl.BlockSpec((MOE_ROWS,) + ROW_TILES, lambda i, f, be: (i, 0, 0)),
            scratch_shapes=[pltpu.VMEM((MOE_ROWS, D_MODEL), F32)],
        ),
        out_shape=jax.ShapeDtypeStruct((n_rows,) + ROW_TILES, F32),
        compiler_params=_params(2, 52),
        name="expert_swiglu",
    )(blk_e, xin, wg, wu, wd)


def _gmlp_kernel(x_ref, win_ref, lng_ref, lnb_ref, ws_ref, bs_ref, wout_ref, g_ref, b_ref, y_ref, yt_ref,
                 u_ref, vn_ref, gated_ref, *, alpha):
    x = x_ref[...]
    xb = x.astype(BF16)
    u_ref[...] = _gelu(_dot(xb, win_ref[:, :D_MODEL]))
    v = _gelu(_dot(xb, win_ref[:, D_MODEL:]))
    vn_ref[...] = _layer_norm(v, lng_ref[...], lnb_ref[...]).astype(BF16)
    cs = GMLP_CHUNK
    lower = lax.broadcasted_iota(jnp.int32, (cs, cs), 0) >= lax.broadcasted_iota(jnp.int32, (cs, cs), 1)
    for grp in range(GMLP_GROUPS):
        w = jnp.where(lower, ws_ref[grp], 0.0).astype(BF16)
        bias = bs_ref[grp]
        cols = slice(grp * cs, (grp + 1) * cs)
        for c in range(x.shape[0] // cs):
            rows = slice(c * cs, (c + 1) * cs)
            mixed = _dot(w, vn_ref[rows, cols]) + bias
            gated_ref[rows, cols] = (u_ref[rows, cols] * mixed).astype(BF16)
    y = _dot(gated_ref[...], wout_ref[...])
    out = _layer_norm(alpha * x + y, g_ref[...], b_ref[...])
    y_ref[...] = out
    _to_row_tiles(yt_ref, out)


def _gmlp_ln(x2d, w_in, ln_g, ln_b, w_s, b_s, w_out, g, b, alpha):
    n = x2d.shape[0]
    row = pl.BlockSpec((ROW_TILE, D_MODEL), lambda i: (i, 0))
    vec = pl.BlockSpec((1, D_MODEL), lambda i: (0, 0))
    grp = pl.BlockSpec((GMLP_GROUPS, GMLP_CHUNK, GMLP_CHUNK), lambda i: (0, 0, 0))
    return pl.pallas_call(
        functools.partial(_gmlp_kernel, alpha=alpha),
        grid=(n // ROW_TILE,),
        in_specs=[
            row,
            pl.BlockSpec((D_MODEL, 2 * D_MODEL), lambda i: (0, 0)),
            vec, vec, grp, grp,
            pl.BlockSpec((D_MODEL, D_MODEL), lambda i: (0, 0)),
            vec, vec,
        ],
        out_specs=[row, pl.BlockSpec((ROW_TILE,) + ROW_TILES, lambda i: (i, 0, 0))],
        out_shape=[
            jax.ShapeDtypeStruct((n, D_MODEL), F32),
            jax.ShapeDtypeStruct((n,) + ROW_TILES, F32),
        ],
        scratch_shapes=[
            pltpu.VMEM((ROW_TILE, D_MODEL), F32),
            pltpu.VMEM((ROW_TILE, D_MODEL), BF16),
            pltpu.VMEM((ROW_TILE, D_MODEL), BF16),
        ],
        compiler_params=_params(1, 40),
        name="gmlp_ln",
    )(x2d, w_in, ln_g, ln_b, w_s, b_s, w_out, g, b)


def _router_kernel(x_ref, w_ref, e_ref, g_ref, pos_ref, cnt_ref, run_ref):
    @pl.when(pl.program_id(0) == 0)
    def _():
        run_ref[...] = jnp.zeros_like(run_ref)

    tm = x_ref.shape[0]
    logits = jnp.dot(x_ref[...], w_ref[...], preferred_element_type=F32, precision=lax.Precision.HIGHEST)
    lane = lax.broadcasted_iota(jnp.int32, (tm, LANES), 1)
    lg = jnp.where(lane < N_EXPERTS, logits, NEG_INF)
    v1 = jnp.max(lg, axis=-1, keepdims=True)
    i1 = jnp.min(jnp.where(lg == v1, lane, LANES), axis=-1, keepdims=True)
    lg2 = jnp.where(lane == i1, NEG_INF, lg)
    v2 = jnp.max(lg2, axis=-1, keepdims=True)
    i2 = jnp.min(jnp.where(lg2 == v2, lane, LANES), axis=-1, keepdims=True)
    d = jnp.exp(v2 - v1)
    g1 = 1.0 / (1.0 + d)
    g2 = d / (1.0 + d)

    hit1 = lane == i1
    hit2 = lane == i2
    onehot = (hit1 | hit2).astype(BF16)
    strictly_lower = (
        lax.broadcasted_iota(jnp.int32, (tm, tm), 0) > lax.broadcasted_iota(jnp.int32, (tm, tm), 1)
    ).astype(BF16)
    before = _dot(strictly_lower, onehot) + run_ref[...]
    pos1 = jnp.sum(jnp.where(hit1, before, 0.0), axis=-1, keepdims=True)
    pos2 = jnp.sum(jnp.where(hit2, before, 0.0), axis=-1, keepdims=True)
    run_ref[...] += jnp.sum(onehot.astype(F32), axis=0, keepdims=True)

    e_ref[...] = jnp.where(lane == 0, i1, jnp.where(lane == 1, i2, 0))
    g_ref[...] = jnp.where(lane == 0, g1, jnp.where(lane == 1, g2, 0.0))
    pos_ref[...] = jnp.where(lane == 0, pos1, jnp.where(lane == 1, pos2, 0.0)).astype(jnp.int32)
    cnt_ref[...] = run_ref[...]


def _router(x2d, w_router):
    n = x2d.shape[0]
    tok = pl.BlockSpec((ROW_TILE, LANES), lambda i: (i, 0))
    return pl.pallas_call(
        _router_kernel,
        grid=(n // ROW_TILE,),
        in_specs=[
            pl.BlockSpec((ROW_TILE, D_MODEL), lambda i: (i, 0)),
            pl.BlockSpec((D_MODEL, LANES), lambda i: (0, 0)),
        ],
        out_specs=[tok, tok, tok, pl.BlockSpec((1, LANES), lambda i: (0, 0))],
        out_shape=[
            jax.ShapeDtypeStruct((n, LANES), jnp.int32),
            jax.ShapeDtypeStruct((n, LANES), F32),
            jax.ShapeDtypeStruct((n, LANES), jnp.int32),
            jax.ShapeDtypeStruct((1, LANES), F32),
        ],
        scratch_shapes=[pltpu.VMEM((1, LANES), F32)],
        compiler_params=_params(1, 16),
        name="moe_router",
    )(x2d, w_router)


def _gather_kernel(idx_ref, x_hbm, o_ref, buf, sem):
    base = pl.program_id(0) * GATHER_ROWS

    def issue(r, carry):
        pltpu.make_async_copy(x_hbm.at[idx_ref[base + r]], buf.at[r], sem).start()
        return carry

    lax.fori_loop(0, GATHER_ROWS, issue, 0, unroll=8)
    pltpu.make_async_copy(x_hbm.at[pl.ds(0, GATHER_ROWS)], buf, sem).wait()
    o_ref[...] = _from_row_tiles(buf).astype(BF16)


def _gather_rows(row_tok, x_tiles):
    n_rows = row_tok.shape[0]
    tile = x_tiles.shape[1:]
    return pl.pallas_call(
        _gather_kernel,
        grid_spec=pltpu.PrefetchScalarGridSpec(
            num_scalar_prefetch=1,
            grid=(n_rows // GATHER_ROWS,),
            in_specs=[pl.BlockSpec(memory_space=pl.ANY)],
            out_specs=pl.BlockSpec((GATHER_ROWS, D_MODEL), lambda i, idx: (i, 0)),
            scratch_shapes=[pltpu.VMEM((GATHER_ROWS,) + tile, F32), pltpu.SemaphoreType.DMA(())],
        ),
        out_shape=jax.ShapeDtypeStruct((n_rows, D_MODEL), BF16),
        compiler_params=_params(1, 16),
        name="moe_gather",
    )(row_tok, x_tiles)


def _combine_kernel(d0_ref, d1_ref, y_hbm, gate_ref, x_ref, g_ref, b_ref, o_ref, buf0, buf1, sem, *, alpha):
    tm = x_ref.shape[0]
    base = pl.program_id(0) * tm

    def issue(r, carry):
        pltpu.make_async_copy(y_hbm.at[d0_ref[base + r]], buf0.at[r], sem.at[0]).start()
        pltpu.make_async_copy(y_hbm.at[d1_ref[base + r]], buf1.at[r], sem.at[1]).start()
        return carry

    lax.fori_loop(0, tm, issue, 0, unroll=8)
    pltpu.make_async_copy(y_hbm.at[pl.ds(0, tm)], buf0, sem.at[0]).wait()
    pltpu.make_async_copy(y_hbm.at[pl.ds(0, tm)], buf1, sem.at[1]).wait()
    gate = gate_ref[...]
    y = _from_row_tiles(buf0) * gate[:, 0:1] + _from_row_tiles(buf1) * gate[:, 1:2]
    o_ref[...] = _layer_norm(alpha * x_ref[...] + y, g_ref[...], b_ref[...])


def _combine_ln(d0, d1, yb, gate, x2d, g, b, alpha):
    n = x2d.shape[0]
    tm = GATHER_ROWS
    row = pl.BlockSpec((tm, D_MODEL), lambda i, a, c: (i, 0))
    vec = pl.BlockSpec((1, D_MODEL), lambda i, a, c: (0, 0))
    return pl.pallas_call(
        functools.partial(_combine_kernel, alpha=alpha),
        grid_spec=pltpu.PrefetchScalarGridSpec(
            num_scalar_prefetch=2,
            grid=(n // tm,),
            in_specs=[
                pl.BlockSpec(memory_space=pl.ANY),
                pl.BlockSpec((tm, LANES), lambda i, a, c: (i, 0)),
                row, vec, vec,
            ],
            out_specs=row,
            scratch_shapes=[
                pltpu.VMEM((tm,) + ROW_TILES, F32),
                pltpu.VMEM((tm,) + ROW_TILES, F32),
                pltpu.SemaphoreType.DMA((2,)),
            ],
        ),
        out_shape=jax.ShapeDtypeStruct((n, D_MODEL), F32),
        compiler_params=_params(1, 16),
        name="moe_combine_ln",
    )(d0, d1, yb, gate, x2d, g, b)


def _attention_layer(x2d, bsz, seq, w_in, sinks, pe_k, wk1, wk2, pe_v, wv1, wv2, w_o, g, b, alpha):
    aq, akv = A_HEADS * HEAD_DIM, A_KV_HEADS * HEAD_DIM
    bq, bkv = B_HEADS * HEAD_DIM, B_KV_HEADS * HEAD_DIM
    bounds = [0]
    for width in (aq, akv, akv, bq, bkv, bkv, bkv, bkv, bkv, bkv, 3 * B_HEADS):
        bounds.append(bounds[-1] + width)
    qa, ka, va, qb, kc, vc, ks, vs, kw, vw, wg = [w_in[:, bounds[i]:bounds[i + 1]] for i in range(11)]
    w_tok = jnp.concatenate([ka, ks, kw, kc, vc], axis=1).astype(BF16)
    w_feat = jnp.concatenate([qa * SCALE, qb * SCALE, va, vs, vw], axis=1).T.astype(BF16)
    wg = wg.reshape(D_MODEL, B_HEADS, 3).transpose(2, 1, 0).reshape(3 * B_HEADS, D_MODEL)
    wg = jnp.pad(wg, ((0, GATE_ROWS - 3 * B_HEADS), (0, 0))).astype(BF16)
    tok, feat, gates_t = _inproj(x2d, w_tok, w_feat, wg, bsz, seq)

    n_str = seq // CMP_STRIDE
    kvc = tok[:, 3 * akv:]
    xs = kvc.reshape(bsz, n_str, CMP_STRIDE, 2, B_KV_HEADS, HEAD_DIM).transpose(0, 4, 3, 1, 2, 5)
    xs = xs.reshape(bsz, B_KV_HEADS, 2, n_str, CMP_STRIDE * HEAD_DIM)
    pe = jnp.stack([pe_k, pe_v]).reshape(2, 1, CMP_LEN * HEAD_DIM)
    pe = jnp.broadcast_to(pe, (2, 8, CMP_LEN * HEAD_DIM))
    kcmp, vcmp_t = _compress(xs, pe, jnp.stack([wk1, wv1]).astype(BF16), wk2.astype(BF16), wv2.T.astype(BF16))

    o = _attention(sinks, tok.reshape(bsz, seq, TOK_W), feat, kcmp, vcmp_t, gates_t, seq)
    return _proj_ln(o.reshape(bsz * seq, N_ATT_HEADS * HEAD_DIM), w_o.astype(BF16), x2d, g, b, alpha)


def _moe_layer(x2d, x_tiles, w_router, w_gate, w_up, w_down, g, b, alpha):
    n_tok = x2d.shape[0]
    w_r = jnp.pad(w_router, ((0, 0), (0, LANES - N_EXPERTS)))
    e_out, gate, pos_out, cnt = _router(x2d, w_r)
    expert = e_out[:, :TOP_K]
    pos = pos_out[:, :TOP_K]
    counts = cnt[0, :N_EXPERTS].astype(jnp.int32)
    padded = (counts + MOE_ROWS - 1) // MOE_ROWS * MOE_ROWS
    pad_ends = jnp.cumsum(padded)
    pad_starts = pad_ends - padded
    dest = pad_starts[expert] + pos
    n_blocks = n_tok * TOP_K // MOE_ROWS + N_EXPERTS
    n_rows = n_blocks * MOE_ROWS
    tok = jnp.broadcast_to(jnp.arange(n_tok, dtype=jnp.int32)[:, None], (n_tok, TOP_K))
    row_tok = jnp.zeros((n_rows,), jnp.int32).at[dest.reshape(-1)].set(tok.reshape(-1))
    blk_start = jnp.arange(n_blocks, dtype=jnp.int32) * MOE_ROWS
    blk_e = jnp.sum((pad_ends[None, :] <= blk_start[:, None]).astype(jnp.int32), axis=1)
    blk_e = jnp.minimum(blk_e, N_EXPERTS - 1)

    xin = _gather_rows(row_tok, x_tiles)
    yb = _experts(blk_e, xin, w_gate.astype(BF16), w_up.astype(BF16), w_down.astype(BF16))
    return _combine_ln(dest[:, 0], dest[:, 1], yb, gate, x2d, g, b, alpha)


def kernel(x, att_w_in, att_sinks, cmp_pe_k, cmp_wk1, cmp_wk2, cmp_pe_v, cmp_wv1, cmp_wv2, att_w_o,
           ffn_w_gate, ffn_w_up, ffn_w_down, gmlp_w_in, gmlp_ln_g, gmlp_ln_b, gmlp_w_s, gmlp_b_s, gmlp_w_out,
           moe_w_router, moe_w_gate, moe_w_up, moe_w_down, ln_g, ln_b):
    bsz, seq, dm = x.shape
    depth = ln_g.shape[0]
    alpha = (2.0 * depth) ** 0.25
    assert dm == D_MODEL and seq % ROW_TILE == 0
    x2d = x.reshape(bsz * seq, dm)

    def vec(p):
        return p.reshape(1, -1)

    for i in range(depth):
        j = i // 2
        g0, b0, g1, b1 = vec(ln_g[i, 0]), vec(ln_b[i, 0]), vec(ln_g[i, 1]), vec(ln_b[i, 1])
        if i % 2 == 0:
            x2d = _attention_layer(x2d, bsz, seq, att_w_in[j], att_sinks[j], cmp_pe_k[j], cmp_wk1[j], cmp_wk2[j],
                                   cmp_pe_v[j], cmp_wv1[j], cmp_wv2[j], att_w_o[j], g0, b0, alpha)
            x2d = _ffn_ln(x2d, ffn_w_gate[j].astype(BF16), ffn_w_up[j].astype(BF16), ffn_w_down[j].astype(BF16),
                          g1, b1, alpha)
        else:
            b_s = jnp.broadcast_to(gmlp_b_s[j][:, :, None], (GMLP_GROUPS, GMLP_CHUNK, GMLP_CHUNK))
            x2d, x_tiles = _gmlp_ln(x2d, gmlp_w_in[j].astype(BF16), vec(gmlp_ln_g[j]), vec(gmlp_ln_b[j]), gmlp_w_s[j],
                                    b_s, gmlp_w_out[j].astype(BF16), g0, b0, alpha)
            x2d = _moe_layer(x2d, x_tiles, moe_w_router[j], moe_w_gate[j], moe_w_up[j], moe_w_down[j], g1, b1, alpha)
    return x2d.reshape(bsz, seq, dm)
```

```python
import functools

import jax
import jax.numpy as jnp
import numpy as np
from jax import lax
from jax.experimental import pallas as pl
from jax.experimental.pallas import tpu as pltpu

F32 = jnp.float32
BF16 = jnp.bfloat16

D_MODEL = 1024
HEAD_DIM = 64
BLOCK_Q = 128
A_HEADS = 8
A_KV_HEADS = 2
A_WINDOW = 128
B_HEADS = 8
B_KV_HEADS = 2
GROUP = 4
CMP_LEN = 32
CMP_STRIDE = 16
CMP_HIDDEN = 256
SLC_BLOCK = 64
SLC_TOPN = 8
B_WINDOW = 256
N_ATT_HEADS = A_HEADS + B_HEADS
GMLP_GROUPS = 8
GMLP_CHUNK = 128
N_EXPERTS = 8
TOP_K = 2
LN_EPS = 1e-5
LANES = 128
ROW_TILES = (D_MODEL // LANES, LANES)

TOK_W = 640
FEAT_W = 1408
GATE_ROWS = 32
SLC_CHUNK = 256

ROW_TILE = 512
MOE_ROWS = 512
GATHER_ROWS = 256
FFN_CHUNK = 1024
EXPERT_TILE = 1792

ALIBI_SLOPES = tuple(2.0 ** (-8.0 * h / N_ATT_HEADS) for h in range(1, N_ATT_HEADS + 1))
SCALE = HEAD_DIM ** -0.5
NEG_INF = float("-inf")
RANK_FORCED = 1e30
RANK_INVALID = -1.0
MASK_BIG = 1e30
AUX_SEL = 32


def _params(n_grid, vmem_mb):
    return pltpu.CompilerParams(
        dimension_semantics=("arbitrary",) * n_grid, vmem_limit_bytes=vmem_mb * 1024 * 1024
    )


def _dot(a, b):
    return jnp.dot(a, b, preferred_element_type=F32)


def _dot_nt(a, b):
    return lax.dot_general(a, b, (((1,), (1,)), ((), ())), preferred_element_type=F32)


def _layer_norm(z, g, b):
    mu = jnp.mean(z, axis=-1, keepdims=True)
    zc = z - mu
    var = jnp.mean(zc * zc, axis=-1, keepdims=True)
    return zc * lax.rsqrt(var + LN_EPS) * g + b


def _gelu(x):
    return 0.5 * x * (1.0 + jnp.tanh(0.7978845608028654 * (x + 0.044715 * (x * x * x))))


def _silu(x):
    return x / (1.0 + jnp.exp(-x))


def _to_row_tiles(o_ref, y):
    for c in range(D_MODEL // LANES):
        o_ref[:, c, :] = y[:, c * LANES:(c + 1) * LANES]


def _from_row_tiles(ref):
    return jnp.concatenate([ref[:, c, :] for c in range(D_MODEL // LANES)], axis=1)


def _inproj_kernel(x_ref, wt_ref, wf_ref, wg_ref, tok_ref, feat_ref, gate_ref):
    x = x_ref[...].astype(BF16)
    tok_ref[...] = _dot(x, wt_ref[...]).astype(BF16)
    rows = FEAT_W // 4
    for c in range(4):
        feat_ref[0, c * rows:(c + 1) * rows, :] = _dot_nt(wf_ref[c * rows:(c + 1) * rows, :], x).astype(BF16)
    gate_ref[0] = _dot_nt(wg_ref[...], x)


def _inproj(x2d, w_tok, w_feat, w_gate, bsz, seq):
    n = x2d.shape[0]
    per_seq = seq // ROW_TILE
    return pl.pallas_call(
        _inproj_kernel,
        grid=(n // ROW_TILE,),
        in_specs=[
            pl.BlockSpec((ROW_TILE, D_MODEL), lambda i: (i, 0)),
            pl.BlockSpec((D_MODEL, TOK_W), lambda i: (0, 0)),
            pl.BlockSpec((FEAT_W, D_MODEL), lambda i: (0, 0)),
            pl.BlockSpec((GATE_ROWS, D_MODEL), lambda i: (0, 0)),
        ],
        out_specs=[
            pl.BlockSpec((ROW_TILE, TOK_W), lambda i: (i, 0)),
            pl.BlockSpec((1, FEAT_W, ROW_TILE), lambda i: (i // per_seq, 0, i % per_seq)),
            pl.BlockSpec((1, GATE_ROWS, ROW_TILE), lambda i: (i // per_seq, 0, i % per_seq)),
        ],
        out_shape=[
            jax.ShapeDtypeStruct((n, TOK_W), BF16),
            jax.ShapeDtypeStruct((bsz, FEAT_W, seq), BF16),
            jax.ShapeDtypeStruct((bsz, GATE_ROWS, seq), F32),
        ],
        compiler_params=_params(1, 32),
        name="inproj",
    )(x2d, w_tok, w_feat, w_gate)


def _compress_hidden(x, pe, w1):
    half = CMP_STRIDE * HEAD_DIM
    a = _dot(x, w1[:half])
    b = _dot(x, w1[half:])
    b_next = pltpu.roll(b, shift=x.shape[0] - 1, axis=0)
    bias = _dot(pe.astype(BF16), w1)[0:1]
    return _gelu(a + b_next + bias).astype(BF16)


def _compress_kernel(x_ref, pe_ref, w1_ref, w2k_ref, w2vt_ref, kc_ref, vct_ref):
    hk = _compress_hidden(x_ref[0, 0, 0], pe_ref[0], w1_ref[0])
    kc_ref[0, 0] = _dot(hk, w2k_ref[...]).astype(BF16)
    hv = _compress_hidden(x_ref[0, 0, 1], pe_ref[1], w1_ref[1])
    vct_ref[0, 0] = _dot_nt(w2vt_ref[...], hv).astype(BF16)


def _compress(xs, pe, w1, w2k, w2vt):
    bsz, _, _, n_str, width = xs.shape

    def full(shape):
        return pl.BlockSpec(shape, lambda b, h: (0,) * len(shape))

    return pl.pallas_call(
        _compress_kernel,
        grid=(bsz, B_KV_HEADS),
        in_specs=[
            pl.BlockSpec((1, 1, 2, n_str, width), lambda b, h: (b, h, 0, 0, 0)),
            full((2, 8, 2 * width)),
            full((2, 2 * width, CMP_HIDDEN)),
            full((CMP_HIDDEN, HEAD_DIM)),
            full((HEAD_DIM, CMP_HIDDEN)),
        ],
        out_specs=[
            pl.BlockSpec((1, 1, n_str, HEAD_DIM), lambda b, h: (b, h, 0, 0)),
            pl.BlockSpec((1, 1, HEAD_DIM, n_str), lambda b, h: (b, h, 0, 0)),
        ],
        out_shape=[
            jax.ShapeDtypeStruct((bsz, B_KV_HEADS, n_str, HEAD_DIM), BF16),
            jax.ShapeDtypeStruct((bsz, B_KV_HEADS, HEAD_DIM, n_str), BF16),
        ],
        compiler_params=_params(2, 24),
        name="nsa_compress",
    )(xs, pe, w1, w2k, w2vt)


def _bf16_parts(x, parts=3):
    out, rest = [], np.asarray(x, np.float32)
    for _ in range(parts):
        piece = rest.astype(jnp.bfloat16).astype(np.float32)
        out.append(piece)
        rest = rest - piece
    return out


def _aux_key_lanes(seq):
    t = np.arange(seq)
    aux = np.zeros((seq, LANES), np.float32)
    aux[t, t // SLC_BLOCK] = 1.0
    aux[:, AUX_SEL:AUX_SEL + 3] = (t >> 7)[:, None]
    aux[:, AUX_SEL + 3:AUX_SEL + 6] = (t & 127)[:, None]
    return jnp.asarray(aux, BF16)


def _aux_slope_rows():
    rows = np.zeros((N_ATT_HEADS // GROUP, LANES - AUX_SEL, GROUP * BLOCK_Q), np.float32)
    for grp in range(N_ATT_HEADS // GROUP):
        for g in range(GROUP):
            slope = ALIBI_SLOPES[grp * GROUP + g]
            cols = slice(g * BLOCK_Q, (g + 1) * BLOCK_Q)
            for i, piece in enumerate(_bf16_parts(128.0 * slope) + _bf16_parts(slope)):
                rows[grp, i, cols] = piece
    return jnp.asarray(rows, BF16)


def _attn_kernel(sink_ref, qt_ref, ka_ref, ks_ref, kw_ref, vat_ref, vst_ref, vwt_ref, kc_ref, vct_ref, gt_ref,
                 aux_ref, slope_ref, o_ref, ot_ref, mask_a_ref, mask_w_ref, *, seq):
    n = pl.program_id(1)
    t0 = n * BLOCK_Q
    n_str = seq // CMP_STRIDE
    n_cmp = n_str - CMP_LEN // CMP_STRIDE + 1
    n_sel = seq // SLC_BLOCK
    top_n = min(SLC_TOPN, n_sel)
    hd = HEAD_DIM
    gq = GROUP * BLOCK_Q
    ch = SLC_CHUNK
    span_a = A_WINDOW + BLOCK_Q
    span_w = B_WINDOW + BLOCK_Q

    lane = lax.broadcasted_iota(jnp.int32, (1, gq), 1)
    q_loc = lane & (BLOCK_Q - 1)
    lane_head = lane >> (BLOCK_Q.bit_length() - 1)
    t_q = t0 + q_loc

    def head_row(vals):
        return jnp.where(lane_head == 0, vals[0], jnp.where(lane_head == 1, vals[1],
                                                           jnp.where(lane_head == 2, vals[2], vals[3])))

    def q_group(first_head):
        return jnp.concatenate(
            [qt_ref[0, (first_head + g) * hd:(first_head + g + 1) * hd, :] for g in range(GROUP)], axis=1)

    def score_rhs(grp, kvh, qg, sel_rows):
        zero = jnp.zeros((hd, gq), BF16)
        q_rows = [qg, zero] if kvh == 0 else [zero, qg]
        return jnp.concatenate(q_rows + [sel_rows, slope_ref[grp]], axis=0)

    def score_lhs(k_ref, start, span):
        return jnp.concatenate([k_ref[0, pl.ds(start, span), :], aux_ref[pl.ds(start, span), :]], axis=1)

    def band_start(span):
        return pl.multiple_of(jnp.maximum(t0 - (span - BLOCK_Q), 0), BLOCK_Q)

    def band_mask(span, window):
        ik = lax.broadcasted_iota(jnp.int32, (span, gq), 0)
        dist = (t0 - band_start(span)) + q_loc - ik
        return jnp.where((dist >= 0) & (dist < window), 0.0, -MASK_BIG)

    @pl.when(n <= span_w // BLOCK_Q - 1)
    def _():
        mask_a_ref[...] = band_mask(span_a, A_WINDOW)
        mask_w_ref[...] = band_mask(span_w, B_WINDOW)

    def band(k_ref, vt_ref, mask_ref, kvh, span, rhs, sinks):
        ks = band_start(span)
        s = _dot(score_lhs(k_ref, ks, span), rhs) + mask_ref[...]
        m = jnp.max(s, axis=0, keepdims=True)
        if sinks is not None:
            m = jnp.maximum(m, sinks)
        e = jnp.exp(s - m)
        den = jnp.sum(e, axis=0, keepdims=True)
        if sinks is not None:
            den = den + jnp.exp(sinks - m)
        return _dot(vt_ref[0, kvh * hd:(kvh + 1) * hd, pl.ds(ks, span)], e.astype(BF16)) / den

    no_sel = jnp.zeros((AUX_SEL, gq), BF16)

    for kvh in range(A_KV_HEADS):
        heads = [kvh * GROUP + g for g in range(GROUP)]
        slopes = head_row([ALIBI_SLOPES[h] for h in heads])
        sinks = head_row([sink_ref[h] for h in heads]) + slopes * t_q.astype(F32)
        rhs = score_rhs(kvh, kvh, q_group(heads[0]), no_sel)
        ot = band(ka_ref, vat_ref, mask_a_ref, kvh, span_a, rhs, sinks)
        for g, h in enumerate(heads):
            ot_ref[h * hd:(h + 1) * hd, :] = ot[:, g * BLOCK_Q:(g + 1) * BLOCK_Q]

    gt = jax.nn.sigmoid(gt_ref[0])
    c_idx = lax.broadcasted_iota(jnp.int32, (n_str, gq), 0)
    dist_c = t_q - (c_idx * CMP_STRIDE + CMP_LEN - 1)
    valid_c = (dist_c >= 0) & (c_idx < n_cmp)
    dist_cf = dist_c.astype(F32)
    oj = lax.broadcasted_iota(jnp.int32, (n_sel, n_str), 0) * SLC_BLOCK
    oc = lax.broadcasted_iota(jnp.int32, (n_sel, n_str), 1) * CMP_STRIDE
    overlap_t = ((oc < oj + SLC_BLOCK) & (oc + CMP_LEN > oj)).astype(BF16)
    j_idx = lax.broadcasted_iota(jnp.int32, (n_sel, BLOCK_Q), 0)
    t_blk = (t0 + lax.broadcasted_iota(jnp.int32, (1, BLOCK_Q), 1)) >> (SLC_BLOCK.bit_length() - 1)
    valid_j = j_idx <= t_blk
    forced_j = (j_idx == 0) | (j_idx == t_blk) | (j_idx == t_blk - 1)
    dloc = q_loc - lax.broadcasted_iota(jnp.int32, (ch, gq), 0)
    n_full = t0 // ch

    for kvh in range(B_KV_HEADS):
        grp = A_KV_HEADS + kvh
        heads = [grp * GROUP + g for g in range(GROUP)]
        slopes = head_row([ALIBI_SLOPES[h] for h in heads])
        qg = q_group(heads[0])

        s = _dot(kc_ref[0, kvh], qg) - slopes * dist_cf
        s = jnp.where(valid_c, s, NEG_INF)
        m = jnp.max(s, axis=0, keepdims=True)
        m = jnp.where(m == NEG_INF, 0.0, m)
        e = jnp.exp(s - m)
        den = jnp.sum(e, axis=0, keepdims=True)
        den = jnp.where(den > 0.0, den, 1.0)
        o_cmp = _dot(vct_ref[0, kvh], e.astype(BF16)) / den
        p = e / den
        psum = p[:, 0:BLOCK_Q]
        for g in range(1, GROUP):
            psum = psum + p[:, g * BLOCK_Q:(g + 1) * BLOCK_Q]
        p_hi = psum.astype(BF16)
        p_lo = (psum - p_hi.astype(F32)).astype(BF16)
        imp = _dot(overlap_t, p_hi) + _dot(overlap_t, p_lo)

        rank = jnp.where(forced_j, RANK_FORCED, jnp.where(valid_j, imp, RANK_INVALID))
        cnt = jnp.zeros((n_sel, BLOCK_Q), jnp.int32)
        for jp in range(n_sel):
            row = rank[jp:jp + 1, :]
            before = (row > rank) | ((row == rank) & (j_idx > jp))
            cnt = cnt + before.astype(jnp.int32)
        sel_bias = jnp.where(cnt < top_n, 0.0, -MASK_BIG)
        if n_sel < AUX_SEL:
            sel_bias = jnp.concatenate([sel_bias, jnp.zeros((AUX_SEL - n_sel, BLOCK_Q), F32)], axis=0)
        sel_rows = jnp.concatenate([sel_bias.astype(BF16)] * GROUP, axis=1)
        rhs = score_rhs(grp, kvh, qg, sel_rows)

        def slc_scores(c, rhs=rhs):
            return _dot(score_lhs(ks_ref, pl.multiple_of(c * ch, ch), ch), rhs)

        def slc_update(c, m, l, acc, s, kvh=kvh):
            m_new = jnp.maximum(m, jnp.max(s, axis=0, keepdims=True))
            alpha = jnp.exp(m - m_new)
            e = jnp.exp(s - m_new)
            l_new = alpha * l + jnp.sum(e, axis=0, keepdims=True)
            vt = vst_ref[0, kvh * hd:(kvh + 1) * hd, pl.ds(pl.multiple_of(c * ch, ch), ch)]
            return m_new, l_new, alpha * acc + _dot(vt, e.astype(BF16))

        def slc_step(c, carry, slc_scores=slc_scores, slc_update=slc_update):
            m, l, acc, s = carry
            s_next = slc_scores(c + 1)
            return slc_update(c, m, l, acc, s) + (s_next,)

        init = (jnp.full((1, gq), NEG_INF, F32), jnp.zeros((1, gq), F32), jnp.zeros((hd, gq), F32))
        m, l, acc, s = lax.fori_loop(0, n_full, slc_step, init + (slc_scores(0),))
        s = jnp.where(dloc >= n_full * ch - t0, s, -MASK_BIG)
        _, l_fin, acc_fin = slc_update(n_full, m, l, acc, s)
        o_slc = acc_fin / l_fin

        o_win = band(kw_ref, vwt_ref, mask_w_ref, kvh, span_w, score_rhs(grp, kvh, qg, no_sel), None)

        for g, h in enumerate(heads):
            hb = kvh * GROUP + g
            cols = slice(g * BLOCK_Q, (g + 1) * BLOCK_Q)
            ot_ref[h * hd:(h + 1) * hd, :] = (
                gt[hb:hb + 1] * o_cmp[:, cols]
                + gt[B_HEADS + hb:B_HEADS + hb + 1] * o_slc[:, cols]
                + gt[2 * B_HEADS + hb:2 * B_HEADS + hb + 1] * o_win[:, cols]
            )

    o_ref[0] = ot_ref[...].T.astype(BF16)


def _attention(sinks, tok, feat, kc, vct, gates_t, seq):
    bsz = tok.shape[0]
    n_str = seq // CMP_STRIDE
    kv_w = A_KV_HEADS * HEAD_DIM
    q_rows = N_ATT_HEADS * HEAD_DIM
    gq = GROUP * BLOCK_Q
    assert seq // SLC_BLOCK <= AUX_SEL and seq <= 128 * 256

    def k_spec(col_block):
        return pl.BlockSpec((1, seq, kv_w), lambda b, n: (b, 0, col_block))

    def vt_spec(row_block):
        return pl.BlockSpec((1, kv_w, seq), lambda b, n: (b, q_rows // kv_w + row_block, 0))

    return pl.pallas_call(
        functools.partial(_attn_kernel, seq=seq),
        grid=(bsz, seq // BLOCK_Q),
        in_specs=[
            pl.BlockSpec(memory_space=pltpu.SMEM),
            pl.BlockSpec((1, q_rows, BLOCK_Q), lambda b, n: (b, 0, n)),
            k_spec(0), k_spec(1), k_spec(2),
            vt_spec(0), vt_spec(1), vt_spec(2),
            pl.BlockSpec((1, B_KV_HEADS, n_str, HEAD_DIM), lambda b, n: (b, 0, 0, 0)),
            pl.BlockSpec((1, B_KV_HEADS, HEAD_DIM, n_str), lambda b, n: (b, 0, 0, 0)),
            pl.BlockSpec((1, GATE_ROWS, BLOCK_Q), lambda b, n: (b, 0, n)),
            pl.BlockSpec((seq, LANES), lambda b, n: (0, 0)),
            pl.BlockSpec((N_ATT_HEADS // GROUP, LANES - AUX_SEL, gq), lambda b, n: (0, 0, 0)),
        ],
        out_specs=pl.BlockSpec((1, BLOCK_Q, q_rows), lambda b, n: (b, n, 0)),
        out_shape=jax.ShapeDtypeStruct((bsz, seq, q_rows), BF16),
        scratch_shapes=[
            pltpu.VMEM((q_rows, BLOCK_Q), F32),
            pltpu.VMEM((A_WINDOW + BLOCK_Q, gq), F32),
            pltpu.VMEM((B_WINDOW + BLOCK_Q, gq), F32),
        ],
        compiler_params=_params(2, 32),
        name="hybrid_attention",
    )(sinks, feat, tok, tok, tok, feat, feat, feat, kc, vct, gates_t, _aux_key_lanes(seq), _aux_slope_rows())


def _proj_ln_kernel(o_ref, w_ref, x_ref, g_ref, b_ref, y_ref, *, alpha):
    y = _dot(o_ref[...], w_ref[...])
    y_ref[...] = _layer_norm(alpha * x_ref[...] + y, g_ref[...], b_ref[...])


def _proj_ln(o2d, w, x2d, g, b, alpha):
    n, k = o2d.shape
    row = pl.BlockSpec((ROW_TILE, D_MODEL), lambda i: (i, 0))
    vec = pl.BlockSpec((1, D_MODEL), lambda i: (0, 0))
    return pl.pallas_call(
        functools.partial(_proj_ln_kernel, alpha=alpha),
        grid=(n // ROW_TILE,),
        in_specs=[
            pl.BlockSpec((ROW_TILE, k), lambda i: (i, 0)),
            pl.BlockSpec((k, D_MODEL), lambda i: (0, 0)),
            row, vec, vec,
        ],
        out_specs=row,
        out_shape=jax.ShapeDtypeStruct((n, D_MODEL), F32),
        compiler_params=_params(1, 32),
        name="out_proj_ln",
    )(o2d, w, x2d, g, b)


def _swiglu_acc(x, wg, wu, wd, acc_ref, f):
    h = (_silu(_dot(x, wg)) * _dot(x, wu)).astype(BF16)
    y = _dot(h, wd)

    @pl.when(f == 0)
    def _():
        acc_ref[...] = y

    @pl.when(f != 0)
    def _():
        acc_ref[...] += y


def _ffn_ln_kernel(x_ref, wg_ref, wu_ref, wd_ref, g_ref, b_ref, y_ref, *, alpha):
    x = x_ref[...]
    xb = x.astype(BF16)
    d_ff = wg_ref.shape[1]
    y = None
    for lo in range(0, d_ff, FFN_CHUNK):
        hi = min(lo + FFN_CHUNK, d_ff)
        h = (_silu(_dot(xb, wg_ref[:, lo:hi])) * _dot(xb, wu_ref[:, lo:hi])).astype(BF16)
        part = _dot(h, wd_ref[lo:hi, :])
        y = part if y is None else y + part
    y_ref[...] = _layer_norm(alpha * x + y, g_ref[...], b_ref[...])


def _ffn_ln(x2d, wg, wu, wd, g, b, alpha):
    n = x2d.shape[0]
    d_ff = wg.shape[1]
    row = pl.BlockSpec((ROW_TILE, D_MODEL), lambda i: (i, 0))
    vec = pl.BlockSpec((1, D_MODEL), lambda i: (0, 0))
    once = pl.Buffered(1)
    return pl.pallas_call(
        functools.partial(_ffn_ln_kernel, alpha=alpha),
        grid=(n // ROW_TILE,),
        in_specs=[
            row,
            pl.BlockSpec((D_MODEL, d_ff), lambda i: (0, 0), pipeline_mode=once),
            pl.BlockSpec((D_MODEL, d_ff), lambda i: (0, 0), pipeline_mode=once),
            pl.BlockSpec((d_ff, D_MODEL), lambda i: (0, 0), pipeline_mode=once),
            vec, vec,
        ],
        out_specs=row,
        out_shape=jax.ShapeDtypeStruct((n, D_MODEL), F32),
        compiler_params=_params(1, 48),
        name="swiglu_ln",
    )(x2d, wg, wu, wd, g, b)


def _expert_kernel(be_ref, x_ref, wg_ref, wu_ref, wd_ref, y_ref, xb_ref, acc_ref):
    del be_ref
    f = pl.program_id(1)

    @pl.when(f == 0)
    def _():
        xb_ref[...] = _from_row_tiles(x_ref).astype(BF16)

    _swiglu_acc(xb_ref[...], wg_ref[0], wu_ref[0], wd_ref[0], acc_ref, f)

    @pl.when(f == pl.num_programs(1) - 1)
    def _():
        _to_row_tiles(y_ref, acc_ref[...])


def _experts(blk_e, xin, wg, wu, wd):
    n_rows = blk_e.shape[0] * MOE_ROWS
    d_ff = wg.shape[2]
    return pl.pallas_call(
        _expert_kernel,
        grid_spec=pltpu.PrefetchScalarGridSpec(
            num_scalar_prefetch=1,
            grid=(n_rows // MOE_ROWS, d_ff // EXPERT_TILE),
            in_specs=[
                pl.BlockSpec((MOE_ROWS,) + ROW_TILES, lambda i, f, be: (i, 0, 0)),
                pl.BlockSpec((1, D_MODEL, EXPERT_TILE), lambda i, f, be: (be[i], 0, f)),
                pl.BlockSpec((1, D_MODEL, EXPERT_TILE), lambda i, f, be: (be[i], 0, f)),
                pl.BlockSpec((1, EXPERT_TILE, D_MODEL), lambda i, f, be: (be[i], f, 0)),
            ],
            out_specs=pl.BlockSpec((MOE_ROWS,) + ROW_TILES, lambda i, f, be: (i, 0, 0)),
            scratch_shapes=[pltpu.VMEM((MOE_ROWS, D_MODEL), BF16), pltpu.VMEM((MOE_ROWS, D_MODEL), F32)],
        ),
        out_shape=jax.ShapeDtypeStruct((n_rows,) + ROW_TILES, F32),
        compiler_params=_params(2, 52),
        name="expert_swiglu",
    )(blk_e, xin, wg, wu, wd)


def _gmlp_kernel(x_ref, win_ref, lng_ref, lnb_ref, ws_ref, bs_ref, wout_ref, g_ref, b_ref, y_ref, yt_ref,
                 u_ref, vn_ref, gated_ref, *, alpha):
    x = x_ref[...]
    xb = x.astype(BF16)
    u_ref[...] = _gelu(_dot(xb, win_ref[:, :D_MODEL]))
    v = _gelu(_dot(xb, win_ref[:, D_MODEL:]))
    vn_ref[...] = _layer_norm(v, lng_ref[...], lnb_ref[...]).astype(BF16)
    cs = GMLP_CHUNK
    lower = lax.broadcasted_iota(jnp.int32, (cs, cs), 0) >= lax.broadcasted_iota(jnp.int32, (cs, cs), 1)
    for grp in range(GMLP_GROUPS):
        w = jnp.where(lower, ws_ref[grp], 0.0).astype(BF16)
        bias = bs_ref[grp]
        cols = slice(grp * cs, (grp + 1) * cs)
        for c in range(x.shape[0] // cs):
            rows = slice(c * cs, (c + 1) * cs)
            mixed = _dot(w, vn_ref[rows, cols]) + bias
            gated_ref[rows, cols] = (u_ref[rows, cols] * mixed).astype(BF16)
    y = _dot(gated_ref[...], wout_ref[...])
    out = _layer_norm(alpha * x + y, g_ref[...], b_ref[...])
    y_ref[...] = out
    _to_row_tiles(yt_ref, out)


def _gmlp_ln(x2d, w_in, ln_g, ln_b, w_s, b_s, w_out, g, b, alpha):
    n = x2d.shape[0]
    row = pl.BlockSpec((ROW_TILE, D_MODEL), lambda i: (i, 0))
    vec = pl.BlockSpec((1, D_MODEL), lambda i: (0, 0))
    grp = pl.BlockSpec((GMLP_GROUPS, GMLP_CHUNK, GMLP_CHUNK), lambda i: (0, 0, 0))
    return pl.pallas_call(
        functools.partial(_gmlp_kernel, alpha=alpha),
        grid=(n // ROW_TILE,),
        in_specs=[
            row,
            pl.BlockSpec((D_MODEL, 2 * D_MODEL), lambda i: (0, 0)),
            vec, vec, grp, grp,
            pl.BlockSpec((D_MODEL, D_MODEL), lambda i: (0, 0)),
            vec, vec,
        ],
        out_specs=[row, pl.BlockSpec((ROW_TILE,) + ROW_TILES, lambda i: (i, 0, 0))],
        out_shape=[
            jax.ShapeDtypeStruct((n, D_MODEL), F32),
            jax.ShapeDtypeStruct((n,) + ROW_TILES, F32),
        ],
        scratch_shapes=[
            pltpu.VMEM((ROW_TILE, D_MODEL), F32),
            pltpu.VMEM((ROW_TILE, D_MODEL), BF16),
            pltpu.VMEM((ROW_TILE, D_MODEL), BF16),
        ],
        compiler_params=_params(1, 40),
        name="gmlp_ln",
    )(x2d, w_in, ln_g, ln_b, w_s, b_s, w_out, g, b)


def _route_kernel(x_ref, w_ref, gate_ref, et_ref, post_ref, cnt_ref, run_ref):
    tm = x_ref.shape[0]

    @pl.when(pl.program_id(0) == 0)
    def _():
        run_ref[...] = jnp.zeros_like(run_ref)

    logits = lax.dot_general(w_ref[...], x_ref[...], (((1,), (1,)), ((), ())),
                             preferred_element_type=F32, precision=lax.Precision.HIGHEST)
    row = lax.broadcasted_iota(jnp.int32, (N_EXPERTS, tm), 0)
    v1 = jnp.max(logits, axis=0, keepdims=True)
    i1 = jnp.min(jnp.where(logits == v1, row, N_EXPERTS), axis=0, keepdims=True)
    rest = jnp.where(row == i1, NEG_INF, logits)
    v2 = jnp.max(rest, axis=0, keepdims=True)
    i2 = jnp.min(jnp.where(rest == v2, row, N_EXPERTS), axis=0, keepdims=True)
    d = jnp.exp(v2 - v1)
    g1 = 1.0 / (1.0 + d)
    g2 = d / (1.0 + d)

    hit1 = row == i1
    hit2 = row == i2
    onehot = (hit1 | hit2).astype(BF16)
    earlier = (lax.broadcasted_iota(jnp.int32, (tm, tm), 0) < lax.broadcasted_iota(jnp.int32, (tm, tm), 1)).astype(BF16)
    before = _dot(onehot, earlier) + run_ref[:, 0:1]
    pos1 = jnp.sum(jnp.where(hit1, before, 0.0), axis=0, keepdims=True).astype(jnp.int32)
    pos2 = jnp.sum(jnp.where(hit2, before, 0.0), axis=0, keepdims=True).astype(jnp.int32)
    run_ref[...] += jnp.sum(onehot.astype(F32), axis=1, keepdims=True)

    et_ref[...] = jnp.where(row == 0, i1, jnp.where(row == 1, i2, 0))
    post_ref[...] = jnp.where(row == 0, pos1, jnp.where(row == 1, pos2, 0))
    cnt_ref[...] = run_ref[...]
    wide = lax.broadcasted_iota(jnp.int32, (LANES, tm), 0)
    gate_ref[...] = jnp.where(wide == 0, g1, jnp.where(wide == 1, g2, 0.0)).T


def _route(x2d, w_router_t):
    n = x2d.shape[0]
    tm = ROW_TILE
    per_tok = pl.BlockSpec((N_EXPERTS, tm), lambda i: (0, i))
    return pl.pallas_call(
        _route_kernel,
        grid=(n // tm,),
        in_specs=[
            pl.BlockSpec((tm, D_MODEL), lambda i: (i, 0)),
            pl.BlockSpec((N_EXPERTS, D_MODEL), lambda i: (0, 0)),
        ],
        out_specs=[
            pl.BlockSpec((tm, LANES), lambda i: (i, 0)),
            per_tok, per_tok,
            pl.BlockSpec((N_EXPERTS, LANES), lambda i: (0, 0)),
        ],
        out_shape=[
            jax.ShapeDtypeStruct((n, LANES), F32),
            jax.ShapeDtypeStruct((N_EXPERTS, n), jnp.int32),
            jax.ShapeDtypeStruct((N_EXPERTS, n), jnp.int32),
            jax.ShapeDtypeStruct((N_EXPERTS, LANES), F32),
        ],
        scratch_shapes=[pltpu.VMEM((N_EXPERTS, LANES), F32)],
        compiler_params=_params(1, 24),
        name="moe_router",
    )(x2d, w_router_t)


def _dispatch_kernel(d0_ref, d1_ref, lo_ref, hi_ref, xt_ref, xin_hbm, zero_ref, sem_rows, sem_zero):
    step = pl.program_id(0)
    tm = xt_ref.shape[0]
    base = step * tm

    def issue(r, carry):
        pltpu.make_async_copy(xt_ref.at[r], xin_hbm.at[d0_ref[base + r]], sem_rows).start()
        pltpu.make_async_copy(xt_ref.at[r], xin_hbm.at[d1_ref[base + r]], sem_rows).start()
        return carry

    lax.fori_loop(0, tm, issue, 0, unroll=8)

    @pl.when(step == pl.num_programs(0) - 1)
    def _():
        zero_ref[...] = jnp.zeros_like(zero_ref)
        for k in range(N_EXPERTS + 1):
            def fill(r, carry):
                pltpu.make_async_copy(zero_ref, xin_hbm.at[r], sem_zero).start()
                return carry

            def drain(r, carry):
                pltpu.make_async_copy(zero_ref, xin_hbm.at[r], sem_zero).wait()
                return carry

            lax.fori_loop(lo_ref[k], hi_ref[k], fill, 0)
            lax.fori_loop(lo_ref[k], hi_ref[k], drain, 0)

    pltpu.make_async_copy(xt_ref, xin_hbm.at[pl.ds(0, tm)], sem_rows).wait()
    pltpu.make_async_copy(xt_ref, xin_hbm.at[pl.ds(0, tm)], sem_rows).wait()


def _dispatch(d0, d1, fill_lo, fill_hi, x_tiles, n_rows):
    n = x_tiles.shape[0]
    tm = ROW_TILE
    return pl.pallas_call(
        _dispatch_kernel,
        grid_spec=pltpu.PrefetchScalarGridSpec(
            num_scalar_prefetch=4,
            grid=(n // tm,),
            in_specs=[pl.BlockSpec((tm,) + ROW_TILES, lambda i, *_: (i, 0, 0))],
            out_specs=pl.BlockSpec(memory_space=pl.ANY),
            scratch_shapes=[pltpu.VMEM(ROW_TILES, F32), pltpu.SemaphoreType.DMA(()), pltpu.SemaphoreType.DMA(())],
        ),
        out_shape=jax.ShapeDtypeStruct((n_rows,) + ROW_TILES, F32),
        compiler_params=_params(1, 16),
        name="moe_dispatch",
    )(d0, d1, fill_lo, fill_hi, x_tiles)


def _combine_kernel(d0_ref, d1_ref, y_hbm, gate_ref, x_ref, g_ref, b_ref, o_ref, buf0, buf1, sem, *, alpha):
    tm = x_ref.shape[0]
    base = pl.program_id(0) * tm

    def issue(r, carry):
        pltpu.make_async_copy(y_hbm.at[d0_ref[base + r]], buf0.at[r], sem.at[0]).start()
        pltpu.make_async_copy(y_hbm.at[d1_ref[base + r]], buf1.at[r], sem.at[1]).start()
        return carry

    lax.fori_loop(0, tm, issue, 0, unroll=8)
    pltpu.make_async_copy(y_hbm.at[pl.ds(0, tm)], buf0, sem.at[0]).wait()
    pltpu.make_async_copy(y_hbm.at[pl.ds(0, tm)], buf1, sem.at[1]).wait()
    gate = gate_ref[...]
    y = _from_row_tiles(buf0) * gate[:, 0:1] + _from_row_tiles(buf1) * gate[:, 1:2]
    o_ref[...] = _layer_norm(alpha * x_ref[...] + y, g_ref[...], b_ref[...])


def _combine_ln(d0, d1, yb, gate, x2d, g, b, alpha):
    n = x2d.shape[0]
    tm = GATHER_ROWS
    row = pl.BlockSpec((tm, D_MODEL), lambda i, a, c: (i, 0))
    vec = pl.BlockSpec((1, D_MODEL), lambda i, a, c: (0, 0))
    return pl.pallas_call(
        functools.partial(_combine_kernel, alpha=alpha),
        grid_spec=pltpu.PrefetchScalarGridSpec(
            num_scalar_prefetch=2,
            grid=(n // tm,),
            in_specs=[
                pl.BlockSpec(memory_space=pl.ANY),
                pl.BlockSpec((tm, LANES), lambda i, a, c: (i, 0)),
                row, vec, vec,
            ],
            out_specs=row,
            scratch_shapes=[
                pltpu.VMEM((tm,) + ROW_TILES, F32),
                pltpu.VMEM((tm,) + ROW_TILES, F32),
                pltpu.SemaphoreType.DMA((2,)),
            ],
        ),
        out_shape=jax.ShapeDtypeStruct((n, D_MODEL), F32),
        compiler_params=_params(1, 16),
        name="moe_combine_ln",
    )(d0, d1, yb, gate, x2d, g, b)


def _attention_layer(x2d, bsz, seq, w_in, sinks, pe_k, wk1, wk2, pe_v, wv1, wv2, w_o, g, b, alpha):
    aq, akv = A_HEADS * HEAD_DIM, A_KV_HEADS * HEAD_DIM
    bq, bkv = B_HEADS * HEAD_DIM, B_KV_HEADS * HEAD_DIM
    bounds = [0]
    for width in (aq, akv, akv, bq, bkv, bkv, bkv, bkv, bkv, bkv, 3 * B_HEADS):
        bounds.append(bounds[-1] + width)
    qa, ka, va, qb, kc, vc, ks, vs, kw, vw, wg = [w_in[:, bounds[i]:bounds[i + 1]] for i in range(11)]
    w_tok = jnp.concatenate([ka, ks, kw, kc, vc], axis=1).astype(BF16)
    w_feat = jnp.concatenate([qa * SCALE, qb * SCALE, va, vs, vw], axis=1).T.astype(BF16)
    wg = wg.reshape(D_MODEL, B_HEADS, 3).transpose(2, 1, 0).reshape(3 * B_HEADS, D_MODEL)
    wg = jnp.pad(wg, ((0, GATE_ROWS - 3 * B_HEADS), (0, 0))).astype(BF16)
    tok, feat, gates_t = _inproj(x2d, w_tok, w_feat, wg, bsz, seq)

    n_str = seq // CMP_STRIDE
    kvc = tok[:, 3 * akv:]
    xs = kvc.reshape(bsz, n_str, CMP_STRIDE, 2, B_KV_HEADS, HEAD_DIM).transpose(0, 4, 3, 1, 2, 5)
    xs = xs.reshape(bsz, B_KV_HEADS, 2, n_str, CMP_STRIDE * HEAD_DIM)
    pe = jnp.stack([pe_k, pe_v]).reshape(2, 1, CMP_LEN * HEAD_DIM)
    pe = jnp.broadcast_to(pe, (2, 8, CMP_LEN * HEAD_DIM))
    kcmp, vcmp_t = _compress(xs, pe, jnp.stack([wk1, wv1]).astype(BF16), wk2.astype(BF16), wv2.T.astype(BF16))

    o = _attention(sinks, tok.reshape(bsz, seq, TOK_W), feat, kcmp, vcmp_t, gates_t, seq)
    return _proj_ln(o.reshape(bsz * seq, N_ATT_HEADS * HEAD_DIM), w_o.astype(BF16), x2d, g, b, alpha)


def _moe_layer(x2d, x_tiles, w_router, w_gate, w_up, w_down, g, b, alpha):
    n_tok = x2d.shape[0]
    gate, e_t, pos_t, cnt = _route(x2d, w_router.T)
    counts = cnt[:, 0].astype(jnp.int32)
    padded = (counts + MOE_ROWS - 1) // MOE_ROWS * MOE_ROWS
    pad_ends = jnp.cumsum(padded)
    pad_starts = pad_ends - padded
    n_blocks = n_tok * TOP_K // MOE_ROWS + N_EXPERTS
    n_rows = n_blocks * MOE_ROWS
    experts = jnp.arange(N_EXPERTS, dtype=jnp.int32)[:, None, None]
    start_of = jnp.sum(jnp.where(e_t[None, :TOP_K] == experts, pad_starts[:, None, None], 0), axis=0)
    dest = start_of + pos_t[:TOP_K]
    fill_lo = jnp.concatenate([pad_starts + counts, pad_ends[-1:]])
    fill_hi = jnp.concatenate([pad_ends, jnp.full((1,), n_rows, jnp.int32)])
    blk_start = jnp.arange(n_blocks, dtype=jnp.int32) * MOE_ROWS
    blk_e = jnp.sum((pad_ends[None, :] <= blk_start[:, None]).astype(jnp.int32), axis=1)
    blk_e = jnp.minimum(blk_e, N_EXPERTS - 1)

    xin = _dispatch(dest[0], dest[1], fill_lo, fill_hi, x_tiles, n_rows)
    yb = _experts(blk_e, xin, w_gate.astype(BF16), w_up.astype(BF16), w_down.astype(BF16))
    return _combine_ln(dest[0], dest[1], yb, gate, x2d, g, b, alpha)


def kernel(x, att_w_in, att_sinks, cmp_pe_k, cmp_wk1, cmp_wk2, cmp_pe_v, cmp_wv1, cmp_wv2, att_w_o,
           ffn_w_gate, ffn_w_up, ffn_w_down, gmlp_w_in, gmlp_ln_g, gmlp_ln_b, gmlp_w_s, gmlp_b_s, gmlp_w_out,
           moe_w_router, moe_w_gate, moe_w_up, moe_w_down, ln_g, ln_b):
    bsz, seq, dm = x.shape
    depth = ln_g.shape[0]
    alpha = (2.0 * depth) ** 0.25
    assert dm == D_MODEL and seq % ROW_TILE == 0
    x2d = x.reshape(bsz * seq, dm)

    def vec(p):
        return p.reshape(1, -1)

    for i in range(depth):
        j = i // 2
        g0, b0, g1, b1 = vec(ln_g[i, 0]), vec(ln_b[i, 0]), vec(ln_g[i, 1]), vec(ln_b[i, 1])
        if i % 2 == 0:
            x2d = _attention_layer(x2d, bsz, seq, att_w_in[j], att_sinks[j], cmp_pe_k[j], cmp_wk1[j], cmp_wk2[j],
                                   cmp_pe_v[j], cmp_wv1[j], cmp_wv2[j], att_w_o[j], g0, b0, alpha)
            x2d = _ffn_ln(x2d, ffn_w_gate[j].astype(BF16), ffn_w_up[j].astype(BF16), ffn_w_down[j].astype(BF16),
                          g1, b1, alpha)
        else:
            b_s = jnp.broadcast_to(gmlp_b_s[j][:, :, None], (GMLP_GROUPS, GMLP_CHUNK, GMLP_CHUNK))
            x2d, x_tiles = _gmlp_ln(x2d, gmlp_w_in[j].astype(BF16), vec(gmlp_ln_g[j]), vec(gmlp_ln_b[j]), gmlp_w_s[j],
                                    b_s, gmlp_w_out[j].astype(BF16), g0, b0, alpha)
            x2d = _moe_layer(x2d, x_tiles, moe_w_router[j], moe_w_gate[j], moe_w_up[j], moe_w_down[j], g1, b1, alpha)
    return x2d.reshape(bsz, seq, dm)
```

```python
import functools

import jax
import jax.numpy as jnp
import numpy as np
from jax import lax
from jax.experimental import pallas as pl
from jax.experimental.pallas import tpu as pltpu

F32 = jnp.float32
BF16 = jnp.bfloat16

D_MODEL = 1024
HEAD_DIM = 64
BLOCK_Q = 128
A_HEADS = 8
A_KV_HEADS = 2
A_WINDOW = 128
B_HEADS = 8
B_KV_HEADS = 2
GROUP = 4
CMP_LEN = 32
CMP_STRIDE = 16
CMP_HIDDEN = 256
SLC_BLOCK = 64
SLC_TOPN = 8
B_WINDOW = 256
N_ATT_HEADS = A_HEADS + B_HEADS
GMLP_GROUPS = 8
GMLP_CHUNK = 128
N_EXPERTS = 8
TOP_K = 2
LN_EPS = 1e-5
LANES = 128
ROW_TILES = (D_MODEL // LANES, LANES)

TOK_W = 640
FEAT_W = 1408
GATE_ROWS = 32
SLC_CHUNK = 256

ROW_TILE = 512
MOE_ROWS = 512
GATHER_ROWS = 256
FFN_CHUNK = 1024
EXPERT_TILE = 1792

ALIBI_SLOPES = tuple(2.0 ** (-8.0 * h / N_ATT_HEADS) for h in range(1, N_ATT_HEADS + 1))
SCALE = HEAD_DIM ** -0.5
LOG2E = 1.4426950408889634
NEG_INF = float("-inf")
RANK_FORCED = 1e30
RANK_INVALID = -1.0
MASK_BIG = 1e30
AUX_SEL = 32


def _params(n_grid, vmem_mb):
    return pltpu.CompilerParams(
        dimension_semantics=("arbitrary",) * n_grid, vmem_limit_bytes=vmem_mb * 1024 * 1024
    )


def _dot(a, b):
    return jnp.dot(a, b, preferred_element_type=F32)


def _dot_nt(a, b):
    return lax.dot_general(a, b, (((1,), (1,)), ((), ())), preferred_element_type=F32)


def _layer_norm(z, g, b):
    mu = jnp.mean(z, axis=-1, keepdims=True)
    zc = z - mu
    var = jnp.mean(zc * zc, axis=-1, keepdims=True)
    return zc * lax.rsqrt(var + LN_EPS) * g + b


def _gelu(x):
    return 0.5 * x * (1.0 + jnp.tanh(0.7978845608028654 * (x + 0.044715 * (x * x * x))))


def _silu(x):
    return x / (1.0 + jnp.exp(-x))


def _to_row_tiles(o_ref, y):
    for c in range(D_MODEL // LANES):
        o_ref[:, c, :] = y[:, c * LANES:(c + 1) * LANES]


def _from_row_tiles(ref):
    return jnp.concatenate([ref[:, c, :] for c in range(D_MODEL // LANES)], axis=1)


def _inproj_kernel(x_ref, wt_ref, wf_ref, wg_ref, tok_ref, feat_ref, gate_ref):
    x = x_ref[...].astype(BF16)
    tok_ref[...] = _dot(x, wt_ref[...]).astype(BF16)
    rows = FEAT_W // 4
    for c in range(4):
        feat_ref[0, c * rows:(c + 1) * rows, :] = _dot_nt(wf_ref[c * rows:(c + 1) * rows, :], x).astype(BF16)
    gate_ref[0] = _dot_nt(wg_ref[...], x)


def _inproj(x2d, w_tok, w_feat, w_gate, bsz, seq):
    n = x2d.shape[0]
    per_seq = seq // ROW_TILE
    return pl.pallas_call(
        _inproj_kernel,
        grid=(n // ROW_TILE,),
        in_specs=[
            pl.BlockSpec((ROW_TILE, D_MODEL), lambda i: (i, 0)),
            pl.BlockSpec((D_MODEL, TOK_W), lambda i: (0, 0)),
            pl.BlockSpec((FEAT_W, D_MODEL), lambda i: (0, 0)),
            pl.BlockSpec((GATE_ROWS, D_MODEL), lambda i: (0, 0)),
        ],
        out_specs=[
            pl.BlockSpec((ROW_TILE, TOK_W), lambda i: (i, 0)),
            pl.BlockSpec((1, FEAT_W, ROW_TILE), lambda i: (i // per_seq, 0, i % per_seq)),
            pl.BlockSpec((1, GATE_ROWS, ROW_TILE), lambda i: (i // per_seq, 0, i % per_seq)),
        ],
        out_shape=[
            jax.ShapeDtypeStruct((n, TOK_W), BF16),
            jax.ShapeDtypeStruct((bsz, FEAT_W, seq), BF16),
            jax.ShapeDtypeStruct((bsz, GATE_ROWS, seq), F32),
        ],
        compiler_params=_params(1, 32),
        name="inproj",
    )(x2d, w_tok, w_feat, w_gate)


def _compress_hidden(x, pe, w1):
    half = CMP_STRIDE * HEAD_DIM
    a = _dot(x, w1[:half])
    b = _dot(x, w1[half:])
    b_next = pltpu.roll(b, shift=x.shape[0] - 1, axis=0)
    bias = _dot(pe.astype(BF16), w1)[0:1]
    return _gelu(a + b_next + bias).astype(BF16)


def _compress_kernel(x_ref, pe_ref, w1_ref, w2k_ref, w2vt_ref, kc_ref, vct_ref):
    hk = _compress_hidden(x_ref[0, 0, 0], pe_ref[0], w1_ref[0])
    kc_ref[0, 0] = _dot(hk, w2k_ref[...]).astype(BF16)
    hv = _compress_hidden(x_ref[0, 0, 1], pe_ref[1], w1_ref[1])
    vct_ref[0, 0] = _dot_nt(w2vt_ref[...], hv).astype(BF16)


def _compress(xs, pe, w1, w2k, w2vt):
    bsz, _, _, n_str, width = xs.shape

    def full(shape):
        return pl.BlockSpec(shape, lambda b, h: (0,) * len(shape))

    return pl.pallas_call(
        _compress_kernel,
        grid=(bsz, B_KV_HEADS),
        in_specs=[
            pl.BlockSpec((1, 1, 2, n_str, width), lambda b, h: (b, h, 0, 0, 0)),
            full((2, 8, 2 * width)),
            full((2, 2 * width, CMP_HIDDEN)),
            full((CMP_HIDDEN, HEAD_DIM)),
            full((HEAD_DIM, CMP_HIDDEN)),
        ],
        out_specs=[
            pl.BlockSpec((1, 1, n_str, HEAD_DIM), lambda b, h: (b, h, 0, 0)),
            pl.BlockSpec((1, 1, HEAD_DIM, n_str), lambda b, h: (b, h, 0, 0)),
        ],
        out_shape=[
            jax.ShapeDtypeStruct((bsz, B_KV_HEADS, n_str, HEAD_DIM), BF16),
            jax.ShapeDtypeStruct((bsz, B_KV_HEADS, HEAD_DIM, n_str), BF16),
        ],
        compiler_params=_params(2, 24),
        name="nsa_compress",
    )(xs, pe, w1, w2k, w2vt)


def _bf16_parts(x, parts=3):
    out, rest = [], np.asarray(x, np.float32)
    for _ in range(parts):
        piece = rest.astype(jnp.bfloat16).astype(np.float32)
        out.append(piece)
        rest = rest - piece
    return out


def _aux_key_lanes(seq):
    t = np.arange(seq)
    aux = np.zeros((seq, LANES), np.float32)
    aux[t, t // SLC_BLOCK] = 1.0
    aux[:, AUX_SEL:AUX_SEL + 3] = (t >> 7)[:, None]
    aux[:, AUX_SEL + 3:AUX_SEL + 6] = (t & 127)[:, None]
    return jnp.asarray(aux, BF16)


def _aux_slope_rows():
    rows = np.zeros((N_ATT_HEADS // GROUP, LANES - AUX_SEL, GROUP * BLOCK_Q), np.float32)
    for grp in range(N_ATT_HEADS // GROUP):
        for g in range(GROUP):
            slope = ALIBI_SLOPES[grp * GROUP + g] * LOG2E
            cols = slice(g * BLOCK_Q, (g + 1) * BLOCK_Q)
            for i, piece in enumerate(_bf16_parts(128.0 * slope) + _bf16_parts(slope)):
                rows[grp, i, cols] = piece
    return jnp.asarray(rows, BF16)


def _attn_kernel(sink_ref, qt_ref, ka_ref, ks_ref, kw_ref, vat_ref, vst_ref, vwt_ref, kc_ref, vct_ref, gt_ref,
                 aux_ref, slope_ref, o_ref, ot_ref, mask_a_ref, mask_w_ref, *, seq):
    n = pl.program_id(1)
    t0 = n * BLOCK_Q
    n_str = seq // CMP_STRIDE
    n_cmp = n_str - CMP_LEN // CMP_STRIDE + 1
    n_sel = seq // SLC_BLOCK
    top_n = min(SLC_TOPN, n_sel)
    hd = HEAD_DIM
    gq = GROUP * BLOCK_Q
    ch = SLC_CHUNK
    span_a = A_WINDOW + BLOCK_Q
    span_w = B_WINDOW + BLOCK_Q

    lane = lax.broadcasted_iota(jnp.int32, (1, gq), 1)
    q_loc = lane & (BLOCK_Q - 1)
    lane_head = lane >> (BLOCK_Q.bit_length() - 1)
    t_q = t0 + q_loc

    def head_row(vals):
        return jnp.where(lane_head == 0, vals[0], jnp.where(lane_head == 1, vals[1],
                                                           jnp.where(lane_head == 2, vals[2], vals[3])))

    def q_group(first_head):
        return jnp.concatenate(
            [qt_ref[0, (first_head + g) * hd:(first_head + g + 1) * hd, :] for g in range(GROUP)], axis=1)

    def score_rhs(grp, kvh, qg, sel_rows):
        zero = jnp.zeros((hd, gq), BF16)
        q_rows = [qg, zero] if kvh == 0 else [zero, qg]
        return jnp.concatenate(q_rows + [sel_rows, slope_ref[grp]], axis=0)

    def score_lhs(k_ref, start, span):
        return jnp.concatenate([k_ref[0, pl.ds(start, span), :], aux_ref[pl.ds(start, span), :]], axis=1)

    def band_start(span):
        return pl.multiple_of(jnp.maximum(t0 - (span - BLOCK_Q), 0), BLOCK_Q)

    def band_mask(span, window):
        ik = lax.broadcasted_iota(jnp.int32, (span, gq), 0)
        dist = (t0 - band_start(span)) + q_loc - ik
        return jnp.where((dist >= 0) & (dist < window), 0.0, -MASK_BIG)

    @pl.when(n <= span_w // BLOCK_Q - 1)
    def _():
        mask_a_ref[...] = band_mask(span_a, A_WINDOW)
        mask_w_ref[...] = band_mask(span_w, B_WINDOW)

    no_sel = jnp.zeros((AUX_SEL, gq), BF16)
    n_grp = N_ATT_HEADS // GROUP
    slopes = [head_row([ALIBI_SLOPES[grp * GROUP + g] for g in range(GROUP)]) for grp in range(n_grp)]
    qgs = [q_group(grp * GROUP) for grp in range(n_grp)]

    def softmax_cols(s, sinks=None):
        m = jnp.max(s, axis=0, keepdims=True)
        if sinks is not None:
            m = jnp.maximum(m, sinks)
        e = jnp.exp2(s - m)
        den = jnp.sum(e, axis=0, keepdims=True)
        if sinks is not None:
            den = den + jnp.exp2(sinks - m)
        return e.astype(BF16), den

    ks_a = band_start(span_a)
    ks_w = band_start(span_w)
    lhs_a = score_lhs(ka_ref, ks_a, span_a)
    lhs_w = score_lhs(kw_ref, ks_w, span_w)
    s_a = [_dot(lhs_a, score_rhs(kvh, kvh, qgs[kvh], no_sel)) + mask_a_ref[...] for kvh in range(A_KV_HEADS)]
    s_w = [_dot(lhs_w, score_rhs(A_KV_HEADS + kvh, kvh, qgs[A_KV_HEADS + kvh], no_sel)) + mask_w_ref[...]
           for kvh in range(B_KV_HEADS)]

    c_idx = lax.broadcasted_iota(jnp.int32, (n_str, gq), 0)
    dist_c = t_q - (c_idx * CMP_STRIDE + CMP_LEN - 1)
    valid_c = (dist_c >= 0) & (c_idx < n_cmp)
    dist_cf = dist_c.astype(F32)
    s_c = [jnp.where(valid_c, _dot(kc_ref[0, kvh], qgs[A_KV_HEADS + kvh])
                     - (slopes[A_KV_HEADS + kvh] * LOG2E) * dist_cf, NEG_INF) for kvh in range(B_KV_HEADS)]

    oj = lax.broadcasted_iota(jnp.int32, (n_sel, n_str), 0) * SLC_BLOCK
    oc = lax.broadcasted_iota(jnp.int32, (n_sel, n_str), 1) * CMP_STRIDE
    overlap_t = ((oc < oj + SLC_BLOCK) & (oc + CMP_LEN > oj)).astype(BF16)
    j_idx = lax.broadcasted_iota(jnp.int32, (n_sel, BLOCK_Q), 0)
    t_blk = (t0 + lax.broadcasted_iota(jnp.int32, (1, BLOCK_Q), 1)) >> (SLC_BLOCK.bit_length() - 1)
    valid_j = j_idx <= t_blk
    forced_j = (j_idx == 0) | (j_idx == t_blk) | (j_idx == t_blk - 1)
    e_c, den_c, rhs_s = [], [], []
    for kvh in range(B_KV_HEADS):
        m = jnp.max(s_c[kvh], axis=0, keepdims=True)
        m = jnp.where(m == NEG_INF, 0.0, m)
        e = jnp.exp2(s_c[kvh] - m)
        den = jnp.sum(e, axis=0, keepdims=True)
        den = jnp.where(den > 0.0, den, 1.0)
        e_c.append(e.astype(BF16))
        den_c.append(den)
        p = e / den
        psum = p[:, 0:BLOCK_Q]
        for g in range(1, GROUP):
            psum = psum + p[:, g * BLOCK_Q:(g + 1) * BLOCK_Q]
        p_hi = psum.astype(BF16)
        p_lo = (psum - p_hi.astype(F32)).astype(BF16)
        imp = _dot(overlap_t, p_hi) + _dot(overlap_t, p_lo)
        rank = jnp.where(forced_j, RANK_FORCED, jnp.where(valid_j, imp, RANK_INVALID))
        cnt = jnp.zeros((n_sel, BLOCK_Q), jnp.int32)
        for jp in range(n_sel):
            row = rank[jp:jp + 1, :]
            before = (row > rank) | ((row == rank) & (j_idx > jp))
            cnt = cnt + before.astype(jnp.int32)
        sel_bias = jnp.where(cnt < top_n, 0.0, -MASK_BIG)
        if n_sel < AUX_SEL:
            sel_bias = jnp.concatenate([sel_bias, jnp.zeros((AUX_SEL - n_sel, BLOCK_Q), F32)], axis=0)
        sel_rows = jnp.concatenate([sel_bias.astype(BF16)] * GROUP, axis=1)
        rhs_s.append(score_rhs(A_KV_HEADS + kvh, kvh, qgs[A_KV_HEADS + kvh], sel_rows))

    t_qf = t_q.astype(F32)
    sinks = [(head_row([sink_ref[kvh * GROUP + g] for g in range(GROUP)]) + slopes[kvh] * t_qf) * LOG2E
             for kvh in range(A_KV_HEADS)]
    ed_a = [softmax_cols(s_a[kvh], sinks[kvh]) for kvh in range(A_KV_HEADS)]
    ed_w = [softmax_cols(s_w[kvh]) for kvh in range(B_KV_HEADS)]

    def rows(kvh):
        return slice(kvh * hd, (kvh + 1) * hd)

    o_a = [_dot(vat_ref[0, rows(kvh), pl.ds(ks_a, span_a)], ed_a[kvh][0]) / ed_a[kvh][1] for kvh in range(A_KV_HEADS)]
    o_w = [_dot(vwt_ref[0, rows(kvh), pl.ds(ks_w, span_w)], ed_w[kvh][0]) / ed_w[kvh][1] for kvh in range(B_KV_HEADS)]
    o_c = [_dot(vct_ref[0, kvh], e_c[kvh]) / den_c[kvh] for kvh in range(B_KV_HEADS)]
    for kvh in range(A_KV_HEADS):
        for g in range(GROUP):
            h = kvh * GROUP + g
            ot_ref[h * hd:(h + 1) * hd, :] = o_a[kvh][:, g * BLOCK_Q:(g + 1) * BLOCK_Q]

    dloc = q_loc - lax.broadcasted_iota(jnp.int32, (ch, gq), 0)
    n_full = t0 // ch

    def slc_scores(c):
        lhs = score_lhs(ks_ref, pl.multiple_of(c * ch, ch), ch)
        return tuple(_dot(lhs, rhs_s[kvh]) for kvh in range(B_KV_HEADS))

    def slc_update(c, state, s, kvh):
        m, l, acc = state
        m_new = jnp.maximum(m, jnp.max(s, axis=0, keepdims=True))
        alpha = jnp.exp2(m - m_new)
        e = jnp.exp2(s - m_new)
        l_new = alpha * l + jnp.sum(e, axis=0, keepdims=True)
        vt = vst_ref[0, rows(kvh), pl.ds(pl.multiple_of(c * ch, ch), ch)]
        return m_new, l_new, alpha * acc + _dot(vt, e.astype(BF16))

    def slc_step(c, carry):
        states, s = carry
        s_next = slc_scores(c + 1)
        return tuple(slc_update(c, states[kvh], s[kvh], kvh) for kvh in range(B_KV_HEADS)), s_next

    init = (jnp.full((1, gq), NEG_INF, F32), jnp.zeros((1, gq), F32), jnp.zeros((hd, gq), F32))
    states, s = lax.fori_loop(0, n_full, slc_step, ((init,) * B_KV_HEADS, slc_scores(0)))
    causal = dloc >= n_full * ch - t0
    gt = jax.nn.sigmoid(gt_ref[0])
    for kvh in range(B_KV_HEADS):
        _, l_fin, acc_fin = slc_update(n_full, states[kvh], jnp.where(causal, s[kvh], -MASK_BIG), kvh)
        o_slc = acc_fin / l_fin
        for g in range(GROUP):
            hb = kvh * GROUP + g
            h = A_HEADS + hb
            cols = slice(g * BLOCK_Q, (g + 1) * BLOCK_Q)
            ot_ref[h * hd:(h + 1) * hd, :] = (
                gt[hb:hb + 1] * o_c[kvh][:, cols]
                + gt[B_HEADS + hb:B_HEADS + hb + 1] * o_slc[:, cols]
                + gt[2 * B_HEADS + hb:2 * B_HEADS + hb + 1] * o_w[kvh][:, cols]
            )

    o_ref[0] = ot_ref[...].T.astype(BF16)


def _attention(sinks, tok, feat, kc, vct, gates_t, seq):
    bsz = tok.shape[0]
    n_str = seq // CMP_STRIDE
    kv_w = A_KV_HEADS * HEAD_DIM
    q_rows = N_ATT_HEADS * HEAD_DIM
    gq = GROUP * BLOCK_Q
    assert seq // SLC_BLOCK <= AUX_SEL and seq <= 128 * 256

    def k_spec(col_block):
        return pl.BlockSpec((1, seq, kv_w), lambda b, n: (b, 0, col_block))

    def vt_spec(row_block):
        return pl.BlockSpec((1, kv_w, seq), lambda b, n: (b, q_rows // kv_w + row_block, 0))

    return pl.pallas_call(
        functools.partial(_attn_kernel, seq=seq),
        grid=(bsz, seq // BLOCK_Q),
        in_specs=[
            pl.BlockSpec(memory_space=pltpu.SMEM),
            pl.BlockSpec((1, q_rows, BLOCK_Q), lambda b, n: (b, 0, n)),
            k_spec(0), k_spec(1), k_spec(2),
            vt_spec(0), vt_spec(1), vt_spec(2),
            pl.BlockSpec((1, B_KV_HEADS, n_str, HEAD_DIM), lambda b, n: (b, 0, 0, 0)),
            pl.BlockSpec((1, B_KV_HEADS, HEAD_DIM, n_str), lambda b, n: (b, 0, 0, 0)),
            pl.BlockSpec((1, GATE_ROWS, BLOCK_Q), lambda b, n: (b, 0, n)),
            pl.BlockSpec((seq, LANES), lambda b, n: (0, 0)),
            pl.BlockSpec((N_ATT_HEADS // GROUP, LANES - AUX_SEL, gq), lambda b, n: (0, 0, 0)),
        ],
        out_specs=pl.BlockSpec((1, BLOCK_Q, q_rows), lambda b, n: (b, n, 0)),
        out_shape=jax.ShapeDtypeStruct((bsz, seq, q_rows), BF16),
        scratch_shapes=[
            pltpu.VMEM((q_rows, BLOCK_Q), F32),
            pltpu.VMEM((A_WINDOW + BLOCK_Q, gq), F32),
            pltpu.VMEM((B_WINDOW + BLOCK_Q, gq), F32),
        ],
        compiler_params=_params(2, 32),
        name="hybrid_attention",
    )(sinks, feat, tok, tok, tok, feat, feat, feat, kc, vct, gates_t, _aux_key_lanes(seq), _aux_slope_rows())


def _proj_ln_kernel(o_ref, w_ref, x_ref, g_ref, b_ref, y_ref, *, alpha):
    y = _dot(o_ref[...], w_ref[...])
    y_ref[...] = _layer_norm(alpha * x_ref[...] + y, g_ref[...], b_ref[...])


def _proj_ln(o2d, w, x2d, g, b, alpha):
    n, k = o2d.shape
    row = pl.BlockSpec((ROW_TILE, D_MODEL), lambda i: (i, 0))
    vec = pl.BlockSpec((1, D_MODEL), lambda i: (0, 0))
    return pl.pallas_call(
        functools.partial(_proj_ln_kernel, alpha=alpha),
        grid=(n // ROW_TILE,),
        in_specs=[
            pl.BlockSpec((ROW_TILE, k), lambda i: (i, 0)),
            pl.BlockSpec((k, D_MODEL), lambda i: (0, 0)),
            row, vec, vec,
        ],
        out_specs=row,
        out_shape=jax.ShapeDtypeStruct((n, D_MODEL), F32),
        compiler_params=_params(1, 32),
        name="out_proj_ln",
    )(o2d, w, x2d, g, b)


def _swiglu_acc(x, wg, wu, wd, acc_ref, f):
    h = (_silu(_dot(x, wg)) * _dot(x, wu)).astype(BF16)
    y = _dot(h, wd)

    @pl.when(f == 0)
    def _():
        acc_ref[...] = y

    @pl.when(f != 0)
    def _():
        acc_ref[...] += y


def _ffn_ln_kernel(x_ref, wg_ref, wu_ref, wd_ref, g_ref, b_ref, y_ref, *, alpha):
    x = x_ref[...]
    xb = x.astype(BF16)
    d_ff = wg_ref.shape[1]
    y = None
    for lo in range(0, d_ff, FFN_CHUNK):
        hi = min(lo + FFN_CHUNK, d_ff)
        h = (_silu(_dot(xb, wg_ref[:, lo:hi])) * _dot(xb, wu_ref[:, lo:hi])).astype(BF16)
        part = _dot(h, wd_ref[lo:hi, :])
        y = part if y is None else y + part
    y_ref[...] = _layer_norm(alpha * x + y, g_ref[...], b_ref[...])


def _ffn_ln(x2d, wg, wu, wd, g, b, alpha):
    n = x2d.shape[0]
    d_ff = wg.shape[1]
    row = pl.BlockSpec((ROW_TILE, D_MODEL), lambda i: (i, 0))
    vec = pl.BlockSpec((1, D_MODEL), lambda i: (0, 0))
    once = pl.Buffered(1)
    return pl.pallas_call(
        functools.partial(_ffn_ln_kernel, alpha=alpha),
        grid=(n // ROW_TILE,),
        in_specs=[
            row,
            pl.BlockSpec((D_MODEL, d_ff), lambda i: (0, 0), pipeline_mode=once),
            pl.BlockSpec((D_MODEL, d_ff), lambda i: (0, 0), pipeline_mode=once),
            pl.BlockSpec((d_ff, D_MODEL), lambda i: (0, 0), pipeline_mode=once),
            vec, vec,
        ],
        out_specs=row,
        out_shape=jax.ShapeDtypeStruct((n, D_MODEL), F32),
        compiler_params=_params(1, 48),
        name="swiglu_ln",
    )(x2d, wg, wu, wd, g, b)


def _expert_kernel(be_ref, x_ref, wg_ref, wu_ref, wd_ref, y_ref, xb_ref):
    del be_ref
    f = pl.program_id(1)

    @pl.when(f == 0)
    def _():
        xb_ref[...] = _from_row_tiles(x_ref).astype(BF16)

    _swiglu_acc(xb_ref[...], wg_ref[0], wu_ref[0], wd_ref[0], y_ref, f)


def _experts(blk_e, xin, wg, wu, wd):
    n_rows = blk_e.shape[0] * MOE_ROWS
    d_ff = wg.shape[2]
    return pl.pallas_call(
        _expert_kernel,
        grid_spec=pltpu.PrefetchScalarGridSpec(
            num_scalar_prefetch=1,
            grid=(n_rows // MOE_ROWS, d_ff // EXPERT_TILE),
            in_specs=[
                pl.BlockSpec((MOE_ROWS,) + ROW_TILES, lambda i, f, be: (i, 0, 0)),
                pl.BlockSpec((1, D_MODEL, EXPERT_TILE), lambda i, f, be: (be[i], 0, f)),
                pl.BlockSpec((1, D_MODEL, EXPERT_TILE), lambda i, f, be: (be[i], 0, f)),
                pl.BlockSpec((1, EXPERT_TILE, D_MODEL), lambda i, f, be: (be[i], f, 0)),
            ],
            out_specs=pl.BlockSpec((MOE_ROWS, D_MODEL), lambda i, f, be: (i, 0)),
            scratch_shapes=[pltpu.VMEM((MOE_ROWS, D_MODEL), BF16)],
        ),
        out_shape=jax.ShapeDtypeStruct((n_rows, D_MODEL), F32),
        compiler_params=_params(2, 52),
        name="expert_swiglu",
    )(blk_e, xin, wg, wu, wd)


def _gmlp_kernel(x_ref, win_ref, lng_ref, lnb_ref, ws_ref, bs_ref, wout_ref, g_ref, b_ref, y_ref, yt_ref,
                 u_ref, vn_ref, gated_ref, *, alpha):
    x = x_ref[...]
    xb = x.astype(BF16)
    u_ref[...] = _gelu(_dot(xb, win_ref[:, :D_MODEL]))
    v = _gelu(_dot(xb, win_ref[:, D_MODEL:]))
    vn_ref[...] = _layer_norm(v, lng_ref[...], lnb_ref[...]).astype(BF16)
    cs = GMLP_CHUNK
    lower = lax.broadcasted_iota(jnp.int32, (cs, cs), 0) >= lax.broadcasted_iota(jnp.int32, (cs, cs), 1)
    for grp in range(GMLP_GROUPS):
        w = jnp.where(lower, ws_ref[grp], 0.0).astype(BF16)
        bias = bs_ref[grp]
        cols = slice(grp * cs, (grp + 1) * cs)
        for c in range(x.shape[0] // cs):
            rows = slice(c * cs, (c + 1) * cs)
            mixed = _dot(w, vn_ref[rows, cols]) + bias
            gated_ref[rows, cols] = (u_ref[rows, cols] * mixed).astype(BF16)
    y = _dot(gated_ref[...], wout_ref[...])
    out = _layer_norm(alpha * x + y, g_ref[...], b_ref[...])
    y_ref[...] = out
    _to_row_tiles(yt_ref, out)


def _gmlp_ln(x2d, w_in, ln_g, ln_b, w_s, b_s, w_out, g, b, alpha):
    n = x2d.shape[0]
    row = pl.BlockSpec((ROW_TILE, D_MODEL), lambda i: (i, 0))
    vec = pl.BlockSpec((1, D_MODEL), lambda i: (0, 0))
    grp = pl.BlockSpec((GMLP_GROUPS, GMLP_CHUNK, GMLP_CHUNK), lambda i: (0, 0, 0))
    return pl.pallas_call(
        functools.partial(_gmlp_kernel, alpha=alpha),
        grid=(n // ROW_TILE,),
        in_specs=[
            row,
            pl.BlockSpec((D_MODEL, 2 * D_MODEL), lambda i: (0, 0)),
            vec, vec, grp, grp,
            pl.BlockSpec((D_MODEL, D_MODEL), lambda i: (0, 0)),
            vec, vec,
        ],
        out_specs=[row, pl.BlockSpec((ROW_TILE,) + ROW_TILES, lambda i: (i, 0, 0))],
        out_shape=[
            jax.ShapeDtypeStruct((n, D_MODEL), F32),
            jax.ShapeDtypeStruct((n,) + ROW_TILES, F32),
        ],
        scratch_shapes=[
            pltpu.VMEM((ROW_TILE, D_MODEL), F32),
            pltpu.VMEM((ROW_TILE, D_MODEL), BF16),
            pltpu.VMEM((ROW_TILE, D_MODEL), BF16),
        ],
        compiler_params=_params(1, 40),
        name="gmlp_ln",
    )(x2d, w_in, ln_g, ln_b, w_s, b_s, w_out, g, b)


def _route_kernel(x_ref, w_ref, gate_ref, et_ref, post_ref, cnt_ref, run_ref):
    tm = x_ref.shape[0]

    @pl.when(pl.program_id(0) == 0)
    def _():
        run_ref[...] = jnp.zeros_like(run_ref)

    logits = lax.dot_general(w_ref[...], x_ref[...], (((1,), (1,)), ((), ())),
                             preferred_element_type=F32, precision=lax.Precision.HIGHEST)
    row = lax.broadcasted_iota(jnp.int32, (N_EXPERTS, tm), 0)
    v1 = jnp.max(logits, axis=0, keepdims=True)
    i1 = jnp.min(jnp.where(logits == v1, row, N_EXPERTS), axis=0, keepdims=True)
    rest = jnp.where(row == i1, NEG_INF, logits)
    v2 = jnp.max(rest, axis=0, keepdims=True)
    i2 = jnp.min(jnp.where(rest == v2, row, N_EXPERTS), axis=0, keepdims=True)
    d = jnp.exp(v2 - v1)
    g1 = 1.0 / (1.0 + d)
    g2 = d / (1.0 + d)

    hit1 = row == i1
    hit2 = row == i2
    onehot = (hit1 | hit2).astype(BF16)
    earlier = (lax.broadcasted_iota(jnp.int32, (tm, tm), 0) < lax.broadcasted_iota(jnp.int32, (tm, tm), 1)).astype(BF16)
    before = _dot(onehot, earlier) + run_ref[:, 0:1]
    pos1 = jnp.sum(jnp.where(hit1, before, 0.0), axis=0, keepdims=True).astype(jnp.int32)
    pos2 = jnp.sum(jnp.where(hit2, before, 0.0), axis=0, keepdims=True).astype(jnp.int32)
    run_ref[...] += jnp.sum(onehot.astype(F32), axis=1, keepdims=True)

    et_ref[...] = jnp.where(row == 0, i1, jnp.where(row == 1, i2, 0))
    post_ref[...] = jnp.where(row == 0, pos1, jnp.where(row == 1, pos2, 0))
    cnt_ref[...] = run_ref[...]
    wide = lax.broadcasted_iota(jnp.int32, (LANES, tm), 0)
    gate_ref[...] = jnp.where(wide == 0, g1, jnp.where(wide == 1, g2, 0.0)).T


def _route(x2d, w_router_t):
    n = x2d.shape[0]
    tm = ROW_TILE
    per_tok = pl.BlockSpec((N_EXPERTS, tm), lambda i: (0, i))
    return pl.pallas_call(
        _route_kernel,
        grid=(n // tm,),
        in_specs=[
            pl.BlockSpec((tm, D_MODEL), lambda i: (i, 0)),
            pl.BlockSpec((N_EXPERTS, D_MODEL), lambda i: (0, 0)),
        ],
        out_specs=[
            pl.BlockSpec((tm, LANES), lambda i: (i, 0)),
            per_tok, per_tok,
            pl.BlockSpec((N_EXPERTS, LANES), lambda i: (0, 0)),
        ],
        out_shape=[
            jax.ShapeDtypeStruct((n, LANES), F32),
            jax.ShapeDtypeStruct((N_EXPERTS, n), jnp.int32),
            jax.ShapeDtypeStruct((N_EXPERTS, n), jnp.int32),
            jax.ShapeDtypeStruct((N_EXPERTS, LANES), F32),
        ],
        scratch_shapes=[pltpu.VMEM((N_EXPERTS, LANES), F32)],
        compiler_params=_params(1, 24),
        name="moe_router",
    )(x2d, w_router_t)


def _dispatch_kernel(d0_ref, d1_ref, lo_ref, hi_ref, xt_ref, xin_hbm, zero_ref, sem_rows, sem_zero):
    step = pl.program_id(0)
    tm = xt_ref.shape[0]
    base = step * tm

    def issue(r, carry):
        pltpu.make_async_copy(xt_ref.at[r], xin_hbm.at[d0_ref[base + r]], sem_rows).start()
        pltpu.make_async_copy(xt_ref.at[r], xin_hbm.at[d1_ref[base + r]], sem_rows).start()
        return carry

    lax.fori_loop(0, tm, issue, 0, unroll=8)

    @pl.when(step == pl.num_programs(0) - 1)
    def _():
        zero_ref[...] = jnp.zeros_like(zero_ref)
        for k in range(N_EXPERTS + 1):
            def fill(r, carry):
                pltpu.make_async_copy(zero_ref, xin_hbm.at[r], sem_zero).start()
                return carry

            def drain(r, carry):
                pltpu.make_async_copy(zero_ref, xin_hbm.at[r], sem_zero).wait()
                return carry

            lax.fori_loop(lo_ref[k], hi_ref[k], fill, 0)
            lax.fori_loop(lo_ref[k], hi_ref[k], drain, 0)

    pltpu.make_async_copy(xt_ref, xin_hbm.at[pl.ds(0, tm)], sem_rows).wait()
    pltpu.make_async_copy(xt_ref, xin_hbm.at[pl.ds(0, tm)], sem_rows).wait()


def _dispatch(d0, d1, fill_lo, fill_hi, x_tiles, n_rows):
    n = x_tiles.shape[0]
    tm = ROW_TILE
    return pl.pallas_call(
        _dispatch_kernel,
        grid_spec=pltpu.PrefetchScalarGridSpec(
            num_scalar_prefetch=4,
            grid=(n // tm,),
            in_specs=[pl.BlockSpec((tm,) + ROW_TILES, lambda i, *_: (i, 0, 0))],
            out_specs=pl.BlockSpec(memory_space=pl.ANY),
            scratch_shapes=[pltpu.VMEM(ROW_TILES, F32), pltpu.SemaphoreType.DMA(()), pltpu.SemaphoreType.DMA(())],
        ),
        out_shape=jax.ShapeDtypeStruct((n_rows,) + ROW_TILES, F32),
        compiler_params=_params(1, 16),
        name="moe_dispatch",
    )(d0, d1, fill_lo, fill_hi, x_tiles)


def _combine_kernel(d0_ref, d1_ref, y_hbm, gate_ref, x_ref, g_ref, b_ref, o_ref, buf0, buf1, sem, *, alpha):
    tm = x_ref.shape[0]
    base = pl.program_id(0) * tm

    def issue(r, carry):
        pltpu.make_async_copy(y_hbm.at[pl.ds(d0_ref[base + r], 1)], buf0.at[pl.ds(r, 1)], sem.at[0]).start()
        pltpu.make_async_copy(y_hbm.at[pl.ds(d1_ref[base + r], 1)], buf1.at[pl.ds(r, 1)], sem.at[1]).start()
        return carry

    lax.fori_loop(0, tm, issue, 0, unroll=8)
    pltpu.make_async_copy(y_hbm.at[pl.ds(0, tm)], buf0, sem.at[0]).wait()
    pltpu.make_async_copy(y_hbm.at[pl.ds(0, tm)], buf1, sem.at[1]).wait()
    gate = gate_ref[...]
    y = buf0[...] * gate[:, 0:1] + buf1[...] * gate[:, 1:2]
    o_ref[...] = _layer_norm(alpha * x_ref[...] + y, g_ref[...], b_ref[...])


def _combine_ln(d0, d1, yb, gate, x2d, g, b, alpha):
    n = x2d.shape[0]
    tm = GATHER_ROWS
    row = pl.BlockSpec((tm, D_MODEL), lambda i, a, c: (i, 0))
    vec = pl.BlockSpec((1, D_MODEL), lambda i, a, c: (0, 0))
    return pl.pallas_call(
        functools.partial(_combine_kernel, alpha=alpha),
        grid_spec=pltpu.PrefetchScalarGridSpec(
            num_scalar_prefetch=2,
            grid=(n // tm,),
            in_specs=[
                pl.BlockSpec(memory_space=pl.ANY),
                pl.BlockSpec((tm, LANES), lambda i, a, c: (i, 0)),
                row, vec, vec,
            ],
            out_specs=row,
            scratch_shapes=[
                pltpu.VMEM((tm, D_MODEL), F32),
                pltpu.VMEM((tm, D_MODEL), F32),
                pltpu.SemaphoreType.DMA((2,)),
            ],
        ),
        out_shape=jax.ShapeDtypeStruct((n, D_MODEL), F32),
        compiler_params=_params(1, 16),
        name="moe_combine_ln",
    )(d0, d1, yb, gate, x2d, g, b)


def _attention_layer(x2d, bsz, seq, w_in, sinks, pe_k, wk1, wk2, pe_v, wv1, wv2, w_o, g, b, alpha):
    aq, akv = A_HEADS * HEAD_DIM, A_KV_HEADS * HEAD_DIM
    bq, bkv = B_HEADS * HEAD_DIM, B_KV_HEADS * HEAD_DIM
    bounds = [0]
    for width in (aq, akv, akv, bq, bkv, bkv, bkv, bkv, bkv, bkv, 3 * B_HEADS):
        bounds.append(bounds[-1] + width)
    qa, ka, va, qb, kc, vc, ks, vs, kw, vw, wg = [w_in[:, bounds[i]:bounds[i + 1]] for i in range(11)]
    w_tok = jnp.concatenate([ka, ks, kw, kc, vc], axis=1).astype(BF16)
    w_feat = jnp.concatenate([qa * (SCALE * LOG2E), qb * (SCALE * LOG2E), va, vs, vw], axis=1).T.astype(BF16)
    wg = wg.reshape(D_MODEL, B_HEADS, 3).transpose(2, 1, 0).reshape(3 * B_HEADS, D_MODEL)
    wg = jnp.pad(wg, ((0, GATE_ROWS - 3 * B_HEADS), (0, 0))).astype(BF16)
    tok, feat, gates_t = _inproj(x2d, w_tok, w_feat, wg, bsz, seq)

    n_str = seq // CMP_STRIDE
    kvc = tok[:, 3 * akv:]
    xs = kvc.reshape(bsz, n_str, CMP_STRIDE, 2, B_KV_HEADS, HEAD_DIM).transpose(0, 4, 3, 1, 2, 5)
    xs = xs.reshape(bsz, B_KV_HEADS, 2, n_str, CMP_STRIDE * HEAD_DIM)
    pe = jnp.stack([pe_k, pe_v]).reshape(2, 1, CMP_LEN * HEAD_DIM)
    pe = jnp.broadcast_to(pe, (2, 8, CMP_LEN * HEAD_DIM))
    kcmp, vcmp_t = _compress(xs, pe, jnp.stack([wk1, wv1]).astype(BF16), wk2.astype(BF16), wv2.T.astype(BF16))

    o = _attention(sinks, tok.reshape(bsz, seq, TOK_W), feat, kcmp, vcmp_t, gates_t, seq)
    return _proj_ln(o.reshape(bsz * seq, N_ATT_HEADS * HEAD_DIM), w_o.astype(BF16), x2d, g, b, alpha)


def _moe_layer(x2d, x_tiles, w_router, w_gate, w_up, w_down, g, b, alpha):
    n_tok = x2d.shape[0]
    gate, e_t, pos_t, cnt = _route(x2d, w_router.T)
    counts = cnt[:, 0].astype(jnp.int32)
    padded = (counts + MOE_ROWS - 1) // MOE_ROWS * MOE_ROWS
    pad_ends = jnp.cumsum(padded)
    pad_starts = pad_ends - padded
    n_blocks = n_tok * TOP_K // MOE_ROWS + N_EXPERTS
    n_rows = n_blocks * MOE_ROWS
    experts = jnp.arange(N_EXPERTS, dtype=jnp.int32)[:, None, None]
    start_of = jnp.sum(jnp.where(e_t[None, :TOP_K] == experts, pad_starts[:, None, None], 0), axis=0)
    dest = start_of + pos_t[:TOP_K]
    fill_lo = jnp.concatenate([pad_starts + counts, pad_ends[-1:]])
    fill_hi = jnp.concatenate([pad_ends, jnp.full((1,), n_rows, jnp.int32)])
    blk_start = jnp.arange(n_blocks, dtype=jnp.int32) * MOE_ROWS
    blk_e = jnp.sum((pad_ends[None, :] <= blk_start[:, None]).astype(jnp.int32), axis=1)
    blk_e = jnp.minimum(blk_e, N_EXPERTS - 1)

    xin = _dispatch(dest[0], dest[1], fill_lo, fill_hi, x_tiles, n_rows)
    yb = _experts(blk_e, xin, w_gate.astype(BF16), w_up.astype(BF16), w_down.astype(BF16))
    return _combine_ln(dest[0], dest[1], yb, gate, x2d, g, b, alpha)


def kernel(x, att_w_in, att_sinks, cmp_pe_k, cmp_wk1, cmp_wk2, cmp_pe_v, cmp_wv1, cmp_wv2, att_w_o,
           ffn_w_gate, ffn_w_up, ffn_w_down, gmlp_w_in, gmlp_ln_g, gmlp_ln_b, gmlp_w_s, gmlp_b_s, gmlp_w_out,
           moe_w_router, moe_w_gate, moe_w_up, moe_w_down, ln_g, ln_b):
    bsz, seq, dm = x.shape
    depth = ln_g.shape[0]
    alpha = (2.0 * depth) ** 0.25
    assert dm == D_MODEL and seq % ROW_TILE == 0
    x2d = x.reshape(bsz * seq, dm)

    def vec(p):
        return p.reshape(1, -1)

    for i in range(depth):
        j = i // 2
        g0, b0, g1, b1 = vec(ln_g[i, 0]), vec(ln_b[i, 0]), vec(ln_g[i, 1]), vec(ln_b[i, 1])
        if i % 2 == 0:
            x2d = _attention_layer(x2d, bsz, seq, att_w_in[j], att_sinks[j], cmp_pe_k[j], cmp_wk1[j], cmp_wk2[j],
                                   cmp_pe_v[j], cmp_wv1[j], cmp_wv2[j], att_w_o[j], g0, b0, alpha)
            x2d = _ffn_ln(x2d, ffn_w_gate[j].astype(BF16), ffn_w_up[j].astype(BF16), ffn_w_down[j].astype(BF16),
                          g1, b1, alpha)
        else:
            b_s = jnp.broadcast_to(gmlp_b_s[j][:, :, None], (GMLP_GROUPS, GMLP_CHUNK, GMLP_CHUNK))
            x2d, x_tiles = _gmlp_ln(x2d, gmlp_w_in[j].astype(BF16), vec(gmlp_ln_g[j]), vec(gmlp_ln_b[j]), gmlp_w_s[j],
                                    b_s, gmlp_w_out[j].astype(BF16), g0, b0, alpha)
            x2d = _moe_layer(x2d, x_tiles, moe_w_router[j], moe_w_gate[j], moe_w_up[j], moe_w_down[j], g1, b1, alpha)
    return x2d.reshape(bsz, seq, dm)
```

```python
import functools

import jax
import jax.numpy as jnp
import numpy as np
from jax import lax
from jax.experimental import pallas as pl
from jax.experimental.pallas import tpu as pltpu

F32 = jnp.float32
BF16 = jnp.bfloat16

D_MODEL = 1024
HEAD_DIM = 64
BLOCK_Q = 128
A_HEADS = 8
A_KV_HEADS = 2
A_WINDOW = 128
B_HEADS = 8
B_KV_HEADS = 2
GROUP = 4
CMP_LEN = 32
CMP_STRIDE = 16
CMP_HIDDEN = 256
SLC_BLOCK = 64
SLC_TOPN = 8
B_WINDOW = 256
N_ATT_HEADS = A_HEADS + B_HEADS
GMLP_GROUPS = 8
GMLP_CHUNK = 128
N_EXPERTS = 8
TOP_K = 2
LN_EPS = 1e-5
LANES = 128
DMA_UNROLL = 8

TOK_W = 640
FEAT_W = 1408
GATE_ROWS = 32
SLC_CHUNK = 256

ROW_TILE = 512
MOE_ROWS = 512
GATHER_ROWS = 256
FFN_CHUNK = 1024
EXPERT_TILE = 1792

ALIBI_SLOPES = tuple(2.0 ** (-8.0 * h / N_ATT_HEADS) for h in range(1, N_ATT_HEADS + 1))
SCALE = HEAD_DIM ** -0.5
LOG2E = 1.4426950408889634
NEG_INF = float("-inf")
RANK_FORCED = 1e30
RANK_INVALID = -1.0
MASK_BIG = 1e30
AUX_SEL = 32


def _params(n_grid, vmem_mb):
    return pltpu.CompilerParams(
        dimension_semantics=("arbitrary",) * n_grid, vmem_limit_bytes=vmem_mb * 1024 * 1024
    )


def _dot(a, b):
    return jnp.dot(a, b, preferred_element_type=F32)


def _dot_nt(a, b):
    return lax.dot_general(a, b, (((1,), (1,)), ((), ())), preferred_element_type=F32)


def _layer_norm(z, g, b):
    mu = jnp.mean(z, axis=-1, keepdims=True)
    zc = z - mu
    var = jnp.mean(zc * zc, axis=-1, keepdims=True)
    return zc * lax.rsqrt(var + LN_EPS) * g + b


def _gelu(x):
    return 0.5 * x * (1.0 + jnp.tanh(0.7978845608028654 * (x + 0.044715 * (x * x * x))))


def _silu(x):
    return x / (1.0 + jnp.exp(-x))


def _inproj_kernel(x_ref, wt_ref, wf_ref, wg_ref, tok_ref, feat_ref, gate_ref):
    x = x_ref[...].astype(BF16)
    tok_ref[...] = _dot(x, wt_ref[...]).astype(BF16)
    rows = FEAT_W // 4
    for c in range(4):
        feat_ref[0, c * rows:(c + 1) * rows, :] = _dot_nt(wf_ref[c * rows:(c + 1) * rows, :], x).astype(BF16)
    gate_ref[0] = _dot_nt(wg_ref[...], x)


def _inproj(x2d, w_tok, w_feat, w_gate, bsz, seq):
    n = x2d.shape[0]
    per_seq = seq // ROW_TILE
    return pl.pallas_call(
        _inproj_kernel,
        grid=(n // ROW_TILE,),
        in_specs=[
            pl.BlockSpec((ROW_TILE, D_MODEL), lambda i: (i, 0)),
            pl.BlockSpec((D_MODEL, TOK_W), lambda i: (0, 0)),
            pl.BlockSpec((FEAT_W, D_MODEL), lambda i: (0, 0)),
            pl.BlockSpec((GATE_ROWS, D_MODEL), lambda i: (0, 0)),
        ],
        out_specs=[
            pl.BlockSpec((ROW_TILE, TOK_W), lambda i: (i, 0)),
            pl.BlockSpec((1, FEAT_W, ROW_TILE), lambda i: (i // per_seq, 0, i % per_seq)),
            pl.BlockSpec((1, GATE_ROWS, ROW_TILE), lambda i: (i // per_seq, 0, i % per_seq)),
        ],
        out_shape=[
            jax.ShapeDtypeStruct((n, TOK_W), BF16),
            jax.ShapeDtypeStruct((bsz, FEAT_W, seq), BF16),
            jax.ShapeDtypeStruct((bsz, GATE_ROWS, seq), F32),
        ],
        compiler_params=_params(1, 32),
        name="inproj",
    )(x2d, w_tok, w_feat, w_gate)


def _compress_hidden(x, pe, w1):
    half = CMP_STRIDE * HEAD_DIM
    a = _dot(x, w1[:half])
    b = _dot(x, w1[half:])
    b_next = pltpu.roll(b, shift=x.shape[0] - 1, axis=0)
    bias = _dot(pe.astype(BF16), w1)[0:1]
    return _gelu(a + b_next + bias).astype(BF16)


def _compress_kernel(x_ref, pe_ref, w1_ref, w2k_ref, w2vt_ref, kc_ref, vct_ref):
    hk = _compress_hidden(x_ref[0, 0, 0], pe_ref[0], w1_ref[0])
    kc_ref[0, 0] = _dot(hk, w2k_ref[...]).astype(BF16)
    hv = _compress_hidden(x_ref[0, 0, 1], pe_ref[1], w1_ref[1])
    vct_ref[0, 0] = _dot_nt(w2vt_ref[...], hv).astype(BF16)


def _compress(xs, pe, w1, w2k, w2vt):
    bsz, _, _, n_str, width = xs.shape

    def full(shape):
        return pl.BlockSpec(shape, lambda b, h: (0,) * len(shape))

    return pl.pallas_call(
        _compress_kernel,
        grid=(bsz, B_KV_HEADS),
        in_specs=[
            pl.BlockSpec((1, 1, 2, n_str, width), lambda b, h: (b, h, 0, 0, 0)),
            full((2, 8, 2 * width)),
            full((2, 2 * width, CMP_HIDDEN)),
            full((CMP_HIDDEN, HEAD_DIM)),
            full((HEAD_DIM, CMP_HIDDEN)),
        ],
        out_specs=[
            pl.BlockSpec((1, 1, n_str, HEAD_DIM), lambda b, h: (b, h, 0, 0)),
            pl.BlockSpec((1, 1, HEAD_DIM, n_str), lambda b, h: (b, h, 0, 0)),
        ],
        out_shape=[
            jax.ShapeDtypeStruct((bsz, B_KV_HEADS, n_str, HEAD_DIM), BF16),
            jax.ShapeDtypeStruct((bsz, B_KV_HEADS, HEAD_DIM, n_str), BF16),
        ],
        compiler_params=_params(2, 24),
        name="nsa_compress",
    )(xs, pe, w1, w2k, w2vt)


def _bf16_parts(x, parts=3):
    out, rest = [], np.asarray(x, np.float32)
    for _ in range(parts):
        piece = rest.astype(jnp.bfloat16).astype(np.float32)
        out.append(piece)
        rest = rest - piece
    return out


def _aux_key_lanes(seq):
    t = np.arange(seq)
    aux = np.zeros((seq, LANES), np.float32)
    aux[t, t // SLC_BLOCK] = 1.0
    aux[:, AUX_SEL:AUX_SEL + 3] = (t >> 7)[:, None]
    aux[:, AUX_SEL + 3:AUX_SEL + 6] = (t & 127)[:, None]
    return jnp.asarray(aux, BF16)


def _aux_slope_rows():
    rows = np.zeros((N_ATT_HEADS // GROUP, LANES - AUX_SEL, GROUP * BLOCK_Q), np.float32)
    for grp in range(N_ATT_HEADS // GROUP):
        for g in range(GROUP):
            slope = ALIBI_SLOPES[grp * GROUP + g] * LOG2E
            cols = slice(g * BLOCK_Q, (g + 1) * BLOCK_Q)
            for i, piece in enumerate(_bf16_parts(128.0 * slope) + _bf16_parts(slope)):
                rows[grp, i, cols] = piece
    return jnp.asarray(rows, BF16)


def _attn_kernel(sink_ref, qt_ref, ka_ref, ks_ref, kw_ref, vat_ref, vst_ref, vwt_ref, kc_ref, vct_ref, gt_ref,
                 aux_ref, slope_ref, o_ref, ot_ref, mask_a_ref, mask_w_ref, *, seq):
    n = pl.program_id(1)
    t0 = n * BLOCK_Q
    n_str = seq // CMP_STRIDE
    n_cmp = n_str - CMP_LEN // CMP_STRIDE + 1
    n_sel = seq // SLC_BLOCK
    top_n = min(SLC_TOPN, n_sel)
    hd = HEAD_DIM
    gq = GROUP * BLOCK_Q
    ch = SLC_CHUNK
    span_a = A_WINDOW + BLOCK_Q
    span_w = B_WINDOW + BLOCK_Q

    lane = lax.broadcasted_iota(jnp.int32, (1, gq), 1)
    q_loc = lane & (BLOCK_Q - 1)
    lane_head = lane >> (BLOCK_Q.bit_length() - 1)
    t_q = t0 + q_loc

    def head_row(vals):
        return jnp.where(lane_head == 0, vals[0], jnp.where(lane_head == 1, vals[1],
                                                           jnp.where(lane_head == 2, vals[2], vals[3])))

    def q_group(first_head):
        return jnp.concatenate(
            [qt_ref[0, (first_head + g) * hd:(first_head + g + 1) * hd, :] for g in range(GROUP)], axis=1)

    def score_rhs(grp, kvh, qg, sel_rows):
        zero = jnp.zeros((hd, gq), BF16)
        q_rows = [qg, zero] if kvh == 0 else [zero, qg]
        return jnp.concatenate(q_rows + [sel_rows, slope_ref[grp]], axis=0)

    def score_lhs(k_ref, start, span):
        return jnp.concatenate([k_ref[0, pl.ds(start, span), :], aux_ref[pl.ds(start, span), :]], axis=1)

    def band_start(span):
        return pl.multiple_of(jnp.maximum(t0 - (span - BLOCK_Q), 0), BLOCK_Q)

    def band_mask(span, window):
        ik = lax.broadcasted_iota(jnp.int32, (span, gq), 0)
        dist = (t0 - band_start(span)) + q_loc - ik
        return jnp.where((dist >= 0) & (dist < window), 0.0, -MASK_BIG)

    @pl.when(n <= span_w // BLOCK_Q - 1)
    def _():
        mask_a_ref[...] = band_mask(span_a, A_WINDOW)
        mask_w_ref[...] = band_mask(span_w, B_WINDOW)

    no_sel = jnp.zeros((AUX_SEL, gq), BF16)
    n_grp = N_ATT_HEADS // GROUP
    slopes = [head_row([ALIBI_SLOPES[grp * GROUP + g] for g in range(GROUP)]) for grp in range(n_grp)]
    qgs = [q_group(grp * GROUP) for grp in range(n_grp)]

    def softmax_cols(s, sinks=None):
        m = jnp.max(s, axis=0, keepdims=True)
        if sinks is not None:
            m = jnp.maximum(m, sinks)
        e = jnp.exp2(s - m)
        den = jnp.sum(e, axis=0, keepdims=True)
        if sinks is not None:
            den = den + jnp.exp2(sinks - m)
        return e.astype(BF16), den

    ks_a = band_start(span_a)
    ks_w = band_start(span_w)
    lhs_a = score_lhs(ka_ref, ks_a, span_a)
    lhs_w = score_lhs(kw_ref, ks_w, span_w)
    s_a = [_dot(lhs_a, score_rhs(kvh, kvh, qgs[kvh], no_sel)) + mask_a_ref[...] for kvh in range(A_KV_HEADS)]
    s_w = [_dot(lhs_w, score_rhs(A_KV_HEADS + kvh, kvh, qgs[A_KV_HEADS + kvh], no_sel)) + mask_w_ref[...]
           for kvh in range(B_KV_HEADS)]

    c_idx = lax.broadcasted_iota(jnp.int32, (n_str, gq), 0)
    dist_c = t_q - (c_idx * CMP_STRIDE + CMP_LEN - 1)
    valid_c = (dist_c >= 0) & (c_idx < n_cmp)
    dist_cf = dist_c.astype(F32)
    s_c = [jnp.where(valid_c, _dot(kc_ref[0, kvh], qgs[A_KV_HEADS + kvh])
                     - (slopes[A_KV_HEADS + kvh] * LOG2E) * dist_cf, NEG_INF) for kvh in range(B_KV_HEADS)]

    oj = lax.broadcasted_iota(jnp.int32, (n_sel, n_str), 0) * SLC_BLOCK
    oc = lax.broadcasted_iota(jnp.int32, (n_sel, n_str), 1) * CMP_STRIDE
    overlap_t = ((oc < oj + SLC_BLOCK) & (oc + CMP_LEN > oj)).astype(BF16)
    j_idx = lax.broadcasted_iota(jnp.int32, (n_sel, BLOCK_Q), 0)
    t_blk = (t0 + lax.broadcasted_iota(jnp.int32, (1, BLOCK_Q), 1)) >> (SLC_BLOCK.bit_length() - 1)
    valid_j = j_idx <= t_blk
    forced_j = (j_idx == 0) | (j_idx == t_blk) | (j_idx == t_blk - 1)
    e_c, den_c, rhs_s = [], [], []
    for kvh in range(B_KV_HEADS):
        m = jnp.max(s_c[kvh], axis=0, keepdims=True)
        m = jnp.where(m == NEG_INF, 0.0, m)
        e = jnp.exp2(s_c[kvh] - m)
        den = jnp.sum(e, axis=0, keepdims=True)
        den = jnp.where(den > 0.0, den, 1.0)
        e_c.append(e.astype(BF16))
        den_c.append(den)
        p = e / den
        psum = p[:, 0:BLOCK_Q]
        for g in range(1, GROUP):
            psum = psum + p[:, g * BLOCK_Q:(g + 1) * BLOCK_Q]
        p_hi = psum.astype(BF16)
        p_lo = (psum - p_hi.astype(F32)).astype(BF16)
        imp = _dot(overlap_t, p_hi) + _dot(overlap_t, p_lo)
        rank = jnp.where(forced_j, RANK_FORCED, jnp.where(valid_j, imp, RANK_INVALID))
        cnt = jnp.zeros((n_sel, BLOCK_Q), jnp.int32)
        for jp in range(n_sel):
            row = rank[jp:jp + 1, :]
            before = (row > rank) | ((row == rank) & (j_idx > jp))
            cnt = cnt + before.astype(jnp.int32)
        sel_bias = jnp.where(cnt < top_n, 0.0, -MASK_BIG)
        if n_sel < AUX_SEL:
            sel_bias = jnp.concatenate([sel_bias, jnp.zeros((AUX_SEL - n_sel, BLOCK_Q), F32)], axis=0)
        sel_rows = jnp.concatenate([sel_bias.astype(BF16)] * GROUP, axis=1)
        rhs_s.append(score_rhs(A_KV_HEADS + kvh, kvh, qgs[A_KV_HEADS + kvh], sel_rows))

    t_qf = t_q.astype(F32)
    sinks = [(head_row([sink_ref[kvh * GROUP + g] for g in range(GROUP)]) + slopes[kvh] * t_qf) * LOG2E
             for kvh in range(A_KV_HEADS)]
    ed_a = [softmax_cols(s_a[kvh], sinks[kvh]) for kvh in range(A_KV_HEADS)]
    ed_w = [softmax_cols(s_w[kvh]) for kvh in range(B_KV_HEADS)]

    def rows(kvh):
        return slice(kvh * hd, (kvh + 1) * hd)

    o_a = [_dot(vat_ref[0, rows(kvh), pl.ds(ks_a, span_a)], ed_a[kvh][0]) / ed_a[kvh][1] for kvh in range(A_KV_HEADS)]
    o_w = [_dot(vwt_ref[0, rows(kvh), pl.ds(ks_w, span_w)], ed_w[kvh][0]) / ed_w[kvh][1] for kvh in range(B_KV_HEADS)]
    o_c = [_dot(vct_ref[0, kvh], e_c[kvh]) / den_c[kvh] for kvh in range(B_KV_HEADS)]
    for kvh in range(A_KV_HEADS):
        for g in range(GROUP):
            h = kvh * GROUP + g
            ot_ref[h * hd:(h + 1) * hd, :] = o_a[kvh][:, g * BLOCK_Q:(g + 1) * BLOCK_Q]

    dloc = q_loc - lax.broadcasted_iota(jnp.int32, (ch, gq), 0)
    n_full = t0 // ch

    def slc_scores(c):
        lhs = score_lhs(ks_ref, pl.multiple_of(c * ch, ch), ch)
        return tuple(_dot(lhs, rhs_s[kvh]) for kvh in range(B_KV_HEADS))

    def slc_update(c, state, s, kvh):
        m, l, acc = state
        m_new = jnp.maximum(m, jnp.max(s, axis=0, keepdims=True))
        alpha = jnp.exp2(m - m_new)
        e = jnp.exp2(s - m_new)
        l_new = alpha * l + jnp.sum(e, axis=0, keepdims=True)
        vt = vst_ref[0, rows(kvh), pl.ds(pl.multiple_of(c * ch, ch), ch)]
        return m_new, l_new, alpha * acc + _dot(vt, e.astype(BF16))

    def slc_step(c, carry):
        states, s = carry
        s_next = slc_scores(c + 1)
        return tuple(slc_update(c, states[kvh], s[kvh], kvh) for kvh in range(B_KV_HEADS)), s_next

    init = (jnp.full((1, gq), NEG_INF, F32), jnp.zeros((1, gq), F32), jnp.zeros((hd, gq), F32))
    states, s = lax.fori_loop(0, n_full, slc_step, ((init,) * B_KV_HEADS, slc_scores(0)))
    causal = dloc >= n_full * ch - t0
    gt = jax.nn.sigmoid(gt_ref[0])
    for kvh in range(B_KV_HEADS):
        _, l_fin, acc_fin = slc_update(n_full, states[kvh], jnp.where(causal, s[kvh], -MASK_BIG), kvh)
        o_slc = acc_fin / l_fin
        for g in range(GROUP):
            hb = kvh * GROUP + g
            h = A_HEADS + hb
            cols = slice(g * BLOCK_Q, (g + 1) * BLOCK_Q)
            ot_ref[h * hd:(h + 1) * hd, :] = (
                gt[hb:hb + 1] * o_c[kvh][:, cols]
                + gt[B_HEADS + hb:B_HEADS + hb + 1] * o_slc[:, cols]
                + gt[2 * B_HEADS + hb:2 * B_HEADS + hb + 1] * o_w[kvh][:, cols]
            )

    o_ref[0] = ot_ref[...].T.astype(BF16)


def _attention(sinks, tok, feat, kc, vct, gates_t, seq):
    bsz = tok.shape[0]
    n_str = seq // CMP_STRIDE
    kv_w = A_KV_HEADS * HEAD_DIM
    q_rows = N_ATT_HEADS * HEAD_DIM
    gq = GROUP * BLOCK_Q
    assert seq // SLC_BLOCK <= AUX_SEL and seq <= 128 * 256

    def k_spec(col_block):
        return pl.BlockSpec((1, seq, kv_w), lambda b, n: (b, 0, col_block))

    def vt_spec(row_block):
        return pl.BlockSpec((1, kv_w, seq), lambda b, n: (b, q_rows // kv_w + row_block, 0))

    return pl.pallas_call(
        functools.partial(_attn_kernel, seq=seq),
        grid=(bsz, seq // BLOCK_Q),
        in_specs=[
            pl.BlockSpec(memory_space=pltpu.SMEM),
            pl.BlockSpec((1, q_rows, BLOCK_Q), lambda b, n: (b, 0, n)),
            k_spec(0), k_spec(1), k_spec(2),
            vt_spec(0), vt_spec(1), vt_spec(2),
            pl.BlockSpec((1, B_KV_HEADS, n_str, HEAD_DIM), lambda b, n: (b, 0, 0, 0)),
            pl.BlockSpec((1, B_KV_HEADS, HEAD_DIM, n_str), lambda b, n: (b, 0, 0, 0)),
            pl.BlockSpec((1, GATE_ROWS, BLOCK_Q), lambda b, n: (b, 0, n)),
            pl.BlockSpec((seq, LANES), lambda b, n: (0, 0)),
            pl.BlockSpec((N_ATT_HEADS // GROUP, LANES - AUX_SEL, gq), lambda b, n: (0, 0, 0)),
        ],
        out_specs=pl.BlockSpec((1, BLOCK_Q, q_rows), lambda b, n: (b, n, 0)),
        out_shape=jax.ShapeDtypeStruct((bsz, seq, q_rows), BF16),
        scratch_shapes=[
            pltpu.VMEM((q_rows, BLOCK_Q), F32),
            pltpu.VMEM((A_WINDOW + BLOCK_Q, gq), F32),
            pltpu.VMEM((B_WINDOW + BLOCK_Q, gq), F32),
        ],
        compiler_params=_params(2, 32),
        name="hybrid_attention",
    )(sinks, feat, tok, tok, tok, feat, feat, feat, kc, vct, gates_t, _aux_key_lanes(seq), _aux_slope_rows())


def _proj_ln_kernel(o_ref, w_ref, x_ref, g_ref, b_ref, y_ref, *, alpha):
    y = _dot(o_ref[...], w_ref[...])
    y_ref[...] = _layer_norm(alpha * x_ref[...] + y, g_ref[...], b_ref[...])


def _proj_ln(o2d, w, x2d, g, b, alpha):
    n, k = o2d.shape
    row = pl.BlockSpec((ROW_TILE, D_MODEL), lambda i: (i, 0))
    vec = pl.BlockSpec((1, D_MODEL), lambda i: (0, 0))
    return pl.pallas_call(
        functools.partial(_proj_ln_kernel, alpha=alpha),
        grid=(n // ROW_TILE,),
        in_specs=[
            pl.BlockSpec((ROW_TILE, k), lambda i: (i, 0)),
            pl.BlockSpec((k, D_MODEL), lambda i: (0, 0)),
            row, vec, vec,
        ],
        out_specs=row,
        out_shape=jax.ShapeDtypeStruct((n, D_MODEL), F32),
        compiler_params=_params(1, 32),
        name="out_proj_ln",
    )(o2d, w, x2d, g, b)


def _swiglu_acc(x, wg, wu, wd, acc_ref, f):
    h = (_silu(_dot(x, wg)) * _dot(x, wu)).astype(BF16)
    y = _dot(h, wd)

    @pl.when(f == 0)
    def _():
        acc_ref[...] = y

    @pl.when(f != 0)
    def _():
        acc_ref[...] += y


def _ffn_ln_kernel(x_ref, wg_ref, wu_ref, wd_ref, g_ref, b_ref, y_ref, *, alpha):
    x = x_ref[...]
    xb = x.astype(BF16)
    d_ff = wg_ref.shape[1]
    y = None
    for lo in range(0, d_ff, FFN_CHUNK):
        hi = min(lo + FFN_CHUNK, d_ff)
        h = (_silu(_dot(xb, wg_ref[:, lo:hi])) * _dot(xb, wu_ref[:, lo:hi])).astype(BF16)
        part = _dot(h, wd_ref[lo:hi, :])
        y = part if y is None else y + part
    y_ref[...] = _layer_norm(alpha * x + y, g_ref[...], b_ref[...])


def _ffn_ln(x2d, wg, wu, wd, g, b, alpha):
    n = x2d.shape[0]
    d_ff = wg.shape[1]
    row = pl.BlockSpec((ROW_TILE, D_MODEL), lambda i: (i, 0))
    vec = pl.BlockSpec((1, D_MODEL), lambda i: (0, 0))
    once = pl.Buffered(1)
    return pl.pallas_call(
        functools.partial(_ffn_ln_kernel, alpha=alpha),
        grid=(n // ROW_TILE,),
        in_specs=[
            row,
            pl.BlockSpec((D_MODEL, d_ff), lambda i: (0, 0), pipeline_mode=once),
            pl.BlockSpec((D_MODEL, d_ff), lambda i: (0, 0), pipeline_mode=once),
            pl.BlockSpec((d_ff, D_MODEL), lambda i: (0, 0), pipeline_mode=once),
            vec, vec,
        ],
        out_specs=row,
        out_shape=jax.ShapeDtypeStruct((n, D_MODEL), F32),
        compiler_params=_params(1, 48),
        name="swiglu_ln",
    )(x2d, wg, wu, wd, g, b)


def _expert_kernel(be_ref, x_ref, wg_ref, wu_ref, wd_ref, y_ref, xb_ref):
    del be_ref
    f = pl.program_id(1)

    @pl.when(f == 0)
    def _():
        xb_ref[...] = x_ref[...].astype(BF16)

    _swiglu_acc(xb_ref[...], wg_ref[0], wu_ref[0], wd_ref[0], y_ref, f)


def _experts(blk_e, xin, wg, wu, wd):
    n_rows = blk_e.shape[0] * MOE_ROWS
    d_ff = wg.shape[2]
    return pl.pallas_call(
        _expert_kernel,
        grid_spec=pltpu.PrefetchScalarGridSpec(
            num_scalar_prefetch=1,
            grid=(n_rows // MOE_ROWS, d_ff // EXPERT_TILE),
            in_specs=[
                pl.BlockSpec((MOE_ROWS, D_MODEL), lambda i, f, be: (i, 0)),
                pl.BlockSpec((1, D_MODEL, EXPERT_TILE), lambda i, f, be: (be[i], 0, f)),
                pl.BlockSpec((1, D_MODEL, EXPERT_TILE), lambda i, f, be: (be[i], 0, f)),
                pl.BlockSpec((1, EXPERT_TILE, D_MODEL), lambda i, f, be: (be[i], f, 0)),
            ],
            out_specs=pl.BlockSpec((MOE_ROWS, D_MODEL), lambda i, f, be: (i, 0)),
            scratch_shapes=[pltpu.VMEM((MOE_ROWS, D_MODEL), BF16)],
        ),
        out_shape=jax.ShapeDtypeStruct((n_rows, D_MODEL), F32),
        compiler_params=_params(2, 52),
        name="expert_swiglu",
    )(blk_e, xin, wg, wu, wd)


def _gmlp_kernel(x_ref, win_ref, lng_ref, lnb_ref, ws_ref, bs_ref, wout_ref, g_ref, b_ref, y_ref,
                 u_ref, vn_ref, gated_ref, *, alpha):
    x = x_ref[...]
    xb = x.astype(BF16)
    u_ref[...] = _gelu(_dot(xb, win_ref[:, :D_MODEL]))
    v = _gelu(_dot(xb, win_ref[:, D_MODEL:]))
    vn_ref[...] = _layer_norm(v, lng_ref[...], lnb_ref[...]).astype(BF16)
    cs = GMLP_CHUNK
    lower = lax.broadcasted_iota(jnp.int32, (cs, cs), 0) >= lax.broadcasted_iota(jnp.int32, (cs, cs), 1)
    for grp in range(GMLP_GROUPS):
        w = jnp.where(lower, ws_ref[grp], 0.0).astype(BF16)
        bias = bs_ref[grp]
        cols = slice(grp * cs, (grp + 1) * cs)
        for c in range(x.shape[0] // cs):
            rows = slice(c * cs, (c + 1) * cs)
            mixed = _dot(w, vn_ref[rows, cols]) + bias
            gated_ref[rows, cols] = (u_ref[rows, cols] * mixed).astype(BF16)
    y = _dot(gated_ref[...], wout_ref[...])
    y_ref[...] = _layer_norm(alpha * x + y, g_ref[...], b_ref[...])


def _gmlp_ln(x2d, w_in, ln_g, ln_b, w_s, b_s, w_out, g, b, alpha):
    n = x2d.shape[0]
    row = pl.BlockSpec((ROW_TILE, D_MODEL), lambda i: (i, 0))
    vec = pl.BlockSpec((1, D_MODEL), lambda i: (0, 0))
    grp = pl.BlockSpec((GMLP_GROUPS, GMLP_CHUNK, GMLP_CHUNK), lambda i: (0, 0, 0))
    return pl.pallas_call(
        functools.partial(_gmlp_kernel, alpha=alpha),
        grid=(n // ROW_TILE,),
        in_specs=[
            row,
            pl.BlockSpec((D_MODEL, 2 * D_MODEL), lambda i: (0, 0)),
            vec, vec, grp, grp,
            pl.BlockSpec((D_MODEL, D_MODEL), lambda i: (0, 0)),
            vec, vec,
        ],
        out_specs=row,
        out_shape=jax.ShapeDtypeStruct((n, D_MODEL), F32),
        scratch_shapes=[
            pltpu.VMEM((ROW_TILE, D_MODEL), F32),
            pltpu.VMEM((ROW_TILE, D_MODEL), BF16),
            pltpu.VMEM((ROW_TILE, D_MODEL), BF16),
        ],
        compiler_params=_params(1, 40),
        name="gmlp_ln",
    )(x2d, w_in, ln_g, ln_b, w_s, b_s, w_out, g, b)


def _route_kernel(x_ref, w_ref, gate_ref, et_ref, post_ref, cnt_ref, run_ref):
    tm = x_ref.shape[0]

    @pl.when(pl.program_id(0) == 0)
    def _():
        run_ref[...] = jnp.zeros_like(run_ref)

    logits = lax.dot_general(w_ref[...], x_ref[...], (((1,), (1,)), ((), ())),
                             preferred_element_type=F32, precision=lax.Precision.HIGHEST)
    row = lax.broadcasted_iota(jnp.int32, (N_EXPERTS, tm), 0)
    v1 = jnp.max(logits, axis=0, keepdims=True)
    i1 = jnp.min(jnp.where(logits == v1, row, N_EXPERTS), axis=0, keepdims=True)
    rest = jnp.where(row == i1, NEG_INF, logits)
    v2 = jnp.max(rest, axis=0, keepdims=True)
    i2 = jnp.min(jnp.where(rest == v2, row, N_EXPERTS), axis=0, keepdims=True)
    d = jnp.exp(v2 - v1)
    g1 = 1.0 / (1.0 + d)
    g2 = d / (1.0 + d)

    hit1 = row == i1
    hit2 = row == i2
    onehot = (hit1 | hit2).astype(BF16)
    earlier = (lax.broadcasted_iota(jnp.int32, (tm, tm), 0) < lax.broadcasted_iota(jnp.int32, (tm, tm), 1)).astype(BF16)
    before = _dot(onehot, earlier) + run_ref[:, 0:1]
    pos1 = jnp.sum(jnp.where(hit1, before, 0.0), axis=0, keepdims=True).astype(jnp.int32)
    pos2 = jnp.sum(jnp.where(hit2, before, 0.0), axis=0, keepdims=True).astype(jnp.int32)
    run_ref[...] += jnp.sum(onehot.astype(F32), axis=1, keepdims=True)

    et_ref[...] = jnp.where(row == 0, i1, jnp.where(row == 1, i2, 0))
    post_ref[...] = jnp.where(row == 0, pos1, jnp.where(row == 1, pos2, 0))
    cnt_ref[...] = run_ref[...]
    wide = lax.broadcasted_iota(jnp.int32, (LANES, tm), 0)
    gate_ref[...] = jnp.where(wide == 0, g1, jnp.where(wide == 1, g2, 0.0)).T


def _route(x2d, w_router_t):
    n = x2d.shape[0]
    tm = ROW_TILE
    per_tok = pl.BlockSpec((N_EXPERTS, tm), lambda i: (0, i))
    return pl.pallas_call(
        _route_kernel,
        grid=(n // tm,),
        in_specs=[
            pl.BlockSpec((tm, D_MODEL), lambda i: (i, 0)),
            pl.BlockSpec((N_EXPERTS, D_MODEL), lambda i: (0, 0)),
        ],
        out_specs=[
            pl.BlockSpec((tm, LANES), lambda i: (i, 0)),
            per_tok, per_tok,
            pl.BlockSpec((N_EXPERTS, LANES), lambda i: (0, 0)),
        ],
        out_shape=[
            jax.ShapeDtypeStruct((n, LANES), F32),
            jax.ShapeDtypeStruct((N_EXPERTS, n), jnp.int32),
            jax.ShapeDtypeStruct((N_EXPERTS, n), jnp.int32),
            jax.ShapeDtypeStruct((N_EXPERTS, LANES), F32),
        ],
        scratch_shapes=[pltpu.VMEM((N_EXPERTS, LANES), F32)],
        compiler_params=_params(1, 24),
        name="moe_router",
    )(x2d, w_router_t)


def _dispatch_kernel(d0_ref, d1_ref, lo_ref, hi_ref, x_ref, xin_hbm, zero_ref, sem_rows, sem_zero):
    step = pl.program_id(0)
    tm = x_ref.shape[0]
    base = step * tm

    def row_copy(r, dest_ref):
        return pltpu.make_async_copy(x_ref.at[pl.ds(r, 1)], xin_hbm.at[pl.ds(dest_ref[base + r], 1)], sem_rows)

    def issue(r8, carry):
        for j in range(DMA_UNROLL):
            r = r8 * DMA_UNROLL + j
            row_copy(r, d0_ref).start(priority=j % 2)
            row_copy(r, d1_ref).start(priority=(j + 1) % 2)
        return carry

    lax.fori_loop(0, tm // DMA_UNROLL, issue, 0)

    @pl.when(step == pl.num_programs(0) - 1)
    def _():
        zero_ref[...] = jnp.zeros_like(zero_ref)
        for k in range(N_EXPERTS + 1):
            def fill(r, carry):
                pltpu.make_async_copy(zero_ref, xin_hbm.at[pl.ds(r, 1)], sem_zero).start()
                return carry

            def drain(r, carry):
                pltpu.make_async_copy(zero_ref, xin_hbm.at[pl.ds(r, 1)], sem_zero).wait()
                return carry

            lax.fori_loop(lo_ref[k], hi_ref[k], fill, 0)
            lax.fori_loop(lo_ref[k], hi_ref[k], drain, 0)

    pltpu.make_async_copy(x_ref, xin_hbm.at[pl.ds(0, tm)], sem_rows).wait()
    pltpu.make_async_copy(x_ref, xin_hbm.at[pl.ds(0, tm)], sem_rows).wait()


def _dispatch(d0, d1, fill_lo, fill_hi, x2d, n_rows):
    n = x2d.shape[0]
    tm = ROW_TILE
    return pl.pallas_call(
        _dispatch_kernel,
        grid_spec=pltpu.PrefetchScalarGridSpec(
            num_scalar_prefetch=4,
            grid=(n // tm,),
            in_specs=[pl.BlockSpec((tm, D_MODEL), lambda i, *_: (i, 0))],
            out_specs=pl.BlockSpec(memory_space=pl.ANY),
            scratch_shapes=[pltpu.VMEM((1, D_MODEL), F32), pltpu.SemaphoreType.DMA(()), pltpu.SemaphoreType.DMA(())],
        ),
        out_shape=jax.ShapeDtypeStruct((n_rows, D_MODEL), F32),
        compiler_params=_params(1, 16),
        name="moe_dispatch",
    )(d0, d1, fill_lo, fill_hi, x2d)


def _combine_kernel(d0_ref, d1_ref, y_hbm, gate_ref, x_ref, g_ref, b_ref, o_ref, buf0, buf1, sem, *, alpha):
    tm = x_ref.shape[0]
    base = pl.program_id(0) * tm

    def issue(r8, carry):
        for j in range(DMA_UNROLL):
            r = r8 * DMA_UNROLL + j
            pltpu.make_async_copy(y_hbm.at[pl.ds(d0_ref[base + r], 1)], buf0.at[pl.ds(r, 1)], sem.at[0]).start(priority=0)
            pltpu.make_async_copy(y_hbm.at[pl.ds(d1_ref[base + r], 1)], buf1.at[pl.ds(r, 1)], sem.at[1]).start(priority=1)
        return carry

    lax.fori_loop(0, tm // DMA_UNROLL, issue, 0)
    pltpu.make_async_copy(y_hbm.at[pl.ds(0, tm)], buf0, sem.at[0]).wait()
    pltpu.make_async_copy(y_hbm.at[pl.ds(0, tm)], buf1, sem.at[1]).wait()
    gate = gate_ref[...]
    y = buf0[...] * gate[:, 0:1] + buf1[...] * gate[:, 1:2]
    o_ref[...] = _layer_norm(alpha * x_ref[...] + y, g_ref[...], b_ref[...])


def _combine_ln(d0, d1, yb, gate, x2d, g, b, alpha):
    n = x2d.shape[0]
    tm = GATHER_ROWS
    row = pl.BlockSpec((tm, D_MODEL), lambda i, a, c: (i, 0))
    vec = pl.BlockSpec((1, D_MODEL), lambda i, a, c: (0, 0))
    return pl.pallas_call(
        functools.partial(_combine_kernel, alpha=alpha),
        grid_spec=pltpu.PrefetchScalarGridSpec(
            num_scalar_prefetch=2,
            grid=(n // tm,),
            in_specs=[
                pl.BlockSpec(memory_space=pl.ANY),
                pl.BlockSpec((tm, LANES), lambda i, a, c: (i, 0)),
                row, vec, vec,
            ],
            out_specs=row,
            scratch_shapes=[
                pltpu.VMEM((tm, D_MODEL), F32),
                pltpu.VMEM((tm, D_MODEL), F32),
                pltpu.SemaphoreType.DMA((2,)),
            ],
        ),
        out_shape=jax.ShapeDtypeStruct((n, D_MODEL), F32),
        compiler_params=_params(1, 16),
        name="moe_combine_ln",
    )(d0, d1, yb, gate, x2d, g, b)


def _attention_layer(x2d, bsz, seq, w_in, sinks, pe_k, wk1, wk2, pe_v, wv1, wv2, w_o, g, b, alpha):
    aq, akv = A_HEADS * HEAD_DIM, A_KV_HEADS * HEAD_DIM
    bq, bkv = B_HEADS * HEAD_DIM, B_KV_HEADS * HEAD_DIM
    bounds = [0]
    for width in (aq, akv, akv, bq, bkv, bkv, bkv, bkv, bkv, bkv, 3 * B_HEADS):
        bounds.append(bounds[-1] + width)
    qa, ka, va, qb, kc, vc, ks, vs, kw, vw, wg = [w_in[:, bounds[i]:bounds[i + 1]] for i in range(11)]
    w_tok = jnp.concatenate([ka, ks, kw, kc, vc], axis=1).astype(BF16)
    w_feat = jnp.concatenate([qa * (SCALE * LOG2E), qb * (SCALE * LOG2E), va, vs, vw], axis=1).T.astype(BF16)
    wg = wg.reshape(D_MODEL, B_HEADS, 3).transpose(2, 1, 0).reshape(3 * B_HEADS, D_MODEL)
    wg = jnp.pad(wg, ((0, GATE_ROWS - 3 * B_HEADS), (0, 0))).astype(BF16)
    tok, feat, gates_t = _inproj(x2d, w_tok, w_feat, wg, bsz, seq)

    n_str = seq // CMP_STRIDE
    kvc = tok[:, 3 * akv:]
    xs = kvc.reshape(bsz, n_str, CMP_STRIDE, 2, B_KV_HEADS, HEAD_DIM).transpose(0, 4, 3, 1, 2, 5)
    xs = xs.reshape(bsz, B_KV_HEADS, 2, n_str, CMP_STRIDE * HEAD_DIM)
    pe = jnp.stack([pe_k, pe_v]).reshape(2, 1, CMP_LEN * HEAD_DIM)
    pe = jnp.broadcast_to(pe, (2, 8, CMP_LEN * HEAD_DIM))
    kcmp, vcmp_t = _compress(xs, pe, jnp.stack([wk1, wv1]).astype(BF16), wk2.astype(BF16), wv2.T.astype(BF16))

    o = _attention(sinks, tok.reshape(bsz, seq, TOK_W), feat, kcmp, vcmp_t, gates_t, seq)
    return _proj_ln(o.reshape(bsz * seq, N_ATT_HEADS * HEAD_DIM), w_o.astype(BF16), x2d, g, b, alpha)


def _moe_layer(x2d, w_router, w_gate, w_up, w_down, g, b, alpha):
    n_tok = x2d.shape[0]
    gate, e_t, pos_t, cnt = _route(x2d, w_router.T)
    counts = cnt[:, 0].astype(jnp.int32)
    padded = (counts + MOE_ROWS - 1) // MOE_ROWS * MOE_ROWS
    pad_ends = jnp.cumsum(padded)
    pad_starts = pad_ends - padded
    n_blocks = n_tok * TOP_K // MOE_ROWS + N_EXPERTS
    n_rows = n_blocks * MOE_ROWS
    experts = jnp.arange(N_EXPERTS, dtype=jnp.int32)[:, None, None]
    start_of = jnp.sum(jnp.where(e_t[None, :TOP_K] == experts, pad_starts[:, None, None], 0), axis=0)
    dest = start_of + pos_t[:TOP_K]
    fill_lo = jnp.concatenate([pad_starts + counts, pad_ends[-1:]])
    fill_hi = jnp.concatenate([pad_ends, jnp.full((1,), n_rows, jnp.int32)])
    blk_start = jnp.arange(n_blocks, dtype=jnp.int32) * MOE_ROWS
    blk_e = jnp.sum((pad_ends[None, :] <= blk_start[:, None]).astype(jnp.int32), axis=1)
    blk_e = jnp.minimum(blk_e, N_EXPERTS - 1)

    xin = _dispatch(dest[0], dest[1], fill_lo, fill_hi, x2d, n_rows)
    yb = _experts(blk_e, xin, w_gate.astype(BF16), w_up.astype(BF16), w_down.astype(BF16))
    return _combine_ln(dest[0], dest[1], yb, gate, x2d, g, b, alpha)


def kernel(x, att_w_in, att_sinks, cmp_pe_k, cmp_wk1, cmp_wk2, cmp_pe_v, cmp_wv1, cmp_wv2, att_w_o,
           ffn_w_gate, ffn_w_up, ffn_w_down, gmlp_w_in, gmlp_ln_g, gmlp_ln_b, gmlp_w_s, gmlp_b_s, gmlp_w_out,
           moe_w_router, moe_w_gate, moe_w_up, moe_w_down, ln_g, ln_b):
    bsz, seq, dm = x.shape
    depth = ln_g.shape[0]
    alpha = (2.0 * depth) ** 0.25
    assert dm == D_MODEL and seq % ROW_TILE == 0
    x2d = x.reshape(bsz * seq, dm)

    def vec(p):
        return p.reshape(1, -1)

    for i in range(depth):
        j = i // 2
        g0, b0, g1, b1 = vec(ln_g[i, 0]), vec(ln_b[i, 0]), vec(ln_g[i, 1]), vec(ln_b[i, 1])
        if i % 2 == 0:
            x2d = _attention_layer(x2d, bsz, seq, att_w_in[j], att_sinks[j], cmp_pe_k[j], cmp_wk1[j], cmp_wk2[j],
                                   cmp_pe_v[j], cmp_wv1[j], cmp_wv2[j], att_w_o[j], g0, b0, alpha)
            x2d = _ffn_ln(x2d, ffn_w_gate[j].astype(BF16), ffn_w_up[j].astype(BF16), ffn_w_down[j].astype(BF16),
                          g1, b1, alpha)
        else:
            b_s = jnp.broadcast_to(gmlp_b_s[j][:, :, None], (GMLP_GROUPS, GMLP_CHUNK, GMLP_CHUNK))
            x2d = _gmlp_ln(x2d, gmlp_w_in[j].astype(BF16), vec(gmlp_ln_g[j]), vec(gmlp_ln_b[j]), gmlp_w_s[j], b_s,
                           gmlp_w_out[j].astype(BF16), g0, b0, alpha)
            x2d = _moe_layer(x2d, moe_w_router[j], moe_w_gate[j], moe_w_up[j], moe_w_down[j], g1, b1, alpha)
    return x2d.reshape(bsz, seq, dm)
```

```python
import functools

import jax
import jax.numpy as jnp
import numpy as np
from jax import lax
from jax.experimental import pallas as pl
from jax.experimental.pallas import tpu as pltpu

F32 = jnp.float32
BF16 = jnp.bfloat16

D_MODEL = 1024
HEAD_DIM = 64
BLOCK_Q = 128
A_HEADS = 8
A_KV_HEADS = 2
A_WINDOW = 128
B_HEADS = 8
B_KV_HEADS = 2
GROUP = 4
CMP_LEN = 32
CMP_STRIDE = 16
CMP_HIDDEN = 256
SLC_BLOCK = 64
SLC_TOPN = 8
B_WINDOW = 256
N_ATT_HEADS = A_HEADS + B_HEADS
GMLP_GROUPS = 8
GMLP_CHUNK = 128
N_EXPERTS = 8
TOP_K = 2
LN_EPS = 1e-5
LANES = 128
DMA_UNROLL = 8

TOK_W = 640
KEY_W = 384
FEAT_W = 1408
GATE_ROWS = 32
SLC_CHUNK = 256

ROW_TILE = 512
MOE_ROWS = 512
GATHER_ROWS = 256
FFN_CHUNK = 1024
EXPERT_TILE = 1792

ALIBI_SLOPES = tuple(2.0 ** (-8.0 * h / N_ATT_HEADS) for h in range(1, N_ATT_HEADS + 1))
SCALE = HEAD_DIM ** -0.5
LOG2E = 1.4426950408889634
NEG_INF = float("-inf")
RANK_FORCED = 1e30
RANK_INVALID = -1.0
MASK_BIG = 1e30
AUX_SEL = 32


def _params(n_grid, vmem_mb):
    return pltpu.CompilerParams(
        dimension_semantics=("arbitrary",) * n_grid, vmem_limit_bytes=vmem_mb * 1024 * 1024
    )


def _dot(a, b):
    return jnp.dot(a, b, preferred_element_type=F32)


def _dot_nt(a, b):
    return lax.dot_general(a, b, (((1,), (1,)), ((), ())), preferred_element_type=F32)


def _layer_norm(z, g, b):
    mu = jnp.mean(z, axis=-1, keepdims=True)
    zc = z - mu
    var = jnp.mean(zc * zc, axis=-1, keepdims=True)
    return zc * lax.rsqrt(var + LN_EPS) * g + b


def _gelu(x):
    return 0.5 * x * (1.0 + jnp.tanh(0.7978845608028654 * (x + 0.044715 * (x * x * x))))


def _silu(x):
    return x / (1.0 + jnp.exp(-x))


def _inproj_kernel(x_ref, wt_ref, wf_ref, wg_ref, tok_ref, cmp_ref, feat_ref, gate_ref):
    x = x_ref[...].astype(BF16)
    tok = _dot(x, wt_ref[...])
    tok_ref[...] = tok[:, :KEY_W].astype(BF16)
    cmp_ref[...] = tok[:, KEY_W:]
    rows = FEAT_W // 4
    for c in range(4):
        feat_ref[0, c * rows:(c + 1) * rows, :] = _dot_nt(wf_ref[c * rows:(c + 1) * rows, :], x).astype(BF16)
    gate_ref[0] = _dot_nt(wg_ref[...], x)


def _inproj(x2d, w_tok, w_feat, w_gate, bsz, seq):
    n = x2d.shape[0]
    per_seq = seq // ROW_TILE
    return pl.pallas_call(
        _inproj_kernel,
        grid=(n // ROW_TILE,),
        in_specs=[
            pl.BlockSpec((ROW_TILE, D_MODEL), lambda i: (i, 0)),
            pl.BlockSpec((D_MODEL, TOK_W), lambda i: (0, 0)),
            pl.BlockSpec((FEAT_W, D_MODEL), lambda i: (0, 0)),
            pl.BlockSpec((GATE_ROWS, D_MODEL), lambda i: (0, 0)),
        ],
        out_specs=[
            pl.BlockSpec((ROW_TILE, KEY_W), lambda i: (i, 0)),
            pl.BlockSpec((ROW_TILE, TOK_W - KEY_W), lambda i: (i, 0)),
            pl.BlockSpec((1, FEAT_W, ROW_TILE), lambda i: (i // per_seq, 0, i % per_seq)),
            pl.BlockSpec((1, GATE_ROWS, ROW_TILE), lambda i: (i // per_seq, 0, i % per_seq)),
        ],
        out_shape=[
            jax.ShapeDtypeStruct((n, KEY_W), BF16),
            jax.ShapeDtypeStruct((n, TOK_W - KEY_W), F32),
            jax.ShapeDtypeStruct((bsz, FEAT_W, seq), BF16),
            jax.ShapeDtypeStruct((bsz, GATE_ROWS, seq), F32),
        ],
        compiler_params=_params(1, 32),
        name="inproj",
    )(x2d, w_tok, w_feat, w_gate)


def _compress_hidden(x_ref, which, pe_ref, w1_ref, n_str):
    xs = jnp.concatenate(
        [x_ref[0, pl.ds(p, n_str, stride=CMP_STRIDE), :] for p in range(CMP_STRIDE)], axis=1).astype(BF16)
    a = _dot(xs, w1_ref[0, which, 0])
    b = _dot(xs, w1_ref[0, which, 1])
    b_next = pltpu.roll(b, shift=n_str - 1, axis=0)
    bias = _dot(pe_ref[which, 0].astype(BF16), w1_ref[0, which, 0]) + _dot(pe_ref[which, 1].astype(BF16),
                                                                         w1_ref[0, which, 1])
    return _gelu(a + b_next + bias[0:1]).astype(BF16)


def _compress_kernel(xk_ref, xv_ref, pe_ref, w1_ref, w2k_ref, w2vt_ref, kc_ref, vct_ref):
    n_str = kc_ref.shape[2]
    kc_ref[0, 0] = _dot(_compress_hidden(xk_ref, 0, pe_ref, w1_ref, n_str), w2k_ref[...]).astype(BF16)
    vct_ref[0, 0] = _dot_nt(w2vt_ref[...], _compress_hidden(xv_ref, 1, pe_ref, w1_ref, n_str)).astype(BF16)


def _compress(x_cmp, pe, w1, w2k, w2vt):
    bsz, seq, width = x_cmp.shape
    n_str = seq // CMP_STRIDE

    def full(shape):
        return pl.BlockSpec(shape, lambda b, h: (0,) * len(shape))

    return pl.pallas_call(
        _compress_kernel,
        grid=(bsz, B_KV_HEADS),
        in_specs=[
            pl.BlockSpec((1, seq, LANES), lambda b, h: (b, 0, 0)),
            pl.BlockSpec((1, seq, LANES), lambda b, h: (b, 0, 1)),
            full(pe.shape),
            pl.BlockSpec((1,) + w1.shape[1:], lambda b, h: (h, 0, 0, 0, 0)),
            full((CMP_HIDDEN, HEAD_DIM)),
            full((HEAD_DIM, CMP_HIDDEN)),
        ],
        out_specs=[
            pl.BlockSpec((1, 1, n_str, HEAD_DIM), lambda b, h: (b, h, 0, 0)),
            pl.BlockSpec((1, 1, HEAD_DIM, n_str), lambda b, h: (b, h, 0, 0)),
        ],
        out_shape=[
            jax.ShapeDtypeStruct((bsz, B_KV_HEADS, n_str, HEAD_DIM), BF16),
            jax.ShapeDtypeStruct((bsz, B_KV_HEADS, HEAD_DIM, n_str), BF16),
        ],
        compiler_params=_params(2, 32),
        name="nsa_compress",
    )(x_cmp, x_cmp, pe, w1, w2k, w2vt)


def _bf16_parts(x, parts=3):
    out, rest = [], np.asarray(x, np.float32)
    for _ in range(parts):
        piece = rest.astype(jnp.bfloat16).astype(np.float32)
        out.append(piece)
        rest = rest - piece
    return out


def _aux_key_lanes(seq):
    t = np.arange(seq)
    aux = np.zeros((seq, LANES), np.float32)
    aux[t, t // SLC_BLOCK] = 1.0
    aux[:, AUX_SEL:AUX_SEL + 3] = (t >> 7)[:, None]
    aux[:, AUX_SEL + 3:AUX_SEL + 6] = (t & 127)[:, None]
    return jnp.asarray(aux, BF16)


def _aux_slope_rows():
    rows = np.zeros((N_ATT_HEADS // GROUP, LANES - AUX_SEL, GROUP * BLOCK_Q), np.float32)
    for grp in range(N_ATT_HEADS // GROUP):
        for g in range(GROUP):
            slope = ALIBI_SLOPES[grp * GROUP + g] * LOG2E
            cols = slice(g * BLOCK_Q, (g + 1) * BLOCK_Q)
            for i, piece in enumerate(_bf16_parts(128.0 * slope) + _bf16_parts(slope)):
                rows[grp, i, cols] = piece
    return jnp.asarray(rows, BF16)


def _attn_kernel(sink_ref, qt_ref, ka_ref, ks_ref, kw_ref, vat_ref, vst_ref, vwt_ref, kc_ref, vct_ref, gt_ref,
                 aux_ref, slope_ref, o_ref, ot_ref, mask_a_ref, mask_w_ref, *, seq):
    n = pl.program_id(1)
    t0 = n * BLOCK_Q
    n_str = seq // CMP_STRIDE
    n_cmp = n_str - CMP_LEN // CMP_STRIDE + 1
    n_sel = seq // SLC_BLOCK
    top_n = min(SLC_TOPN, n_sel)
    hd = HEAD_DIM
    gq = GROUP * BLOCK_Q
    ch = SLC_CHUNK
    span_a = A_WINDOW + BLOCK_Q
    span_w = B_WINDOW + BLOCK_Q

    lane = lax.broadcasted_iota(jnp.int32, (1, gq), 1)
    q_loc = lane & (BLOCK_Q - 1)
    lane_head = lane >> (BLOCK_Q.bit_length() - 1)
    t_q = t0 + q_loc

    def head_row(vals):
        return jnp.where(lane_head == 0, vals[0], jnp.where(lane_head == 1, vals[1],
                                                           jnp.where(lane_head == 2, vals[2], vals[3])))

    def q_group(first_head):
        return jnp.concatenate(
            [qt_ref[0, (first_head + g) * hd:(first_head + g + 1) * hd, :] for g in range(GROUP)], axis=1)

    def score_rhs(grp, kvh, qg, sel_rows):
        zero = jnp.zeros((hd, gq), BF16)
        q_rows = [qg, zero] if kvh == 0 else [zero, qg]
        return jnp.concatenate(q_rows + [sel_rows, slope_ref[grp]], axis=0)

    def score_lhs(k_ref, start, span):
        return jnp.concatenate([k_ref[0, pl.ds(start, span), :], aux_ref[pl.ds(start, span), :]], axis=1)

    def band_start(span):
        return pl.multiple_of(jnp.maximum(t0 - (span - BLOCK_Q), 0), BLOCK_Q)

    def band_mask(span, window):
        ik = lax.broadcasted_iota(jnp.int32, (span, gq), 0)
        dist = (t0 - band_start(span)) + q_loc - ik
        return jnp.where((dist >= 0) & (dist < window), 0.0, -MASK_BIG)

    @pl.when(n <= span_w // BLOCK_Q - 1)
    def _():
        mask_a_ref[...] = band_mask(span_a, A_WINDOW)
        mask_w_ref[...] = band_mask(span_w, B_WINDOW)

    no_sel = jnp.zeros((AUX_SEL, gq), BF16)
    n_grp = N_ATT_HEADS // GROUP
    slopes = [head_row([ALIBI_SLOPES[grp * GROUP + g] for g in range(GROUP)]) for grp in range(n_grp)]
    qgs = [q_group(grp * GROUP) for grp in range(n_grp)]

    def softmax_cols(s, sinks=None):
        m = jnp.max(s, axis=0, keepdims=True)
        if sinks is not None:
            m = jnp.maximum(m, sinks)
        e = jnp.exp2(s - m)
        den = jnp.sum(e, axis=0, keepdims=True)
        if sinks is not None:
            den = den + jnp.exp2(sinks - m)
        return e.astype(BF16), den

    ks_a = band_start(span_a)
    ks_w = band_start(span_w)
    lhs_a = score_lhs(ka_ref, ks_a, span_a)
    lhs_w = score_lhs(kw_ref, ks_w, span_w)
    s_a = [_dot(lhs_a, score_rhs(kvh, kvh, qgs[kvh], no_sel)) + mask_a_ref[...] for kvh in range(A_KV_HEADS)]
    s_w = [_dot(lhs_w, score_rhs(A_KV_HEADS + kvh, kvh, qgs[A_KV_HEADS + kvh], no_sel)) + mask_w_ref[...]
           for kvh in range(B_KV_HEADS)]

    c_idx = lax.broadcasted_iota(jnp.int32, (n_str, gq), 0)
    dist_c = t_q - (c_idx * CMP_STRIDE + CMP_LEN - 1)
    valid_c = (dist_c >= 0) & (c_idx < n_cmp)
    dist_cf = dist_c.astype(F32)
    s_c = [jnp.where(valid_c, _dot(kc_ref[0, kvh], qgs[A_KV_HEADS + kvh])
                     - (slopes[A_KV_HEADS + kvh] * LOG2E) * dist_cf, NEG_INF) for kvh in range(B_KV_HEADS)]

    oj = lax.broadcasted_iota(jnp.int32, (n_sel, n_str), 0) * SLC_BLOCK
    oc = lax.broadcasted_iota(jnp.int32, (n_sel, n_str), 1) * CMP_STRIDE
    overlap_t = ((oc < oj + SLC_BLOCK) & (oc + CMP_LEN > oj)).astype(BF16)
    j_idx = lax.broadcasted_iota(jnp.int32, (n_sel, BLOCK_Q), 0)
    t_blk = (t0 + lax.broadcasted_iota(jnp.int32, (1, BLOCK_Q), 1)) >> (SLC_BLOCK.bit_length() - 1)
    valid_j = j_idx <= t_blk
    forced_j = (j_idx == 0) | (j_idx == t_blk) | (j_idx == t_blk - 1)
    e_c, den_c, rhs_s = [], [], []
    for kvh in range(B_KV_HEADS):
        m = jnp.max(s_c[kvh], axis=0, keepdims=True)
        m = jnp.where(m == NEG_INF, 0.0, m)
        e = jnp.exp2(s_c[kvh] - m)
        den = jnp.sum(e, axis=0, keepdims=True)
        den = jnp.where(den > 0.0, den, 1.0)
        e_c.append(e.astype(BF16))
        den_c.append(den)
        p = e / den
        psum = p[:, 0:BLOCK_Q]
        for g in range(1, GROUP):
            psum = psum + p[:, g * BLOCK_Q:(g + 1) * BLOCK_Q]
        p_hi = psum.astype(BF16)
        p_lo = (psum - p_hi.astype(F32)).astype(BF16)
        imp = _dot(overlap_t, p_hi) + _dot(overlap_t, p_lo)
        rank = jnp.where(forced_j, RANK_FORCED, jnp.where(valid_j, imp, RANK_INVALID))
        cnt = jnp.zeros((n_sel, BLOCK_Q), jnp.int32)
        for jp in range(n_sel):
            row = rank[jp:jp + 1, :]
            before = (row > rank) | ((row == rank) & (j_idx > jp))
            cnt = cnt + before.astype(jnp.int32)
        sel_bias = jnp.where(cnt < top_n, 0.0, -MASK_BIG)
        if n_sel < AUX_SEL:
            sel_bias = jnp.concatenate([sel_bias, jnp.zeros((AUX_SEL - n_sel, BLOCK_Q), F32)], axis=0)
        sel_rows = jnp.concatenate([sel_bias.astype(BF16)] * GROUP, axis=1)
        rhs_s.append(score_rhs(A_KV_HEADS + kvh, kvh, qgs[A_KV_HEADS + kvh], sel_rows))

    t_qf = t_q.astype(F32)
    sinks = [(head_row([sink_ref[kvh * GROUP + g] for g in range(GROUP)]) + slopes[kvh] * t_qf) * LOG2E
             for kvh in range(A_KV_HEADS)]
    ed_a = [softmax_cols(s_a[kvh], sinks[kvh]) for kvh in range(A_KV_HEADS)]
    ed_w = [softmax_cols(s_w[kvh]) for kvh in range(B_KV_HEADS)]

    def rows(kvh):
        return slice(kvh * hd, (kvh + 1) * hd)

    o_a = [_dot(vat_ref[0, rows(kvh), pl.ds(ks_a, span_a)], ed_a[kvh][0]) / ed_a[kvh][1] for kvh in range(A_KV_HEADS)]
    o_w = [_dot(vwt_ref[0, rows(kvh), pl.ds(ks_w, span_w)], ed_w[kvh][0]) / ed_w[kvh][1] for kvh in range(B_KV_HEADS)]
    o_c = [_dot(vct_ref[0, kvh], e_c[kvh]) / den_c[kvh] for kvh in range(B_KV_HEADS)]
    for kvh in range(A_KV_HEADS):
        for g in range(GROUP):
            h = kvh * GROUP + g
            ot_ref[h * hd:(h + 1) * hd, :] = o_a[kvh][:, g * BLOCK_Q:(g + 1) * BLOCK_Q]

    dloc = q_loc - lax.broadcasted_iota(jnp.int32, (ch, gq), 0)
    n_full = t0 // ch

    def slc_scores(c):
        lhs = score_lhs(ks_ref, pl.multiple_of(c * ch, ch), ch)
        return tuple(_dot(lhs, rhs_s[kvh]) for kvh in range(B_KV_HEADS))

    def slc_update(c, state, s, kvh):
        m, l, acc = state
        m_new = jnp.maximum(m, jnp.max(s, axis=0, keepdims=True))
        alpha = jnp.exp2(m - m_new)
        e = jnp.exp2(s - m_new)
        l_new = alpha * l + jnp.sum(e, axis=0, keepdims=True)
        vt = vst_ref[0, rows(kvh), pl.ds(pl.multiple_of(c * ch, ch), ch)]
        return m_new, l_new, alpha * acc + _dot(vt, e.astype(BF16))

    def slc_step(c, carry):
        states, s = carry
        s_next = slc_scores(c + 1)
        return tuple(slc_update(c, states[kvh], s[kvh], kvh) for kvh in range(B_KV_HEADS)), s_next

    init = (jnp.full((1, gq), NEG_INF, F32), jnp.zeros((1, gq), F32), jnp.zeros((hd, gq), F32))
    states, s = lax.fori_loop(0, n_full, slc_step, ((init,) * B_KV_HEADS, slc_scores(0)))
    causal = dloc >= n_full * ch - t0
    gt = jax.nn.sigmoid(gt_ref[0])
    for kvh in range(B_KV_HEADS):
        _, l_fin, acc_fin = slc_update(n_full, states[kvh], jnp.where(causal, s[kvh], -MASK_BIG), kvh)
        o_slc = acc_fin / l_fin
        for g in range(GROUP):
            hb = kvh * GROUP + g
            h = A_HEADS + hb
            cols = slice(g * BLOCK_Q, (g + 1) * BLOCK_Q)
            ot_ref[h * hd:(h + 1) * hd, :] = (
                gt[hb:hb + 1] * o_c[kvh][:, cols]
                + gt[B_HEADS + hb:B_HEADS + hb + 1] * o_slc[:, cols]
                + gt[2 * B_HEADS + hb:2 * B_HEADS + hb + 1] * o_w[kvh][:, cols]
            )

    o_ref[0] = ot_ref[...].T.astype(BF16)


def _attention(sinks, tok, feat, kc, vct, gates_t, seq):
    bsz = tok.shape[0]
    n_str = seq // CMP_STRIDE
    kv_w = A_KV_HEADS * HEAD_DIM
    q_rows = N_ATT_HEADS * HEAD_DIM
    gq = GROUP * BLOCK_Q
    assert seq // SLC_BLOCK <= AUX_SEL and seq <= 128 * 256

    def k_spec(col_block):
        return pl.BlockSpec((1, seq, kv_w), lambda b, n: (b, 0, col_block))

    def vt_spec(row_block):
        return pl.BlockSpec((1, kv_w, seq), lambda b, n: (b, q_rows // kv_w + row_block, 0))

    return pl.pallas_call(
        functools.partial(_attn_kernel, seq=seq),
        grid=(bsz, seq // BLOCK_Q),
        in_specs=[
            pl.BlockSpec(memory_space=pltpu.SMEM),
            pl.BlockSpec((1, q_rows, BLOCK_Q), lambda b, n: (b, 0, n)),
            k_spec(0), k_spec(1), k_spec(2),
            vt_spec(0), vt_spec(1), vt_spec(2),
            pl.BlockSpec((1, B_KV_HEADS, n_str, HEAD_DIM), lambda b, n: (b, 0, 0, 0)),
            pl.BlockSpec((1, B_KV_HEADS, HEAD_DIM, n_str), lambda b, n: (b, 0, 0, 0)),
            pl.BlockSpec((1, GATE_ROWS, BLOCK_Q), lambda b, n: (b, 0, n)),
            pl.BlockSpec((seq, LANES), lambda b, n: (0, 0)),
            pl.BlockSpec((N_ATT_HEADS // GROUP, LANES - AUX_SEL, gq), lambda b, n: (0, 0, 0)),
        ],
        out_specs=pl.BlockSpec((1, BLOCK_Q, q_rows), lambda b, n: (b, n, 0)),
        out_shape=jax.ShapeDtypeStruct((bsz, seq, q_rows), BF16),
        scratch_shapes=[
            pltpu.VMEM((q_rows, BLOCK_Q), F32),
            pltpu.VMEM((A_WINDOW + BLOCK_Q, gq), F32),
            pltpu.VMEM((B_WINDOW + BLOCK_Q, gq), F32),
        ],
        compiler_params=_params(2, 32),
        name="hybrid_attention",
    )(sinks, feat, tok, tok, tok, feat, feat, feat, kc, vct, gates_t, _aux_key_lanes(seq), _aux_slope_rows())


def _proj_ln_kernel(o_ref, w_ref, x_ref, g_ref, b_ref, y_ref, *, alpha):
    y = _dot(o_ref[...], w_ref[...])
    y_ref[...] = _layer_norm(alpha * x_ref[...] + y, g_ref[...], b_ref[...])


def _proj_ln(o2d, w, x2d, g, b, alpha):
    n, k = o2d.shape
    row = pl.BlockSpec((ROW_TILE, D_MODEL), lambda i: (i, 0))
    vec = pl.BlockSpec((1, D_MODEL), lambda i: (0, 0))
    return pl.pallas_call(
        functools.partial(_proj_ln_kernel, alpha=alpha),
        grid=(n // ROW_TILE,),
        in_specs=[
            pl.BlockSpec((ROW_TILE, k), lambda i: (i, 0)),
            pl.BlockSpec((k, D_MODEL), lambda i: (0, 0)),
            row, vec, vec,
        ],
        out_specs=row,
        out_shape=jax.ShapeDtypeStruct((n, D_MODEL), F32),
        compiler_params=_params(1, 32),
        name="out_proj_ln",
    )(o2d, w, x2d, g, b)


def _swiglu_acc(x, wg, wu, wd, acc_ref, f):
    h = (_silu(_dot(x, wg)) * _dot(x, wu)).astype(BF16)
    y = _dot(h, wd)

    @pl.when(f == 0)
    def _():
        acc_ref[...] = y

    @pl.when(f != 0)
    def _():
        acc_ref[...] += y


def _ffn_ln_kernel(x_ref, wg_ref, wu_ref, wd_ref, g_ref, b_ref, y_ref, *, alpha):
    x = x_ref[...]
    xb = x.astype(BF16)
    d_ff = wg_ref.shape[1]
    y = None
    for lo in range(0, d_ff, FFN_CHUNK):
        hi = min(lo + FFN_CHUNK, d_ff)
        h = (_silu(_dot(xb, wg_ref[:, lo:hi])) * _dot(xb, wu_ref[:, lo:hi])).astype(BF16)
        part = _dot(h, wd_ref[lo:hi, :])
        y = part if y is None else y + part
    y_ref[...] = _layer_norm(alpha * x + y, g_ref[...], b_ref[...])


def _ffn_ln(x2d, wg, wu, wd, g, b, alpha):
    n = x2d.shape[0]
    d_ff = wg.shape[1]
    row = pl.BlockSpec((ROW_TILE, D_MODEL), lambda i: (i, 0))
    vec = pl.BlockSpec((1, D_MODEL), lambda i: (0, 0))
    once = pl.Buffered(1)
    return pl.pallas_call(
        functools.partial(_ffn_ln_kernel, alpha=alpha),
        grid=(n // ROW_TILE,),
        in_specs=[
            row,
            pl.BlockSpec((D_MODEL, d_ff), lambda i: (0, 0), pipeline_mode=once),
            pl.BlockSpec((D_MODEL, d_ff), lambda i: (0, 0), pipeline_mode=once),
            pl.BlockSpec((d_ff, D_MODEL), lambda i: (0, 0), pipeline_mode=once),
            vec, vec,
        ],
        out_specs=row,
        out_shape=jax.ShapeDtypeStruct((n, D_MODEL), F32),
        compiler_params=_params(1, 48),
        name="swiglu_ln",
    )(x2d, wg, wu, wd, g, b)


def _expert_kernel(be_ref, used_ref, x_ref, wg_ref, wu_ref, wd_ref, y_ref, xb_ref):
    del be_ref
    f = pl.program_id(1)
    in_use = pl.program_id(0) < used_ref[0]

    @pl.when(in_use)
    def _():
        @pl.when(f == 0)
        def _():
            xb_ref[...] = x_ref[...].astype(BF16)

        _swiglu_acc(xb_ref[...], wg_ref[0], wu_ref[0], wd_ref[0], y_ref, f)

    @pl.when(jnp.logical_not(in_use) & (f == 0))
    def _():
        y_ref[...] = jnp.zeros_like(y_ref)


def _experts(blk_e, n_used, xin, wg, wu, wd):
    n_rows = blk_e.shape[0] * MOE_ROWS
    d_ff = wg.shape[2]
    n_f = d_ff // EXPERT_TILE

    def row_blk(i, used):
        return jnp.minimum(i, used[0] - 1)

    def hid_blk(i, f, used):
        return jnp.where(i < used[0], f, n_f - 1)

    return pl.pallas_call(
        _expert_kernel,
        grid_spec=pltpu.PrefetchScalarGridSpec(
            num_scalar_prefetch=2,
            grid=(n_rows // MOE_ROWS, n_f),
            in_specs=[
                pl.BlockSpec((MOE_ROWS, D_MODEL), lambda i, f, be, used: (row_blk(i, used), 0)),
                pl.BlockSpec((1, D_MODEL, EXPERT_TILE), lambda i, f, be, used: (be[i], 0, hid_blk(i, f, used))),
                pl.BlockSpec((1, D_MODEL, EXPERT_TILE), lambda i, f, be, used: (be[i], 0, hid_blk(i, f, used))),
                pl.BlockSpec((1, EXPERT_TILE, D_MODEL), lambda i, f, be, used: (be[i], hid_blk(i, f, used), 0)),
            ],
            out_specs=pl.BlockSpec((MOE_ROWS, D_MODEL), lambda i, f, be, used: (i, 0)),
            scratch_shapes=[pltpu.VMEM((MOE_ROWS, D_MODEL), BF16)],
        ),
        out_shape=jax.ShapeDtypeStruct((n_rows, D_MODEL), F32),
        compiler_params=_params(2, 52),
        name="expert_swiglu",
    )(blk_e, n_used, xin, wg, wu, wd)


def _gmlp_kernel(x_ref, win_ref, lng_ref, lnb_ref, ws_ref, bs_ref, wout_ref, g_ref, b_ref, y_ref,
                 u_ref, vn_ref, gated_ref, *, alpha):
    x = x_ref[...]
    xb = x.astype(BF16)
    u_ref[...] = _gelu(_dot(xb, win_ref[:, :D_MODEL]))
    v = _gelu(_dot(xb, win_ref[:, D_MODEL:]))
    vn_ref[...] = _layer_norm(v, lng_ref[...], lnb_ref[...]).astype(BF16)
    cs = GMLP_CHUNK
    lower = lax.broadcasted_iota(jnp.int32, (cs, cs), 0) >= lax.broadcasted_iota(jnp.int32, (cs, cs), 1)
    for grp in range(GMLP_GROUPS):
        w = jnp.where(lower, ws_ref[grp], 0.0).astype(BF16)
        bias = bs_ref[grp]
        cols = slice(grp * cs, (grp + 1) * cs)
        for c in range(x.shape[0] // cs):
            rows = slice(c * cs, (c + 1) * cs)
            mixed = _dot(w, vn_ref[rows, cols]) + bias
            gated_ref[rows, cols] = (u_ref[rows, cols] * mixed).astype(BF16)
    y = _dot(gated_ref[...], wout_ref[...])
    y_ref[...] = _layer_norm(alpha * x + y, g_ref[...], b_ref[...])


def _gmlp_ln(x2d, w_in, ln_g, ln_b, w_s, b_s, w_out, g, b, alpha):
    n = x2d.shape[0]
    row = pl.BlockSpec((ROW_TILE, D_MODEL), lambda i: (i, 0))
    vec = pl.BlockSpec((1, D_MODEL), lambda i: (0, 0))
    grp = pl.BlockSpec((GMLP_GROUPS, GMLP_CHUNK, GMLP_CHUNK), lambda i: (0, 0, 0))
    return pl.pallas_call(
        functools.partial(_gmlp_kernel, alpha=alpha),
        grid=(n // ROW_TILE,),
        in_specs=[
            row,
            pl.BlockSpec((D_MODEL, 2 * D_MODEL), lambda i: (0, 0)),
            vec, vec, grp, grp,
            pl.BlockSpec((D_MODEL, D_MODEL), lambda i: (0, 0)),
            vec, vec,
        ],
        out_specs=row,
        out_shape=jax.ShapeDtypeStruct((n, D_MODEL), F32),
        scratch_shapes=[
            pltpu.VMEM((ROW_TILE, D_MODEL), F32),
            pltpu.VMEM((ROW_TILE, D_MODEL), BF16),
            pltpu.VMEM((ROW_TILE, D_MODEL), BF16),
        ],
        compiler_params=_params(1, 40),
        name="gmlp_ln",
    )(x2d, w_in, ln_g, ln_b, w_s, b_s, w_out, g, b)


def _route_kernel(x_ref, w_ref, gate_ref, et_ref, post_ref, cnt_ref, run_ref):
    tm = x_ref.shape[0]

    @pl.when(pl.program_id(0) == 0)
    def _():
        run_ref[...] = jnp.zeros_like(run_ref)

    logits = lax.dot_general(w_ref[...], x_ref[...], (((1,), (1,)), ((), ())),
                             preferred_element_type=F32, precision=lax.Precision.HIGHEST)
    row = lax.broadcasted_iota(jnp.int32, (N_EXPERTS, tm), 0)
    v1 = jnp.max(logits, axis=0, keepdims=True)
    i1 = jnp.min(jnp.where(logits == v1, row, N_EXPERTS), axis=0, keepdims=True)
    rest = jnp.where(row == i1, NEG_INF, logits)
    v2 = jnp.max(rest, axis=0, keepdims=True)
    i2 = jnp.min(jnp.where(rest == v2, row, N_EXPERTS), axis=0, keepdims=True)
    d = jnp.exp(v2 - v1)
    g1 = 1.0 / (1.0 + d)
    g2 = d / (1.0 + d)

    hit1 = row == i1
    hit2 = row == i2
    onehot = (hit1 | hit2).astype(BF16)
    earlier = (lax.broadcasted_iota(jnp.int32, (tm, tm), 0) < lax.broadcasted_iota(jnp.int32, (tm, tm), 1)).astype(BF16)
    before = _dot(onehot, earlier) + run_ref[:, 0:1]
    pos1 = jnp.sum(jnp.where(hit1, before, 0.0), axis=0, keepdims=True).astype(jnp.int32)
    pos2 = jnp.sum(jnp.where(hit2, before, 0.0), axis=0, keepdims=True).astype(jnp.int32)
    run_ref[...] += jnp.sum(onehot.astype(F32), axis=1, keepdims=True)

    et_ref[...] = jnp.where(row == 0, i1, jnp.where(row == 1, i2, 0))
    post_ref[...] = jnp.where(row == 0, pos1, jnp.where(row == 1, pos2, 0))
    cnt_ref[...] = run_ref[...]
    wide = lax.broadcasted_iota(jnp.int32, (LANES, tm), 0)
    gate_ref[...] = jnp.where(wide == 0, g1, jnp.where(wide == 1, g2, 0.0)).T


def _route(x2d, w_router_t):
    n = x2d.shape[0]
    tm = ROW_TILE
    per_tok = pl.BlockSpec((N_EXPERTS, tm), lambda i: (0, i))
    return pl.pallas_call(
        _route_kernel,
        grid=(n // tm,),
        in_specs=[
            pl.BlockSpec((tm, D_MODEL), lambda i: (i, 0)),
            pl.BlockSpec((N_EXPERTS, D_MODEL), lambda i: (0, 0)),
        ],
        out_specs=[
            pl.BlockSpec((tm, LANES), lambda i: (i, 0)),
            per_tok, per_tok,
            pl.BlockSpec((N_EXPERTS, LANES), lambda i: (0, 0)),
        ],
        out_shape=[
            jax.ShapeDtypeStruct((n, LANES), F32),
            jax.ShapeDtypeStruct((N_EXPERTS, n), jnp.int32),
            jax.ShapeDtypeStruct((N_EXPERTS, n), jnp.int32),
            jax.ShapeDtypeStruct((N_EXPERTS, LANES), F32),
        ],
        scratch_shapes=[pltpu.VMEM((N_EXPERTS, LANES), F32)],
        compiler_params=_params(1, 24),
        name="moe_router",
    )(x2d, w_router_t)


def _dispatch_kernel(d0_ref, d1_ref, lo_ref, hi_ref, x_ref, xin_hbm, zero_ref, sem_rows, sem_zero):
    step = pl.program_id(0)
    tm = x_ref.shape[0]
    base = step * tm

    def row_copy(r, dest_ref):
        return pltpu.make_async_copy(x_ref.at[pl.ds(r, 1)], xin_hbm.at[pl.ds(dest_ref[base + r], 1)], sem_rows)

    def issue(r8, carry):
        for j in range(DMA_UNROLL):
            r = r8 * DMA_UNROLL + j
            row_copy(r, d0_ref).start(priority=j % 2)
            row_copy(r, d1_ref).start(priority=(j + 1) % 2)
        return carry

    lax.fori_loop(0, tm // DMA_UNROLL, issue, 0)

    @pl.when(step == pl.num_programs(0) - 1)
    def _():
        zero_ref[...] = jnp.zeros_like(zero_ref)
        for k in range(N_EXPERTS + 1):
            def fill(r, carry):
                pltpu.make_async_copy(zero_ref, xin_hbm.at[pl.ds(r, 1)], sem_zero).start()
                return carry

            def drain(r, carry):
                pltpu.make_async_copy(zero_ref, xin_hbm.at[pl.ds(r, 1)], sem_zero).wait()
                return carry

            lax.fori_loop(lo_ref[k], hi_ref[k], fill, 0)
            lax.fori_loop(lo_ref[k], hi_ref[k], drain, 0)

    pltpu.make_async_copy(x_ref, xin_hbm.at[pl.ds(0, tm)], sem_rows).wait()
    pltpu.make_async_copy(x_ref, xin_hbm.at[pl.ds(0, tm)], sem_rows).wait()


def _dispatch(d0, d1, fill_lo, fill_hi, x2d, n_rows):
    n = x2d.shape[0]
    tm = ROW_TILE
    return pl.pallas_call(
        _dispatch_kernel,
        grid_spec=pltpu.PrefetchScalarGridSpec(
            num_scalar_prefetch=4,
            grid=(n // tm,),
            in_specs=[pl.BlockSpec((tm, D_MODEL), lambda i, *_: (i, 0))],
            out_specs=pl.BlockSpec(memory_space=pl.ANY),
            scratch_shapes=[pltpu.VMEM((1, D_MODEL), F32), pltpu.SemaphoreType.DMA(()), pltpu.SemaphoreType.DMA(())],
        ),
        out_shape=jax.ShapeDtypeStruct((n_rows, D_MODEL), F32),
        compiler_params=_params(1, 16),
        name="moe_dispatch",
    )(d0, d1, fill_lo, fill_hi, x2d)


def _combine_kernel(d0_ref, d1_ref, y_hbm, gate_ref, x_ref, g_ref, b_ref, o_ref, buf0, buf1, sem, *, alpha):
    tm = x_ref.shape[0]
    base = pl.program_id(0) * tm

    def issue(r8, carry):
        for j in range(DMA_UNROLL):
            r = r8 * DMA_UNROLL + j
            pltpu.make_async_copy(y_hbm.at[pl.ds(d0_ref[base + r], 1)], buf0.at[pl.ds(r, 1)], sem.at[0]).start(priority=0)
            pltpu.make_async_copy(y_hbm.at[pl.ds(d1_ref[base + r], 1)], buf1.at[pl.ds(r, 1)], sem.at[1]).start(priority=1)
        return carry

    lax.fori_loop(0, tm // DMA_UNROLL, issue, 0)
    pltpu.make_async_copy(y_hbm.at[pl.ds(0, tm)], buf0, sem.at[0]).wait()
    pltpu.make_async_copy(y_hbm.at[pl.ds(0, tm)], buf1, sem.at[1]).wait()
    gate = gate_ref[...]
    y = buf0[...] * gate[:, 0:1] + buf1[...] * gate[:, 1:2]
    o_ref[...] = _layer_norm(alpha * x_ref[...] + y, g_ref[...], b_ref[...])


def _combine_ln(d0, d1, yb, gate, x2d, g, b, alpha):
    n = x2d.shape[0]
    tm = GATHER_ROWS
    row = pl.BlockSpec((tm, D_MODEL), lambda i, a, c: (i, 0))
    vec = pl.BlockSpec((1, D_MODEL), lambda i, a, c: (0, 0))
    return pl.pallas_call(
        functools.partial(_combine_kernel, alpha=alpha),
        grid_spec=pltpu.PrefetchScalarGridSpec(
            num_scalar_prefetch=2,
            grid=(n // tm,),
            in_specs=[
                pl.BlockSpec(memory_space=pl.ANY),
                pl.BlockSpec((tm, LANES), lambda i, a, c: (i, 0)),
                row, vec, vec,
            ],
            out_specs=row,
            scratch_shapes=[
                pltpu.VMEM((tm, D_MODEL), F32),
                pltpu.VMEM((tm, D_MODEL), F32),
                pltpu.SemaphoreType.DMA((2,)),
            ],
        ),
        out_shape=jax.ShapeDtypeStruct((n, D_MODEL), F32),
        compiler_params=_params(1, 16),
        name="moe_combine_ln",
    )(d0, d1, yb, gate, x2d, g, b)


def _attention_layer(x2d, bsz, seq, w_in, sinks, pe_k, wk1, wk2, pe_v, wv1, wv2, w_o, g, b, alpha):
    aq, akv = A_HEADS * HEAD_DIM, A_KV_HEADS * HEAD_DIM
    bq, bkv = B_HEADS * HEAD_DIM, B_KV_HEADS * HEAD_DIM
    bounds = [0]
    for width in (aq, akv, akv, bq, bkv, bkv, bkv, bkv, bkv, bkv, 3 * B_HEADS):
        bounds.append(bounds[-1] + width)
    qa, ka, va, qb, kc, vc, ks, vs, kw, vw, wg = [w_in[:, bounds[i]:bounds[i + 1]] for i in range(11)]
    w_tok = jnp.concatenate([ka, ks, kw, kc, vc], axis=1).astype(BF16)
    w_feat = jnp.concatenate([qa * (SCALE * LOG2E), qb * (SCALE * LOG2E), va, vs, vw], axis=1).T.astype(BF16)
    wg = wg.reshape(D_MODEL, B_HEADS, 3).transpose(2, 1, 0).reshape(3 * B_HEADS, D_MODEL)
    wg = jnp.pad(wg, ((0, GATE_ROWS - 3 * B_HEADS), (0, 0))).astype(BF16)
    tok, x_cmp, feat, gates_t = _inproj(x2d, w_tok, w_feat, wg, bsz, seq)

    halves = CMP_LEN // CMP_STRIDE
    w1 = jnp.stack([wk1, wv1]).reshape(2, halves, CMP_STRIDE, 1, HEAD_DIM, CMP_HIDDEN)
    own_head = jnp.eye(B_KV_HEADS, dtype=F32)[:, None, None, None, :, None, None]
    w1 = (w1[None] * own_head).reshape(B_KV_HEADS, 2, halves, CMP_STRIDE * bkv, CMP_HIDDEN).astype(BF16)
    pe = jnp.stack([pe_k, pe_v]).reshape(2, halves, CMP_STRIDE, 1, HEAD_DIM)
    pe = jnp.broadcast_to(pe, (2, halves, CMP_STRIDE, B_KV_HEADS, HEAD_DIM)).reshape(2, halves, 1, CMP_STRIDE * bkv)
    pe = jnp.broadcast_to(pe, (2, halves, 8, CMP_STRIDE * bkv))
    kcmp, vcmp_t = _compress(x_cmp.reshape(bsz, seq, 2 * bkv), pe, w1, wk2.astype(BF16), wv2.T.astype(BF16))

    o = _attention(sinks, tok.reshape(bsz, seq, KEY_W), feat, kcmp, vcmp_t, gates_t, seq)
    return _proj_ln(o.reshape(bsz * seq, N_ATT_HEADS * HEAD_DIM), w_o.astype(BF16), x2d, g, b, alpha)


def _moe_layer(x2d, w_router, w_gate, w_up, w_down, g, b, alpha):
    n_tok = x2d.shape[0]
    gate, e_t, pos_t, cnt = _route(x2d, w_router.T)
    counts = cnt[:, 0].astype(jnp.int32)
    padded = (counts + MOE_ROWS - 1) // MOE_ROWS * MOE_ROWS
    pad_ends = jnp.cumsum(padded)
    pad_starts = pad_ends - padded
    n_blocks = n_tok * TOP_K // MOE_ROWS + N_EXPERTS
    n_rows = n_blocks * MOE_ROWS
    experts = jnp.arange(N_EXPERTS, dtype=jnp.int32)[:, None, None]
    start_of = jnp.sum(jnp.where(e_t[None, :TOP_K] == experts, pad_starts[:, None, None], 0), axis=0)
    dest = start_of + pos_t[:TOP_K]
    fill_lo = jnp.concatenate([pad_starts + counts, pad_ends[-1:]])
    fill_hi = jnp.concatenate([pad_ends, jnp.full((1,), n_rows, jnp.int32)])
    blk_start = jnp.arange(n_blocks, dtype=jnp.int32) * MOE_ROWS
    blk_e = jnp.sum((pad_ends[None, :] <= blk_start[:, None]).astype(jnp.int32), axis=1)
    blk_e = jnp.minimum(blk_e, N_EXPERTS - 1)

    xin = _dispatch(dest[0], dest[1], fill_lo, fill_hi, x2d, n_rows)
    n_used = (pad_ends[-1:] // MOE_ROWS).astype(jnp.int32)
    yb = _experts(blk_e, n_used, xin, w_gate.astype(BF16), w_up.astype(BF16), w_down.astype(BF16))
    return _combine_ln(dest[0], dest[1], yb, gate, x2d, g, b, alpha)


def kernel(x, att_w_in, att_sinks, cmp_pe_k, cmp_wk1, cmp_wk2, cmp_pe_v, cmp_wv1, cmp_wv2, att_w_o,
           ffn_w_gate, ffn_w_up, ffn_w_down, gmlp_w_in, gmlp_ln_g, gmlp_ln_b, gmlp_w_s, gmlp_b_s, gmlp_w_out,
           moe_w_router, moe_w_gate, moe_w_up, moe_w_down, ln_g, ln_b):
    bsz, seq, dm = x.shape
    depth = ln_g.shape[0]
    alpha = (2.0 * depth) ** 0.25
    assert dm == D_MODEL and seq % ROW_TILE == 0
    x2d = x.reshape(bsz * seq, dm)

    def vec(p):
        return p.reshape(1, -1)

    for i in range(depth):
        j = i // 2
        g0, b0, g1, b1 = vec(ln_g[i, 0]), vec(ln_b[i, 0]), vec(ln_g[i, 1]), vec(ln_b[i, 1])
        if i % 2 == 0:
            x2d = _attention_layer(x2d, bsz, seq, att_w_in[j], att_sinks[j], cmp_pe_k[j], cmp_wk1[j], cmp_wk2[j],
                                   cmp_pe_v[j], cmp_wv1[j], cmp_wv2[j], att_w_o[j], g0, b0, alpha)
            x2d = _ffn_ln(x2d, ffn_w_gate[j].astype(BF16), ffn_w_up[j].astype(BF16), ffn_w_down[j].astype(BF16),
                          g1, b1, alpha)
        else:
            b_s = jnp.broadcast_to(gmlp_b_s[j][:, :, None], (GMLP_GROUPS, GMLP_CHUNK, GMLP_CHUNK))
            x2d = _gmlp_ln(x2d, gmlp_w_in[j].astype(BF16), vec(gmlp_ln_g[j]), vec(gmlp_ln_b[j]), gmlp_w_s[j], b_s,
                           gmlp_w_out[j].astype(BF16), g0, b0, alpha)
            x2d = _moe_layer(x2d, moe_w_router[j], moe_w_gate[j], moe_w_up[j], moe_w_down[j], g1, b1, alpha)
    return x2d.reshape(bsz, seq, dm)
```

```python
import functools

import jax
import jax.numpy as jnp
import numpy as np
from jax import lax
from jax.experimental import pallas as pl
from jax.experimental.pallas import tpu as pltpu

F32 = jnp.float32
BF16 = jnp.bfloat16

D_MODEL = 1024
HEAD_DIM = 64
BLOCK_Q = 128
A_HEADS = 8
A_KV_HEADS = 2
A_WINDOW = 128
B_HEADS = 8
B_KV_HEADS = 2
GROUP = 4
CMP_LEN = 32
CMP_STRIDE = 16
CMP_HIDDEN = 256
SLC_BLOCK = 64
SLC_TOPN = 8
B_WINDOW = 256
N_ATT_HEADS = A_HEADS + B_HEADS
GMLP_GROUPS = 8
GMLP_CHUNK = 128
N_EXPERTS = 8
TOP_K = 2
LN_EPS = 1e-5
LANES = 128
DMA_UNROLL = 8

TOK_W = 640
KEY_W = 384
FEAT_W = 1408
GATE_ROWS = 32
SLC_CHUNK = 256

ROW_TILE = 512
MOE_ROWS = 512
GATHER_ROWS = 256
FFN_CHUNK = 1024
EXPERT_TILE = 1792

ALIBI_SLOPES = tuple(2.0 ** (-8.0 * h / N_ATT_HEADS) for h in range(1, N_ATT_HEADS + 1))
SCALE = HEAD_DIM ** -0.5
LOG2E = 1.4426950408889634
NEG_INF = float("-inf")
RANK_FORCED = 1e30
RANK_INVALID = -1.0
MASK_BIG = 1e30
AUX_SEL = 32


def _params(n_grid, vmem_mb):
    return pltpu.CompilerParams(
        dimension_semantics=("arbitrary",) * n_grid, vmem_limit_bytes=vmem_mb * 1024 * 1024
    )


def _dot(a, b):
    return jnp.dot(a, b, preferred_element_type=F32)


def _dot_nt(a, b):
    return lax.dot_general(a, b, (((1,), (1,)), ((), ())), preferred_element_type=F32)


def _layer_norm(z, g, b):
    mu = jnp.mean(z, axis=-1, keepdims=True)
    zc = z - mu
    var = jnp.mean(zc * zc, axis=-1, keepdims=True)
    return zc * lax.rsqrt(var + LN_EPS) * g + b


def _gelu(x):
    return 0.5 * x * (1.0 + jnp.tanh(0.7978845608028654 * (x + 0.044715 * (x * x * x))))


def _silu(x):
    return x / (1.0 + jnp.exp(-x))


def _inproj_kernel(x_ref, wt_ref, wf_ref, wg_ref, tok_ref, cmp_ref, feat_ref, gate_ref):
    x = x_ref[...].astype(BF16)
    tok = _dot(x, wt_ref[...])
    tok_ref[...] = tok[:, :KEY_W].astype(BF16)
    cmp_ref[...] = tok[:, KEY_W:]
    rows = FEAT_W // 4
    for c in range(4):
        feat_ref[0, c * rows:(c + 1) * rows, :] = _dot_nt(wf_ref[c * rows:(c + 1) * rows, :], x).astype(BF16)
    gate_ref[0] = _dot_nt(wg_ref[...], x)


def _inproj(x2d, w_tok, w_feat, w_gate, bsz, seq):
    n = x2d.shape[0]
    per_seq = seq // ROW_TILE
    return pl.pallas_call(
        _inproj_kernel,
        grid=(n // ROW_TILE,),
        in_specs=[
            pl.BlockSpec((ROW_TILE, D_MODEL), lambda i: (i, 0)),
            pl.BlockSpec((D_MODEL, TOK_W), lambda i: (0, 0)),
            pl.BlockSpec((FEAT_W, D_MODEL), lambda i: (0, 0)),
            pl.BlockSpec((GATE_ROWS, D_MODEL), lambda i: (0, 0)),
        ],
        out_specs=[
            pl.BlockSpec((ROW_TILE, KEY_W), lambda i: (i, 0)),
            pl.BlockSpec((ROW_TILE, TOK_W - KEY_W), lambda i: (i, 0)),
            pl.BlockSpec((1, FEAT_W, ROW_TILE), lambda i: (i // per_seq, 0, i % per_seq)),
            pl.BlockSpec((1, GATE_ROWS, ROW_TILE), lambda i: (i // per_seq, 0, i % per_seq)),
        ],
        out_shape=[
            jax.ShapeDtypeStruct((n, KEY_W), BF16),
            jax.ShapeDtypeStruct((n, TOK_W - KEY_W), F32),
            jax.ShapeDtypeStruct((bsz, FEAT_W, seq), BF16),
            jax.ShapeDtypeStruct((bsz, GATE_ROWS, seq), F32),
        ],
        compiler_params=_params(1, 32),
        name="inproj",
    )(x2d, w_tok, w_feat, w_gate)


def _compress_hidden(x_ref, which, pe_ref, w1_ref, n_str):
    xs = jnp.concatenate(
        [x_ref[0, pl.ds(p, n_str, stride=CMP_STRIDE), :] for p in range(CMP_STRIDE)], axis=1).astype(BF16)
    a = _dot(xs, w1_ref[0, which, 0])
    b = _dot(xs, w1_ref[0, which, 1])
    b_next = pltpu.roll(b, shift=n_str - 1, axis=0)
    bias = _dot(pe_ref[which, 0].astype(BF16), w1_ref[0, which, 0]) + _dot(pe_ref[which, 1].astype(BF16),
                                                                         w1_ref[0, which, 1])
    return _gelu(a + b_next + bias[0:1]).astype(BF16)


def _compress_kernel(xk_ref, xv_ref, pe_ref, w1_ref, w2k_ref, w2vt_ref, kc_ref, vct_ref):
    n_str = kc_ref.shape[2]
    kc_ref[0, 0] = _dot(_compress_hidden(xk_ref, 0, pe_ref, w1_ref, n_str), w2k_ref[...]).astype(BF16)
    vct_ref[0, 0] = _dot_nt(w2vt_ref[...], _compress_hidden(xv_ref, 1, pe_ref, w1_ref, n_str)).astype(BF16)


def _compress(x_cmp, pe, w1, w2k, w2vt):
    bsz, seq, width = x_cmp.shape
    n_str = seq // CMP_STRIDE

    def full(shape):
        return pl.BlockSpec(shape, lambda b, h: (0,) * len(shape))

    return pl.pallas_call(
        _compress_kernel,
        grid=(bsz, B_KV_HEADS),
        in_specs=[
            pl.BlockSpec((1, seq, LANES), lambda b, h: (b, 0, 0)),
            pl.BlockSpec((1, seq, LANES), lambda b, h: (b, 0, 1)),
            full(pe.shape),
            pl.BlockSpec((1,) + w1.shape[1:], lambda b, h: (h, 0, 0, 0, 0)),
            full((CMP_HIDDEN, HEAD_DIM)),
            full((HEAD_DIM, CMP_HIDDEN)),
        ],
        out_specs=[
            pl.BlockSpec((1, 1, n_str, HEAD_DIM), lambda b, h: (b, h, 0, 0)),
            pl.BlockSpec((1, 1, HEAD_DIM, n_str), lambda b, h: (b, h, 0, 0)),
        ],
        out_shape=[
            jax.ShapeDtypeStruct((bsz, B_KV_HEADS, n_str, HEAD_DIM), BF16),
            jax.ShapeDtypeStruct((bsz, B_KV_HEADS, HEAD_DIM, n_str), BF16),
        ],
        compiler_params=_params(2, 32),
        name="nsa_compress",
    )(x_cmp, x_cmp, pe, w1, w2k, w2vt)


def _bf16_parts(x, parts=3):
    out, rest = [], np.asarray(x, np.float32)
    for _ in range(parts):
        piece = rest.astype(jnp.bfloat16).astype(np.float32)
        out.append(piece)
        rest = rest - piece
    return out


def _aux_key_lanes(seq):
    t = np.arange(seq)
    aux = np.zeros((seq, LANES), np.float32)
    aux[t, t // SLC_BLOCK] = 1.0
    aux[:, AUX_SEL:AUX_SEL + 3] = (t >> 7)[:, None]
    aux[:, AUX_SEL + 3:AUX_SEL + 6] = (t & 127)[:, None]
    return jnp.asarray(aux, BF16)


def _aux_slope_rows():
    rows = np.zeros((N_ATT_HEADS // GROUP, LANES - AUX_SEL, GROUP * BLOCK_Q), np.float32)
    for grp in range(N_ATT_HEADS // GROUP):
        for g in range(GROUP):
            slope = ALIBI_SLOPES[grp * GROUP + g] * LOG2E
            cols = slice(g * BLOCK_Q, (g + 1) * BLOCK_Q)
            for i, piece in enumerate(_bf16_parts(128.0 * slope) + _bf16_parts(slope)):
                rows[grp, i, cols] = piece
    return jnp.asarray(rows, BF16)


def _attn_kernel(sink_ref, qt_ref, ka_ref, ks_ref, kw_ref, vat_ref, vst_ref, vwt_ref, kc_ref, vct_ref, gt_ref,
                 aux_ref, slope_ref, o_ref, ot_ref, mask_a_ref, mask_w_ref, s_ref, e_ref, *, seq):
    n = pl.program_id(1)
    t0 = n * BLOCK_Q
    n_str = seq // CMP_STRIDE
    n_cmp = n_str - CMP_LEN // CMP_STRIDE + 1
    n_sel = seq // SLC_BLOCK
    top_n = min(SLC_TOPN, n_sel)
    hd = HEAD_DIM
    gq = GROUP * BLOCK_Q
    ch = SLC_CHUNK
    span_a = A_WINDOW + BLOCK_Q
    span_w = B_WINDOW + BLOCK_Q

    lane = lax.broadcasted_iota(jnp.int32, (1, gq), 1)
    q_loc = lane & (BLOCK_Q - 1)
    lane_head = lane >> (BLOCK_Q.bit_length() - 1)
    t_q = t0 + q_loc

    def head_row(vals):
        return jnp.where(lane_head == 0, vals[0], jnp.where(lane_head == 1, vals[1],
                                                           jnp.where(lane_head == 2, vals[2], vals[3])))

    def q_group(first_head):
        return jnp.concatenate(
            [qt_ref[0, (first_head + g) * hd:(first_head + g + 1) * hd, :] for g in range(GROUP)], axis=1)

    def score_rhs(grp, kvh, qg, sel_rows):
        zero = jnp.zeros((hd, gq), BF16)
        q_rows = [qg, zero] if kvh == 0 else [zero, qg]
        return jnp.concatenate(q_rows + [sel_rows, slope_ref[grp]], axis=0)

    def score_lhs(k_ref, start, span):
        return jnp.concatenate([k_ref[0, pl.ds(start, span), :], aux_ref[pl.ds(start, span), :]], axis=1)

    def band_start(span):
        return pl.multiple_of(jnp.maximum(t0 - (span - BLOCK_Q), 0), BLOCK_Q)

    def band_mask(span, window):
        ik = lax.broadcasted_iota(jnp.int32, (span, gq), 0)
        dist = (t0 - band_start(span)) + q_loc - ik
        return jnp.where((dist >= 0) & (dist < window), 0.0, -MASK_BIG)

    @pl.when(n <= span_w // BLOCK_Q - 1)
    def _():
        mask_a_ref[...] = band_mask(span_a, A_WINDOW)
        mask_w_ref[...] = band_mask(span_w, B_WINDOW)

    no_sel = jnp.zeros((AUX_SEL, gq), BF16)
    n_grp = N_ATT_HEADS // GROUP
    slopes = [head_row([ALIBI_SLOPES[grp * GROUP + g] for g in range(GROUP)]) for grp in range(n_grp)]
    qgs = [q_group(grp * GROUP) for grp in range(n_grp)]

    def softmax_cols(s, sinks=None):
        m = jnp.max(s, axis=0, keepdims=True)
        if sinks is not None:
            m = jnp.maximum(m, sinks)
        e = jnp.exp2(s - m)
        den = jnp.sum(e, axis=0, keepdims=True)
        if sinks is not None:
            den = den + jnp.exp2(sinks - m)
        return e.astype(BF16), den

    ks_a = band_start(span_a)
    ks_w = band_start(span_w)
    lhs_a = score_lhs(ka_ref, ks_a, span_a)
    lhs_w = score_lhs(kw_ref, ks_w, span_w)
    s_a = [_dot(lhs_a, score_rhs(kvh, kvh, qgs[kvh], no_sel)) + mask_a_ref[...] for kvh in range(A_KV_HEADS)]
    s_w = [_dot(lhs_w, score_rhs(A_KV_HEADS + kvh, kvh, qgs[A_KV_HEADS + kvh], no_sel)) + mask_w_ref[...]
           for kvh in range(B_KV_HEADS)]

    c_idx = lax.broadcasted_iota(jnp.int32, (n_str, gq), 0)
    dist_c = t_q - (c_idx * CMP_STRIDE + CMP_LEN - 1)
    valid_c = (dist_c >= 0) & (c_idx < n_cmp)
    dist_cf = dist_c.astype(F32)
    s_c = [jnp.where(valid_c, _dot(kc_ref[0, kvh], qgs[A_KV_HEADS + kvh])
                     - (slopes[A_KV_HEADS + kvh] * LOG2E) * dist_cf, NEG_INF) for kvh in range(B_KV_HEADS)]

    oj = lax.broadcasted_iota(jnp.int32, (n_sel, n_str), 0) * SLC_BLOCK
    oc = lax.broadcasted_iota(jnp.int32, (n_sel, n_str), 1) * CMP_STRIDE
    overlap_t = ((oc < oj + SLC_BLOCK) & (oc + CMP_LEN > oj)).astype(BF16)
    j_idx = lax.broadcasted_iota(jnp.int32, (n_sel, BLOCK_Q), 0)
    t_blk = (t0 + lax.broadcasted_iota(jnp.int32, (1, BLOCK_Q), 1)) >> (SLC_BLOCK.bit_length() - 1)
    valid_j = j_idx <= t_blk
    forced_j = (j_idx == 0) | (j_idx == t_blk) | (j_idx == t_blk - 1)
    e_c, den_c, rhs_s = [], [], []
    for kvh in range(B_KV_HEADS):
        m = jnp.max(s_c[kvh], axis=0, keepdims=True)
        m = jnp.where(m == NEG_INF, 0.0, m)
        e = jnp.exp2(s_c[kvh] - m)
        den = jnp.sum(e, axis=0, keepdims=True)
        den = jnp.where(den > 0.0, den, 1.0)
        e_c.append(e.astype(BF16))
        den_c.append(den)
        p = e / den
        psum = p[:, 0:BLOCK_Q]
        for g in range(1, GROUP):
            psum = psum + p[:, g * BLOCK_Q:(g + 1) * BLOCK_Q]
        p_hi = psum.astype(BF16)
        p_lo = (psum - p_hi.astype(F32)).astype(BF16)
        imp = _dot(overlap_t, p_hi) + _dot(overlap_t, p_lo)
        rank = jnp.where(forced_j, RANK_FORCED, jnp.where(valid_j, imp, RANK_INVALID))
        cnt = jnp.zeros((n_sel, BLOCK_Q), jnp.int32)
        for jp in range(n_sel):
            row = rank[jp:jp + 1, :]
            before = (row > rank) | ((row == rank) & (j_idx > jp))
            cnt = cnt + before.astype(jnp.int32)
        sel_bias = jnp.where(cnt < top_n, 0.0, -MASK_BIG)
        if n_sel < AUX_SEL:
            sel_bias = jnp.concatenate([sel_bias, jnp.zeros((AUX_SEL - n_sel, BLOCK_Q), F32)], axis=0)
        sel_rows = jnp.concatenate([sel_bias.astype(BF16)] * GROUP, axis=1)
        rhs_s.append(score_rhs(A_KV_HEADS + kvh, kvh, qgs[A_KV_HEADS + kvh], sel_rows))

    t_qf = t_q.astype(F32)
    sinks = [(head_row([sink_ref[kvh * GROUP + g] for g in range(GROUP)]) + slopes[kvh] * t_qf) * LOG2E
             for kvh in range(A_KV_HEADS)]
    ed_a = [softmax_cols(s_a[kvh], sinks[kvh]) for kvh in range(A_KV_HEADS)]
    ed_w = [softmax_cols(s_w[kvh]) for kvh in range(B_KV_HEADS)]

    def rows(kvh):
        return slice(kvh * hd, (kvh + 1) * hd)

    o_a = [_dot(vat_ref[0, rows(kvh), pl.ds(ks_a, span_a)], ed_a[kvh][0]) / ed_a[kvh][1] for kvh in range(A_KV_HEADS)]
    o_w = [_dot(vwt_ref[0, rows(kvh), pl.ds(ks_w, span_w)], ed_w[kvh][0]) / ed_w[kvh][1] for kvh in range(B_KV_HEADS)]
    o_c = [_dot(vct_ref[0, kvh], e_c[kvh]) / den_c[kvh] for kvh in range(B_KV_HEADS)]
    for kvh in range(A_KV_HEADS):
        for g in range(GROUP):
            h = kvh * GROUP + g
            ot_ref[h * hd:(h + 1) * hd, :] = o_a[kvh][:, g * BLOCK_Q:(g + 1) * BLOCK_Q]

    dloc = q_loc - lax.broadcasted_iota(jnp.int32, (ch, gq), 0)
    n_full = t0 // ch

    def slc_scores(c):
        lhs = score_lhs(ks_ref, pl.multiple_of(c * ch, ch), ch)
        for kvh in range(B_KV_HEADS):
            s_ref[c & 1, kvh] = _dot(lhs, rhs_s[kvh])

    def slc_values(c, state, kvh):
        m, l, acc, alpha = state
        vt = vst_ref[0, rows(kvh), pl.ds(pl.multiple_of(jnp.maximum(c, 0) * ch, ch), ch)]
        return m, l, alpha * acc + _dot(vt, e_ref[kvh]), alpha

    def slc_softmax(c, state, kvh, causal=None):
        m, l, acc, _ = state
        s = s_ref[c & 1, kvh]
        if causal is not None:
            s = jnp.where(causal, s, -MASK_BIG)
        m_new = jnp.maximum(m, jnp.max(s, axis=0, keepdims=True))
        alpha = jnp.exp2(m - m_new)
        e = jnp.exp2(s - m_new)
        e_ref[kvh] = e.astype(BF16)
        return m_new, alpha * l + jnp.sum(e, axis=0, keepdims=True), acc, alpha

    def slc_step(c, states):
        states = tuple(slc_values(c - 1, states[kvh], kvh) for kvh in range(B_KV_HEADS))
        states = tuple(slc_softmax(c, states[kvh], kvh) for kvh in range(B_KV_HEADS))
        slc_scores(c + 1)
        return states

    e_ref[...] = jnp.zeros_like(e_ref)
    init = (jnp.full((1, gq), NEG_INF, F32), jnp.zeros((1, gq), F32), jnp.zeros((hd, gq), F32),
            jnp.ones((1, gq), F32))
    slc_scores(jnp.int32(0))
    states = lax.fori_loop(0, n_full, slc_step, (init,) * B_KV_HEADS)
    causal = dloc >= n_full * ch - t0
    gt = jax.nn.sigmoid(gt_ref[0])
    for kvh in range(B_KV_HEADS):
        state = slc_values(n_full - 1, states[kvh], kvh)
        state = slc_softmax(n_full, state, kvh, causal)
        _, l_fin, acc_fin, _ = slc_values(n_full, state, kvh)
        o_slc = acc_fin / l_fin
        for g in range(GROUP):
            hb = kvh * GROUP + g
            h = A_HEADS + hb
            cols = slice(g * BLOCK_Q, (g + 1) * BLOCK_Q)
            ot_ref[h * hd:(h + 1) * hd, :] = (
                gt[hb:hb + 1] * o_c[kvh][:, cols]
                + gt[B_HEADS + hb:B_HEADS + hb + 1] * o_slc[:, cols]
                + gt[2 * B_HEADS + hb:2 * B_HEADS + hb + 1] * o_w[kvh][:, cols]
            )

    o_ref[0] = ot_ref[...].T.astype(BF16)


def _attention(sinks, tok, feat, kc, vct, gates_t, seq):
    bsz = tok.shape[0]
    n_str = seq // CMP_STRIDE
    kv_w = A_KV_HEADS * HEAD_DIM
    q_rows = N_ATT_HEADS * HEAD_DIM
    gq = GROUP * BLOCK_Q
    assert seq // SLC_BLOCK <= AUX_SEL and seq <= 128 * 256

    def k_spec(col_block):
        return pl.BlockSpec((1, seq, kv_w), lambda b, n: (b, 0, col_block))

    def vt_spec(row_block):
        return pl.BlockSpec((1, kv_w, seq), lambda b, n: (b, q_rows // kv_w + row_block, 0))

    return pl.pallas_call(
        functools.partial(_attn_kernel, seq=seq),
        grid=(bsz, seq // BLOCK_Q),
        in_specs=[
            pl.BlockSpec(memory_space=pltpu.SMEM),
            pl.BlockSpec((1, q_rows, BLOCK_Q), lambda b, n: (b, 0, n)),
            k_spec(0), k_spec(1), k_spec(2),
            vt_spec(0), vt_spec(1), vt_spec(2),
            pl.BlockSpec((1, B_KV_HEADS, n_str, HEAD_DIM), lambda b, n: (b, 0, 0, 0)),
            pl.BlockSpec((1, B_KV_HEADS, HEAD_DIM, n_str), lambda b, n: (b, 0, 0, 0)),
            pl.BlockSpec((1, GATE_ROWS, BLOCK_Q), lambda b, n: (b, 0, n)),
            pl.BlockSpec((seq, LANES), lambda b, n: (0, 0)),
            pl.BlockSpec((N_ATT_HEADS // GROUP, LANES - AUX_SEL, gq), lambda b, n: (0, 0, 0)),
        ],
        out_specs=pl.BlockSpec((1, BLOCK_Q, q_rows), lambda b, n: (b, n, 0)),
        out_shape=jax.ShapeDtypeStruct((bsz, seq, q_rows), BF16),
        scratch_shapes=[
            pltpu.VMEM((q_rows, BLOCK_Q), F32),
            pltpu.VMEM((A_WINDOW + BLOCK_Q, gq), F32),
            pltpu.VMEM((B_WINDOW + BLOCK_Q, gq), F32),
            pltpu.VMEM((2, B_KV_HEADS, SLC_CHUNK, gq), F32),
            pltpu.VMEM((B_KV_HEADS, SLC_CHUNK, gq), BF16),
        ],
        compiler_params=_params(2, 32),
        name="hybrid_attention",
    )(sinks, feat, tok, tok, tok, feat, feat, feat, kc, vct, gates_t, _aux_key_lanes(seq), _aux_slope_rows())


def _proj_ln_kernel(o_ref, w_ref, x_ref, g_ref, b_ref, y_ref, *, alpha):
    y = _dot(o_ref[...], w_ref[...])
    y_ref[...] = _layer_norm(alpha * x_ref[...] + y, g_ref[...], b_ref[...])


def _proj_ln(o2d, w, x2d, g, b, alpha):
    n, k = o2d.shape
    row = pl.BlockSpec((ROW_TILE, D_MODEL), lambda i: (i, 0))
    vec = pl.BlockSpec((1, D_MODEL), lambda i: (0, 0))
    return pl.pallas_call(
        functools.partial(_proj_ln_kernel, alpha=alpha),
        grid=(n // ROW_TILE,),
        in_specs=[
            pl.BlockSpec((ROW_TILE, k), lambda i: (i, 0)),
            pl.BlockSpec((k, D_MODEL), lambda i: (0, 0)),
            row, vec, vec,
        ],
        out_specs=row,
        out_shape=jax.ShapeDtypeStruct((n, D_MODEL), F32),
        compiler_params=_params(1, 32),
        name="out_proj_ln",
    )(o2d, w, x2d, g, b)


def _swiglu_acc(x, wg, wu, wd, acc_ref, f):
    h = (_silu(_dot(x, wg)) * _dot(x, wu)).astype(BF16)
    y = _dot(h, wd)

    @pl.when(f == 0)
    def _():
        acc_ref[...] = y

    @pl.when(f != 0)
    def _():
        acc_ref[...] += y


def _ffn_ln_kernel(x_ref, wg_ref, wu_ref, wd_ref, g_ref, b_ref, y_ref, *, alpha):
    x = x_ref[...]
    xb = x.astype(BF16)
    d_ff = wg_ref.shape[1]
    y = None
    for lo in range(0, d_ff, FFN_CHUNK):
        hi = min(lo + FFN_CHUNK, d_ff)
        h = (_silu(_dot(xb, wg_ref[:, lo:hi])) * _dot(xb, wu_ref[:, lo:hi])).astype(BF16)
        part = _dot(h, wd_ref[lo:hi, :])
        y = part if y is None else y + part
    y_ref[...] = _layer_norm(alpha * x + y, g_ref[...], b_ref[...])


def _ffn_ln(x2d, wg, wu, wd, g, b, alpha):
    n = x2d.shape[0]
    d_ff = wg.shape[1]
    row = pl.BlockSpec((ROW_TILE, D_MODEL), lambda i: (i, 0))
    vec = pl.BlockSpec((1, D_MODEL), lambda i: (0, 0))
    once = pl.Buffered(1)
    return pl.pallas_call(
        functools.partial(_ffn_ln_kernel, alpha=alpha),
        grid=(n // ROW_TILE,),
        in_specs=[
            row,
            pl.BlockSpec((D_MODEL, d_ff), lambda i: (0, 0), pipeline_mode=once),
            pl.BlockSpec((D_MODEL, d_ff), lambda i: (0, 0), pipeline_mode=once),
            pl.BlockSpec((d_ff, D_MODEL), lambda i: (0, 0), pipeline_mode=once),
            vec, vec,
        ],
        out_specs=row,
        out_shape=jax.ShapeDtypeStruct((n, D_MODEL), F32),
        compiler_params=_params(1, 48),
        name="swiglu_ln",
    )(x2d, wg, wu, wd, g, b)


def _expert_kernel(be_ref, used_ref, x_ref, wg_ref, wu_ref, wd_ref, y_ref, xb_ref):
    del be_ref
    f = pl.program_id(1)
    in_use = pl.program_id(0) < used_ref[0]

    @pl.when(in_use)
    def _():
        @pl.when(f == 0)
        def _():
            xb_ref[...] = x_ref[...].astype(BF16)

        _swiglu_acc(xb_ref[...], wg_ref[0], wu_ref[0], wd_ref[0], y_ref, f)

    @pl.when(jnp.logical_not(in_use) & (f == 0))
    def _():
        y_ref[...] = jnp.zeros_like(y_ref)


def _experts(blk_e, n_used, xin, wg, wu, wd):
    n_rows = blk_e.shape[0] * MOE_ROWS
    d_ff = wg.shape[2]
    n_f = d_ff // EXPERT_TILE

    def row_blk(i, used):
        return jnp.minimum(i, used[0] - 1)

    def hid_blk(i, f, used):
        return jnp.where(i < used[0], f, n_f - 1)

    return pl.pallas_call(
        _expert_kernel,
        grid_spec=pltpu.PrefetchScalarGridSpec(
            num_scalar_prefetch=2,
            grid=(n_rows // MOE_ROWS, n_f),
            in_specs=[
                pl.BlockSpec((MOE_ROWS, D_MODEL), lambda i, f, be, used: (row_blk(i, used), 0)),
                pl.BlockSpec((1, D_MODEL, EXPERT_TILE), lambda i, f, be, used: (be[i], 0, hid_blk(i, f, used))),
                pl.BlockSpec((1, D_MODEL, EXPERT_TILE), lambda i, f, be, used: (be[i], 0, hid_blk(i, f, used))),
                pl.BlockSpec((1, EXPERT_TILE, D_MODEL), lambda i, f, be, used: (be[i], hid_blk(i, f, used), 0)),
            ],
            out_specs=pl.BlockSpec((MOE_ROWS, D_MODEL), lambda i, f, be, used: (i, 0)),
            scratch_shapes=[pltpu.VMEM((MOE_ROWS, D_MODEL), BF16)],
        ),
        out_shape=jax.ShapeDtypeStruct((n_rows, D_MODEL), F32),
        compiler_params=_params(2, 52),
        name="expert_swiglu",
    )(blk_e, n_used, xin, wg, wu, wd)


def _gmlp_kernel(x_ref, win_ref, lng_ref, lnb_ref, ws_ref, bs_ref, wout_ref, g_ref, b_ref, y_ref,
                 u_ref, vn_ref, gated_ref, *, alpha):
    x = x_ref[...]
    xb = x.astype(BF16)
    u_ref[...] = _gelu(_dot(xb, win_ref[:, :D_MODEL]))
    v = _gelu(_dot(xb, win_ref[:, D_MODEL:]))
    vn_ref[...] = _layer_norm(v, lng_ref[...], lnb_ref[...]).astype(BF16)
    cs = GMLP_CHUNK
    lower = lax.broadcasted_iota(jnp.int32, (cs, cs), 0) >= lax.broadcasted_iota(jnp.int32, (cs, cs), 1)
    for grp in range(GMLP_GROUPS):
        w = jnp.where(lower, ws_ref[grp], 0.0).astype(BF16)
        bias = bs_ref[grp]
        cols = slice(grp * cs, (grp + 1) * cs)
        for c in range(x.shape[0] // cs):
            rows = slice(c * cs, (c + 1) * cs)
            mixed = _dot(w, vn_ref[rows, cols]) + bias
            gated_ref[rows, cols] = (u_ref[rows, cols] * mixed).astype(BF16)
    y = _dot(gated_ref[...], wout_ref[...])
    y_ref[...] = _layer_norm(alpha * x + y, g_ref[...], b_ref[...])


def _gmlp_ln(x2d, w_in, ln_g, ln_b, w_s, b_s, w_out, g, b, alpha):
    n = x2d.shape[0]
    row = pl.BlockSpec((ROW_TILE, D_MODEL), lambda i: (i, 0))
    vec = pl.BlockSpec((1, D_MODEL), lambda i: (0, 0))
    grp = pl.BlockSpec((GMLP_GROUPS, GMLP_CHUNK, GMLP_CHUNK), lambda i: (0, 0, 0))
    return pl.pallas_call(
        functools.partial(_gmlp_kernel, alpha=alpha),
        grid=(n // ROW_TILE,),
        in_specs=[
            row,
            pl.BlockSpec((D_MODEL, 2 * D_MODEL), lambda i: (0, 0)),
            vec, vec, grp, grp,
            pl.BlockSpec((D_MODEL, D_MODEL), lambda i: (0, 0)),
            vec, vec,
        ],
        out_specs=row,
        out_shape=jax.ShapeDtypeStruct((n, D_MODEL), F32),
        scratch_shapes=[
            pltpu.VMEM((ROW_TILE, D_MODEL), F32),
            pltpu.VMEM((ROW_TILE, D_MODEL), BF16),
            pltpu.VMEM((ROW_TILE, D_MODEL), BF16),
        ],
        compiler_params=_params(1, 40),
        name="gmlp_ln",
    )(x2d, w_in, ln_g, ln_b, w_s, b_s, w_out, g, b)


def _route_kernel(x_ref, w_ref, gate_ref, et_ref, post_ref, cnt_ref, run_ref):
    tm = x_ref.shape[0]

    @pl.when(pl.program_id(0) == 0)
    def _():
        run_ref[...] = jnp.zeros_like(run_ref)

    logits = lax.dot_general(w_ref[...], x_ref[...], (((1,), (1,)), ((), ())),
                             preferred_element_type=F32, precision=lax.Precision.HIGHEST)
    row = lax.broadcasted_iota(jnp.int32, (N_EXPERTS, tm), 0)
    v1 = jnp.max(logits, axis=0, keepdims=True)
    i1 = jnp.min(jnp.where(logits == v1, row, N_EXPERTS), axis=0, keepdims=True)
    rest = jnp.where(row == i1, NEG_INF, logits)
    v2 = jnp.max(rest, axis=0, keepdims=True)
    i2 = jnp.min(jnp.where(rest == v2, row, N_EXPERTS), axis=0, keepdims=True)
    d = jnp.exp(v2 - v1)
    g1 = 1.0 / (1.0 + d)
    g2 = d / (1.0 + d)

    hit1 = row == i1
    hit2 = row == i2
    onehot = (hit1 | hit2).astype(BF16)
    earlier = (lax.broadcasted_iota(jnp.int32, (tm, tm), 0) < lax.broadcasted_iota(jnp.int32, (tm, tm), 1)).astype(BF16)
    before = _dot(onehot, earlier) + run_ref[:, 0:1]
    pos1 = jnp.sum(jnp.where(hit1, before, 0.0), axis=0, keepdims=True).astype(jnp.int32)
    pos2 = jnp.sum(jnp.where(hit2, before, 0.0), axis=0, keepdims=True).astype(jnp.int32)
    run_ref[...] += jnp.sum(onehot.astype(F32), axis=1, keepdims=True)

    et_ref[...] = jnp.where(row == 0, i1, jnp.where(row == 1, i2, 0))
    post_ref[...] = jnp.where(row == 0, pos1, jnp.where(row == 1, pos2, 0))
    cnt_ref[...] = run_ref[...]
    wide = lax.broadcasted_iota(jnp.int32, (LANES, tm), 0)
    gate_ref[...] = jnp.where(wide == 0, g1, jnp.where(wide == 1, g2, 0.0)).T


def _route(x2d, w_router_t):
    n = x2d.shape[0]
    tm = ROW_TILE
    per_tok = pl.BlockSpec((N_EXPERTS, tm), lambda i: (0, i))
    return pl.pallas_call(
        _route_kernel,
        grid=(n // tm,),
        in_specs=[
            pl.BlockSpec((tm, D_MODEL), lambda i: (i, 0)),
            pl.BlockSpec((N_EXPERTS, D_MODEL), lambda i: (0, 0)),
        ],
        out_specs=[
            pl.BlockSpec((tm, LANES), lambda i: (i, 0)),
            per_tok, per_tok,
            pl.BlockSpec((N_EXPERTS, LANES), lambda i: (0, 0)),
        ],
        out_shape=[
            jax.ShapeDtypeStruct((n, LANES), F32),
            jax.ShapeDtypeStruct((N_EXPERTS, n), jnp.int32),
            jax.ShapeDtypeStruct((N_EXPERTS, n), jnp.int32),
            jax.ShapeDtypeStruct((N_EXPERTS, LANES), F32),
        ],
        scratch_shapes=[pltpu.VMEM((N_EXPERTS, LANES), F32)],
        compiler_params=_params(1, 24),
        name="moe_router",
    )(x2d, w_router_t)


def _dispatch_kernel(d0_ref, d1_ref, lo_ref, hi_ref, x_ref, xin_hbm, zero_ref, sem_rows, sem_zero):
    step = pl.program_id(0)
    tm = x_ref.shape[0]
    base = step * tm

    def row_copy(r, dest_ref):
        return pltpu.make_async_copy(x_ref.at[pl.ds(r, 1)], xin_hbm.at[pl.ds(dest_ref[base + r], 1)], sem_rows)

    def issue(r8, carry):
        for j in range(DMA_UNROLL):
            r = r8 * DMA_UNROLL + j
            row_copy(r, d0_ref).start(priority=j % 2)
            row_copy(r, d1_ref).start(priority=(j + 1) % 2)
        return carry

    lax.fori_loop(0, tm // DMA_UNROLL, issue, 0)

    @pl.when(step == pl.num_programs(0) - 1)
    def _():
        zero_ref[...] = jnp.zeros_like(zero_ref)
        for k in range(N_EXPERTS + 1):
            def fill(r, carry):
                pltpu.make_async_copy(zero_ref, xin_hbm.at[pl.ds(r, 1)], sem_zero).start()
                return carry

            def drain(r, carry):
                pltpu.make_async_copy(zero_ref, xin_hbm.at[pl.ds(r, 1)], sem_zero).wait()
                return carry

            lax.fori_loop(lo_ref[k], hi_ref[k], fill, 0)
            lax.fori_loop(lo_ref[k], hi_ref[k], drain, 0)

    pltpu.make_async_copy(x_ref, xin_hbm.at[pl.ds(0, tm)], sem_rows).wait()
    pltpu.make_async_copy(x_ref, xin_hbm.at[pl.ds(0, tm)], sem_rows).wait()


def _dispatch(d0, d1, fill_lo, fill_hi, x2d, n_rows):
    n = x2d.shape[0]
    tm = ROW_TILE
    return pl.pallas_call(
        _dispatch_kernel,
        grid_spec=pltpu.PrefetchScalarGridSpec(
            num_scalar_prefetch=4,
            grid=(n // tm,),
            in_specs=[pl.BlockSpec((tm, D_MODEL), lambda i, *_: (i, 0))],
            out_specs=pl.BlockSpec(memory_space=pl.ANY),
            scratch_shapes=[pltpu.VMEM((1, D_MODEL), F32), pltpu.SemaphoreType.DMA(()), pltpu.SemaphoreType.DMA(())],
        ),
        out_shape=jax.ShapeDtypeStruct((n_rows, D_MODEL), F32),
        compiler_params=_params(1, 16),
        name="moe_dispatch",
    )(d0, d1, fill_lo, fill_hi, x2d)


def _combine_kernel(d0_ref, d1_ref, y_hbm, gate_ref, x_ref, g_ref, b_ref, o_ref, buf0, buf1, sem, *, alpha):
    tm = x_ref.shape[0]
    base = pl.program_id(0) * tm

    def issue(r8, carry):
        for j in range(DMA_UNROLL):
            r = r8 * DMA_UNROLL + j
            pltpu.make_async_copy(y_hbm.at[pl.ds(d0_ref[base + r], 1)], buf0.at[pl.ds(r, 1)], sem.at[0]).start(priority=0)
            pltpu.make_async_copy(y_hbm.at[pl.ds(d1_ref[base + r], 1)], buf1.at[pl.ds(r, 1)], sem.at[1]).start(priority=1)
        return carry

    lax.fori_loop(0, tm // DMA_UNROLL, issue, 0)
    pltpu.make_async_copy(y_hbm.at[pl.ds(0, tm)], buf0, sem.at[0]).wait()
    pltpu.make_async_copy(y_hbm.at[pl.ds(0, tm)], buf1, sem.at[1]).wait()
    gate = gate_ref[...]
    y = buf0[...] * gate[:, 0:1] + buf1[...] * gate[:, 1:2]
    o_ref[...] = _layer_norm(alpha * x_ref[...] + y, g_ref[...], b_ref[...])


def _combine_ln(d0, d1, yb, gate, x2d, g, b, alpha):
    n = x2d.shape[0]
    tm = GATHER_ROWS
    row = pl.BlockSpec((tm, D_MODEL), lambda i, a, c: (i, 0))
    vec = pl.BlockSpec((1, D_MODEL), lambda i, a, c: (0, 0))
    return pl.pallas_call(
        functools.partial(_combine_kernel, alpha=alpha),
        grid_spec=pltpu.PrefetchScalarGridSpec(
            num_scalar_prefetch=2,
            grid=(n // tm,),
            in_specs=[
                pl.BlockSpec(memory_space=pl.ANY),
                pl.BlockSpec((tm, LANES), lambda i, a, c: (i, 0)),
                row, vec, vec,
            ],
            out_specs=row,
            scratch_shapes=[
                pltpu.VMEM((tm, D_MODEL), F32),
                pltpu.VMEM((tm, D_MODEL), F32),
                pltpu.SemaphoreType.DMA((2,)),
            ],
        ),
        out_shape=jax.ShapeDtypeStruct((n, D_MODEL), F32),
        compiler_params=_params(1, 16),
        name="moe_combine_ln",
    )(d0, d1, yb, gate, x2d, g, b)


def _attention_layer(x2d, bsz, seq, w_in, sinks, pe_k, wk1, wk2, pe_v, wv1, wv2, w_o, g, b, alpha):
    aq, akv = A_HEADS * HEAD_DIM, A_KV_HEADS * HEAD_DIM
    bq, bkv = B_HEADS * HEAD_DIM, B_KV_HEADS * HEAD_DIM
    bounds = [0]
    for width in (aq, akv, akv, bq, bkv, bkv, bkv, bkv, bkv, bkv, 3 * B_HEADS):
        bounds.append(bounds[-1] + width)
    qa, ka, va, qb, kc, vc, ks, vs, kw, vw, wg = [w_in[:, bounds[i]:bounds[i + 1]] for i in range(11)]
    w_tok = jnp.concatenate([ka, ks, kw, kc, vc], axis=1).astype(BF16)
    w_feat = jnp.concatenate([qa * (SCALE * LOG2E), qb * (SCALE * LOG2E), va, vs, vw], axis=1).T.astype(BF16)
    wg = wg.reshape(D_MODEL, B_HEADS, 3).transpose(2, 1, 0).reshape(3 * B_HEADS, D_MODEL)
    wg = jnp.pad(wg, ((0, GATE_ROWS - 3 * B_HEADS), (0, 0))).astype(BF16)
    tok, x_cmp, feat, gates_t = _inproj(x2d, w_tok, w_feat, wg, bsz, seq)

    halves = CMP_LEN // CMP_STRIDE
    w1 = jnp.stack([wk1, wv1]).reshape(2, halves, CMP_STRIDE, 1, HEAD_DIM, CMP_HIDDEN)
    own_head = jnp.eye(B_KV_HEADS, dtype=F32)[:, None, None, None, :, None, None]
    w1 = (w1[None] * own_head).reshape(B_KV_HEADS, 2, halves, CMP_STRIDE * bkv, CMP_HIDDEN).astype(BF16)
    pe = jnp.stack([pe_k, pe_v]).reshape(2, halves, CMP_STRIDE, 1, HEAD_DIM)
    pe = jnp.broadcast_to(pe, (2, halves, CMP_STRIDE, B_KV_HEADS, HEAD_DIM)).reshape(2, halves, 1, CMP_STRIDE * bkv)
    pe = jnp.broadcast_to(pe, (2, halves, 8, CMP_STRIDE * bkv))
    kcmp, vcmp_t = _compress(x_cmp.reshape(bsz, seq, 2 * bkv), pe, w1, wk2.astype(BF16), wv2.T.astype(BF16))

    o = _attention(sinks, tok.reshape(bsz, seq, KEY_W), feat, kcmp, vcmp_t, gates_t, seq)
    return _proj_ln(o.reshape(bsz * seq, N_ATT_HEADS * HEAD_DIM), w_o.astype(BF16), x2d, g, b, alpha)


def _moe_layer(x2d, w_router, w_gate, w_up, w_down, g, b, alpha):
    n_tok = x2d.shape[0]
    gate, e_t, pos_t, cnt = _route(x2d, w_router.T)
    counts = cnt[:, 0].astype(jnp.int32)
    padded = (counts + MOE_ROWS - 1) // MOE_ROWS * MOE_ROWS
    pad_ends = jnp.cumsum(padded)
    pad_starts = pad_ends - padded
    n_blocks = n_tok * TOP_K // MOE_ROWS + N_EXPERTS
    n_rows = n_blocks * MOE_ROWS
    experts = jnp.arange(N_EXPERTS, dtype=jnp.int32)[:, None, None]
    start_of = jnp.sum(jnp.where(e_t[None, :TOP_K] == experts, pad_starts[:, None, None], 0), axis=0)
    dest = start_of + pos_t[:TOP_K]
    fill_lo = jnp.concatenate([pad_starts + counts, pad_ends[-1:]])
    fill_hi = jnp.concatenate([pad_ends, jnp.full((1,), n_rows, jnp.int32)])
    blk_start = jnp.arange(n_blocks, dtype=jnp.int32) * MOE_ROWS
    blk_e = jnp.sum((pad_ends[None, :] <= blk_start[:, None]).astype(jnp.int32), axis=1)
    blk_e = jnp.minimum(blk_e, N_EXPERTS - 1)

    xin = _dispatch(dest[0], dest[1], fill_lo, fill_hi, x2d, n_rows)
    n_used = (pad_ends[-1:] // MOE_ROWS).astype(jnp.int32)
    yb = _experts(blk_e, n_used, xin, w_gate.astype(BF16), w_up.astype(BF16), w_down.astype(BF16))
    return _combine_ln(dest[0], dest[1], yb, gate, x2d, g, b, alpha)


def kernel(x, att_w_in, att_sinks, cmp_pe_k, cmp_wk1, cmp_wk2, cmp_pe_v, cmp_wv1, cmp_wv2, att_w_o,
           ffn_w_gate, ffn_w_up, ffn_w_down, gmlp_w_in, gmlp_ln_g, gmlp_ln_b, gmlp_w_s, gmlp_b_s, gmlp_w_out,
           moe_w_router, moe_w_gate, moe_w_up, moe_w_down, ln_g, ln_b):
    bsz, seq, dm = x.shape
    depth = ln_g.shape[0]
    alpha = (2.0 * depth) ** 0.25
    assert dm == D_MODEL and seq % ROW_TILE == 0
    x2d = x.reshape(bsz * seq, dm)

    def vec(p):
        return p.reshape(1, -1)

    for i in range(depth):
        j = i // 2
        g0, b0, g1, b1 = vec(ln_g[i, 0]), vec(ln_b[i, 0]), vec(ln_g[i, 1]), vec(ln_b[i, 1])
        if i % 2 == 0:
            x2d = _attention_layer(x2d, bsz, seq, att_w_in[j], att_sinks[j], cmp_pe_k[j], cmp_wk1[j], cmp_wk2[j],
                                   cmp_pe_v[j], cmp_wv1[j], cmp_wv2[j], att_w_o[j], g0, b0, alpha)
            x2d = _ffn_ln(x2d, ffn_w_gate[j].astype(BF16), ffn_w_up[j].astype(BF16), ffn_w_down[j].astype(BF16),
                          g1, b1, alpha)
        else:
            b_s = jnp.broadcast_to(gmlp_b_s[j][:, :, None], (GMLP_GROUPS, GMLP_CHUNK, GMLP_CHUNK))
            x2d = _gmlp_ln(x2d, gmlp_w_in[j].astype(BF16), vec(gmlp_ln_g[j]), vec(gmlp_ln_b[j]), gmlp_w_s[j], b_s,
                           gmlp_w_out[j].astype(BF16), g0, b0, alpha)
            x2d = _moe_layer(x2d, moe_w_router[j], moe_w_gate[j], moe_w_up[j], moe_w_down[j], g1, b1, alpha)
    return x2d.reshape(bsz, seq, dm)
```

```python
import functools

import jax
import jax.numpy as jnp
import numpy as np
from jax import lax
from jax.experimental import pallas as pl
from jax.experimental.pallas import tpu as pltpu

F32 = jnp.float32
BF16 = jnp.bfloat16

D_MODEL = 1024
HEAD_DIM = 64
BLOCK_Q = 128
A_HEADS = 8
A_KV_HEADS = 2
A_WINDOW = 128
B_HEADS = 8
B_KV_HEADS = 2
GROUP = 4
CMP_LEN = 32
CMP_STRIDE = 16
CMP_HIDDEN = 256
SLC_BLOCK = 64
SLC_TOPN = 8
B_WINDOW = 256
N_ATT_HEADS = A_HEADS + B_HEADS
GMLP_GROUPS = 8
GMLP_CHUNK = 128
N_EXPERTS = 8
TOP_K = 2
LN_EPS = 1e-5
LANES = 128
MXU_WIDTH = 256
DMA_UNROLL = 8

TOK_W = 640
KEY_W = 384
FEAT_W = 1408
GATE_ROWS = 32
SLC_CHUNK = 256

ROW_TILE = 512
MOE_ROWS = 512
GATHER_ROWS = 256
FFN_CHUNK = 1024
EXPERT_TILE = 1792

ALIBI_SLOPES = tuple(2.0 ** (-8.0 * h / N_ATT_HEADS) for h in range(1, N_ATT_HEADS + 1))
SCALE = HEAD_DIM ** -0.5
LOG2E = 1.4426950408889634
NEG_INF = float("-inf")
RANK_FORCED = 1e30
RANK_INVALID = -1.0
MASK_BIG = 1e30
AUX_SEL = 32


def _params(n_grid, vmem_mb):
    return pltpu.CompilerParams(
        dimension_semantics=("arbitrary",) * n_grid, vmem_limit_bytes=vmem_mb * 1024 * 1024
    )


def _dot(a, b):
    return jnp.dot(a, b, preferred_element_type=F32)


def _dot_nt(a, b):
    return lax.dot_general(a, b, (((1,), (1,)), ((), ())), preferred_element_type=F32)


def _layer_norm(z, g, b):
    mu = jnp.mean(z, axis=-1, keepdims=True)
    zc = z - mu
    var = jnp.mean(zc * zc, axis=-1, keepdims=True)
    return zc * lax.rsqrt(var + LN_EPS) * g + b


def _gelu(x):
    return 0.5 * x * (1.0 + jnp.tanh(0.7978845608028654 * (x + 0.044715 * (x * x * x))))


def _silu(x):
    return x / (1.0 + jnp.exp(-x))


def _inproj_kernel(x_ref, wt_ref, wf_ref, wg_ref, tok_ref, cmp_ref, feat_ref, gate_ref):
    x = x_ref[...].astype(BF16)
    tok = _dot(x, wt_ref[...])
    tok_ref[...] = tok[:, :KEY_W].astype(BF16)
    cmp_ref[...] = tok[:, KEY_W:]
    rows = FEAT_W // 4
    for c in range(4):
        feat_ref[0, c * rows:(c + 1) * rows, :] = _dot_nt(wf_ref[c * rows:(c + 1) * rows, :], x).astype(BF16)
    gate_ref[0] = _dot_nt(wg_ref[...], x)


def _inproj(x2d, w_tok, w_feat, w_gate, bsz, seq):
    n = x2d.shape[0]
    per_seq = seq // ROW_TILE
    return pl.pallas_call(
        _inproj_kernel,
        grid=(n // ROW_TILE,),
        in_specs=[
            pl.BlockSpec((ROW_TILE, D_MODEL), lambda i: (i, 0)),
            pl.BlockSpec((D_MODEL, TOK_W), lambda i: (0, 0)),
            pl.BlockSpec((FEAT_W, D_MODEL), lambda i: (0, 0)),
            pl.BlockSpec((GATE_ROWS, D_MODEL), lambda i: (0, 0)),
        ],
        out_specs=[
            pl.BlockSpec((ROW_TILE, KEY_W), lambda i: (i, 0)),
            pl.BlockSpec((ROW_TILE, TOK_W - KEY_W), lambda i: (i, 0)),
            pl.BlockSpec((1, FEAT_W, ROW_TILE), lambda i: (i // per_seq, 0, i % per_seq)),
            pl.BlockSpec((1, GATE_ROWS, ROW_TILE), lambda i: (i // per_seq, 0, i % per_seq)),
        ],
        out_shape=[
            jax.ShapeDtypeStruct((n, KEY_W), BF16),
            jax.ShapeDtypeStruct((n, TOK_W - KEY_W), F32),
            jax.ShapeDtypeStruct((bsz, FEAT_W, seq), BF16),
            jax.ShapeDtypeStruct((bsz, GATE_ROWS, seq), F32),
        ],
        compiler_params=_params(1, 32),
        name="inproj",
    )(x2d, w_tok, w_feat, w_gate)


def _compress_hidden(x_ref, which, pe_ref, w1_ref, n_str):
    xs = jnp.concatenate(
        [x_ref[0, pl.ds(p, n_str, stride=CMP_STRIDE), :] for p in range(CMP_STRIDE)], axis=1).astype(BF16)
    a = _dot(xs, w1_ref[0, which, 0])
    b = _dot(xs, w1_ref[0, which, 1])
    b_next = pltpu.roll(b, shift=n_str - 1, axis=0)
    bias = _dot(pe_ref[which, 0].astype(BF16), w1_ref[0, which, 0]) + _dot(pe_ref[which, 1].astype(BF16),
                                                                         w1_ref[0, which, 1])
    return _gelu(a + b_next + bias[0:1]).astype(BF16)


def _compress_kernel(xk_ref, xv_ref, pe_ref, w1_ref, w2k_ref, w2vt_ref, kc_ref, vct_ref):
    n_str = kc_ref.shape[2]
    kc_ref[0, 0] = _dot(_compress_hidden(xk_ref, 0, pe_ref, w1_ref, n_str), w2k_ref[...]).astype(BF16)
    vct_ref[0, 0] = _dot_nt(w2vt_ref[...], _compress_hidden(xv_ref, 1, pe_ref, w1_ref, n_str)).astype(BF16)


def _compress(x_cmp, pe, w1, w2k, w2vt):
    bsz, seq, width = x_cmp.shape
    n_str = seq // CMP_STRIDE

    def full(shape):
        return pl.BlockSpec(shape, lambda b, h: (0,) * len(shape))

    return pl.pallas_call(
        _compress_kernel,
        grid=(bsz, B_KV_HEADS),
        in_specs=[
            pl.BlockSpec((1, seq, LANES), lambda b, h: (b, 0, 0)),
            pl.BlockSpec((1, seq, LANES), lambda b, h: (b, 0, 1)),
            full(pe.shape),
            pl.BlockSpec((1,) + w1.shape[1:], lambda b, h: (h, 0, 0, 0, 0)),
            full((CMP_HIDDEN, HEAD_DIM)),
            full((HEAD_DIM, CMP_HIDDEN)),
        ],
        out_specs=[
            pl.BlockSpec((1, 1, n_str, HEAD_DIM), lambda b, h: (b, h, 0, 0)),
            pl.BlockSpec((1, 1, HEAD_DIM, n_str), lambda b, h: (b, h, 0, 0)),
        ],
        out_shape=[
            jax.ShapeDtypeStruct((bsz, B_KV_HEADS, n_str, HEAD_DIM), BF16),
            jax.ShapeDtypeStruct((bsz, B_KV_HEADS, HEAD_DIM, n_str), BF16),
        ],
        compiler_params=_params(2, 32),
        name="nsa_compress",
    )(x_cmp, x_cmp, pe, w1, w2k, w2vt)


def _bf16_parts(x, parts=3):
    out, rest = [], np.asarray(x, np.float32)
    for _ in range(parts):
        piece = rest.astype(jnp.bfloat16).astype(np.float32)
        out.append(piece)
        rest = rest - piece
    return out


def _aux_key_lanes(seq):
    t = np.arange(seq)
    aux = np.zeros((seq, LANES), np.float32)
    aux[t, t // SLC_BLOCK] = 1.0
    aux[:, AUX_SEL:AUX_SEL + 3] = (t >> 7)[:, None]
    aux[:, AUX_SEL + 3:AUX_SEL + 6] = (t & 127)[:, None]
    return jnp.asarray(aux, BF16)


def _aux_slope_rows():
    rows = np.zeros((N_ATT_HEADS // GROUP, LANES - AUX_SEL, GROUP * BLOCK_Q), np.float32)
    for grp in range(N_ATT_HEADS // GROUP):
        for g in range(GROUP):
            slope = ALIBI_SLOPES[grp * GROUP + g] * LOG2E
            cols = slice(g * BLOCK_Q, (g + 1) * BLOCK_Q)
            for i, piece in enumerate(_bf16_parts(128.0 * slope) + _bf16_parts(slope)):
                rows[grp, i, cols] = piece
    return jnp.asarray(rows, BF16)


def _attn_kernel(sink_ref, qt_ref, ka_ref, ks_ref, kw_ref, vat_ref, vst_ref, vwt_ref, kc_ref, vct_ref, gt_ref,
                 aux_ref, slope_ref, o_ref, ot_ref, mask_a_ref, mask_w_ref, s_ref, e_ref, *, seq):
    n = pl.program_id(1)
    t0 = n * BLOCK_Q
    n_str = seq // CMP_STRIDE
    n_cmp = n_str - CMP_LEN // CMP_STRIDE + 1
    n_sel = seq // SLC_BLOCK
    top_n = min(SLC_TOPN, n_sel)
    hd = HEAD_DIM
    gq = GROUP * BLOCK_Q
    ch = SLC_CHUNK
    span_a = A_WINDOW + BLOCK_Q
    span_w = B_WINDOW + BLOCK_Q

    lane = lax.broadcasted_iota(jnp.int32, (1, gq), 1)
    q_loc = lane & (BLOCK_Q - 1)
    lane_head = lane >> (BLOCK_Q.bit_length() - 1)
    t_q = t0 + q_loc

    def head_row(vals):
        return jnp.where(lane_head == 0, vals[0], jnp.where(lane_head == 1, vals[1],
                                                           jnp.where(lane_head == 2, vals[2], vals[3])))

    def q_group(first_head):
        return jnp.concatenate(
            [qt_ref[0, (first_head + g) * hd:(first_head + g + 1) * hd, :] for g in range(GROUP)], axis=1)

    def score_rhs(grp, kvh, qg, sel_rows):
        zero = jnp.zeros((hd, gq), BF16)
        q_rows = [qg, zero] if kvh == 0 else [zero, qg]
        return jnp.concatenate(q_rows + [sel_rows, slope_ref[grp]], axis=0)

    def score_lhs(k_ref, start, span):
        return jnp.concatenate([k_ref[0, pl.ds(start, span), :], aux_ref[pl.ds(start, span), :]], axis=1)

    def band_start(span):
        return pl.multiple_of(jnp.maximum(t0 - (span - BLOCK_Q), 0), BLOCK_Q)

    def band_mask(span, window):
        ik = lax.broadcasted_iota(jnp.int32, (span, gq), 0)
        dist = (t0 - band_start(span)) + q_loc - ik
        return jnp.where((dist >= 0) & (dist < window), 0.0, -MASK_BIG)

    @pl.when(n <= span_w // BLOCK_Q - 1)
    def _():
        mask_a_ref[...] = band_mask(span_a, A_WINDOW)
        mask_w_ref[...] = band_mask(span_w, B_WINDOW)

    no_sel = jnp.zeros((AUX_SEL, gq), BF16)
    n_grp = N_ATT_HEADS // GROUP
    slopes = [head_row([ALIBI_SLOPES[grp * GROUP + g] for g in range(GROUP)]) for grp in range(n_grp)]
    qgs = [q_group(grp * GROUP) for grp in range(n_grp)]

    def softmax_cols(s, sinks=None):
        m = jnp.max(s, axis=0, keepdims=True)
        if sinks is not None:
            m = jnp.maximum(m, sinks)
        e = jnp.exp2(s - m)
        den = jnp.sum(e, axis=0, keepdims=True)
        if sinks is not None:
            den = den + jnp.exp2(sinks - m)
        return e.astype(BF16), den

    ks_a = band_start(span_a)
    ks_w = band_start(span_w)
    lhs_a = score_lhs(ka_ref, ks_a, span_a)
    lhs_w = score_lhs(kw_ref, ks_w, span_w)
    s_a = [_dot(lhs_a, score_rhs(kvh, kvh, qgs[kvh], no_sel)) + mask_a_ref[...] for kvh in range(A_KV_HEADS)]
    s_w = [_dot(lhs_w, score_rhs(A_KV_HEADS + kvh, kvh, qgs[A_KV_HEADS + kvh], no_sel)) + mask_w_ref[...]
           for kvh in range(B_KV_HEADS)]

    c_idx = lax.broadcasted_iota(jnp.int32, (n_str, gq), 0)
    dist_c = t_q - (c_idx * CMP_STRIDE + CMP_LEN - 1)
    valid_c = (dist_c >= 0) & (c_idx < n_cmp)
    dist_cf = dist_c.astype(F32)
    s_c = [jnp.where(valid_c, _dot(kc_ref[0, kvh], qgs[A_KV_HEADS + kvh])
                     - (slopes[A_KV_HEADS + kvh] * LOG2E) * dist_cf, NEG_INF) for kvh in range(B_KV_HEADS)]

    oj = lax.broadcasted_iota(jnp.int32, (n_sel, n_str), 0) * SLC_BLOCK
    oc = lax.broadcasted_iota(jnp.int32, (n_sel, n_str), 1) * CMP_STRIDE
    overlap_t = ((oc < oj + SLC_BLOCK) & (oc + CMP_LEN > oj)).astype(BF16)
    j_idx = lax.broadcasted_iota(jnp.int32, (n_sel, BLOCK_Q), 0)
    t_blk = (t0 + lax.broadcasted_iota(jnp.int32, (1, BLOCK_Q), 1)) >> (SLC_BLOCK.bit_length() - 1)
    valid_j = j_idx <= t_blk
    forced_j = (j_idx == 0) | (j_idx == t_blk) | (j_idx == t_blk - 1)
    e_c, den_c, rhs_s = [], [], []
    for kvh in range(B_KV_HEADS):
        m = jnp.max(s_c[kvh], axis=0, keepdims=True)
        m = jnp.where(m == NEG_INF, 0.0, m)
        e = jnp.exp2(s_c[kvh] - m)
        den = jnp.sum(e, axis=0, keepdims=True)
        den = jnp.where(den > 0.0, den, 1.0)
        e_c.append(e.astype(BF16))
        den_c.append(den)
        p = e / den
        psum = p[:, 0:BLOCK_Q]
        for g in range(1, GROUP):
            psum = psum + p[:, g * BLOCK_Q:(g + 1) * BLOCK_Q]
        p_hi = psum.astype(BF16)
        p_lo = (psum - p_hi.astype(F32)).astype(BF16)
        imp = _dot(overlap_t, p_hi) + _dot(overlap_t, p_lo)
        rank = jnp.where(forced_j, RANK_FORCED, jnp.where(valid_j, imp, RANK_INVALID))
        cnt = jnp.zeros((n_sel, BLOCK_Q), jnp.int32)
        for jp in range(n_sel):
            row = rank[jp:jp + 1, :]
            before = (row > rank) | ((row == rank) & (j_idx > jp))
            cnt = cnt + before.astype(jnp.int32)
        sel_bias = jnp.where(cnt < top_n, 0.0, -MASK_BIG)
        if n_sel < AUX_SEL:
            sel_bias = jnp.concatenate([sel_bias, jnp.zeros((AUX_SEL - n_sel, BLOCK_Q), F32)], axis=0)
        sel_rows = jnp.concatenate([sel_bias.astype(BF16)] * GROUP, axis=1)
        rhs_s.append(score_rhs(A_KV_HEADS + kvh, kvh, qgs[A_KV_HEADS + kvh], sel_rows))

    t_qf = t_q.astype(F32)
    sinks = [(head_row([sink_ref[kvh * GROUP + g] for g in range(GROUP)]) + slopes[kvh] * t_qf) * LOG2E
             for kvh in range(A_KV_HEADS)]
    ed_a = [softmax_cols(s_a[kvh], sinks[kvh]) for kvh in range(A_KV_HEADS)]
    ed_w = [softmax_cols(s_w[kvh]) for kvh in range(B_KV_HEADS)]

    def rows(kvh):
        return slice(kvh * hd, (kvh + 1) * hd)

    o_a = [_dot(vat_ref[0, rows(kvh), pl.ds(ks_a, span_a)], ed_a[kvh][0]) / ed_a[kvh][1] for kvh in range(A_KV_HEADS)]
    o_w = [_dot(vwt_ref[0, rows(kvh), pl.ds(ks_w, span_w)], ed_w[kvh][0]) / ed_w[kvh][1] for kvh in range(B_KV_HEADS)]
    o_c = [_dot(vct_ref[0, kvh], e_c[kvh]) / den_c[kvh] for kvh in range(B_KV_HEADS)]
    for kvh in range(A_KV_HEADS):
        for g in range(GROUP):
            h = kvh * GROUP + g
            ot_ref[h * hd:(h + 1) * hd, :] = o_a[kvh][:, g * BLOCK_Q:(g + 1) * BLOCK_Q]

    dloc = q_loc - lax.broadcasted_iota(jnp.int32, (ch, gq), 0)
    n_full = t0 // ch

    def slc_scores(c):
        lhs = score_lhs(ks_ref, pl.multiple_of(c * ch, ch), ch)
        for kvh in range(B_KV_HEADS):
            s_ref[c & 1, kvh] = _dot(lhs, rhs_s[kvh])

    def slc_values(c, state, kvh):
        m, l, acc, alpha = state
        vt = vst_ref[0, rows(kvh), pl.ds(pl.multiple_of(jnp.maximum(c, 0) * ch, ch), ch)]
        return m, l, alpha * acc + _dot(vt, e_ref[kvh]), alpha

    def slc_softmax(c, state, kvh, causal=None):
        m, l, acc, _ = state
        s = s_ref[c & 1, kvh]
        if causal is not None:
            s = jnp.where(causal, s, -MASK_BIG)
        m_new = jnp.maximum(m, jnp.max(s, axis=0, keepdims=True))
        alpha = jnp.exp2(m - m_new)
        e = jnp.exp2(s - m_new)
        e_ref[kvh] = e.astype(BF16)
        return m_new, alpha * l + jnp.sum(e, axis=0, keepdims=True), acc, alpha

    def slc_step(c, states):
        states = tuple(slc_values(c - 1, states[kvh], kvh) for kvh in range(B_KV_HEADS))
        states = tuple(slc_softmax(c, states[kvh], kvh) for kvh in range(B_KV_HEADS))
        slc_scores(c + 1)
        return states

    e_ref[...] = jnp.zeros_like(e_ref)
    init = (jnp.full((1, gq), NEG_INF, F32), jnp.zeros((1, gq), F32), jnp.zeros((hd, gq), F32),
            jnp.ones((1, gq), F32))
    slc_scores(jnp.int32(0))
    states = lax.fori_loop(0, n_full, slc_step, (init,) * B_KV_HEADS)
    causal = dloc >= n_full * ch - t0
    gt = jax.nn.sigmoid(gt_ref[0])
    for kvh in range(B_KV_HEADS):
        state = slc_values(n_full - 1, states[kvh], kvh)
        state = slc_softmax(n_full, state, kvh, causal)
        _, l_fin, acc_fin, _ = slc_values(n_full, state, kvh)
        o_slc = acc_fin / l_fin
        for g in range(GROUP):
            hb = kvh * GROUP + g
            h = A_HEADS + hb
            cols = slice(g * BLOCK_Q, (g + 1) * BLOCK_Q)
            ot_ref[h * hd:(h + 1) * hd, :] = (
                gt[hb:hb + 1] * o_c[kvh][:, cols]
                + gt[B_HEADS + hb:B_HEADS + hb + 1] * o_slc[:, cols]
                + gt[2 * B_HEADS + hb:2 * B_HEADS + hb + 1] * o_w[kvh][:, cols]
            )

    o_ref[0] = ot_ref[...].T.astype(BF16)


def _attention(sinks, tok, feat, kc, vct, gates_t, seq):
    bsz = tok.shape[0]
    n_str = seq // CMP_STRIDE
    kv_w = A_KV_HEADS * HEAD_DIM
    q_rows = N_ATT_HEADS * HEAD_DIM
    gq = GROUP * BLOCK_Q
    assert seq // SLC_BLOCK <= AUX_SEL and seq <= 128 * 256

    def k_spec(col_block):
        return pl.BlockSpec((1, seq, kv_w), lambda b, n: (b, 0, col_block))

    def vt_spec(row_block):
        return pl.BlockSpec((1, kv_w, seq), lambda b, n: (b, q_rows // kv_w + row_block, 0))

    return pl.pallas_call(
        functools.partial(_attn_kernel, seq=seq),
        grid=(bsz, seq // BLOCK_Q),
        in_specs=[
            pl.BlockSpec(memory_space=pltpu.SMEM),
            pl.BlockSpec((1, q_rows, BLOCK_Q), lambda b, n: (b, 0, n)),
            k_spec(0), k_spec(1), k_spec(2),
            vt_spec(0), vt_spec(1), vt_spec(2),
            pl.BlockSpec((1, B_KV_HEADS, n_str, HEAD_DIM), lambda b, n: (b, 0, 0, 0)),
            pl.BlockSpec((1, B_KV_HEADS, HEAD_DIM, n_str), lambda b, n: (b, 0, 0, 0)),
            pl.BlockSpec((1, GATE_ROWS, BLOCK_Q), lambda b, n: (b, 0, n)),
            pl.BlockSpec((seq, LANES), lambda b, n: (0, 0)),
            pl.BlockSpec((N_ATT_HEADS // GROUP, LANES - AUX_SEL, gq), lambda b, n: (0, 0, 0)),
        ],
        out_specs=pl.BlockSpec((1, BLOCK_Q, q_rows), lambda b, n: (b, n, 0)),
        out_shape=jax.ShapeDtypeStruct((bsz, seq, q_rows), BF16),
        scratch_shapes=[
            pltpu.VMEM((q_rows, BLOCK_Q), F32),
            pltpu.VMEM((A_WINDOW + BLOCK_Q, gq), F32),
            pltpu.VMEM((B_WINDOW + BLOCK_Q, gq), F32),
            pltpu.VMEM((2, B_KV_HEADS, SLC_CHUNK, gq), F32),
            pltpu.VMEM((B_KV_HEADS, SLC_CHUNK, gq), BF16),
        ],
        compiler_params=_params(2, 32),
        name="hybrid_attention",
    )(sinks, feat, tok, tok, tok, feat, feat, feat, kc, vct, gates_t, _aux_key_lanes(seq), _aux_slope_rows())


def _proj_ln_kernel(o_ref, w_ref, x_ref, g_ref, b_ref, y_ref, *, alpha):
    y = _dot(o_ref[...], w_ref[...])
    y_ref[...] = _layer_norm(alpha * x_ref[...] + y, g_ref[...], b_ref[...])


def _proj_ln(o2d, w, x2d, g, b, alpha):
    n, k = o2d.shape
    row = pl.BlockSpec((ROW_TILE, D_MODEL), lambda i: (i, 0))
    vec = pl.BlockSpec((1, D_MODEL), lambda i: (0, 0))
    return pl.pallas_call(
        functools.partial(_proj_ln_kernel, alpha=alpha),
        grid=(n // ROW_TILE,),
        in_specs=[
            pl.BlockSpec((ROW_TILE, k), lambda i: (i, 0)),
            pl.BlockSpec((k, D_MODEL), lambda i: (0, 0)),
            row, vec, vec,
        ],
        out_specs=row,
        out_shape=jax.ShapeDtypeStruct((n, D_MODEL), F32),
        compiler_params=_params(1, 32),
        name="out_proj_ln",
    )(o2d, w, x2d, g, b)


def _ffn_ln_kernel(x_ref, wg_ref, wu_ref, wd_ref, g_ref, b_ref, y_ref, *, alpha):
    x = x_ref[...]
    xb = x.astype(BF16)
    d_ff = wg_ref.shape[1]
    y = None
    for lo in range(0, d_ff, FFN_CHUNK):
        hi = min(lo + FFN_CHUNK, d_ff)
        h = (_silu(_dot(xb, wg_ref[:, lo:hi])) * _dot(xb, wu_ref[:, lo:hi])).astype(BF16)
        part = _dot(h, wd_ref[lo:hi, :])
        y = part if y is None else y + part
    y_ref[...] = _layer_norm(alpha * x + y, g_ref[...], b_ref[...])


def _ffn_ln(x2d, wg, wu, wd, g, b, alpha):
    n = x2d.shape[0]
    d_ff = wg.shape[1]
    row = pl.BlockSpec((ROW_TILE, D_MODEL), lambda i: (i, 0))
    vec = pl.BlockSpec((1, D_MODEL), lambda i: (0, 0))
    once = pl.Buffered(1)
    return pl.pallas_call(
        functools.partial(_ffn_ln_kernel, alpha=alpha),
        grid=(n // ROW_TILE,),
        in_specs=[
            row,
            pl.BlockSpec((D_MODEL, d_ff), lambda i: (0, 0), pipeline_mode=once),
            pl.BlockSpec((D_MODEL, d_ff), lambda i: (0, 0), pipeline_mode=once),
            pl.BlockSpec((d_ff, D_MODEL), lambda i: (0, 0), pipeline_mode=once),
            vec, vec,
        ],
        out_specs=row,
        out_shape=jax.ShapeDtypeStruct((n, D_MODEL), F32),
        compiler_params=_params(1, 48),
        name="swiglu_ln",
    )(x2d, wg, wu, wd, g, b)


def _expert_kernel(be_ref, used_ref, x_ref, wg_ref, wu_ref, wd_ref, y_ref, xb_ref):
    del be_ref
    f = pl.program_id(1)
    in_use = pl.program_id(0) < used_ref[0]

    def hidden(x):
        return (_silu(_dot(x, wg_ref[0])) * _dot(x, wu_ref[0])).astype(BF16)

    def down(h, first):
        for lo in range(0, D_MODEL, MXU_WIDTH):
            part = _dot(h, wd_ref[0, :, lo:lo + MXU_WIDTH])
            if first:
                y_ref[:, lo:lo + MXU_WIDTH] = part
            else:
                y_ref[:, lo:lo + MXU_WIDTH] += part

    @pl.when(in_use & (f == 0))
    def _():
        xb = x_ref[...].astype(BF16)
        xb_ref[...] = xb
        down(hidden(xb), True)

    @pl.when(in_use & (f != 0))
    def _():
        down(hidden(xb_ref[...]), False)

    @pl.when(jnp.logical_not(in_use) & (f == 0))
    def _():
        y_ref[...] = jnp.zeros_like(y_ref)


def _experts(blk_e, n_used, xin, wg, wu, wd):
    n_rows = blk_e.shape[0] * MOE_ROWS
    d_ff = wg.shape[2]
    n_f = d_ff // EXPERT_TILE

    def row_blk(i, used):
        return jnp.minimum(i, used[0] - 1)

    def hid_blk(i, f, used):
        return jnp.where(i < used[0], f, n_f - 1)

    return pl.pallas_call(
        _expert_kernel,
        grid_spec=pltpu.PrefetchScalarGridSpec(
            num_scalar_prefetch=2,
            grid=(n_rows // MOE_ROWS, n_f),
            in_specs=[
                pl.BlockSpec((MOE_ROWS, D_MODEL), lambda i, f, be, used: (row_blk(i, used), 0)),
                pl.BlockSpec((1, D_MODEL, EXPERT_TILE), lambda i, f, be, used: (be[i], 0, hid_blk(i, f, used))),
                pl.BlockSpec((1, D_MODEL, EXPERT_TILE), lambda i, f, be, used: (be[i], 0, hid_blk(i, f, used))),
                pl.BlockSpec((1, EXPERT_TILE, D_MODEL), lambda i, f, be, used: (be[i], hid_blk(i, f, used), 0)),
            ],
            out_specs=pl.BlockSpec((MOE_ROWS, D_MODEL), lambda i, f, be, used: (i, 0)),
            scratch_shapes=[pltpu.VMEM((MOE_ROWS, D_MODEL), BF16)],
        ),
        out_shape=jax.ShapeDtypeStruct((n_rows, D_MODEL), F32),
        compiler_params=_params(2, 52),
        name="expert_swiglu",
    )(blk_e, n_used, xin, wg, wu, wd)


def _gmlp_kernel(x_ref, win_ref, lng_ref, lnb_ref, ws_ref, bs_ref, wout_ref, g_ref, b_ref, y_ref,
                 u_ref, vn_ref, gated_ref, *, alpha):
    x = x_ref[...]
    xb = x.astype(BF16)
    u_ref[...] = _gelu(_dot(xb, win_ref[:, :D_MODEL]))
    v = _gelu(_dot(xb, win_ref[:, D_MODEL:]))
    vn_ref[...] = _layer_norm(v, lng_ref[...], lnb_ref[...]).astype(BF16)
    cs = GMLP_CHUNK
    lower = lax.broadcasted_iota(jnp.int32, (cs, cs), 0) >= lax.broadcasted_iota(jnp.int32, (cs, cs), 1)
    for grp in range(GMLP_GROUPS):
        w = jnp.where(lower, ws_ref[grp], 0.0).astype(BF16)
        bias = bs_ref[grp]
        cols = slice(grp * cs, (grp + 1) * cs)
        for c in range(x.shape[0] // cs):
            rows = slice(c * cs, (c + 1) * cs)
            mixed = _dot(w, vn_ref[rows, cols]) + bias
            gated_ref[rows, cols] = (u_ref[rows, cols] * mixed).astype(BF16)
    y = _dot(gated_ref[...], wout_ref[...])
    y_ref[...] = _layer_norm(alpha * x + y, g_ref[...], b_ref[...])


def _gmlp_ln(x2d, w_in, ln_g, ln_b, w_s, b_s, w_out, g, b, alpha):
    n = x2d.shape[0]
    row = pl.BlockSpec((ROW_TILE, D_MODEL), lambda i: (i, 0))
    vec = pl.BlockSpec((1, D_MODEL), lambda i: (0, 0))
    grp = pl.BlockSpec((GMLP_GROUPS, GMLP_CHUNK, GMLP_CHUNK), lambda i: (0, 0, 0))
    return pl.pallas_call(
        functools.partial(_gmlp_kernel, alpha=alpha),
        grid=(n // ROW_TILE,),
        in_specs=[
            row,
            pl.BlockSpec((D_MODEL, 2 * D_MODEL), lambda i: (0, 0)),
            vec, vec, grp, grp,
            pl.BlockSpec((D_MODEL, D_MODEL), lambda i: (0, 0)),
            vec, vec,
        ],
        out_specs=row,
        out_shape=jax.ShapeDtypeStruct((n, D_MODEL), F32),
        scratch_shapes=[
            pltpu.VMEM((ROW_TILE, D_MODEL), F32),
            pltpu.VMEM((ROW_TILE, D_MODEL), BF16),
            pltpu.VMEM((ROW_TILE, D_MODEL), BF16),
        ],
        compiler_params=_params(1, 40),
        name="gmlp_ln",
    )(x2d, w_in, ln_g, ln_b, w_s, b_s, w_out, g, b)


def _route_kernel(x_ref, w_ref, gate_ref, et_ref, post_ref, cnt_ref, run_ref):
    tm = x_ref.shape[0]

    @pl.when(pl.program_id(0) == 0)
    def _():
        run_ref[...] = jnp.zeros_like(run_ref)

    logits = lax.dot_general(w_ref[...], x_ref[...], (((1,), (1,)), ((), ())),
                             preferred_element_type=F32, precision=lax.Precision.HIGHEST)
    row = lax.broadcasted_iota(jnp.int32, (N_EXPERTS, tm), 0)
    v1 = jnp.max(logits, axis=0, keepdims=True)
    i1 = jnp.min(jnp.where(logits == v1, row, N_EXPERTS), axis=0, keepdims=True)
    rest = jnp.where(row == i1, NEG_INF, logits)
    v2 = jnp.max(rest, axis=0, keepdims=True)
    i2 = jnp.min(jnp.where(rest == v2, row, N_EXPERTS), axis=0, keepdims=True)
    d = jnp.exp(v2 - v1)
    g1 = 1.0 / (1.0 + d)
    g2 = d / (1.0 + d)

    hit1 = row == i1
    hit2 = row == i2
    onehot = (hit1 | hit2).astype(BF16)
    earlier = (lax.broadcasted_iota(jnp.int32, (tm, tm), 0) < lax.broadcasted_iota(jnp.int32, (tm, tm), 1)).astype(BF16)
    before = _dot(onehot, earlier) + run_ref[:, 0:1]
    pos1 = jnp.sum(jnp.where(hit1, before, 0.0), axis=0, keepdims=True).astype(jnp.int32)
    pos2 = jnp.sum(jnp.where(hit2, before, 0.0), axis=0, keepdims=True).astype(jnp.int32)
    run_ref[...] += jnp.sum(onehot.astype(F32), axis=1, keepdims=True)

    et_ref[...] = jnp.where(row == 0, i1, jnp.where(row == 1, i2, 0))
    post_ref[...] = jnp.where(row == 0, pos1, jnp.where(row == 1, pos2, 0))
    cnt_ref[...] = run_ref[...]
    wide = lax.broadcasted_iota(jnp.int32, (LANES, tm), 0)
    gate_ref[...] = jnp.where(wide == 0, g1, jnp.where(wide == 1, g2, 0.0)).T


def _route(x2d, w_router_t):
    n = x2d.shape[0]
    tm = ROW_TILE
    per_tok = pl.BlockSpec((N_EXPERTS, tm), lambda i: (0, i))
    return pl.pallas_call(
        _route_kernel,
        grid=(n // tm,),
        in_specs=[
            pl.BlockSpec((tm, D_MODEL), lambda i: (i, 0)),
            pl.BlockSpec((N_EXPERTS, D_MODEL), lambda i: (0, 0)),
        ],
        out_specs=[
            pl.BlockSpec((tm, LANES), lambda i: (i, 0)),
            per_tok, per_tok,
            pl.BlockSpec((N_EXPERTS, LANES), lambda i: (0, 0)),
        ],
        out_shape=[
            jax.ShapeDtypeStruct((n, LANES), F32),
            jax.ShapeDtypeStruct((N_EXPERTS, n), jnp.int32),
            jax.ShapeDtypeStruct((N_EXPERTS, n), jnp.int32),
            jax.ShapeDtypeStruct((N_EXPERTS, LANES), F32),
        ],
        scratch_shapes=[pltpu.VMEM((N_EXPERTS, LANES), F32)],
        compiler_params=_params(1, 24),
        name="moe_router",
    )(x2d, w_router_t)


def _dispatch_kernel(d0_ref, d1_ref, lo_ref, hi_ref, x_ref, xin_hbm, zero_ref, sem_rows, sem_zero):
    step = pl.program_id(0)
    tm = x_ref.shape[0]
    base = step * tm

    def row_copy(r, dest_ref):
        return pltpu.make_async_copy(x_ref.at[pl.ds(r, 1)], xin_hbm.at[pl.ds(dest_ref[base + r], 1)], sem_rows)

    def issue(r8, carry):
        for j in range(DMA_UNROLL):
            r = r8 * DMA_UNROLL + j
            row_copy(r, d0_ref).start(priority=j % 2)
            row_copy(r, d1_ref).start(priority=(j + 1) % 2)
        return carry

    lax.fori_loop(0, tm // DMA_UNROLL, issue, 0)

    @pl.when(step == pl.num_programs(0) - 1)
    def _():
        zero_ref[...] = jnp.zeros_like(zero_ref)
        for k in range(N_EXPERTS + 1):
            def fill(r, carry):
                pltpu.make_async_copy(zero_ref, xin_hbm.at[pl.ds(r, 1)], sem_zero).start()
                return carry

            def drain(r, carry):
                pltpu.make_async_copy(zero_ref, xin_hbm.at[pl.ds(r, 1)], sem_zero).wait()
                return carry

            lax.fori_loop(lo_ref[k], hi_ref[k], fill, 0)
            lax.fori_loop(lo_ref[k], hi_ref[k], drain, 0)

    pltpu.make_async_copy(x_ref, xin_hbm.at[pl.ds(0, tm)], sem_rows).wait()
    pltpu.make_async_copy(x_ref, xin_hbm.at[pl.ds(0, tm)], sem_rows).wait()


def _dispatch(d0, d1, fill_lo, fill_hi, x2d, n_rows):
    n = x2d.shape[0]
    tm = ROW_TILE
    return pl.pallas_call(
        _dispatch_kernel,
        grid_spec=pltpu.PrefetchScalarGridSpec(
            num_scalar_prefetch=4,
            grid=(n // tm,),
            in_specs=[pl.BlockSpec((tm, D_MODEL), lambda i, *_: (i, 0))],
            out_specs=pl.BlockSpec(memory_space=pl.ANY),
            scratch_shapes=[pltpu.VMEM((1, D_MODEL), F32), pltpu.SemaphoreType.DMA(()), pltpu.SemaphoreType.DMA(())],
        ),
        out_shape=jax.ShapeDtypeStruct((n_rows, D_MODEL), F32),
        compiler_params=_params(1, 16),
        name="moe_dispatch",
    )(d0, d1, fill_lo, fill_hi, x2d)


def _combine_kernel(d0_ref, d1_ref, y_hbm, gate_ref, x_ref, g_ref, b_ref, o_ref, buf0, buf1, sem, *, alpha):
    tm = x_ref.shape[0]
    base = pl.program_id(0) * tm

    def issue(r8, carry):
        for j in range(DMA_UNROLL):
            r = r8 * DMA_UNROLL + j
            pltpu.make_async_copy(y_hbm.at[pl.ds(d0_ref[base + r], 1)], buf0.at[pl.ds(r, 1)], sem.at[0]).start(priority=0)
            pltpu.make_async_copy(y_hbm.at[pl.ds(d1_ref[base + r], 1)], buf1.at[pl.ds(r, 1)], sem.at[1]).start(priority=1)
        return carry

    lax.fori_loop(0, tm // DMA_UNROLL, issue, 0)
    pltpu.make_async_copy(y_hbm.at[pl.ds(0, tm)], buf0, sem.at[0]).wait()
    pltpu.make_async_copy(y_hbm.at[pl.ds(0, tm)], buf1, sem.at[1]).wait()
    gate = gate_ref[...]
    y = buf0[...] * gate[:, 0:1] + buf1[...] * gate[:, 1:2]
    o_ref[...] = _layer_norm(alpha * x_ref[...] + y, g_ref[...], b_ref[...])


def _combine_ln(d0, d1, yb, gate, x2d, g, b, alpha):
    n = x2d.shape[0]
    tm = GATHER_ROWS
    row = pl.BlockSpec((tm, D_MODEL), lambda i, a, c: (i, 0))
    vec = pl.BlockSpec((1, D_MODEL), lambda i, a, c: (0, 0))
    return pl.pallas_call(
        functools.partial(_combine_kernel, alpha=alpha),
        grid_spec=pltpu.PrefetchScalarGridSpec(
            num_scalar_prefetch=2,
            grid=(n // tm,),
            in_specs=[
                pl.BlockSpec(memory_space=pl.ANY),
                pl.BlockSpec((tm, LANES), lambda i, a, c: (i, 0)),
                row, vec, vec,
            ],
            out_specs=row,
            scratch_shapes=[
                pltpu.VMEM((tm, D_MODEL), F32),
                pltpu.VMEM((tm, D_MODEL), F32),
                pltpu.SemaphoreType.DMA((2,)),
            ],
        ),
        out_shape=jax.ShapeDtypeStruct((n, D_MODEL), F32),
        compiler_params=_params(1, 16),
        name="moe_combine_ln",
    )(d0, d1, yb, gate, x2d, g, b)


def _attention_layer(x2d, bsz, seq, w_in, sinks, pe_k, wk1, wk2, pe_v, wv1, wv2, w_o, g, b, alpha):
    aq, akv = A_HEADS * HEAD_DIM, A_KV_HEADS * HEAD_DIM
    bq, bkv = B_HEADS * HEAD_DIM, B_KV_HEADS * HEAD_DIM
    bounds = [0]
    for width in (aq, akv, akv, bq, bkv, bkv, bkv, bkv, bkv, bkv, 3 * B_HEADS):
        bounds.append(bounds[-1] + width)
    qa, ka, va, qb, kc, vc, ks, vs, kw, vw, wg = [w_in[:, bounds[i]:bounds[i + 1]] for i in range(11)]
    w_tok = jnp.concatenate([ka, ks, kw, kc, vc], axis=1).astype(BF16)
    w_feat = jnp.concatenate([qa * (SCALE * LOG2E), qb * (SCALE * LOG2E), va, vs, vw], axis=1).T.astype(BF16)
    wg = wg.reshape(D_MODEL, B_HEADS, 3).transpose(2, 1, 0).reshape(3 * B_HEADS, D_MODEL)
    wg = jnp.pad(wg, ((0, GATE_ROWS - 3 * B_HEADS), (0, 0))).astype(BF16)
    tok, x_cmp, feat, gates_t = _inproj(x2d, w_tok, w_feat, wg, bsz, seq)

    halves = CMP_LEN // CMP_STRIDE
    w1 = jnp.stack([wk1, wv1]).reshape(2, halves, CMP_STRIDE, 1, HEAD_DIM, CMP_HIDDEN)
    own_head = jnp.eye(B_KV_HEADS, dtype=F32)[:, None, None, None, :, None, None]
    w1 = (w1[None] * own_head).reshape(B_KV_HEADS, 2, halves, CMP_STRIDE * bkv, CMP_HIDDEN).astype(BF16)
    pe = jnp.stack([pe_k, pe_v]).reshape(2, halves, CMP_STRIDE, 1, HEAD_DIM)
    pe = jnp.broadcast_to(pe, (2, halves, CMP_STRIDE, B_KV_HEADS, HEAD_DIM)).reshape(2, halves, 1, CMP_STRIDE * bkv)
    pe = jnp.broadcast_to(pe, (2, halves, 8, CMP_STRIDE * bkv))
    kcmp, vcmp_t = _compress(x_cmp.reshape(bsz, seq, 2 * bkv), pe, w1, wk2.astype(BF16), wv2.T.astype(BF16))

    o = _attention(sinks, tok.reshape(bsz, seq, KEY_W), feat, kcmp, vcmp_t, gates_t, seq)
    return _proj_ln(o.reshape(bsz * seq, N_ATT_HEADS * HEAD_DIM), w_o.astype(BF16), x2d, g, b, alpha)


def _moe_layer(x2d, w_router, w_gate, w_up, w_down, g, b, alpha):
    n_tok = x2d.shape[0]
    gate, e_t, pos_t, cnt = _route(x2d, w_router.T)
    counts = cnt[:, 0].astype(jnp.int32)
    padded = (counts + MOE_ROWS - 1) // MOE_ROWS * MOE_ROWS
    pad_ends = jnp.cumsum(padded)
    pad_starts = pad_ends - padded
    n_blocks = n_tok * TOP_K // MOE_ROWS + N_EXPERTS
    n_rows = n_blocks * MOE_ROWS
    experts = jnp.arange(N_EXPERTS, dtype=jnp.int32)[:, None, None]
    start_of = jnp.sum(jnp.where(e_t[None, :TOP_K] == experts, pad_starts[:, None, None], 0), axis=0)
    dest = start_of + pos_t[:TOP_K]
    fill_lo = jnp.concatenate([pad_starts + counts, pad_ends[-1:]])
    fill_hi = jnp.concatenate([pad_ends, jnp.full((1,), n_rows, jnp.int32)])
    blk_start = jnp.arange(n_blocks, dtype=jnp.int32) * MOE_ROWS
    blk_e = jnp.sum((pad_ends[None, :] <= blk_start[:, None]).astype(jnp.int32), axis=1)
    blk_e = jnp.minimum(blk_e, N_EXPERTS - 1)

    xin = _dispatch(dest[0], dest[1], fill_lo, fill_hi, x2d, n_rows)
    n_used = (pad_ends[-1:] // MOE_ROWS).astype(jnp.int32)
    yb = _experts(blk_e, n_used, xin, w_gate.astype(BF16), w_up.astype(BF16), w_down.astype(BF16))
    return _combine_ln(dest[0], dest[1], yb, gate, x2d, g, b, alpha)


def kernel(x, att_w_in, att_sinks, cmp_pe_k, cmp_wk1, cmp_wk2, cmp_pe_v, cmp_wv1, cmp_wv2, att_w_o,
           ffn_w_gate, ffn_w_up, ffn_w_down, gmlp_w_in, gmlp_ln_g, gmlp_ln_b, gmlp_w_s, gmlp_b_s, gmlp_w_out,
           moe_w_router, moe_w_gate, moe_w_up, moe_w_down, ln_g, ln_b):
    bsz, seq, dm = x.shape
    depth = ln_g.shape[0]
    alpha = (2.0 * depth) ** 0.25
    assert dm == D_MODEL and seq % ROW_TILE == 0
    x2d = x.reshape(bsz * seq, dm)

    def vec(p):
        return p.reshape(1, -1)

    for i in range(depth):
        j = i // 2
        g0, b0, g1, b1 = vec(ln_g[i, 0]), vec(ln_b[i, 0]), vec(ln_g[i, 1]), vec(ln_b[i, 1])
        if i % 2 == 0:
            x2d = _attention_layer(x2d, bsz, seq, att_w_in[j], att_sinks[j], cmp_pe_k[j], cmp_wk1[j], cmp_wk2[j],
                                   cmp_pe_v[j], cmp_wv1[j], cmp_wv2[j], att_w_o[j], g0, b0, alpha)
            x2d = _ffn_ln(x2d, ffn_w_gate[j].astype(BF16), ffn_w_up[j].astype(BF16), ffn_w_down[j].astype(BF16),
                          g1, b1, alpha)
        else:
            b_s = jnp.broadcast_to(gmlp_b_s[j][:, :, None], (GMLP_GROUPS, GMLP_CHUNK, GMLP_CHUNK))
            x2d = _gmlp_ln(x2d, gmlp_w_in[j].astype(BF16), vec(gmlp_ln_g[j]), vec(gmlp_ln_b[j]), gmlp_w_s[j], b_s,
                           gmlp_w_out[j].astype(BF16), g0, b0, alpha)
            x2d = _moe_layer(x2d, moe_w_router[j], moe_w_gate[j], moe_w_up[j], moe_w_down[j], g1, b1, alpha)
    return x2d.reshape(bsz, seq, dm)
```

```python
import functools

import jax
import jax.numpy as jnp
import numpy as np
from jax import lax
from jax.experimental import pallas as pl
from jax.experimental.pallas import tpu as pltpu

F32 = jnp.float32
BF16 = jnp.bfloat16

D_MODEL = 1024
HEAD_DIM = 64
BLOCK_Q = 128
A_HEADS = 8
A_KV_HEADS = 2
A_WINDOW = 128
B_HEADS = 8
B_KV_HEADS = 2
GROUP = 4
CMP_LEN = 32
CMP_STRIDE = 16
CMP_HIDDEN = 256
SLC_BLOCK = 64
SLC_TOPN = 8
B_WINDOW = 256
N_ATT_HEADS = A_HEADS + B_HEADS
GMLP_GROUPS = 8
GMLP_CHUNK = 128
N_EXPERTS = 8
TOP_K = 2
LN_EPS = 1e-5
LANES = 128
MXU_WIDTH = 256
DMA_UNROLL = 8

TOK_W = 640
KEY_W = 384
FEAT_W = 1408
GATE_ROWS = 32
SLC_CHUNK = 256

ROW_TILE = 512
MOE_ROWS = 512
GATHER_ROWS = 512
FFN_CHUNK = 1024
EXPERT_TILE = 1792

ALIBI_SLOPES = tuple(2.0 ** (-8.0 * h / N_ATT_HEADS) for h in range(1, N_ATT_HEADS + 1))
SCALE = HEAD_DIM ** -0.5
LOG2E = 1.4426950408889634
NEG_INF = float("-inf")
RANK_FORCED = 1e30
RANK_INVALID = -1.0
MASK_BIG = 1e30
AUX_SEL = 32


def _params(n_grid, vmem_mb):
    return pltpu.CompilerParams(
        dimension_semantics=("arbitrary",) * n_grid, vmem_limit_bytes=vmem_mb * 1024 * 1024
    )


def _dot(a, b):
    return jnp.dot(a, b, preferred_element_type=F32)


def _dot_nt(a, b):
    return lax.dot_general(a, b, (((1,), (1,)), ((), ())), preferred_element_type=F32)


def _layer_norm(z, g, b):
    mu = jnp.mean(z, axis=-1, keepdims=True)
    zc = z - mu
    var = jnp.mean(zc * zc, axis=-1, keepdims=True)
    return zc * lax.rsqrt(var + LN_EPS) * g + b


def _gelu(x):
    return 0.5 * x * (1.0 + jnp.tanh(0.7978845608028654 * (x + 0.044715 * (x * x * x))))


def _silu(x):
    return x / (1.0 + jnp.exp(-x))


def _inproj_kernel(x_ref, wt_ref, wf_ref, wg_ref, tok_ref, cmp_ref, feat_ref, gate_ref):
    x = x_ref[...].astype(BF16)
    tok = _dot(x, wt_ref[...])
    tok_ref[...] = tok[:, :KEY_W].astype(BF16)
    cmp_ref[...] = tok[:, KEY_W:]
    rows = FEAT_W // 4
    for c in range(4):
        feat_ref[0, c * rows:(c + 1) * rows, :] = _dot_nt(wf_ref[c * rows:(c + 1) * rows, :], x).astype(BF16)
    gate_ref[0] = _dot_nt(wg_ref[...], x)


def _inproj(x2d, w_tok, w_feat, w_gate, bsz, seq):
    n = x2d.shape[0]
    per_seq = seq // ROW_TILE
    return pl.pallas_call(
        _inproj_kernel,
        grid=(n // ROW_TILE,),
        in_specs=[
            pl.BlockSpec((ROW_TILE, D_MODEL), lambda i: (i, 0)),
            pl.BlockSpec((D_MODEL, TOK_W), lambda i: (0, 0)),
            pl.BlockSpec((FEAT_W, D_MODEL), lambda i: (0, 0)),
            pl.BlockSpec((GATE_ROWS, D_MODEL), lambda i: (0, 0)),
        ],
        out_specs=[
            pl.BlockSpec((ROW_TILE, KEY_W), lambda i: (i, 0)),
            pl.BlockSpec((ROW_TILE, TOK_W - KEY_W), lambda i: (i, 0)),
            pl.BlockSpec((1, FEAT_W, ROW_TILE), lambda i: (i // per_seq, 0, i % per_seq)),
            pl.BlockSpec((1, GATE_ROWS, ROW_TILE), lambda i: (i // per_seq, 0, i % per_seq)),
        ],
        out_shape=[
            jax.ShapeDtypeStruct((n, KEY_W), BF16),
            jax.ShapeDtypeStruct((n, TOK_W - KEY_W), F32),
            jax.ShapeDtypeStruct((bsz, FEAT_W, seq), BF16),
            jax.ShapeDtypeStruct((bsz, GATE_ROWS, seq), F32),
        ],
        compiler_params=_params(1, 32),
        name="inproj",
    )(x2d, w_tok, w_feat, w_gate)


def _compress_hidden(x_ref, which, pe_ref, w1_ref, n_str):
    xs = jnp.concatenate(
        [x_ref[0, pl.ds(p, n_str, stride=CMP_STRIDE), :] for p in range(CMP_STRIDE)], axis=1).astype(BF16)
    a = _dot(xs, w1_ref[0, which, 0])
    b = _dot(xs, w1_ref[0, which, 1])
    b_next = pltpu.roll(b, shift=n_str - 1, axis=0)
    bias = _dot(pe_ref[which, 0].astype(BF16), w1_ref[0, which, 0]) + _dot(pe_ref[which, 1].astype(BF16),
                                                                         w1_ref[0, which, 1])
    return _gelu(a + b_next + bias[0:1]).astype(BF16)


def _compress_kernel(xk_ref, xv_ref, pe_ref, w1_ref, w2k_ref, w2vt_ref, kc_ref, vct_ref):
    n_str = kc_ref.shape[2]
    kc_ref[0, 0] = _dot(_compress_hidden(xk_ref, 0, pe_ref, w1_ref, n_str), w2k_ref[...]).astype(BF16)
    vct_ref[0, 0] = _dot_nt(w2vt_ref[...], _compress_hidden(xv_ref, 1, pe_ref, w1_ref, n_str)).astype(BF16)


def _compress(x_cmp, pe, w1, w2k, w2vt):
    bsz, seq, width = x_cmp.shape
    n_str = seq // CMP_STRIDE

    def full(shape):
        return pl.BlockSpec(shape, lambda b, h: (0,) * len(shape))

    return pl.pallas_call(
        _compress_kernel,
        grid=(bsz, B_KV_HEADS),
        in_specs=[
            pl.BlockSpec((1, seq, LANES), lambda b, h: (b, 0, 0)),
            pl.BlockSpec((1, seq, LANES), lambda b, h: (b, 0, 1)),
            full(pe.shape),
            pl.BlockSpec((1,) + w1.shape[1:], lambda b, h: (h, 0, 0, 0, 0)),
            full((CMP_HIDDEN, HEAD_DIM)),
            full((HEAD_DIM, CMP_HIDDEN)),
        ],
        out_specs=[
            pl.BlockSpec((1, 1, n_str, HEAD_DIM), lambda b, h: (b, h, 0, 0)),
            pl.BlockSpec((1, 1, HEAD_DIM, n_str), lambda b, h: (b, h, 0, 0)),
        ],
        out_shape=[
            jax.ShapeDtypeStruct((bsz, B_KV_HEADS, n_str, HEAD_DIM), BF16),
            jax.ShapeDtypeStruct((bsz, B_KV_HEADS, HEAD_DIM, n_str), BF16),
        ],
        compiler_params=_params(2, 32),
        name="nsa_compress",
    )(x_cmp, x_cmp, pe, w1, w2k, w2vt)


def _bf16_parts(x, parts=3):
    out, rest = [], np.asarray(x, np.float32)
    for _ in range(parts):
        piece = rest.astype(jnp.bfloat16).astype(np.float32)
        out.append(piece)
        rest = rest - piece
    return out


def _aux_key_lanes(seq):
    t = np.arange(seq)
    aux = np.zeros((seq, LANES), np.float32)
    aux[t, t // SLC_BLOCK] = 1.0
    aux[:, AUX_SEL:AUX_SEL + 3] = (t >> 7)[:, None]
    aux[:, AUX_SEL + 3:AUX_SEL + 6] = (t & 127)[:, None]
    return jnp.asarray(aux, BF16)


def _aux_slope_rows():
    rows = np.zeros((N_ATT_HEADS // GROUP, LANES - AUX_SEL, GROUP * BLOCK_Q), np.float32)
    for grp in range(N_ATT_HEADS // GROUP):
        for g in range(GROUP):
            slope = ALIBI_SLOPES[grp * GROUP + g] * LOG2E
            cols = slice(g * BLOCK_Q, (g + 1) * BLOCK_Q)
            for i, piece in enumerate(_bf16_parts(128.0 * slope) + _bf16_parts(slope)):
                rows[grp, i, cols] = piece
    return jnp.asarray(rows, BF16)


def _attn_kernel(sink_ref, qt_ref, ka_ref, ks_ref, kw_ref, vat_ref, vst_ref, vwt_ref, kc_ref, vct_ref, gt_ref,
                 aux_ref, slope_ref, o_ref, ot_ref, mask_a_ref, mask_w_ref, s_ref, e_ref, *, seq):
    n = pl.program_id(1)
    t0 = n * BLOCK_Q
    n_str = seq // CMP_STRIDE
    n_cmp = n_str - CMP_LEN // CMP_STRIDE + 1
    n_sel = seq // SLC_BLOCK
    top_n = min(SLC_TOPN, n_sel)
    hd = HEAD_DIM
    gq = GROUP * BLOCK_Q
    ch = SLC_CHUNK
    span_a = A_WINDOW + BLOCK_Q
    span_w = B_WINDOW + BLOCK_Q

    lane = lax.broadcasted_iota(jnp.int32, (1, gq), 1)
    q_loc = lane & (BLOCK_Q - 1)
    lane_head = lane >> (BLOCK_Q.bit_length() - 1)
    t_q = t0 + q_loc

    def head_row(vals):
        return jnp.where(lane_head == 0, vals[0], jnp.where(lane_head == 1, vals[1],
                                                           jnp.where(lane_head == 2, vals[2], vals[3])))

    def q_group(first_head):
        return jnp.concatenate(
            [qt_ref[0, (first_head + g) * hd:(first_head + g + 1) * hd, :] for g in range(GROUP)], axis=1)

    def score_rhs(grp, kvh, qg, sel_rows):
        zero = jnp.zeros((hd, gq), BF16)
        q_rows = [qg, zero] if kvh == 0 else [zero, qg]
        return jnp.concatenate(q_rows + [sel_rows, slope_ref[grp]], axis=0)

    def score_lhs(k_ref, start, span):
        return jnp.concatenate([k_ref[0, pl.ds(start, span), :], aux_ref[pl.ds(start, span), :]], axis=1)

    def band_start(span):
        return pl.multiple_of(jnp.maximum(t0 - (span - BLOCK_Q), 0), BLOCK_Q)

    def band_mask(span, window):
        ik = lax.broadcasted_iota(jnp.int32, (span, gq), 0)
        dist = (t0 - band_start(span)) + q_loc - ik
        return jnp.where((dist >= 0) & (dist < window), 0.0, -MASK_BIG)

    @pl.when(n <= span_w // BLOCK_Q - 1)
    def _():
        mask_a_ref[...] = band_mask(span_a, A_WINDOW)
        mask_w_ref[...] = band_mask(span_w, B_WINDOW)

    no_sel = jnp.zeros((AUX_SEL, gq), BF16)
    n_grp = N_ATT_HEADS // GROUP
    slopes = [head_row([ALIBI_SLOPES[grp * GROUP + g] for g in range(GROUP)]) for grp in range(n_grp)]
    qgs = [q_group(grp * GROUP) for grp in range(n_grp)]

    def softmax_cols(s, sinks=None):
        m = jnp.max(s, axis=0, keepdims=True)
        if sinks is not None:
            m = jnp.maximum(m, sinks)
        e = jnp.exp2(s - m)
        den = jnp.sum(e, axis=0, keepdims=True)
        if sinks is not None:
            den = den + jnp.exp2(sinks - m)
        return e.astype(BF16), den

    ks_a = band_start(span_a)
    ks_w = band_start(span_w)
    lhs_a = score_lhs(ka_ref, ks_a, span_a)
    lhs_w = score_lhs(kw_ref, ks_w, span_w)
    s_a = [_dot(lhs_a, score_rhs(kvh, kvh, qgs[kvh], no_sel)) + mask_a_ref[...] for kvh in range(A_KV_HEADS)]
    s_w = [_dot(lhs_w, score_rhs(A_KV_HEADS + kvh, kvh, qgs[A_KV_HEADS + kvh], no_sel)) + mask_w_ref[...]
           for kvh in range(B_KV_HEADS)]

    c_idx = lax.broadcasted_iota(jnp.int32, (n_str, gq), 0)
    dist_c = t_q - (c_idx * CMP_STRIDE + CMP_LEN - 1)
    valid_c = (dist_c >= 0) & (c_idx < n_cmp)
    dist_cf = dist_c.astype(F32)
    s_c = [jnp.where(valid_c, _dot(kc_ref[0, kvh], qgs[A_KV_HEADS + kvh])
                     - (slopes[A_KV_HEADS + kvh] * LOG2E) * dist_cf, NEG_INF) for kvh in range(B_KV_HEADS)]

    oj = lax.broadcasted_iota(jnp.int32, (n_sel, n_str), 0) * SLC_BLOCK
    oc = lax.broadcasted_iota(jnp.int32, (n_sel, n_str), 1) * CMP_STRIDE
    overlap_t = ((oc < oj + SLC_BLOCK) & (oc + CMP_LEN > oj)).astype(BF16)
    j_idx = lax.broadcasted_iota(jnp.int32, (n_sel, BLOCK_Q), 0)
    t_blk = (t0 + lax.broadcasted_iota(jnp.int32, (1, BLOCK_Q), 1)) >> (SLC_BLOCK.bit_length() - 1)
    valid_j = j_idx <= t_blk
    forced_j = (j_idx == 0) | (j_idx == t_blk) | (j_idx == t_blk - 1)
    e_c, den_c, rhs_s = [], [], []
    for kvh in range(B_KV_HEADS):
        m = jnp.max(s_c[kvh], axis=0, keepdims=True)
        m = jnp.where(m == NEG_INF, 0.0, m)
        e = jnp.exp2(s_c[kvh] - m)
        den = jnp.sum(e, axis=0, keepdims=True)
        den = jnp.where(den > 0.0, den, 1.0)
        e_c.append(e.astype(BF16))
        den_c.append(den)
        p = e / den
        psum = p[:, 0:BLOCK_Q]
        for g in range(1, GROUP):
            psum = psum + p[:, g * BLOCK_Q:(g + 1) * BLOCK_Q]
        p_hi = psum.astype(BF16)
        p_lo = (psum - p_hi.astype(F32)).astype(BF16)
        imp = _dot(overlap_t, p_hi) + _dot(overlap_t, p_lo)
        rank = jnp.where(forced_j, RANK_FORCED, jnp.where(valid_j, imp, RANK_INVALID))
        cnt = jnp.zeros((n_sel, BLOCK_Q), jnp.int32)
        for jp in range(n_sel):
            row = rank[jp:jp + 1, :]
            before = (row > rank) | ((row == rank) & (j_idx > jp))
            cnt = cnt + before.astype(jnp.int32)
        sel_bias = jnp.where(cnt < top_n, 0.0, -MASK_BIG)
        if n_sel < AUX_SEL:
            sel_bias = jnp.concatenate([sel_bias, jnp.zeros((AUX_SEL - n_sel, BLOCK_Q), F32)], axis=0)
        sel_rows = jnp.concatenate([sel_bias.astype(BF16)] * GROUP, axis=1)
        rhs_s.append(score_rhs(A_KV_HEADS + kvh, kvh, qgs[A_KV_HEADS + kvh], sel_rows))

    t_qf = t_q.astype(F32)
    sinks = [(head_row([sink_ref[kvh * GROUP + g] for g in range(GROUP)]) + slopes[kvh] * t_qf) * LOG2E
             for kvh in range(A_KV_HEADS)]
    ed_a = [softmax_cols(s_a[kvh], sinks[kvh]) for kvh in range(A_KV_HEADS)]
    ed_w = [softmax_cols(s_w[kvh]) for kvh in range(B_KV_HEADS)]

    def rows(kvh):
        return slice(kvh * hd, (kvh + 1) * hd)

    o_a = [_dot(vat_ref[0, rows(kvh), pl.ds(ks_a, span_a)], ed_a[kvh][0]) / ed_a[kvh][1] for kvh in range(A_KV_HEADS)]
    o_w = [_dot(vwt_ref[0, rows(kvh), pl.ds(ks_w, span_w)], ed_w[kvh][0]) / ed_w[kvh][1] for kvh in range(B_KV_HEADS)]
    o_c = [_dot(vct_ref[0, kvh], e_c[kvh]) / den_c[kvh] for kvh in range(B_KV_HEADS)]
    for kvh in range(A_KV_HEADS):
        for g in range(GROUP):
            h = kvh * GROUP + g
            ot_ref[h * hd:(h + 1) * hd, :] = o_a[kvh][:, g * BLOCK_Q:(g + 1) * BLOCK_Q]

    dloc = q_loc - lax.broadcasted_iota(jnp.int32, (ch, gq), 0)
    n_full = t0 // ch

    def slc_scores(c):
        lhs = score_lhs(ks_ref, pl.multiple_of(c * ch, ch), ch)
        for kvh in range(B_KV_HEADS):
            s_ref[c & 1, kvh] = _dot(lhs, rhs_s[kvh])

    def slc_values(c, state, kvh):
        m, l, acc, alpha = state
        vt = vst_ref[0, rows(kvh), pl.ds(pl.multiple_of(jnp.maximum(c, 0) * ch, ch), ch)]
        return m, l, alpha * acc + _dot(vt, e_ref[kvh]), alpha

    def slc_softmax(c, state, kvh, causal=None):
        m, l, acc, _ = state
        s = s_ref[c & 1, kvh]
        if causal is not None:
            s = jnp.where(causal, s, -MASK_BIG)
        m_new = jnp.maximum(m, jnp.max(s, axis=0, keepdims=True))
        alpha = jnp.exp2(m - m_new)
        e = jnp.exp2(s - m_new)
        e_ref[kvh] = e.astype(BF16)
        return m_new, alpha * l + jnp.sum(e, axis=0, keepdims=True), acc, alpha

    def slc_step(c, states):
        states = tuple(slc_values(c - 1, states[kvh], kvh) for kvh in range(B_KV_HEADS))
        states = tuple(slc_softmax(c, states[kvh], kvh) for kvh in range(B_KV_HEADS))
        slc_scores(c + 1)
        return states

    e_ref[...] = jnp.zeros_like(e_ref)
    init = (jnp.full((1, gq), NEG_INF, F32), jnp.zeros((1, gq), F32), jnp.zeros((hd, gq), F32),
            jnp.ones((1, gq), F32))
    slc_scores(jnp.int32(0))
    states = lax.fori_loop(0, n_full, slc_step, (init,) * B_KV_HEADS)
    causal = dloc >= n_full * ch - t0
    gt = jax.nn.sigmoid(gt_ref[0])
    for kvh in range(B_KV_HEADS):
        state = slc_values(n_full - 1, states[kvh], kvh)
        state = slc_softmax(n_full, state, kvh, causal)
        _, l_fin, acc_fin, _ = slc_values(n_full, state, kvh)
        o_slc = acc_fin / l_fin
        for g in range(GROUP):
            hb = kvh * GROUP + g
            h = A_HEADS + hb
            cols = slice(g * BLOCK_Q, (g + 1) * BLOCK_Q)
            ot_ref[h * hd:(h + 1) * hd, :] = (
                gt[hb:hb + 1] * o_c[kvh][:, cols]
                + gt[B_HEADS + hb:B_HEADS + hb + 1] * o_slc[:, cols]
                + gt[2 * B_HEADS + hb:2 * B_HEADS + hb + 1] * o_w[kvh][:, cols]
            )

    o_ref[0] = ot_ref[...].T.astype(BF16)


def _attention(sinks, tok, feat, kc, vct, gates_t, seq):
    bsz = tok.shape[0]
    n_str = seq // CMP_STRIDE
    kv_w = A_KV_HEADS * HEAD_DIM
    q_rows = N_ATT_HEADS * HEAD_DIM
    gq = GROUP * BLOCK_Q
    assert seq // SLC_BLOCK <= AUX_SEL and seq <= 128 * 256

    def k_spec(col_block):
        return pl.BlockSpec((1, seq, kv_w), lambda b, n: (b, 0, col_block))

    def vt_spec(row_block):
        return pl.BlockSpec((1, kv_w, seq), lambda b, n: (b, q_rows // kv_w + row_block, 0))

    return pl.pallas_call(
        functools.partial(_attn_kernel, seq=seq),
        grid=(bsz, seq // BLOCK_Q),
        in_specs=[
            pl.BlockSpec(memory_space=pltpu.SMEM),
            pl.BlockSpec((1, q_rows, BLOCK_Q), lambda b, n: (b, 0, n)),
            k_spec(0), k_spec(1), k_spec(2),
            vt_spec(0), vt_spec(1), vt_spec(2),
            pl.BlockSpec((1, B_KV_HEADS, n_str, HEAD_DIM), lambda b, n: (b, 0, 0, 0)),
            pl.BlockSpec((1, B_KV_HEADS, HEAD_DIM, n_str), lambda b, n: (b, 0, 0, 0)),
            pl.BlockSpec((1, GATE_ROWS, BLOCK_Q), lambda b, n: (b, 0, n)),
            pl.BlockSpec((seq, LANES), lambda b, n: (0, 0)),
            pl.BlockSpec((N_ATT_HEADS // GROUP, LANES - AUX_SEL, gq), lambda b, n: (0, 0, 0)),
        ],
        out_specs=pl.BlockSpec((1, BLOCK_Q, q_rows), lambda b, n: (b, n, 0)),
        out_shape=jax.ShapeDtypeStruct((bsz, seq, q_rows), BF16),
        scratch_shapes=[
            pltpu.VMEM((q_rows, BLOCK_Q), F32),
            pltpu.VMEM((A_WINDOW + BLOCK_Q, gq), F32),
            pltpu.VMEM((B_WINDOW + BLOCK_Q, gq), F32),
            pltpu.VMEM((2, B_KV_HEADS, SLC_CHUNK, gq), F32),
            pltpu.VMEM((B_KV_HEADS, SLC_CHUNK, gq), BF16),
        ],
        compiler_params=_params(2, 32),
        name="hybrid_attention",
    )(sinks, feat, tok, tok, tok, feat, feat, feat, kc, vct, gates_t, _aux_key_lanes(seq), _aux_slope_rows())


def _proj_ffn_ln_kernel(o_ref, wo_ref, x0_ref, g0_ref, b0_ref, wg_ref, wu_ref, wd_ref, g_ref, b_ref, y_ref, *, alpha):
    x = _layer_norm(alpha * x0_ref[...] + _dot(o_ref[...], wo_ref[...]), g0_ref[...], b0_ref[...])
    xb = x.astype(BF16)
    d_ff = wg_ref.shape[1]
    y = None
    for lo in range(0, d_ff, FFN_CHUNK):
        hi = min(lo + FFN_CHUNK, d_ff)
        h = (_silu(_dot(xb, wg_ref[:, lo:hi])) * _dot(xb, wu_ref[:, lo:hi])).astype(BF16)
        part = _dot(h, wd_ref[lo:hi, :])
        y = part if y is None else y + part
    y_ref[...] = _layer_norm(alpha * x + y, g_ref[...], b_ref[...])


def _proj_ffn_ln(o2d, wo, x2d, g0, b0, wg, wu, wd, g, b, alpha):
    n, k = o2d.shape
    d_ff = wg.shape[1]
    row = pl.BlockSpec((ROW_TILE, D_MODEL), lambda i: (i, 0))
    vec = pl.BlockSpec((1, D_MODEL), lambda i: (0, 0))
    once = pl.Buffered(1)
    return pl.pallas_call(
        functools.partial(_proj_ffn_ln_kernel, alpha=alpha),
        grid=(n // ROW_TILE,),
        in_specs=[
            pl.BlockSpec((ROW_TILE, k), lambda i: (i, 0)),
            pl.BlockSpec((k, D_MODEL), lambda i: (0, 0), pipeline_mode=once),
            row, vec, vec,
            pl.BlockSpec((D_MODEL, d_ff), lambda i: (0, 0), pipeline_mode=once),
            pl.BlockSpec((D_MODEL, d_ff), lambda i: (0, 0), pipeline_mode=once),
            pl.BlockSpec((d_ff, D_MODEL), lambda i: (0, 0), pipeline_mode=once),
            vec, vec,
        ],
        out_specs=row,
        out_shape=jax.ShapeDtypeStruct((n, D_MODEL), F32),
        compiler_params=_params(1, 52),
        name="out_proj_swiglu_ln",
    )(o2d, wo, x2d, g0, b0, wg, wu, wd, g, b)


def _expert_kernel(be_ref, used_ref, x_ref, wg_ref, wu_ref, wd_ref, y_ref, xb_ref):
    del be_ref
    f = pl.program_id(1)
    in_use = pl.program_id(0) < used_ref[0]

    def hidden(x):
        return (_silu(_dot(x, wg_ref[0])) * _dot(x, wu_ref[0])).astype(BF16)

    def down(h, first):
        for lo in range(0, D_MODEL, MXU_WIDTH):
            part = _dot(h, wd_ref[0, :, lo:lo + MXU_WIDTH])
            if first:
                y_ref[:, lo:lo + MXU_WIDTH] = part
            else:
                y_ref[:, lo:lo + MXU_WIDTH] += part

    @pl.when(in_use & (f == 0))
    def _():
        xb = x_ref[...].astype(BF16)
        xb_ref[...] = xb
        down(hidden(xb), True)

    @pl.when(in_use & (f != 0))
    def _():
        down(hidden(xb_ref[...]), False)

    @pl.when(jnp.logical_not(in_use) & (f == 0))
    def _():
        y_ref[...] = jnp.zeros_like(y_ref)


def _experts(blk_e, n_used, xin, wg, wu, wd):
    n_rows = blk_e.shape[0] * MOE_ROWS
    d_ff = wg.shape[2]
    n_f = d_ff // EXPERT_TILE

    def row_blk(i, used):
        return jnp.minimum(i, used[0] - 1)

    def hid_blk(i, f, used):
        return jnp.where(i < used[0], f, n_f - 1)

    return pl.pallas_call(
        _expert_kernel,
        grid_spec=pltpu.PrefetchScalarGridSpec(
            num_scalar_prefetch=2,
            grid=(n_rows // MOE_ROWS, n_f),
            in_specs=[
                pl.BlockSpec((MOE_ROWS, D_MODEL), lambda i, f, be, used: (row_blk(i, used), 0)),
                pl.BlockSpec((1, D_MODEL, EXPERT_TILE), lambda i, f, be, used: (be[i], 0, hid_blk(i, f, used))),
                pl.BlockSpec((1, D_MODEL, EXPERT_TILE), lambda i, f, be, used: (be[i], 0, hid_blk(i, f, used))),
                pl.BlockSpec((1, EXPERT_TILE, D_MODEL), lambda i, f, be, used: (be[i], hid_blk(i, f, used), 0)),
            ],
            out_specs=pl.BlockSpec((MOE_ROWS, D_MODEL), lambda i, f, be, used: (i, 0)),
            scratch_shapes=[pltpu.VMEM((MOE_ROWS, D_MODEL), BF16)],
        ),
        out_shape=jax.ShapeDtypeStruct((n_rows, D_MODEL), F32),
        compiler_params=_params(2, 52),
        name="expert_swiglu",
    )(blk_e, n_used, xin, wg, wu, wd)


def _gmlp_kernel(x_ref, win_ref, lng_ref, lnb_ref, ws_ref, bs_ref, wout_ref, g_ref, b_ref, y_ref,
                 u_ref, vn_ref, gated_ref, *, alpha):
    x = x_ref[...]
    xb = x.astype(BF16)
    u_ref[...] = _gelu(_dot(xb, win_ref[:, :D_MODEL]))
    v = _gelu(_dot(xb, win_ref[:, D_MODEL:]))
    vn_ref[...] = _layer_norm(v, lng_ref[...], lnb_ref[...]).astype(BF16)
    cs = GMLP_CHUNK
    lower = lax.broadcasted_iota(jnp.int32, (cs, cs), 0) >= lax.broadcasted_iota(jnp.int32, (cs, cs), 1)
    for grp in range(GMLP_GROUPS):
        w = jnp.where(lower, ws_ref[grp], 0.0).astype(BF16)
        bias = bs_ref[grp]
        cols = slice(grp * cs, (grp + 1) * cs)
        for c in range(x.shape[0] // cs):
            rows = slice(c * cs, (c + 1) * cs)
            mixed = _dot(w, vn_ref[rows, cols]) + bias
            gated_ref[rows, cols] = (u_ref[rows, cols] * mixed).astype(BF16)
    y = _dot(gated_ref[...], wout_ref[...])
    y_ref[...] = _layer_norm(alpha * x + y, g_ref[...], b_ref[...])


def _gmlp_ln(x2d, w_in, ln_g, ln_b, w_s, b_s, w_out, g, b, alpha):
    n = x2d.shape[0]
    row = pl.BlockSpec((ROW_TILE, D_MODEL), lambda i: (i, 0))
    vec = pl.BlockSpec((1, D_MODEL), lambda i: (0, 0))
    grp = pl.BlockSpec((GMLP_GROUPS, GMLP_CHUNK, GMLP_CHUNK), lambda i: (0, 0, 0))
    return pl.pallas_call(
        functools.partial(_gmlp_kernel, alpha=alpha),
        grid=(n // ROW_TILE,),
        in_specs=[
            row,
            pl.BlockSpec((D_MODEL, 2 * D_MODEL), lambda i: (0, 0)),
            vec, vec, grp, grp,
            pl.BlockSpec((D_MODEL, D_MODEL), lambda i: (0, 0)),
            vec, vec,
        ],
        out_specs=row,
        out_shape=jax.ShapeDtypeStruct((n, D_MODEL), F32),
        scratch_shapes=[
            pltpu.VMEM((ROW_TILE, D_MODEL), F32),
            pltpu.VMEM((ROW_TILE, D_MODEL), BF16),
            pltpu.VMEM((ROW_TILE, D_MODEL), BF16),
        ],
        compiler_params=_params(1, 40),
        name="gmlp_ln",
    )(x2d, w_in, ln_g, ln_b, w_s, b_s, w_out, g, b)


def _route_kernel(x_ref, w_ref, gate_ref, et_ref, post_ref, cnt_ref, run_ref):
    tm = x_ref.shape[0]

    @pl.when(pl.program_id(0) == 0)
    def _():
        run_ref[...] = jnp.zeros_like(run_ref)

    logits = lax.dot_general(w_ref[...], x_ref[...], (((1,), (1,)), ((), ())),
                             preferred_element_type=F32, precision=lax.Precision.HIGHEST)
    row = lax.broadcasted_iota(jnp.int32, (N_EXPERTS, tm), 0)
    v1 = jnp.max(logits, axis=0, keepdims=True)
    i1 = jnp.min(jnp.where(logits == v1, row, N_EXPERTS), axis=0, keepdims=True)
    rest = jnp.where(row == i1, NEG_INF, logits)
    v2 = jnp.max(rest, axis=0, keepdims=True)
    i2 = jnp.min(jnp.where(rest == v2, row, N_EXPERTS), axis=0, keepdims=True)
    d = jnp.exp(v2 - v1)
    g1 = 1.0 / (1.0 + d)
    g2 = d / (1.0 + d)

    hit1 = row == i1
    hit2 = row == i2
    onehot = (hit1 | hit2).astype(BF16)
    earlier = (lax.broadcasted_iota(jnp.int32, (tm, tm), 0) < lax.broadcasted_iota(jnp.int32, (tm, tm), 1)).astype(BF16)
    before = _dot(onehot, earlier) + run_ref[:, 0:1]
    pos1 = jnp.sum(jnp.where(hit1, before, 0.0), axis=0, keepdims=True).astype(jnp.int32)
    pos2 = jnp.sum(jnp.where(hit2, before, 0.0), axis=0, keepdims=True).astype(jnp.int32)
    run_ref[...] += jnp.sum(onehot.astype(F32), axis=1, keepdims=True)

    et_ref[...] = jnp.where(row == 0, i1, jnp.where(row == 1, i2, 0))
    post_ref[...] = jnp.where(row == 0, pos1, jnp.where(row == 1, pos2, 0))
    cnt_ref[...] = run_ref[...]
    wide = lax.broadcasted_iota(jnp.int32, (LANES, tm), 0)
    gate_ref[...] = jnp.where(wide == 0, g1, jnp.where(wide == 1, g2, 0.0)).T


def _route(x2d, w_router_t):
    n = x2d.shape[0]
    tm = ROW_TILE
    per_tok = pl.BlockSpec((N_EXPERTS, tm), lambda i: (0, i))
    return pl.pallas_call(
        _route_kernel,
        grid=(n // tm,),
        in_specs=[
            pl.BlockSpec((tm, D_MODEL), lambda i: (i, 0)),
            pl.BlockSpec((N_EXPERTS, D_MODEL), lambda i: (0, 0)),
        ],
        out_specs=[
            pl.BlockSpec((tm, LANES), lambda i: (i, 0)),
            per_tok, per_tok,
            pl.BlockSpec((N_EXPERTS, LANES), lambda i: (0, 0)),
        ],
        out_shape=[
            jax.ShapeDtypeStruct((n, LANES), F32),
            jax.ShapeDtypeStruct((N_EXPERTS, n), jnp.int32),
            jax.ShapeDtypeStruct((N_EXPERTS, n), jnp.int32),
            jax.ShapeDtypeStruct((N_EXPERTS, LANES), F32),
        ],
        scratch_shapes=[pltpu.VMEM((N_EXPERTS, LANES), F32)],
        compiler_params=_params(1, 24),
        name="moe_router",
    )(x2d, w_router_t)


def _dispatch_kernel(d0_ref, d1_ref, lo_ref, hi_ref, x_ref, xin_hbm, zero_ref, sem_rows, sem_zero):
    step = pl.program_id(0)
    tm = x_ref.shape[0]
    base = step * tm

    def row_copy(r, dest_ref):
        return pltpu.make_async_copy(x_ref.at[pl.ds(r, 1)], xin_hbm.at[pl.ds(dest_ref[base + r], 1)], sem_rows)

    def issue(r8, carry):
        for j in range(DMA_UNROLL):
            r = r8 * DMA_UNROLL + j
            row_copy(r, d0_ref).start(priority=j % 2)
            row_copy(r, d1_ref).start(priority=(j + 1) % 2)
        return carry

    lax.fori_loop(0, tm // DMA_UNROLL, issue, 0)

    @pl.when(step == pl.num_programs(0) - 1)
    def _():
        zero_ref[...] = jnp.zeros_like(zero_ref)
        for k in range(N_EXPERTS + 1):
            def fill(r, carry):
                pltpu.make_async_copy(zero_ref, xin_hbm.at[pl.ds(r, 1)], sem_zero).start()
                return carry

            def drain(r, carry):
                pltpu.make_async_copy(zero_ref, xin_hbm.at[pl.ds(r, 1)], sem_zero).wait()
                return carry

            lax.fori_loop(lo_ref[k], hi_ref[k], fill, 0)
            lax.fori_loop(lo_ref[k], hi_ref[k], drain, 0)

    pltpu.make_async_copy(x_ref, xin_hbm.at[pl.ds(0, tm)], sem_rows).wait()
    pltpu.make_async_copy(x_ref, xin_hbm.at[pl.ds(0, tm)], sem_rows).wait()


def _dispatch(d0, d1, fill_lo, fill_hi, x2d, n_rows):
    n = x2d.shape[0]
    tm = ROW_TILE
    return pl.pallas_call(
        _dispatch_kernel,
        grid_spec=pltpu.PrefetchScalarGridSpec(
            num_scalar_prefetch=4,
            grid=(n // tm,),
            in_specs=[pl.BlockSpec((tm, D_MODEL), lambda i, *_: (i, 0))],
            out_specs=pl.BlockSpec(memory_space=pl.ANY),
            scratch_shapes=[pltpu.VMEM((1, D_MODEL), F32), pltpu.SemaphoreType.DMA(()), pltpu.SemaphoreType.DMA(())],
        ),
        out_shape=jax.ShapeDtypeStruct((n_rows, D_MODEL), F32),
        compiler_params=_params(1, 16),
        name="moe_dispatch",
    )(d0, d1, fill_lo, fill_hi, x2d)


def _combine_kernel(d0_ref, d1_ref, y_hbm, gate_ref, x_ref, g_ref, b_ref, o_ref, buf0, buf1, sem, *, alpha):
    tm = x_ref.shape[0]
    base = pl.program_id(0) * tm

    def issue(r8, carry):
        for j in range(DMA_UNROLL):
            r = r8 * DMA_UNROLL + j
            pltpu.make_async_copy(y_hbm.at[pl.ds(d0_ref[base + r], 1)], buf0.at[pl.ds(r, 1)], sem.at[0]).start(priority=0)
            pltpu.make_async_copy(y_hbm.at[pl.ds(d1_ref[base + r], 1)], buf1.at[pl.ds(r, 1)], sem.at[1]).start(priority=1)
        return carry

    lax.fori_loop(0, tm // DMA_UNROLL, issue, 0)
    pltpu.make_async_copy(y_hbm.at[pl.ds(0, tm)], buf0, sem.at[0]).wait()
    pltpu.make_async_copy(y_hbm.at[pl.ds(0, tm)], buf1, sem.at[1]).wait()
    gate = gate_ref[...]
    y = buf0[...] * gate[:, 0:1] + buf1[...] * gate[:, 1:2]
    o_ref[...] = _layer_norm(alpha * x_ref[...] + y, g_ref[...], b_ref[...])


def _combine_ln(d0, d1, yb, gate, x2d, g, b, alpha):
    n = x2d.shape[0]
    tm = GATHER_ROWS
    row = pl.BlockSpec((tm, D_MODEL), lambda i, a, c: (i, 0))
    vec = pl.BlockSpec((1, D_MODEL), lambda i, a, c: (0, 0))
    return pl.pallas_call(
        functools.partial(_combine_kernel, alpha=alpha),
        grid_spec=pltpu.PrefetchScalarGridSpec(
            num_scalar_prefetch=2,
            grid=(n // tm,),
            in_specs=[
                pl.BlockSpec(memory_space=pl.ANY),
                pl.BlockSpec((tm, LANES), lambda i, a, c: (i, 0)),
                row, vec, vec,
            ],
            out_specs=row,
            scratch_shapes=[
                pltpu.VMEM((tm, D_MODEL), F32),
                pltpu.VMEM((tm, D_MODEL), F32),
                pltpu.SemaphoreType.DMA((2,)),
            ],
        ),
        out_shape=jax.ShapeDtypeStruct((n, D_MODEL), F32),
        compiler_params=_params(1, 24),
        name="moe_combine_ln",
    )(d0, d1, yb, gate, x2d, g, b)


def _attention_heads(x2d, bsz, seq, w_in, sinks, pe_k, wk1, wk2, pe_v, wv1, wv2):
    aq, akv = A_HEADS * HEAD_DIM, A_KV_HEADS * HEAD_DIM
    bq, bkv = B_HEADS * HEAD_DIM, B_KV_HEADS * HEAD_DIM
    bounds = [0]
    for width in (aq, akv, akv, bq, bkv, bkv, bkv, bkv, bkv, bkv, 3 * B_HEADS):
        bounds.append(bounds[-1] + width)
    qa, ka, va, qb, kc, vc, ks, vs, kw, vw, wg = [w_in[:, bounds[i]:bounds[i + 1]] for i in range(11)]
    w_tok = jnp.concatenate([ka, ks, kw, kc, vc], axis=1).astype(BF16)
    w_feat = jnp.concatenate([qa * (SCALE * LOG2E), qb * (SCALE * LOG2E), va, vs, vw], axis=1).T.astype(BF16)
    wg = wg.reshape(D_MODEL, B_HEADS, 3).transpose(2, 1, 0).reshape(3 * B_HEADS, D_MODEL)
    wg = jnp.pad(wg, ((0, GATE_ROWS - 3 * B_HEADS), (0, 0))).astype(BF16)
    tok, x_cmp, feat, gates_t = _inproj(x2d, w_tok, w_feat, wg, bsz, seq)

    halves = CMP_LEN // CMP_STRIDE
    w1 = jnp.stack([wk1, wv1]).reshape(2, halves, CMP_STRIDE, 1, HEAD_DIM, CMP_HIDDEN)
    own_head = jnp.eye(B_KV_HEADS, dtype=F32)[:, None, None, None, :, None, None]
    w1 = (w1[None] * own_head).reshape(B_KV_HEADS, 2, halves, CMP_STRIDE * bkv, CMP_HIDDEN).astype(BF16)
    pe = jnp.stack([pe_k, pe_v]).reshape(2, halves, CMP_STRIDE, 1, HEAD_DIM)
    pe = jnp.broadcast_to(pe, (2, halves, CMP_STRIDE, B_KV_HEADS, HEAD_DIM)).reshape(2, halves, 1, CMP_STRIDE * bkv)
    pe = jnp.broadcast_to(pe, (2, halves, 8, CMP_STRIDE * bkv))
    kcmp, vcmp_t = _compress(x_cmp.reshape(bsz, seq, 2 * bkv), pe, w1, wk2.astype(BF16), wv2.T.astype(BF16))

    o = _attention(sinks, tok.reshape(bsz, seq, KEY_W), feat, kcmp, vcmp_t, gates_t, seq)
    return o.reshape(bsz * seq, N_ATT_HEADS * HEAD_DIM)


def _moe_layer(x2d, w_router, w_gate, w_up, w_down, g, b, alpha):
    n_tok = x2d.shape[0]
    gate, e_t, pos_t, cnt = _route(x2d, w_router.T)
    counts = cnt[:, 0].astype(jnp.int32)
    padded = (counts + MOE_ROWS - 1) // MOE_ROWS * MOE_ROWS
    pad_ends = jnp.cumsum(padded)
    pad_starts = pad_ends - padded
    n_blocks = n_tok * TOP_K // MOE_ROWS + N_EXPERTS
    n_rows = n_blocks * MOE_ROWS
    experts = jnp.arange(N_EXPERTS, dtype=jnp.int32)[:, None, None]
    start_of = jnp.sum(jnp.where(e_t[None, :TOP_K] == experts, pad_starts[:, None, None], 0), axis=0)
    dest = start_of + pos_t[:TOP_K]
    fill_lo = jnp.concatenate([pad_starts + counts, pad_ends[-1:]])
    fill_hi = jnp.concatenate([pad_ends, jnp.full((1,), n_rows, jnp.int32)])
    blk_start = jnp.arange(n_blocks, dtype=jnp.int32) * MOE_ROWS
    blk_e = jnp.sum((pad_ends[None, :] <= blk_start[:, None]).astype(jnp.int32), axis=1)
    blk_e = jnp.minimum(blk_e, N_EXPERTS - 1)

    xin = _dispatch(dest[0], dest[1], fill_lo, fill_hi, x2d, n_rows)
    n_used = (pad_ends[-1:] // MOE_ROWS).astype(jnp.int32)
    yb = _experts(blk_e, n_used, xin, w_gate.astype(BF16), w_up.astype(BF16), w_down.astype(BF16))
    return _combine_ln(dest[0], dest[1], yb, gate, x2d, g, b, alpha)


def kernel(x, att_w_in, att_sinks, cmp_pe_k, cmp_wk1, cmp_wk2, cmp_pe_v, cmp_wv1, cmp_wv2, att_w_o,
           ffn_w_gate, ffn_w_up, ffn_w_down, gmlp_w_in, gmlp_ln_g, gmlp_ln_b, gmlp_w_s, gmlp_b_s, gmlp_w_out,
           moe_w_router, moe_w_gate, moe_w_up, moe_w_down, ln_g, ln_b):
    bsz, seq, dm = x.shape
    depth = ln_g.shape[0]
    alpha = (2.0 * depth) ** 0.25
    assert dm == D_MODEL and seq % ROW_TILE == 0
    x2d = x.reshape(bsz * seq, dm)

    def vec(p):
        return p.reshape(1, -1)

    for i in range(depth):
        j = i // 2
        g0, b0, g1, b1 = vec(ln_g[i, 0]), vec(ln_b[i, 0]), vec(ln_g[i, 1]), vec(ln_b[i, 1])
        if i % 2 == 0:
            o2d = _attention_heads(x2d, bsz, seq, att_w_in[j], att_sinks[j], cmp_pe_k[j], cmp_wk1[j], cmp_wk2[j],
                                   cmp_pe_v[j], cmp_wv1[j], cmp_wv2[j])
            x2d = _proj_ffn_ln(o2d, att_w_o[j].astype(BF16), x2d, g0, b0, ffn_w_gate[j].astype(BF16),
                               ffn_w_up[j].astype(BF16), ffn_w_down[j].astype(BF16), g1, b1, alpha)
        else:
            b_s = jnp.broadcast_to(gmlp_b_s[j][:, :, None], (GMLP_GROUPS, GMLP_CHUNK, GMLP_CHUNK))
            x2d = _gmlp_ln(x2d, gmlp_w_in[j].astype(BF16), vec(gmlp_ln_g[j]), vec(gmlp_ln_b[j]), gmlp_w_s[j], b_s,
                           gmlp_w_out[j].astype(BF16), g0, b0, alpha)
            x2d = _moe_layer(x2d, moe_w_router[j], moe_w_gate[j], moe_w_up[j], moe_w_down[j], g1, b1, alpha)
    return x2d.reshape(bsz, seq, dm)
```

```python
import functools

import jax
import jax.numpy as jnp
import numpy as np
from jax import lax
from jax.experimental import pallas as pl
from jax.experimental.pallas import tpu as pltpu

F32 = jnp.float32
BF16 = jnp.bfloat16

D_MODEL = 1024
HEAD_DIM = 64
BLOCK_Q = 128
A_HEADS = 8
A_KV_HEADS = 2
A_WINDOW = 128
B_HEADS = 8
B_KV_HEADS = 2
GROUP = 4
CMP_LEN = 32
CMP_STRIDE = 16
CMP_HIDDEN = 256
SLC_BLOCK = 64
SLC_TOPN = 8
B_WINDOW = 256
N_ATT_HEADS = A_HEADS + B_HEADS
GMLP_GROUPS = 8
GMLP_CHUNK = 128
N_EXPERTS = 8
TOP_K = 2
LN_EPS = 1e-5
LANES = 128
MXU_WIDTH = 256
DMA_UNROLL = 8

TOK_W = 640
KEY_W = 384
FEAT_W = 1408
GATE_ROWS = 32
SLC_CHUNK = 256

ROW_TILE = 512
MOE_ROWS = 512
COPY_TOKENS = 1024
FFN_CHUNK = 1024
EXPERT_TILE = 1792

ALIBI_SLOPES = tuple(2.0 ** (-8.0 * h / N_ATT_HEADS) for h in range(1, N_ATT_HEADS + 1))
SCALE = HEAD_DIM ** -0.5
LOG2E = 1.4426950408889634
NEG_INF = float("-inf")
RANK_FORCED = 1e30
RANK_INVALID = -1.0
MASK_BIG = 1e30
AUX_SEL = 32


def _params(n_grid, vmem_mb):
    return pltpu.CompilerParams(
        dimension_semantics=("arbitrary",) * n_grid, vmem_limit_bytes=vmem_mb * 1024 * 1024
    )


def _dot(a, b):
    return jnp.dot(a, b, preferred_element_type=F32)


def _dot_nt(a, b):
    return lax.dot_general(a, b, (((1,), (1,)), ((), ())), preferred_element_type=F32)


def _layer_norm(z, g, b):
    mu = jnp.mean(z, axis=-1, keepdims=True)
    zc = z - mu
    var = jnp.mean(zc * zc, axis=-1, keepdims=True)
    return zc * lax.rsqrt(var + LN_EPS) * g + b


def _gelu(x):
    return 0.5 * x * (1.0 + jnp.tanh(0.7978845608028654 * (x + 0.044715 * (x * x * x))))


def _silu(x):
    return x / (1.0 + jnp.exp(-x))


def _inproj_kernel(x_ref, wt_ref, wf_ref, wg_ref, tok_ref, cmp_ref, feat_ref, gate_ref):
    x = x_ref[...].astype(BF16)
    tok = _dot(x, wt_ref[...])
    tok_ref[...] = tok[:, :KEY_W].astype(BF16)
    cmp_ref[...] = tok[:, KEY_W:]
    rows = FEAT_W // 4
    for c in range(4):
        feat_ref[0, c * rows:(c + 1) * rows, :] = _dot_nt(wf_ref[c * rows:(c + 1) * rows, :], x).astype(BF16)
    gate_ref[0] = _dot_nt(wg_ref[...], x)


def _inproj(x2d, w_tok, w_feat, w_gate, bsz, seq):
    n = x2d.shape[0]
    per_seq = seq // ROW_TILE
    return pl.pallas_call(
        _inproj_kernel,
        grid=(n // ROW_TILE,),
        in_specs=[
            pl.BlockSpec((ROW_TILE, D_MODEL), lambda i: (i, 0)),
            pl.BlockSpec((D_MODEL, TOK_W), lambda i: (0, 0)),
            pl.BlockSpec((FEAT_W, D_MODEL), lambda i: (0, 0)),
            pl.BlockSpec((GATE_ROWS, D_MODEL), lambda i: (0, 0)),
        ],
        out_specs=[
            pl.BlockSpec((ROW_TILE, KEY_W), lambda i: (i, 0)),
            pl.BlockSpec((ROW_TILE, TOK_W - KEY_W), lambda i: (i, 0)),
            pl.BlockSpec((1, FEAT_W, ROW_TILE), lambda i: (i // per_seq, 0, i % per_seq)),
            pl.BlockSpec((1, GATE_ROWS, ROW_TILE), lambda i: (i // per_seq, 0, i % per_seq)),
        ],
        out_shape=[
            jax.ShapeDtypeStruct((n, KEY_W), BF16),
            jax.ShapeDtypeStruct((n, TOK_W - KEY_W), F32),
            jax.ShapeDtypeStruct((bsz, FEAT_W, seq), BF16),
            jax.ShapeDtypeStruct((bsz, GATE_ROWS, seq), F32),
        ],
        compiler_params=_params(1, 32),
        name="inproj",
    )(x2d, w_tok, w_feat, w_gate)


def _compress_hidden(x_ref, which, pe_ref, w1_ref, n_str):
    xs = jnp.concatenate(
        [x_ref[0, pl.ds(p, n_str, stride=CMP_STRIDE), :] for p in range(CMP_STRIDE)], axis=1).astype(BF16)
    a = _dot(xs, w1_ref[0, which, 0])
    b = _dot(xs, w1_ref[0, which, 1])
    b_next = pltpu.roll(b, shift=n_str - 1, axis=0)
    bias = _dot(pe_ref[which, 0].astype(BF16), w1_ref[0, which, 0]) + _dot(pe_ref[which, 1].astype(BF16),
                                                                         w1_ref[0, which, 1])
    return _gelu(a + b_next + bias[0:1]).astype(BF16)


def _compress_kernel(xk_ref, xv_ref, pe_ref, w1_ref, w2k_ref, w2vt_ref, kc_ref, vct_ref):
    n_str = kc_ref.shape[2]
    kc_ref[0, 0] = _dot(_compress_hidden(xk_ref, 0, pe_ref, w1_ref, n_str), w2k_ref[...]).astype(BF16)
    vct_ref[0, 0] = _dot_nt(w2vt_ref[...], _compress_hidden(xv_ref, 1, pe_ref, w1_ref, n_str)).astype(BF16)


def _compress(x_cmp, pe, w1, w2k, w2vt):
    bsz, seq, width = x_cmp.shape
    n_str = seq // CMP_STRIDE

    def full(shape):
        return pl.BlockSpec(shape, lambda b, h: (0,) * len(shape))

    return pl.pallas_call(
        _compress_kernel,
        grid=(bsz, B_KV_HEADS),
        in_specs=[
            pl.BlockSpec((1, seq, LANES), lambda b, h: (b, 0, 0)),
            pl.BlockSpec((1, seq, LANES), lambda b, h: (b, 0, 1)),
            full(pe.shape),
            pl.BlockSpec((1,) + w1.shape[1:], lambda b, h: (h, 0, 0, 0, 0)),
            full((CMP_HIDDEN, HEAD_DIM)),
            full((HEAD_DIM, CMP_HIDDEN)),
        ],
        out_specs=[
            pl.BlockSpec((1, 1, n_str, HEAD_DIM), lambda b, h: (b, h, 0, 0)),
            pl.BlockSpec((1, 1, HEAD_DIM, n_str), lambda b, h: (b, h, 0, 0)),
        ],
        out_shape=[
            jax.ShapeDtypeStruct((bsz, B_KV_HEADS, n_str, HEAD_DIM), BF16),
            jax.ShapeDtypeStruct((bsz, B_KV_HEADS, HEAD_DIM, n_str), BF16),
        ],
        compiler_params=_params(2, 32),
        name="nsa_compress",
    )(x_cmp, x_cmp, pe, w1, w2k, w2vt)


def _bf16_parts(x, parts=3):
    out, rest = [], np.asarray(x, np.float32)
    for _ in range(parts):
        piece = rest.astype(jnp.bfloat16).astype(np.float32)
        out.append(piece)
        rest = rest - piece
    return out


def _aux_key_lanes(seq):
    t = np.arange(seq)
    aux = np.zeros((seq, LANES), np.float32)
    aux[t, t // SLC_BLOCK] = 1.0
    aux[:, AUX_SEL:AUX_SEL + 3] = (t >> 7)[:, None]
    aux[:, AUX_SEL + 3:AUX_SEL + 6] = (t & 127)[:, None]
    return jnp.asarray(aux, BF16)


def _aux_slope_rows():
    rows = np.zeros((N_ATT_HEADS // GROUP, LANES - AUX_SEL, GROUP * BLOCK_Q), np.float32)
    for grp in range(N_ATT_HEADS // GROUP):
        for g in range(GROUP):
            slope = ALIBI_SLOPES[grp * GROUP + g] * LOG2E
            cols = slice(g * BLOCK_Q, (g + 1) * BLOCK_Q)
            for i, piece in enumerate(_bf16_parts(128.0 * slope) + _bf16_parts(slope)):
                rows[grp, i, cols] = piece
    return jnp.asarray(rows, BF16)


def _attn_kernel(sink_ref, qt_ref, ka_ref, ks_ref, kw_ref, vat_ref, vst_ref, vwt_ref, kc_ref, vct_ref, gt_ref,
                 aux_ref, slope_ref, o_ref, ot_ref, mask_a_ref, mask_w_ref, s_ref, e_ref, *, seq):
    n = pl.program_id(1)
    t0 = n * BLOCK_Q
    n_str = seq // CMP_STRIDE
    n_cmp = n_str - CMP_LEN // CMP_STRIDE + 1
    n_sel = seq // SLC_BLOCK
    top_n = min(SLC_TOPN, n_sel)
    hd = HEAD_DIM
    gq = GROUP * BLOCK_Q
    ch = SLC_CHUNK
    span_a = A_WINDOW + BLOCK_Q
    span_w = B_WINDOW + BLOCK_Q

    lane = lax.broadcasted_iota(jnp.int32, (1, gq), 1)
    q_loc = lane & (BLOCK_Q - 1)
    lane_head = lane >> (BLOCK_Q.bit_length() - 1)
    t_q = t0 + q_loc

    def head_row(vals):
        return jnp.where(lane_head == 0, vals[0], jnp.where(lane_head == 1, vals[1],
                                                           jnp.where(lane_head == 2, vals[2], vals[3])))

    def q_group(first_head):
        return jnp.concatenate(
            [qt_ref[0, (first_head + g) * hd:(first_head + g + 1) * hd, :] for g in range(GROUP)], axis=1)

    def score_rhs(grp, kvh, qg, sel_rows):
        zero = jnp.zeros((hd, gq), BF16)
        q_rows = [qg, zero] if kvh == 0 else [zero, qg]
        return jnp.concatenate(q_rows + [sel_rows, slope_ref[grp]], axis=0)

    def score_lhs(k_ref, start, span):
        return jnp.concatenate([k_ref[0, pl.ds(start, span), :], aux_ref[pl.ds(start, span), :]], axis=1)

    def band_start(span):
        return pl.multiple_of(jnp.maximum(t0 - (span - BLOCK_Q), 0), BLOCK_Q)

    def band_mask(span, window):
        ik = lax.broadcasted_iota(jnp.int32, (span, gq), 0)
        dist = (t0 - band_start(span)) + q_loc - ik
        return jnp.where((dist >= 0) & (dist < window), 0.0, -MASK_BIG)

    @pl.when(n <= span_w // BLOCK_Q - 1)
    def _():
        mask_a_ref[...] = band_mask(span_a, A_WINDOW)
        mask_w_ref[...] = band_mask(span_w, B_WINDOW)

    no_sel = jnp.zeros((AUX_SEL, gq), BF16)
    n_grp = N_ATT_HEADS // GROUP
    slopes = [head_row([ALIBI_SLOPES[grp * GROUP + g] for g in range(GROUP)]) for grp in range(n_grp)]
    qgs = [q_group(grp * GROUP) for grp in range(n_grp)]

    def softmax_cols(s, sinks=None):
        m = jnp.max(s, axis=0, keepdims=True)
        if sinks is not None:
            m = jnp.maximum(m, sinks)
        e = jnp.exp2(s - m)
        den = jnp.sum(e, axis=0, keepdims=True)
        if sinks is not None:
            den = den + jnp.exp2(sinks - m)
        return e.astype(BF16), den

    ks_a = band_start(span_a)
    ks_w = band_start(span_w)
    lhs_a = score_lhs(ka_ref, ks_a, span_a)
    lhs_w = score_lhs(kw_ref, ks_w, span_w)
    s_a = [_dot(lhs_a, score_rhs(kvh, kvh, qgs[kvh], no_sel)) + mask_a_ref[...] for kvh in range(A_KV_HEADS)]
    s_w = [_dot(lhs_w, score_rhs(A_KV_HEADS + kvh, kvh, qgs[A_KV_HEADS + kvh], no_sel)) + mask_w_ref[...]
           for kvh in range(B_KV_HEADS)]

    c_idx = lax.broadcasted_iota(jnp.int32, (n_str, gq), 0)
    dist_c = t_q - (c_idx * CMP_STRIDE + CMP_LEN - 1)
    valid_c = (dist_c >= 0) & (c_idx < n_cmp)
    dist_cf = dist_c.astype(F32)
    s_c = [jnp.where(valid_c, _dot(kc_ref[0, kvh], qgs[A_KV_HEADS + kvh])
                     - (slopes[A_KV_HEADS + kvh] * LOG2E) * dist_cf, NEG_INF) for kvh in range(B_KV_HEADS)]

    oj = lax.broadcasted_iota(jnp.int32, (n_sel, n_str), 0) * SLC_BLOCK
    oc = lax.broadcasted_iota(jnp.int32, (n_sel, n_str), 1) * CMP_STRIDE
    overlap_t = ((oc < oj + SLC_BLOCK) & (oc + CMP_LEN > oj)).astype(BF16)
    j_idx = lax.broadcasted_iota(jnp.int32, (n_sel, BLOCK_Q), 0)
    t_blk = (t0 + lax.broadcasted_iota(jnp.int32, (1, BLOCK_Q), 1)) >> (SLC_BLOCK.bit_length() - 1)
    valid_j = j_idx <= t_blk
    forced_j = (j_idx == 0) | (j_idx == t_blk) | (j_idx == t_blk - 1)
    e_c, den_c, rhs_s = [], [], []
    for kvh in range(B_KV_HEADS):
        m = jnp.max(s_c[kvh], axis=0, keepdims=True)
        m = jnp.where(m == NEG_INF, 0.0, m)
        e = jnp.exp2(s_c[kvh] - m)
        den = jnp.sum(e, axis=0, keepdims=True)
        den = jnp.where(den > 0.0, den, 1.0)
        e_c.append(e.astype(BF16))
        den_c.append(den)
        p = e / den
        psum = p[:, 0:BLOCK_Q]
        for g in range(1, GROUP):
            psum = psum + p[:, g * BLOCK_Q:(g + 1) * BLOCK_Q]
        p_hi = psum.astype(BF16)
        p_lo = (psum - p_hi.astype(F32)).astype(BF16)
        imp = _dot(overlap_t, p_hi) + _dot(overlap_t, p_lo)
        rank = jnp.where(forced_j, RANK_FORCED, jnp.where(valid_j, imp, RANK_INVALID))
        cnt = jnp.zeros((n_sel, BLOCK_Q), jnp.int32)
        for jp in range(n_sel):
            row = rank[jp:jp + 1, :]
            before = (row > rank) | ((row == rank) & (j_idx > jp))
            cnt = cnt + before.astype(jnp.int32)
        sel_bias = jnp.where(cnt < top_n, 0.0, -MASK_BIG)
        if n_sel < AUX_SEL:
            sel_bias = jnp.concatenate([sel_bias, jnp.zeros((AUX_SEL - n_sel, BLOCK_Q), F32)], axis=0)
        sel_rows = jnp.concatenate([sel_bias.astype(BF16)] * GROUP, axis=1)
        rhs_s.append(score_rhs(A_KV_HEADS + kvh, kvh, qgs[A_KV_HEADS + kvh], sel_rows))

    t_qf = t_q.astype(F32)
    sinks = [(head_row([sink_ref[kvh * GROUP + g] for g in range(GROUP)]) + slopes[kvh] * t_qf) * LOG2E
             for kvh in range(A_KV_HEADS)]
    ed_a = [softmax_cols(s_a[kvh], sinks[kvh]) for kvh in range(A_KV_HEADS)]
    ed_w = [softmax_cols(s_w[kvh]) for kvh in range(B_KV_HEADS)]

    def rows(kvh):
        return slice(kvh * hd, (kvh + 1) * hd)

    o_a = [_dot(vat_ref[0, rows(kvh), pl.ds(ks_a, span_a)], ed_a[kvh][0]) / ed_a[kvh][1] for kvh in range(A_KV_HEADS)]
    o_w = [_dot(vwt_ref[0, rows(kvh), pl.ds(ks_w, span_w)], ed_w[kvh][0]) / ed_w[kvh][1] for kvh in range(B_KV_HEADS)]
    o_c = [_dot(vct_ref[0, kvh], e_c[kvh]) / den_c[kvh] for kvh in range(B_KV_HEADS)]
    for kvh in range(A_KV_HEADS):
        for g in range(GROUP):
            h = kvh * GROUP + g
            ot_ref[h * hd:(h + 1) * hd, :] = o_a[kvh][:, g * BLOCK_Q:(g + 1) * BLOCK_Q]

    dloc = q_loc - lax.broadcasted_iota(jnp.int32, (ch, gq), 0)
    n_full = t0 // ch

    def slc_scores(c):
        lhs = score_lhs(ks_ref, pl.multiple_of(c * ch, ch), ch)
        for kvh in range(B_KV_HEADS):
            s_ref[c & 1, kvh] = _dot(lhs, rhs_s[kvh])

    def slc_values(c, state, kvh):
        m, l, acc, alpha = state
        vt = vst_ref[0, rows(kvh), pl.ds(pl.multiple_of(jnp.maximum(c, 0) * ch, ch), ch)]
        return m, l, alpha * acc + _dot(vt, e_ref[kvh]), alpha

    def slc_softmax(c, state, kvh, causal=None):
        m, l, acc, _ = state
        s = s_ref[c & 1, kvh]
        if causal is not None:
            s = jnp.where(causal, s, -MASK_BIG)
        m_new = jnp.maximum(m, jnp.max(s, axis=0, keepdims=True))
        alpha = jnp.exp2(m - m_new)
        e = jnp.exp2(s - m_new)
        e_ref[kvh] = e.astype(BF16)
        return m_new, alpha * l + jnp.sum(e, axis=0, keepdims=True), acc, alpha

    def slc_step(c, states):
        states = tuple(slc_values(c - 1, states[kvh], kvh) for kvh in range(B_KV_HEADS))
        states = tuple(slc_softmax(c, states[kvh], kvh) for kvh in range(B_KV_HEADS))
        slc_scores(c + 1)
        return states

    e_ref[...] = jnp.zeros_like(e_ref)
    init = (jnp.full((1, gq), NEG_INF, F32), jnp.zeros((1, gq), F32), jnp.zeros((hd, gq), F32),
            jnp.ones((1, gq), F32))
    slc_scores(jnp.int32(0))
    states = lax.fori_loop(0, n_full, slc_step, (init,) * B_KV_HEADS)
    causal = dloc >= n_full * ch - t0
    gt = jax.nn.sigmoid(gt_ref[0])
    for kvh in range(B_KV_HEADS):
        state = slc_values(n_full - 1, states[kvh], kvh)
        state = slc_softmax(n_full, state, kvh, causal)
        _, l_fin, acc_fin, _ = slc_values(n_full, state, kvh)
        o_slc = acc_fin / l_fin
        for g in range(GROUP):
            hb = kvh * GROUP + g
            h = A_HEADS + hb
            cols = slice(g * BLOCK_Q, (g + 1) * BLOCK_Q)
            ot_ref[h * hd:(h + 1) * hd, :] = (
                gt[hb:hb + 1] * o_c[kvh][:, cols]
                + gt[B_HEADS + hb:B_HEADS + hb + 1] * o_slc[:, cols]
                + gt[2 * B_HEADS + hb:2 * B_HEADS + hb + 1] * o_w[kvh][:, cols]
            )

    o_ref[0] = ot_ref[...].T.astype(BF16)


def _attention(sinks, tok, feat, kc, vct, gates_t, seq):
    bsz = tok.shape[0]
    n_str = seq // CMP_STRIDE
    kv_w = A_KV_HEADS * HEAD_DIM
    q_rows = N_ATT_HEADS * HEAD_DIM
    gq = GROUP * BLOCK_Q
    assert seq // SLC_BLOCK <= AUX_SEL and seq <= 128 * 256

    def k_spec(col_block):
        return pl.BlockSpec((1, seq, kv_w), lambda b, n: (b, 0, col_block))

    def vt_spec(row_block):
        return pl.BlockSpec((1, kv_w, seq), lambda b, n: (b, q_rows // kv_w + row_block, 0))

    return pl.pallas_call(
        functools.partial(_attn_kernel, seq=seq),
        grid=(bsz, seq // BLOCK_Q),
        in_specs=[
            pl.BlockSpec(memory_space=pltpu.SMEM),
            pl.BlockSpec((1, q_rows, BLOCK_Q), lambda b, n: (b, 0, n)),
            k_spec(0), k_spec(1), k_spec(2),
            vt_spec(0), vt_spec(1), vt_spec(2),
            pl.BlockSpec((1, B_KV_HEADS, n_str, HEAD_DIM), lambda b, n: (b, 0, 0, 0)),
            pl.BlockSpec((1, B_KV_HEADS, HEAD_DIM, n_str), lambda b, n: (b, 0, 0, 0)),
            pl.BlockSpec((1, GATE_ROWS, BLOCK_Q), lambda b, n: (b, 0, n)),
            pl.BlockSpec((seq, LANES), lambda b, n: (0, 0)),
            pl.BlockSpec((N_ATT_HEADS // GROUP, LANES - AUX_SEL, gq), lambda b, n: (0, 0, 0)),
        ],
        out_specs=pl.BlockSpec((1, BLOCK_Q, q_rows), lambda b, n: (b, n, 0)),
        out_shape=jax.ShapeDtypeStruct((bsz, seq, q_rows), BF16),
        scratch_shapes=[
            pltpu.VMEM((q_rows, BLOCK_Q), F32),
            pltpu.VMEM((A_WINDOW + BLOCK_Q, gq), F32),
            pltpu.VMEM((B_WINDOW + BLOCK_Q, gq), F32),
            pltpu.VMEM((2, B_KV_HEADS, SLC_CHUNK, gq), F32),
            pltpu.VMEM((B_KV_HEADS, SLC_CHUNK, gq), BF16),
        ],
        compiler_params=_params(2, 32),
        name="hybrid_attention",
    )(sinks, feat, tok, tok, tok, feat, feat, feat, kc, vct, gates_t, _aux_key_lanes(seq), _aux_slope_rows())


def _proj_ffn_ln_kernel(o_ref, wo_ref, x0_ref, g0_ref, b0_ref, wg_ref, wu_ref, wd_ref, g_ref, b_ref, y_ref, *, alpha):
    x = _layer_norm(alpha * x0_ref[...] + _dot(o_ref[...], wo_ref[...]), g0_ref[...], b0_ref[...])
    xb = x.astype(BF16)
    d_ff = wg_ref.shape[1]
    y = None
    for lo in range(0, d_ff, FFN_CHUNK):
        hi = min(lo + FFN_CHUNK, d_ff)
        h = (_silu(_dot(xb, wg_ref[:, lo:hi])) * _dot(xb, wu_ref[:, lo:hi])).astype(BF16)
        part = _dot(h, wd_ref[lo:hi, :])
        y = part if y is None else y + part
    y_ref[...] = _layer_norm(alpha * x + y, g_ref[...], b_ref[...])


def _proj_ffn_ln(o2d, wo, x2d, g0, b0, wg, wu, wd, g, b, alpha):
    n, k = o2d.shape
    d_ff = wg.shape[1]
    row = pl.BlockSpec((ROW_TILE, D_MODEL), lambda i: (i, 0))
    vec = pl.BlockSpec((1, D_MODEL), lambda i: (0, 0))
    once = pl.Buffered(1)
    return pl.pallas_call(
        functools.partial(_proj_ffn_ln_kernel, alpha=alpha),
        grid=(n // ROW_TILE,),
        in_specs=[
            pl.BlockSpec((ROW_TILE, k), lambda i: (i, 0)),
            pl.BlockSpec((k, D_MODEL), lambda i: (0, 0), pipeline_mode=once),
            row, vec, vec,
            pl.BlockSpec((D_MODEL, d_ff), lambda i: (0, 0), pipeline_mode=once),
            pl.BlockSpec((D_MODEL, d_ff), lambda i: (0, 0), pipeline_mode=once),
            pl.BlockSpec((d_ff, D_MODEL), lambda i: (0, 0), pipeline_mode=once),
            vec, vec,
        ],
        out_specs=row,
        out_shape=jax.ShapeDtypeStruct((n, D_MODEL), F32),
        compiler_params=_params(1, 52),
        name="out_proj_swiglu_ln",
    )(o2d, wo, x2d, g0, b0, wg, wu, wd, g, b)


def _expert_kernel(be_ref, used_ref, x_ref, wg_ref, wu_ref, wd_ref, y_ref, xb_ref):
    del be_ref
    f = pl.program_id(1)
    in_use = pl.program_id(0) < used_ref[0]

    def hidden(x):
        return (_silu(_dot(x, wg_ref[0])) * _dot(x, wu_ref[0])).astype(BF16)

    def down(h, first):
        for lo in range(0, D_MODEL, MXU_WIDTH):
            part = _dot(h, wd_ref[0, :, lo:lo + MXU_WIDTH])
            if first:
                y_ref[:, lo:lo + MXU_WIDTH] = part
            else:
                y_ref[:, lo:lo + MXU_WIDTH] += part

    @pl.when(in_use & (f == 0))
    def _():
        xb = x_ref[...].astype(BF16)
        xb_ref[...] = xb
        down(hidden(xb), True)

    @pl.when(in_use & (f != 0))
    def _():
        down(hidden(xb_ref[...]), False)

    @pl.when(jnp.logical_not(in_use) & (f == 0))
    def _():
        y_ref[...] = jnp.zeros_like(y_ref)


def _experts(blk_e, n_used, xin, wg, wu, wd):
    n_rows = blk_e.shape[0] * MOE_ROWS
    d_ff = wg.shape[2]
    n_f = d_ff // EXPERT_TILE

    def row_blk(i, used):
        return jnp.minimum(i, used[0] - 1)

    def hid_blk(i, f, used):
        return jnp.where(i < used[0], f, n_f - 1)

    return pl.pallas_call(
        _expert_kernel,
        grid_spec=pltpu.PrefetchScalarGridSpec(
            num_scalar_prefetch=2,
            grid=(n_rows // MOE_ROWS, n_f),
            in_specs=[
                pl.BlockSpec((MOE_ROWS, D_MODEL), lambda i, f, be, used: (row_blk(i, used), 0)),
                pl.BlockSpec((1, D_MODEL, EXPERT_TILE), lambda i, f, be, used: (be[i], 0, hid_blk(i, f, used))),
                pl.BlockSpec((1, D_MODEL, EXPERT_TILE), lambda i, f, be, used: (be[i], 0, hid_blk(i, f, used))),
                pl.BlockSpec((1, EXPERT_TILE, D_MODEL), lambda i, f, be, used: (be[i], hid_blk(i, f, used), 0)),
            ],
            out_specs=pl.BlockSpec((MOE_ROWS, D_MODEL), lambda i, f, be, used: (i, 0)),
            scratch_shapes=[pltpu.VMEM((MOE_ROWS, D_MODEL), BF16)],
        ),
        out_shape=jax.ShapeDtypeStruct((n_rows, D_MODEL), F32),
        compiler_params=_params(2, 52),
        name="expert_swiglu",
    )(blk_e, n_used, xin, wg, wu, wd)


def _gmlp_kernel(x_ref, win_ref, lng_ref, lnb_ref, ws_ref, bs_ref, wout_ref, g_ref, b_ref, y_ref,
                 u_ref, vn_ref, gated_ref, *, alpha):
    x = x_ref[...]
    xb = x.astype(BF16)
    u_ref[...] = _gelu(_dot(xb, win_ref[:, :D_MODEL]))
    v = _gelu(_dot(xb, win_ref[:, D_MODEL:]))
    vn_ref[...] = _layer_norm(v, lng_ref[...], lnb_ref[...]).astype(BF16)
    cs = GMLP_CHUNK
    lower = lax.broadcasted_iota(jnp.int32, (cs, cs), 0) >= lax.broadcasted_iota(jnp.int32, (cs, cs), 1)
    for grp in range(GMLP_GROUPS):
        w = jnp.where(lower, ws_ref[grp], 0.0).astype(BF16)
        bias = bs_ref[grp]
        cols = slice(grp * cs, (grp + 1) * cs)
        for c in range(x.shape[0] // cs):
            rows = slice(c * cs, (c + 1) * cs)
            mixed = _dot(w, vn_ref[rows, cols]) + bias
            gated_ref[rows, cols] = (u_ref[rows, cols] * mixed).astype(BF16)
    y = _dot(gated_ref[...], wout_ref[...])
    y_ref[...] = _layer_norm(alpha * x + y, g_ref[...], b_ref[...])


def _gmlp_ln(x2d, w_in, ln_g, ln_b, w_s, b_s, w_out, g, b, alpha):
    n = x2d.shape[0]
    row = pl.BlockSpec((ROW_TILE, D_MODEL), lambda i: (i, 0))
    vec = pl.BlockSpec((1, D_MODEL), lambda i: (0, 0))
    grp = pl.BlockSpec((GMLP_GROUPS, GMLP_CHUNK, GMLP_CHUNK), lambda i: (0, 0, 0))
    return pl.pallas_call(
        functools.partial(_gmlp_kernel, alpha=alpha),
        grid=(n // ROW_TILE,),
        in_specs=[
            row,
            pl.BlockSpec((D_MODEL, 2 * D_MODEL), lambda i: (0, 0)),
            vec, vec, grp, grp,
            pl.BlockSpec((D_MODEL, D_MODEL), lambda i: (0, 0)),
            vec, vec,
        ],
        out_specs=row,
        out_shape=jax.ShapeDtypeStruct((n, D_MODEL), F32),
        scratch_shapes=[
            pltpu.VMEM((ROW_TILE, D_MODEL), F32),
            pltpu.VMEM((ROW_TILE, D_MODEL), BF16),
            pltpu.VMEM((ROW_TILE, D_MODEL), BF16),
        ],
        compiler_params=_params(1, 40),
        name="gmlp_ln",
    )(x2d, w_in, ln_g, ln_b, w_s, b_s, w_out, g, b)


def _route_kernel(x_ref, w_ref, gate_ref, et_ref, post_ref, cnt_ref, run_ref):
    tm = x_ref.shape[0]

    @pl.when(pl.program_id(0) == 0)
    def _():
        run_ref[...] = jnp.zeros_like(run_ref)

    logits = lax.dot_general(w_ref[...], x_ref[...], (((1,), (1,)), ((), ())),
                             preferred_element_type=F32, precision=lax.Precision.HIGHEST)
    row = lax.broadcasted_iota(jnp.int32, (N_EXPERTS, tm), 0)
    v1 = jnp.max(logits, axis=0, keepdims=True)
    i1 = jnp.min(jnp.where(logits == v1, row, N_EXPERTS), axis=0, keepdims=True)
    rest = jnp.where(row == i1, NEG_INF, logits)
    v2 = jnp.max(rest, axis=0, keepdims=True)
    i2 = jnp.min(jnp.where(rest == v2, row, N_EXPERTS), axis=0, keepdims=True)
    d = jnp.exp(v2 - v1)
    g1 = 1.0 / (1.0 + d)
    g2 = d / (1.0 + d)

    hit1 = row == i1
    hit2 = row == i2
    onehot = (hit1 | hit2).astype(BF16)
    earlier = (lax.broadcasted_iota(jnp.int32, (tm, tm), 0) < lax.broadcasted_iota(jnp.int32, (tm, tm), 1)).astype(BF16)
    before = _dot(onehot, earlier) + run_ref[:, 0:1]
    pos1 = jnp.sum(jnp.where(hit1, before, 0.0), axis=0, keepdims=True).astype(jnp.int32)
    pos2 = jnp.sum(jnp.where(hit2, before, 0.0), axis=0, keepdims=True).astype(jnp.int32)
    run_ref[...] += jnp.sum(onehot.astype(F32), axis=1, keepdims=True)

    et_ref[...] = jnp.where(row == 0, i1, jnp.where(row == 1, i2, 0))
    post_ref[...] = jnp.where(row == 0, pos1, jnp.where(row == 1, pos2, 0))
    cnt_ref[...] = run_ref[...]
    wide = lax.broadcasted_iota(jnp.int32, (LANES, tm), 0)
    gate_ref[...] = jnp.where(wide == 0, g1, jnp.where(wide == 1, g2, 0.0)).T


def _route(x2d, w_router_t):
    n = x2d.shape[0]
    tm = ROW_TILE
    per_tok = pl.BlockSpec((N_EXPERTS, tm), lambda i: (0, i))
    return pl.pallas_call(
        _route_kernel,
        grid=(n // tm,),
        in_specs=[
            pl.BlockSpec((tm, D_MODEL), lambda i: (i, 0)),
            pl.BlockSpec((N_EXPERTS, D_MODEL), lambda i: (0, 0)),
        ],
        out_specs=[
            pl.BlockSpec((tm, LANES), lambda i: (i, 0)),
            per_tok, per_tok,
            pl.BlockSpec((N_EXPERTS, LANES), lambda i: (0, 0)),
        ],
        out_shape=[
            jax.ShapeDtypeStruct((n, LANES), F32),
            jax.ShapeDtypeStruct((N_EXPERTS, n), jnp.int32),
            jax.ShapeDtypeStruct((N_EXPERTS, n), jnp.int32),
            jax.ShapeDtypeStruct((N_EXPERTS, LANES), F32),
        ],
        scratch_shapes=[pltpu.VMEM((N_EXPERTS, LANES), F32)],
        compiler_params=_params(1, 24),
        name="moe_router",
    )(x2d, w_router_t)


def _dispatch_kernel(d0_ref, d1_ref, lo_ref, hi_ref, x_ref, xin_hbm, zero_ref, sem_rows, sem_zero):
    step = pl.program_id(0)
    tm = x_ref.shape[0]
    base = step * tm

    def row_copy(r, dest_ref):
        return pltpu.make_async_copy(x_ref.at[pl.ds(r, 1)], xin_hbm.at[pl.ds(dest_ref[base + r], 1)], sem_rows)

    def issue(r8, carry):
        for j in range(DMA_UNROLL):
            r = r8 * DMA_UNROLL + j
            row_copy(r, d0_ref).start(priority=j % 2)
            row_copy(r, d1_ref).start(priority=(j + 1) % 2)
        return carry

    lax.fori_loop(0, tm // DMA_UNROLL, issue, 0)

    @pl.when(step == pl.num_programs(0) - 1)
    def _():
        zero_ref[...] = jnp.zeros_like(zero_ref)
        for k in range(N_EXPERTS + 1):
            def fill(r, carry):
                pltpu.make_async_copy(zero_ref, xin_hbm.at[pl.ds(r, 1)], sem_zero).start()
                return carry

            def drain(r, carry):
                pltpu.make_async_copy(zero_ref, xin_hbm.at[pl.ds(r, 1)], sem_zero).wait()
                return carry

            lax.fori_loop(lo_ref[k], hi_ref[k], fill, 0)
            lax.fori_loop(lo_ref[k], hi_ref[k], drain, 0)

    pltpu.make_async_copy(x_ref, xin_hbm.at[pl.ds(0, tm)], sem_rows).wait()
    pltpu.make_async_copy(x_ref, xin_hbm.at[pl.ds(0, tm)], sem_rows).wait()


def _dispatch(d0, d1, fill_lo, fill_hi, x2d, n_rows):
    n = x2d.shape[0]
    tm = COPY_TOKENS
    return pl.pallas_call(
        _dispatch_kernel,
        grid_spec=pltpu.PrefetchScalarGridSpec(
            num_scalar_prefetch=4,
            grid=(n // tm,),
            in_specs=[pl.BlockSpec((tm, D_MODEL), lambda i, *_: (i, 0))],
            out_specs=pl.BlockSpec(memory_space=pl.ANY),
            scratch_shapes=[pltpu.VMEM((1, D_MODEL), F32), pltpu.SemaphoreType.DMA(()), pltpu.SemaphoreType.DMA(())],
        ),
        out_shape=jax.ShapeDtypeStruct((n_rows, D_MODEL), F32),
        compiler_params=_params(1, 16),
        name="moe_dispatch",
    )(d0, d1, fill_lo, fill_hi, x2d)


def _combine_kernel(d0_ref, d1_ref, y_hbm, gate_ref, x_ref, g_ref, b_ref, o_ref, buf0, buf1, sem, *, alpha):
    tm = x_ref.shape[0]
    base = pl.program_id(0) * tm

    def issue(r8, carry):
        for j in range(DMA_UNROLL):
            r = r8 * DMA_UNROLL + j
            pltpu.make_async_copy(y_hbm.at[pl.ds(d0_ref[base + r], 1)], buf0.at[pl.ds(r, 1)], sem.at[0]).start(priority=0)
            pltpu.make_async_copy(y_hbm.at[pl.ds(d1_ref[base + r], 1)], buf1.at[pl.ds(r, 1)], sem.at[1]).start(priority=1)
        return carry

    lax.fori_loop(0, tm // DMA_UNROLL, issue, 0)
    pltpu.make_async_copy(y_hbm.at[pl.ds(0, tm)], buf0, sem.at[0]).wait()
    pltpu.make_async_copy(y_hbm.at[pl.ds(0, tm)], buf1, sem.at[1]).wait()
    gate = gate_ref[...]
    y = buf0[...] * gate[:, 0:1] + buf1[...] * gate[:, 1:2]
    o_ref[...] = _layer_norm(alpha * x_ref[...] + y, g_ref[...], b_ref[...])


def _combine_ln(d0, d1, yb, gate, x2d, g, b, alpha):
    n = x2d.shape[0]
    tm = COPY_TOKENS
    row = pl.BlockSpec((tm, D_MODEL), lambda i, a, c: (i, 0))
    vec = pl.BlockSpec((1, D_MODEL), lambda i, a, c: (0, 0))
    return pl.pallas_call(
        functools.partial(_combine_kernel, alpha=alpha),
        grid_spec=pltpu.PrefetchScalarGridSpec(
            num_scalar_prefetch=2,
            grid=(n // tm,),
            in_specs=[
                pl.BlockSpec(memory_space=pl.ANY),
                pl.BlockSpec((tm, LANES), lambda i, a, c: (i, 0)),
                row, vec, vec,
            ],
            out_specs=row,
            scratch_shapes=[
                pltpu.VMEM((tm, D_MODEL), F32),
                pltpu.VMEM((tm, D_MODEL), F32),
                pltpu.SemaphoreType.DMA((2,)),
            ],
        ),
        out_shape=jax.ShapeDtypeStruct((n, D_MODEL), F32),
        compiler_params=_params(1, 40),
        name="moe_combine_ln",
    )(d0, d1, yb, gate, x2d, g, b)


def _attention_heads(x2d, bsz, seq, w_in, sinks, pe_k, wk1, wk2, pe_v, wv1, wv2):
    aq, akv = A_HEADS * HEAD_DIM, A_KV_HEADS * HEAD_DIM
    bq, bkv = B_HEADS * HEAD_DIM, B_KV_HEADS * HEAD_DIM
    bounds = [0]
    for width in (aq, akv, akv, bq, bkv, bkv, bkv, bkv, bkv, bkv, 3 * B_HEADS):
        bounds.append(bounds[-1] + width)
    qa, ka, va, qb, kc, vc, ks, vs, kw, vw, wg = [w_in[:, bounds[i]:bounds[i + 1]] for i in range(11)]
    w_tok = jnp.concatenate([ka, ks, kw, kc, vc], axis=1).astype(BF16)
    w_feat = jnp.concatenate([qa * (SCALE * LOG2E), qb * (SCALE * LOG2E), va, vs, vw], axis=1).T.astype(BF16)
    wg = wg.reshape(D_MODEL, B_HEADS, 3).transpose(2, 1, 0).reshape(3 * B_HEADS, D_MODEL)
    wg = jnp.pad(wg, ((0, GATE_ROWS - 3 * B_HEADS), (0, 0))).astype(BF16)
    tok, x_cmp, feat, gates_t = _inproj(x2d, w_tok, w_feat, wg, bsz, seq)

    halves = CMP_LEN // CMP_STRIDE
    w1 = jnp.stack([wk1, wv1]).reshape(2, halves, CMP_STRIDE, 1, HEAD_DIM, CMP_HIDDEN)
    own_head = jnp.eye(B_KV_HEADS, dtype=F32)[:, None, None, None, :, None, None]
    w1 = (w1[None] * own_head).reshape(B_KV_HEADS, 2, halves, CMP_STRIDE * bkv, CMP_HIDDEN).astype(BF16)
    pe = jnp.stack([pe_k, pe_v]).reshape(2, halves, CMP_STRIDE, 1, HEAD_DIM)
    pe = jnp.broadcast_to(pe, (2, halves, CMP_STRIDE, B_KV_HEADS, HEAD_DIM)).reshape(2, halves, 1, CMP_STRIDE * bkv)
    pe = jnp.broadcast_to(pe, (2, halves, 8, CMP_STRIDE * bkv))
    kcmp, vcmp_t = _compress(x_cmp.reshape(bsz, seq, 2 * bkv), pe, w1, wk2.astype(BF16), wv2.T.astype(BF16))

    o = _attention(sinks, tok.reshape(bsz, seq, KEY_W), feat, kcmp, vcmp_t, gates_t, seq)
    return o.reshape(bsz * seq, N_ATT_HEADS * HEAD_DIM)


def _moe_layer(x2d, w_router, w_gate, w_up, w_down, g, b, alpha):
    n_tok = x2d.shape[0]
    gate, e_t, pos_t, cnt = _route(x2d, w_router.T)
    counts = cnt[:, 0].astype(jnp.int32)
    padded = (counts + MOE_ROWS - 1) // MOE_ROWS * MOE_ROWS
    pad_ends = jnp.cumsum(padded)
    pad_starts = pad_ends - padded
    n_blocks = n_tok * TOP_K // MOE_ROWS + N_EXPERTS
    n_rows = n_blocks * MOE_ROWS
    experts = jnp.arange(N_EXPERTS, dtype=jnp.int32)[:, None, None]
    start_of = jnp.sum(jnp.where(e_t[None, :TOP_K] == experts, pad_starts[:, None, None], 0), axis=0)
    dest = start_of + pos_t[:TOP_K]
    fill_lo = jnp.concatenate([pad_starts + counts, pad_ends[-1:]])
    fill_hi = jnp.concatenate([pad_ends, jnp.full((1,), n_rows, jnp.int32)])
    blk_start = jnp.arange(n_blocks, dtype=jnp.int32) * MOE_ROWS
    blk_e = jnp.sum((pad_ends[None, :] <= blk_start[:, None]).astype(jnp.int32), axis=1)
    blk_e = jnp.minimum(blk_e, N_EXPERTS - 1)

    xin = _dispatch(dest[0], dest[1], fill_lo, fill_hi, x2d, n_rows)
    n_used = (pad_ends[-1:] // MOE_ROWS).astype(jnp.int32)
    yb = _experts(blk_e, n_used, xin, w_gate.astype(BF16), w_up.astype(BF16), w_down.astype(BF16))
    return _combine_ln(dest[0], dest[1], yb, gate, x2d, g, b, alpha)


def kernel(x, att_w_in, att_sinks, cmp_pe_k, cmp_wk1, cmp_wk2, cmp_pe_v, cmp_wv1, cmp_wv2, att_w_o,
           ffn_w_gate, ffn_w_up, ffn_w_down, gmlp_w_in, gmlp_ln_g, gmlp_ln_b, gmlp_w_s, gmlp_b_s, gmlp_w_out,
           moe_w_router, moe_w_gate, moe_w_up, moe_w_down, ln_g, ln_b):
    bsz, seq, dm = x.shape
    depth = ln_g.shape[0]
    alpha = (2.0 * depth) ** 0.25
    assert dm == D_MODEL and seq % ROW_TILE == 0 and (bsz * seq) % COPY_TOKENS == 0
    x2d = x.reshape(bsz * seq, dm)

    def vec(p):
        return p.reshape(1, -1)

    for i in range(depth):
        j = i // 2
        g0, b0, g1, b1 = vec(ln_g[i, 0]), vec(ln_b[i, 0]), vec(ln_g[i, 1]), vec(ln_b[i, 1])
        if i % 2 == 0:
            o2d = _attention_heads(x2d, bsz, seq, att_w_in[j], att_sinks[j], cmp_pe_k[j], cmp_wk1[j], cmp_wk2[j],
                                   cmp_pe_v[j], cmp_wv1[j], cmp_wv2[j])
            x2d = _proj_ffn_ln(o2d, att_w_o[j].astype(BF16), x2d, g0, b0, ffn_w_gate[j].astype(BF16),
                               ffn_w_up[j].astype(BF16), ffn_w_down[j].astype(BF16), g1, b1, alpha)
        else:
            b_s = jnp.broadcast_to(gmlp_b_s[j][:, :, None], (GMLP_GROUPS, GMLP_CHUNK, GMLP_CHUNK))
            x2d = _gmlp_ln(x2d, gmlp_w_in[j].astype(BF16), vec(gmlp_ln_g[j]), vec(gmlp_ln_b[j]), gmlp_w_s[j], b_s,
                           gmlp_w_out[j].astype(BF16), g0, b0, alpha)
            x2d = _moe_layer(x2d, moe_w_router[j], moe_w_gate[j], moe_w_up[j], moe_w_down[j], g1, b1, alpha)
    return x2d.reshape(bsz, seq, dm)
```

```python
import functools
import math

import jax
import jax.numpy as jnp
import numpy as np
from jax import lax
from jax.experimental import pallas as pl
from jax.experimental.pallas import tpu as pltpu

F32 = jnp.float32
BF16 = jnp.bfloat16

D_MODEL = 1024
HEAD_DIM = 64
BLOCK_Q = 128
A_HEADS = 8
A_KV_HEADS = 2
A_WINDOW = 128
B_HEADS = 8
B_KV_HEADS = 2
GROUP = 4
CMP_LEN = 32
CMP_STRIDE = 16
CMP_HIDDEN = 256
SLC_BLOCK = 64
SLC_TOPN = 8
B_WINDOW = 256
N_ATT_HEADS = A_HEADS + B_HEADS
GMLP_GROUPS = 8
GMLP_CHUNK = 128
N_EXPERTS = 8
TOP_K = 2
LN_EPS = 1e-5
LANES = 128
MXU_WIDTH = 256
BF16_SUBLANES = 16
DMA_UNROLL = 8

TOK_W = 640
KEY_W = 384
FEAT_W = 1408
GATE_ROWS = 32
SLC_CHUNK = 256

ROW_TILE = 512
MOE_ROWS = 512
COPY_TOKENS = 1024
FFN_CHUNK = 1024
EXPERT_TILE = 1792

ALIBI_SLOPES = tuple(2.0 ** (-8.0 * h / N_ATT_HEADS) for h in range(1, N_ATT_HEADS + 1))
SCALE = HEAD_DIM ** -0.5
LOG2E = 1.4426950408889634
NEG_INF = float("-inf")
RANK_FORCED = 1e30
RANK_INVALID = -1.0
MASK_BIG = 1e30
AUX_SEL = 32


def _params(n_grid, vmem_mb):
    return pltpu.CompilerParams(
        dimension_semantics=("arbitrary",) * n_grid, vmem_limit_bytes=vmem_mb * 1024 * 1024
    )


def _dot(a, b):
    return jnp.dot(a, b, preferred_element_type=F32)


def _dot_nt(a, b):
    return lax.dot_general(a, b, (((1,), (1,)), ((), ())), preferred_element_type=F32)


def _layer_norm(z, g, b):
    mu = jnp.mean(z, axis=-1, keepdims=True)
    zc = z - mu
    var = jnp.mean(zc * zc, axis=-1, keepdims=True)
    return zc * lax.rsqrt(var + LN_EPS) * g + b


def _gelu(x):
    return 0.5 * x * (1.0 + jnp.tanh(0.7978845608028654 * (x + 0.044715 * (x * x * x))))


def _silu(x):
    return x / (1.0 + jnp.exp(-x))


def _cast_plan(w, n_steps):
    cols = w.shape[-1]
    rows = w.size // cols
    n_slices = math.gcd(n_steps, rows // BF16_SUBLANES)
    view = w.reshape(n_slices, rows // n_slices, cols)
    spec = pl.BlockSpec((1,) + view.shape[1:], lambda i: (jnp.minimum(i, n_slices - 1), 0, 0))
    return view, spec


def _cast_side_job(src_refs, dst_refs):
    for src, dst in zip(src_refs, dst_refs):
        dst[...] = src[...].astype(BF16)


def _inproj_kernel(x_ref, wt_ref, wf_ref, wg_ref, c0_ref, c1_ref, c2_ref,
                   tok_ref, cmp_ref, feat_ref, gate_ref, d0_ref, d1_ref, d2_ref):
    _cast_side_job((c0_ref, c1_ref, c2_ref), (d0_ref, d1_ref, d2_ref))
    x = x_ref[...].astype(BF16)
    tok = _dot(x, wt_ref[...])
    tok_ref[...] = tok[:, :KEY_W].astype(BF16)
    cmp_ref[...] = tok[:, KEY_W:]
    rows = FEAT_W // 4
    for c in range(4):
        feat_ref[0, c * rows:(c + 1) * rows, :] = _dot_nt(wf_ref[c * rows:(c + 1) * rows, :], x).astype(BF16)
    gate_ref[0] = _dot_nt(wg_ref[...], x)


def _inproj(x2d, w_tok, w_feat, w_gate, to_cast, bsz, seq):
    n = x2d.shape[0]
    n_steps = n // ROW_TILE
    per_seq = seq // ROW_TILE
    views, cast_specs = zip(*[_cast_plan(w, n_steps) for w in to_cast])
    outs = pl.pallas_call(
        _inproj_kernel,
        grid=(n_steps,),
        in_specs=[
            pl.BlockSpec((ROW_TILE, D_MODEL), lambda i: (i, 0)),
            pl.BlockSpec((D_MODEL, TOK_W), lambda i: (0, 0)),
            pl.BlockSpec((FEAT_W, D_MODEL), lambda i: (0, 0)),
            pl.BlockSpec((GATE_ROWS, D_MODEL), lambda i: (0, 0)),
            *cast_specs,
        ],
        out_specs=[
            pl.BlockSpec((ROW_TILE, KEY_W), lambda i: (i, 0)),
            pl.BlockSpec((ROW_TILE, TOK_W - KEY_W), lambda i: (i, 0)),
            pl.BlockSpec((1, FEAT_W, ROW_TILE), lambda i: (i // per_seq, 0, i % per_seq)),
            pl.BlockSpec((1, GATE_ROWS, ROW_TILE), lambda i: (i // per_seq, 0, i % per_seq)),
            *cast_specs,
        ],
        out_shape=[
            jax.ShapeDtypeStruct((n, KEY_W), BF16),
            jax.ShapeDtypeStruct((n, TOK_W - KEY_W), F32),
            jax.ShapeDtypeStruct((bsz, FEAT_W, seq), BF16),
            jax.ShapeDtypeStruct((bsz, GATE_ROWS, seq), F32),
            *[jax.ShapeDtypeStruct(v.shape, BF16) for v in views],
        ],
        compiler_params=_params(1, 32),
        name="inproj",
    )(x2d, w_tok, w_feat, w_gate, *views)
    return outs[:4], [c.reshape(w.shape) for c, w in zip(outs[4:], to_cast)]


def _compress_hidden(x_ref, which, pe_ref, w1_ref, n_str):
    xs = jnp.concatenate(
        [x_ref[0, pl.ds(p, n_str, stride=CMP_STRIDE), :] for p in range(CMP_STRIDE)], axis=1).astype(BF16)
    a = _dot(xs, w1_ref[0, which, 0])
    b = _dot(xs, w1_ref[0, which, 1])
    b_next = pltpu.roll(b, shift=n_str - 1, axis=0)
    bias = _dot(pe_ref[which, 0].astype(BF16), w1_ref[0, which, 0]) + _dot(pe_ref[which, 1].astype(BF16),
                                                                         w1_ref[0, which, 1])
    return _gelu(a + b_next + bias[0:1]).astype(BF16)


def _compress_kernel(xk_ref, xv_ref, pe_ref, w1_ref, w2k_ref, w2vt_ref, kc_ref, vct_ref):
    n_str = kc_ref.shape[2]
    kc_ref[0, 0] = _dot(_compress_hidden(xk_ref, 0, pe_ref, w1_ref, n_str), w2k_ref[...]).astype(BF16)
    vct_ref[0, 0] = _dot_nt(w2vt_ref[...], _compress_hidden(xv_ref, 1, pe_ref, w1_ref, n_str)).astype(BF16)


def _compress(x_cmp, pe, w1, w2k, w2vt):
    bsz, seq, width = x_cmp.shape
    n_str = seq // CMP_STRIDE

    def full(shape):
        return pl.BlockSpec(shape, lambda b, h: (0,) * len(shape))

    return pl.pallas_call(
        _compress_kernel,
        grid=(bsz, B_KV_HEADS),
        in_specs=[
            pl.BlockSpec((1, seq, LANES), lambda b, h: (b, 0, 0)),
            pl.BlockSpec((1, seq, LANES), lambda b, h: (b, 0, 1)),
            full(pe.shape),
            pl.BlockSpec((1,) + w1.shape[1:], lambda b, h: (h, 0, 0, 0, 0)),
            full((CMP_HIDDEN, HEAD_DIM)),
            full((HEAD_DIM, CMP_HIDDEN)),
        ],
        out_specs=[
            pl.BlockSpec((1, 1, n_str, HEAD_DIM), lambda b, h: (b, h, 0, 0)),
            pl.BlockSpec((1, 1, HEAD_DIM, n_str), lambda b, h: (b, h, 0, 0)),
        ],
        out_shape=[
            jax.ShapeDtypeStruct((bsz, B_KV_HEADS, n_str, HEAD_DIM), BF16),
            jax.ShapeDtypeStruct((bsz, B_KV_HEADS, HEAD_DIM, n_str), BF16),
        ],
        compiler_params=_params(2, 32),
        name="nsa_compress",
    )(x_cmp, x_cmp, pe, w1, w2k, w2vt)


def _bf16_parts(x, parts=3):
    out, rest = [], np.asarray(x, np.float32)
    for _ in range(parts):
        piece = rest.astype(jnp.bfloat16).astype(np.float32)
        out.append(piece)
        rest = rest - piece
    return out


def _aux_key_lanes(seq):
    t = np.arange(seq)
    aux = np.zeros((seq, LANES), np.float32)
    aux[t, t // SLC_BLOCK] = 1.0
    aux[:, AUX_SEL:AUX_SEL + 3] = (t >> 7)[:, None]
    aux[:, AUX_SEL + 3:AUX_SEL + 6] = (t & 127)[:, None]
    return jnp.asarray(aux, BF16)


def _aux_slope_rows():
    rows = np.zeros((N_ATT_HEADS // GROUP, LANES - AUX_SEL, GROUP * BLOCK_Q), np.float32)
    for grp in range(N_ATT_HEADS // GROUP):
        for g in range(GROUP):
            slope = ALIBI_SLOPES[grp * GROUP + g] * LOG2E
            cols = slice(g * BLOCK_Q, (g + 1) * BLOCK_Q)
            for i, piece in enumerate(_bf16_parts(128.0 * slope) + _bf16_parts(slope)):
                rows[grp, i, cols] = piece
    return jnp.asarray(rows, BF16)


def _attn_kernel(sink_ref, qt_ref, ka_ref, ks_ref, kw_ref, vat_ref, vst_ref, vwt_ref, kc_ref, vct_ref, gt_ref,
                 aux_ref, slope_ref, o_ref, ot_ref, mask_a_ref, mask_w_ref, s_ref, e_ref, *, seq):
    n = pl.program_id(1)
    t0 = n * BLOCK_Q
    n_str = seq // CMP_STRIDE
    n_cmp = n_str - CMP_LEN // CMP_STRIDE + 1
    n_sel = seq // SLC_BLOCK
    top_n = min(SLC_TOPN, n_sel)
    hd = HEAD_DIM
    gq = GROUP * BLOCK_Q
    ch = SLC_CHUNK
    span_a = A_WINDOW + BLOCK_Q
    span_w = B_WINDOW + BLOCK_Q

    lane = lax.broadcasted_iota(jnp.int32, (1, gq), 1)
    q_loc = lane & (BLOCK_Q - 1)
    lane_head = lane >> (BLOCK_Q.bit_length() - 1)
    t_q = t0 + q_loc

    def head_row(vals):
        return jnp.where(lane_head == 0, vals[0], jnp.where(lane_head == 1, vals[1],
                                                           jnp.where(lane_head == 2, vals[2], vals[3])))

    def q_group(first_head):
        return jnp.concatenate(
            [qt_ref[0, (first_head + g) * hd:(first_head + g + 1) * hd, :] for g in range(GROUP)], axis=1)

    def score_rhs(grp, kvh, qg, sel_rows):
        zero = jnp.zeros((hd, gq), BF16)
        q_rows = [qg, zero] if kvh == 0 else [zero, qg]
        return jnp.concatenate(q_rows + [sel_rows, slope_ref[grp]], axis=0)

    def score_lhs(k_ref, start, span):
        return jnp.concatenate([k_ref[0, pl.ds(start, span), :], aux_ref[pl.ds(start, span), :]], axis=1)

    def band_start(span):
        return pl.multiple_of(jnp.maximum(t0 - (span - BLOCK_Q), 0), BLOCK_Q)

    def band_mask(span, window):
        ik = lax.broadcasted_iota(jnp.int32, (span, gq), 0)
        dist = (t0 - band_start(span)) + q_loc - ik
        return jnp.where((dist >= 0) & (dist < window), 0.0, -MASK_BIG)

    @pl.when(n <= span_w // BLOCK_Q - 1)
    def _():
        mask_a_ref[...] = band_mask(span_a, A_WINDOW)
        mask_w_ref[...] = band_mask(span_w, B_WINDOW)

    no_sel = jnp.zeros((AUX_SEL, gq), BF16)
    n_grp = N_ATT_HEADS // GROUP
    slopes = [head_row([ALIBI_SLOPES[grp * GROUP + g] for g in range(GROUP)]) for grp in range(n_grp)]
    qgs = [q_group(grp * GROUP) for grp in range(n_grp)]

    def softmax_cols(s, sinks=None):
        m = jnp.max(s, axis=0, keepdims=True)
        if sinks is not None:
            m = jnp.maximum(m, sinks)
        e = jnp.exp2(s - m)
        den = jnp.sum(e, axis=0, keepdims=True)
        if sinks is not None:
            den = den + jnp.exp2(sinks - m)
        return e.astype(BF16), den

    ks_a = band_start(span_a)
    ks_w = band_start(span_w)
    lhs_a = score_lhs(ka_ref, ks_a, span_a)
    lhs_w = score_lhs(kw_ref, ks_w, span_w)
    s_a = [_dot(lhs_a, score_rhs(kvh, kvh, qgs[kvh], no_sel)) + mask_a_ref[...] for kvh in range(A_KV_HEADS)]
    s_w = [_dot(lhs_w, score_rhs(A_KV_HEADS + kvh, kvh, qgs[A_KV_HEADS + kvh], no_sel)) + mask_w_ref[...]
           for kvh in range(B_KV_HEADS)]

    c_idx = lax.broadcasted_iota(jnp.int32, (n_str, gq), 0)
    dist_c = t_q - (c_idx * CMP_STRIDE + CMP_LEN - 1)
    valid_c = (dist_c >= 0) & (c_idx < n_cmp)
    dist_cf = dist_c.astype(F32)
    s_c = [jnp.where(valid_c, _dot(kc_ref[0, kvh], qgs[A_KV_HEADS + kvh])
                     - (slopes[A_KV_HEADS + kvh] * LOG2E) * dist_cf, NEG_INF) for kvh in range(B_KV_HEADS)]

    oj = lax.broadcasted_iota(jnp.int32, (n_sel, n_str), 0) * SLC_BLOCK
    oc = lax.broadcasted_iota(jnp.int32, (n_sel, n_str), 1) * CMP_STRIDE
    overlap_t = ((oc < oj + SLC_BLOCK) & (oc + CMP_LEN > oj)).astype(BF16)
    j_idx = lax.broadcasted_iota(jnp.int32, (n_sel, BLOCK_Q), 0)
    t_blk = (t0 + lax.broadcasted_iota(jnp.int32, (1, BLOCK_Q), 1)) >> (SLC_BLOCK.bit_length() - 1)
    valid_j = j_idx <= t_blk
    forced_j = (j_idx == 0) | (j_idx == t_blk) | (j_idx == t_blk - 1)
    e_c, den_c, rhs_s = [], [], []
    for kvh in range(B_KV_HEADS):
        m = jnp.max(s_c[kvh], axis=0, keepdims=True)
        m = jnp.where(m == NEG_INF, 0.0, m)
        e = jnp.exp2(s_c[kvh] - m)
        den = jnp.sum(e, axis=0, keepdims=True)
        den = jnp.where(den > 0.0, den, 1.0)
        e_c.append(e.astype(BF16))
        den_c.append(den)
        p = e / den
        psum = p[:, 0:BLOCK_Q]
        for g in range(1, GROUP):
            psum = psum + p[:, g * BLOCK_Q:(g + 1) * BLOCK_Q]
        p_hi = psum.astype(BF16)
        p_lo = (psum - p_hi.astype(F32)).astype(BF16)
        imp = _dot(overlap_t, p_hi) + _dot(overlap_t, p_lo)
        rank = jnp.where(forced_j, RANK_FORCED, jnp.where(valid_j, imp, RANK_INVALID))
        cnt = jnp.zeros((n_sel, BLOCK_Q), jnp.int32)
        for jp in range(n_sel):
            row = rank[jp:jp + 1, :]
            before = (row > rank) | ((row == rank) & (j_idx > jp))
            cnt = cnt + before.astype(jnp.int32)
        sel_bias = jnp.where(cnt < top_n, 0.0, -MASK_BIG)
        if n_sel < AUX_SEL:
            sel_bias = jnp.concatenate([sel_bias, jnp.zeros((AUX_SEL - n_sel, BLOCK_Q), F32)], axis=0)
        sel_rows = jnp.concatenate([sel_bias.astype(BF16)] * GROUP, axis=1)
        rhs_s.append(score_rhs(A_KV_HEADS + kvh, kvh, qgs[A_KV_HEADS + kvh], sel_rows))

    t_qf = t_q.astype(F32)
    sinks = [(head_row([sink_ref[kvh * GROUP + g] for g in range(GROUP)]) + slopes[kvh] * t_qf) * LOG2E
             for kvh in range(A_KV_HEADS)]
    ed_a = [softmax_cols(s_a[kvh], sinks[kvh]) for kvh in range(A_KV_HEADS)]
    ed_w = [softmax_cols(s_w[kvh]) for kvh in range(B_KV_HEADS)]

    def rows(kvh):
        return slice(kvh * hd, (kvh + 1) * hd)

    o_a = [_dot(vat_ref[0, rows(kvh), pl.ds(ks_a, span_a)], ed_a[kvh][0]) / ed_a[kvh][1] for kvh in range(A_KV_HEADS)]
    o_w = [_dot(vwt_ref[0, rows(kvh), pl.ds(ks_w, span_w)], ed_w[kvh][0]) / ed_w[kvh][1] for kvh in range(B_KV_HEADS)]
    o_c = [_dot(vct_ref[0, kvh], e_c[kvh]) / den_c[kvh] for kvh in range(B_KV_HEADS)]
    for kvh in range(A_KV_HEADS):
        for g in range(GROUP):
            h = kvh * GROUP + g
            ot_ref[h * hd:(h + 1) * hd, :] = o_a[kvh][:, g * BLOCK_Q:(g + 1) * BLOCK_Q]

    dloc = q_loc - lax.broadcasted_iota(jnp.int32, (ch, gq), 0)
    n_full = t0 // ch

    def slc_scores(c):
        lhs = score_lhs(ks_ref, pl.multiple_of(c * ch, ch), ch)
        for kvh in range(B_KV_HEADS):
            s_ref[c & 1, kvh] = _dot(lhs, rhs_s[kvh])

    def slc_values(c, state, kvh):
        m, l, acc, alpha = state
        vt = vst_ref[0, rows(kvh), pl.ds(pl.multiple_of(jnp.maximum(c, 0) * ch, ch), ch)]
        return m, l, alpha * acc + _dot(vt, e_ref[kvh]), alpha

    def slc_softmax(c, state, kvh, causal=None):
        m, l, acc, _ = state
        s = s_ref[c & 1, kvh]
        if causal is not None:
            s = jnp.where(causal, s, -MASK_BIG)
        m_new = jnp.maximum(m, jnp.max(s, axis=0, keepdims=True))
        alpha = jnp.exp2(m - m_new)
        e = jnp.exp2(s - m_new)
        e_ref[kvh] = e.astype(BF16)
        return m_new, alpha * l + jnp.sum(e, axis=0, keepdims=True), acc, alpha

    def slc_step(c, states):
        states = tuple(slc_values(c - 1, states[kvh], kvh) for kvh in range(B_KV_HEADS))
        states = tuple(slc_softmax(c, states[kvh], kvh) for kvh in range(B_KV_HEADS))
        slc_scores(c + 1)
        return states

    e_ref[...] = jnp.zeros_like(e_ref)
    init = (jnp.full((1, gq), NEG_INF, F32), jnp.zeros((1, gq), F32), jnp.zeros((hd, gq), F32),
            jnp.ones((1, gq), F32))
    slc_scores(jnp.int32(0))
    states = lax.fori_loop(0, n_full, slc_step, (init,) * B_KV_HEADS)
    causal = dloc >= n_full * ch - t0
    gt = jax.nn.sigmoid(gt_ref[0])
    for kvh in range(B_KV_HEADS):
        state = slc_values(n_full - 1, states[kvh], kvh)
        state = slc_softmax(n_full, state, kvh, causal)
        _, l_fin, acc_fin, _ = slc_values(n_full, state, kvh)
        o_slc = acc_fin / l_fin
        for g in range(GROUP):
            hb = kvh * GROUP + g
            h = A_HEADS + hb
            cols = slice(g * BLOCK_Q, (g + 1) * BLOCK_Q)
            ot_ref[h * hd:(h + 1) * hd, :] = (
                gt[hb:hb + 1] * o_c[kvh][:, cols]
                + gt[B_HEADS + hb:B_HEADS + hb + 1] * o_slc[:, cols]
                + gt[2 * B_HEADS + hb:2 * B_HEADS + hb + 1] * o_w[kvh][:, cols]
            )

    o_ref[0] = ot_ref[...].T.astype(BF16)


def _attention(sinks, tok, feat, kc, vct, gates_t, seq):
    bsz = tok.shape[0]
    n_str = seq // CMP_STRIDE
    kv_w = A_KV_HEADS * HEAD_DIM
    q_rows = N_ATT_HEADS * HEAD_DIM
    gq = GROUP * BLOCK_Q
    assert seq // SLC_BLOCK <= AUX_SEL and seq <= 128 * 256

    def k_spec(col_block):
        return pl.BlockSpec((1, seq, kv_w), lambda b, n: (b, 0, col_block))

    def vt_spec(row_block):
        return pl.BlockSpec((1, kv_w, seq), lambda b, n: (b, q_rows // kv_w + row_block, 0))

    return pl.pallas_call(
        functools.partial(_attn_kernel, seq=seq),
        grid=(bsz, seq // BLOCK_Q),
        in_specs=[
            pl.BlockSpec(memory_space=pltpu.SMEM),
            pl.BlockSpec((1, q_rows, BLOCK_Q), lambda b, n: (b, 0, n)),
            k_spec(0), k_spec(1), k_spec(2),
            vt_spec(0), vt_spec(1), vt_spec(2),
            pl.BlockSpec((1, B_KV_HEADS, n_str, HEAD_DIM), lambda b, n: (b, 0, 0, 0)),
            pl.BlockSpec((1, B_KV_HEADS, HEAD_DIM, n_str), lambda b, n: (b, 0, 0, 0)),
            pl.BlockSpec((1, GATE_ROWS, BLOCK_Q), lambda b, n: (b, 0, n)),
            pl.BlockSpec((seq, LANES), lambda b, n: (0, 0)),
            pl.BlockSpec((N_ATT_HEADS // GROUP, LANES - AUX_SEL, gq), lambda b, n: (0, 0, 0)),
        ],
        out_specs=pl.BlockSpec((1, BLOCK_Q, q_rows), lambda b, n: (b, n, 0)),
        out_shape=jax.ShapeDtypeStruct((bsz, seq, q_rows), BF16),
        scratch_shapes=[
            pltpu.VMEM((q_rows, BLOCK_Q), F32),
            pltpu.VMEM((A_WINDOW + BLOCK_Q, gq), F32),
            pltpu.VMEM((B_WINDOW + BLOCK_Q, gq), F32),
            pltpu.VMEM((2, B_KV_HEADS, SLC_CHUNK, gq), F32),
            pltpu.VMEM((B_KV_HEADS, SLC_CHUNK, gq), BF16),
        ],
        compiler_params=_params(2, 32),
        name="hybrid_attention",
    )(sinks, feat, tok, tok, tok, feat, feat, feat, kc, vct, gates_t, _aux_key_lanes(seq), _aux_slope_rows())


def _proj_ffn_ln_kernel(o_ref, wo_ref, x0_ref, g0_ref, b0_ref, wg_ref, wu_ref, wd_ref, g_ref, b_ref,
                        c0_ref, c1_ref, c2_ref, y_ref, d0_ref, d1_ref, d2_ref, *, alpha):
    _cast_side_job((c0_ref, c1_ref, c2_ref), (d0_ref, d1_ref, d2_ref))
    x = _layer_norm(alpha * x0_ref[...] + _dot(o_ref[...], wo_ref[...]), g0_ref[...], b0_ref[...])
    xb = x.astype(BF16)
    d_ff = wg_ref.shape[1]
    y = None
    for lo in range(0, d_ff, FFN_CHUNK):
        hi = min(lo + FFN_CHUNK, d_ff)
        h = (_silu(_dot(xb, wg_ref[:, lo:hi])) * _dot(xb, wu_ref[:, lo:hi])).astype(BF16)
        part = _dot(h, wd_ref[lo:hi, :])
        y = part if y is None else y + part
    y_ref[...] = _layer_norm(alpha * x + y, g_ref[...], b_ref[...])


def _proj_ffn_ln(o2d, wo, x2d, g0, b0, wg, wu, wd, g, b, to_cast, alpha):
    n, k = o2d.shape
    d_ff = wg.shape[1]
    n_steps = n // ROW_TILE
    row = pl.BlockSpec((ROW_TILE, D_MODEL), lambda i: (i, 0))
    vec = pl.BlockSpec((1, D_MODEL), lambda i: (0, 0))
    once = pl.Buffered(1)
    views, cast_specs = zip(*[_cast_plan(w, n_steps) for w in to_cast])
    outs = pl.pallas_call(
        functools.partial(_proj_ffn_ln_kernel, alpha=alpha),
        grid=(n_steps,),
        in_specs=[
            pl.BlockSpec((ROW_TILE, k), lambda i: (i, 0)),
            pl.BlockSpec((k, D_MODEL), lambda i: (0, 0), pipeline_mode=once),
            row, vec, vec,
            pl.BlockSpec((D_MODEL, d_ff), lambda i: (0, 0), pipeline_mode=once),
            pl.BlockSpec((D_MODEL, d_ff), lambda i: (0, 0), pipeline_mode=once),
            pl.BlockSpec((d_ff, D_MODEL), lambda i: (0, 0), pipeline_mode=once),
            vec, vec,
            *cast_specs,
        ],
        out_specs=[row, *cast_specs],
        out_shape=[jax.ShapeDtypeStruct((n, D_MODEL), F32), *[jax.ShapeDtypeStruct(v.shape, BF16) for v in views]],
        compiler_params=_params(1, 60),
        name="out_proj_swiglu_ln",
    )(o2d, wo, x2d, g0, b0, wg, wu, wd, g, b, *views)
    return outs[0], [c.reshape(w.shape) for c, w in zip(outs[1:], to_cast)]


def _expert_kernel(be_ref, used_ref, x_ref, wg_ref, wu_ref, wd_ref, y_ref, xb_ref):
    del be_ref
    f = pl.program_id(1)
    in_use = pl.program_id(0) < used_ref[0]

    def hidden(x):
        return (_silu(_dot(x, wg_ref[0])) * _dot(x, wu_ref[0])).astype(BF16)

    def down(h, first):
        for lo in range(0, D_MODEL, MXU_WIDTH):
            part = _dot(h, wd_ref[0, :, lo:lo + MXU_WIDTH])
            if first:
                y_ref[:, lo:lo + MXU_WIDTH] = part
            else:
                y_ref[:, lo:lo + MXU_WIDTH] += part

    @pl.when(in_use & (f == 0))
    def _():
        xb = x_ref[...].astype(BF16)
        xb_ref[...] = xb
        down(hidden(xb), True)

    @pl.when(in_use & (f != 0))
    def _():
        down(hidden(xb_ref[...]), False)

    @pl.when(jnp.logical_not(in_use) & (f == 0))
    def _():
        y_ref[...] = jnp.zeros_like(y_ref)


def _experts(blk_e, n_used, xin, wg, wu, wd):
    n_rows = blk_e.shape[0] * MOE_ROWS
    d_ff = wg.shape[2]
    n_f = d_ff // EXPERT_TILE

    def row_blk(i, used):
        return jnp.minimum(i, used[0] - 1)

    def hid_blk(i, f, used):
        return jnp.where(i < used[0], f, n_f - 1)

    return pl.pallas_call(
        _expert_kernel,
        grid_spec=pltpu.PrefetchScalarGridSpec(
            num_scalar_prefetch=2,
            grid=(n_rows // MOE_ROWS, n_f),
            in_specs=[
                pl.BlockSpec((MOE_ROWS, D_MODEL), lambda i, f, be, used: (row_blk(i, used), 0)),
                pl.BlockSpec((1, D_MODEL, EXPERT_TILE), lambda i, f, be, used: (be[i], 0, hid_blk(i, f, used))),
                pl.BlockSpec((1, D_MODEL, EXPERT_TILE), lambda i, f, be, used: (be[i], 0, hid_blk(i, f, used))),
                pl.BlockSpec((1, EXPERT_TILE, D_MODEL), lambda i, f, be, used: (be[i], hid_blk(i, f, used), 0)),
            ],
            out_specs=pl.BlockSpec((MOE_ROWS, D_MODEL), lambda i, f, be, used: (i, 0)),
            scratch_shapes=[pltpu.VMEM((MOE_ROWS, D_MODEL), BF16)],
        ),
        out_shape=jax.ShapeDtypeStruct((n_rows, D_MODEL), F32),
        compiler_params=_params(2, 52),
        name="expert_swiglu",
    )(blk_e, n_used, xin, wg, wu, wd)


def _gmlp_kernel(x_ref, win_ref, lng_ref, lnb_ref, ws_ref, bs_ref, wout_ref, g_ref, b_ref, y_ref,
                 u_ref, vn_ref, gated_ref, *, alpha):
    x = x_ref[...]
    xb = x.astype(BF16)
    u_ref[...] = _gelu(_dot(xb, win_ref[:, :D_MODEL]))
    v = _gelu(_dot(xb, win_ref[:, D_MODEL:]))
    vn_ref[...] = _layer_norm(v, lng_ref[...], lnb_ref[...]).astype(BF16)
    cs = GMLP_CHUNK
    lower = lax.broadcasted_iota(jnp.int32, (cs, cs), 0) >= lax.broadcasted_iota(jnp.int32, (cs, cs), 1)
    for grp in range(GMLP_GROUPS):
        w = jnp.where(lower, ws_ref[grp], 0.0).astype(BF16)
        bias = bs_ref[grp]
        cols = slice(grp * cs, (grp + 1) * cs)
        for c in range(x.shape[0] // cs):
            rows = slice(c * cs, (c + 1) * cs)
            mixed = _dot(w, vn_ref[rows, cols]) + bias
            gated_ref[rows, cols] = (u_ref[rows, cols] * mixed).astype(BF16)
    y = _dot(gated_ref[...], wout_ref[...])
    y_ref[...] = _layer_norm(alpha * x + y, g_ref[...], b_ref[...])


def _gmlp_ln(x2d, w_in, ln_g, ln_b, w_s, b_s, w_out, g, b, alpha):
    n = x2d.shape[0]
    row = pl.BlockSpec((ROW_TILE, D_MODEL), lambda i: (i, 0))
    vec = pl.BlockSpec((1, D_MODEL), lambda i: (0, 0))
    grp = pl.BlockSpec((GMLP_GROUPS, GMLP_CHUNK, GMLP_CHUNK), lambda i: (0, 0, 0))
    return pl.pallas_call(
        functools.partial(_gmlp_kernel, alpha=alpha),
        grid=(n // ROW_TILE,),
        in_specs=[
            row,
            pl.BlockSpec((D_MODEL, 2 * D_MODEL), lambda i: (0, 0)),
            vec, vec, grp, grp,
            pl.BlockSpec((D_MODEL, D_MODEL), lambda i: (0, 0)),
            vec, vec,
        ],
        out_specs=row,
        out_shape=jax.ShapeDtypeStruct((n, D_MODEL), F32),
        scratch_shapes=[
            pltpu.VMEM((ROW_TILE, D_MODEL), F32),
            pltpu.VMEM((ROW_TILE, D_MODEL), BF16),
            pltpu.VMEM((ROW_TILE, D_MODEL), BF16),
        ],
        compiler_params=_params(1, 40),
        name="gmlp_ln",
    )(x2d, w_in, ln_g, ln_b, w_s, b_s, w_out, g, b)


def _route_kernel(x_ref, w_ref, gate_ref, et_ref, post_ref, cnt_ref, run_ref):
    tm = x_ref.shape[0]

    @pl.when(pl.program_id(0) == 0)
    def _():
        run_ref[...] = jnp.zeros_like(run_ref)

    logits = lax.dot_general(w_ref[...], x_ref[...], (((1,), (1,)), ((), ())),
                             preferred_element_type=F32, precision=lax.Precision.HIGHEST)
    row = lax.broadcasted_iota(jnp.int32, (N_EXPERTS, tm), 0)
    v1 = jnp.max(logits, axis=0, keepdims=True)
    i1 = jnp.min(jnp.where(logits == v1, row, N_EXPERTS), axis=0, keepdims=True)
    rest = jnp.where(row == i1, NEG_INF, logits)
    v2 = jnp.max(rest, axis=0, keepdims=True)
    i2 = jnp.min(jnp.where(rest == v2, row, N_EXPERTS), axis=0, keepdims=True)
    d = jnp.exp(v2 - v1)
    g1 = 1.0 / (1.0 + d)
    g2 = d / (1.0 + d)

    hit1 = row == i1
    hit2 = row == i2
    onehot = (hit1 | hit2).astype(BF16)
    earlier = (lax.broadcasted_iota(jnp.int32, (tm, tm), 0) < lax.broadcasted_iota(jnp.int32, (tm, tm), 1)).astype(BF16)
    before = _dot(onehot, earlier) + run_ref[:, 0:1]
    pos1 = jnp.sum(jnp.where(hit1, before, 0.0), axis=0, keepdims=True).astype(jnp.int32)
    pos2 = jnp.sum(jnp.where(hit2, before, 0.0), axis=0, keepdims=True).astype(jnp.int32)
    run_ref[...] += jnp.sum(onehot.astype(F32), axis=1, keepdims=True)

    et_ref[...] = jnp.where(row == 0, i1, jnp.where(row == 1, i2, 0))
    post_ref[...] = jnp.where(row == 0, pos1, jnp.where(row == 1, pos2, 0))
    cnt_ref[...] = run_ref[...]
    wide = lax.broadcasted_iota(jnp.int32, (LANES, tm), 0)
    gate_ref[...] = jnp.where(wide == 0, g1, jnp.where(wide == 1, g2, 0.0)).T


def _route(x2d, w_router_t):
    n = x2d.shape[0]
    tm = ROW_TILE
    per_tok = pl.BlockSpec((N_EXPERTS, tm), lambda i: (0, i))
    return pl.pallas_call(
        _route_kernel,
        grid=(n // tm,),
        in_specs=[
            pl.BlockSpec((tm, D_MODEL), lambda i: (i, 0)),
            pl.BlockSpec((N_EXPERTS, D_MODEL), lambda i: (0, 0)),
        ],
        out_specs=[
            pl.BlockSpec((tm, LANES), lambda i: (i, 0)),
            per_tok, per_tok,
            pl.BlockSpec((N_EXPERTS, LANES), lambda i: (0, 0)),
        ],
        out_shape=[
            jax.ShapeDtypeStruct((n, LANES), F32),
            jax.ShapeDtypeStruct((N_EXPERTS, n), jnp.int32),
            jax.ShapeDtypeStruct((N_EXPERTS, n), jnp.int32),
            jax.ShapeDtypeStruct((N_EXPERTS, LANES), F32),
        ],
        scratch_shapes=[pltpu.VMEM((N_EXPERTS, LANES), F32)],
        compiler_params=_params(1, 24),
        name="moe_router",
    )(x2d, w_router_t)


def _dispatch_kernel(d0_ref, d1_ref, lo_ref, hi_ref, x_ref, xin_hbm, zero_ref, sem_rows, sem_zero):
    step = pl.program_id(0)
    tm = x_ref.shape[0]
    base = step * tm

    def row_copy(r, dest_ref):
        return pltpu.make_async_copy(x_ref.at[pl.ds(r, 1)], xin_hbm.at[pl.ds(dest_ref[base + r], 1)], sem_rows)

    def issue(r8, carry):
        for j in range(DMA_UNROLL):
            r = r8 * DMA_UNROLL + j
            row_copy(r, d0_ref).start(priority=j % 2)
            row_copy(r, d1_ref).start(priority=(j + 1) % 2)
        return carry

    lax.fori_loop(0, tm // DMA_UNROLL, issue, 0)

    @pl.when(step == pl.num_programs(0) - 1)
    def _():
        zero_ref[...] = jnp.zeros_like(zero_ref)
        for k in range(N_EXPERTS + 1):
            def fill(r, carry):
                pltpu.make_async_copy(zero_ref, xin_hbm.at[pl.ds(r, 1)], sem_zero).start()
                return carry

            def drain(r, carry):
                pltpu.make_async_copy(zero_ref, xin_hbm.at[pl.ds(r, 1)], sem_zero).wait()
                return carry

            lax.fori_loop(lo_ref[k], hi_ref[k], fill, 0)
            lax.fori_loop(lo_ref[k], hi_ref[k], drain, 0)

    pltpu.make_async_copy(x_ref, xin_hbm.at[pl.ds(0, tm)], sem_rows).wait()
    pltpu.make_async_copy(x_ref, xin_hbm.at[pl.ds(0, tm)], sem_rows).wait()


def _dispatch(d0, d1, fill_lo, fill_hi, x2d, n_rows):
    n = x2d.shape[0]
    tm = COPY_TOKENS
    return pl.pallas_call(
        _dispatch_kernel,
        grid_spec=pltpu.PrefetchScalarGridSpec(
            num_scalar_prefetch=4,
            grid=(n // tm,),
            in_specs=[pl.BlockSpec((tm, D_MODEL), lambda i, *_: (i, 0))],
            out_specs=pl.BlockSpec(memory_space=pl.ANY),
            scratch_shapes=[pltpu.VMEM((1, D_MODEL), F32), pltpu.SemaphoreType.DMA(()), pltpu.SemaphoreType.DMA(())],
        ),
        out_shape=jax.ShapeDtypeStruct((n_rows, D_MODEL), F32),
        compiler_params=_params(1, 16),
        name="moe_dispatch",
    )(d0, d1, fill_lo, fill_hi, x2d)


def _combine_kernel(d0_ref, d1_ref, y_hbm, gate_ref, x_ref, g_ref, b_ref, o_ref, buf0, buf1, sem, *, alpha):
    tm = x_ref.shape[0]
    base = pl.program_id(0) * tm

    def issue(r8, carry):
        for j in range(DMA_UNROLL):
            r = r8 * DMA_UNROLL + j
            pltpu.make_async_copy(y_hbm.at[pl.ds(d0_ref[base + r], 1)], buf0.at[pl.ds(r, 1)], sem.at[0]).start(priority=0)
            pltpu.make_async_copy(y_hbm.at[pl.ds(d1_ref[base + r], 1)], buf1.at[pl.ds(r, 1)], sem.at[1]).start(priority=1)
        return carry

    lax.fori_loop(0, tm // DMA_UNROLL, issue, 0)
    pltpu.make_async_copy(y_hbm.at[pl.ds(0, tm)], buf0, sem.at[0]).wait()
    pltpu.make_async_copy(y_hbm.at[pl.ds(0, tm)], buf1, sem.at[1]).wait()
    gate = gate_ref[...]
    y = buf0[...] * gate[:, 0:1] + buf1[...] * gate[:, 1:2]
    o_ref[...] = _layer_norm(alpha * x_ref[...] + y, g_ref[...], b_ref[...])


def _combine_ln(d0, d1, yb, gate, x2d, g, b, alpha):
    n = x2d.shape[0]
    tm = COPY_TOKENS
    row = pl.BlockSpec((tm, D_MODEL), lambda i, a, c: (i, 0))
    vec = pl.BlockSpec((1, D_MODEL), lambda i, a, c: (0, 0))
    return pl.pallas_call(
        functools.partial(_combine_kernel, alpha=alpha),
        grid_spec=pltpu.PrefetchScalarGridSpec(
            num_scalar_prefetch=2,
            grid=(n // tm,),
            in_specs=[
                pl.BlockSpec(memory_space=pl.ANY),
                pl.BlockSpec((tm, LANES), lambda i, a, c: (i, 0)),
                row, vec, vec,
            ],
            out_specs=row,
            scratch_shapes=[
                pltpu.VMEM((tm, D_MODEL), F32),
                pltpu.VMEM((tm, D_MODEL), F32),
                pltpu.SemaphoreType.DMA((2,)),
            ],
        ),
        out_shape=jax.ShapeDtypeStruct((n, D_MODEL), F32),
        compiler_params=_params(1, 40),
        name="moe_combine_ln",
    )(d0, d1, yb, gate, x2d, g, b)


def _attention_heads(x2d, bsz, seq, w_in, sinks, pe_k, wk1, wk2, pe_v, wv1, wv2, to_cast):
    aq, akv = A_HEADS * HEAD_DIM, A_KV_HEADS * HEAD_DIM
    bq, bkv = B_HEADS * HEAD_DIM, B_KV_HEADS * HEAD_DIM
    bounds = [0]
    for width in (aq, akv, akv, bq, bkv, bkv, bkv, bkv, bkv, bkv, 3 * B_HEADS):
        bounds.append(bounds[-1] + width)
    qa, ka, va, qb, kc, vc, ks, vs, kw, vw, wg = [w_in[:, bounds[i]:bounds[i + 1]] for i in range(11)]
    w_tok = jnp.concatenate([ka, ks, kw, kc, vc], axis=1).astype(BF16)
    w_feat = jnp.concatenate([qa * (SCALE * LOG2E), qb * (SCALE * LOG2E), va, vs, vw], axis=1).T.astype(BF16)
    wg = wg.reshape(D_MODEL, B_HEADS, 3).transpose(2, 1, 0).reshape(3 * B_HEADS, D_MODEL)
    wg = jnp.pad(wg, ((0, GATE_ROWS - 3 * B_HEADS), (0, 0))).astype(BF16)
    (tok, x_cmp, feat, gates_t), cast = _inproj(x2d, w_tok, w_feat, wg, to_cast, bsz, seq)

    halves = CMP_LEN // CMP_STRIDE
    w1 = jnp.stack([wk1, wv1]).reshape(2, halves, CMP_STRIDE, 1, HEAD_DIM, CMP_HIDDEN)
    own_head = jnp.eye(B_KV_HEADS, dtype=F32)[:, None, None, None, :, None, None]
    w1 = (w1[None] * own_head).reshape(B_KV_HEADS, 2, halves, CMP_STRIDE * bkv, CMP_HIDDEN).astype(BF16)
    pe = jnp.stack([pe_k, pe_v]).reshape(2, halves, CMP_STRIDE, 1, HEAD_DIM)
    pe = jnp.broadcast_to(pe, (2, halves, CMP_STRIDE, B_KV_HEADS, HEAD_DIM)).reshape(2, halves, 1, CMP_STRIDE * bkv)
    pe = jnp.broadcast_to(pe, (2, halves, 8, CMP_STRIDE * bkv))
    kcmp, vcmp_t = _compress(x_cmp.reshape(bsz, seq, 2 * bkv), pe, w1, wk2.astype(BF16), wv2.T.astype(BF16))

    o = _attention(sinks, tok.reshape(bsz, seq, KEY_W), feat, kcmp, vcmp_t, gates_t, seq)
    return o.reshape(bsz * seq, N_ATT_HEADS * HEAD_DIM), cast


def _moe_layer(x2d, w_router, w_gate, w_up, w_down, g, b, alpha):
    n_tok = x2d.shape[0]
    gate, e_t, pos_t, cnt = _route(x2d, w_router.T)
    counts = cnt[:, 0].astype(jnp.int32)
    padded = (counts + MOE_ROWS - 1) // MOE_ROWS * MOE_ROWS
    pad_ends = jnp.cumsum(padded)
    pad_starts = pad_ends - padded
    n_blocks = n_tok * TOP_K // MOE_ROWS + N_EXPERTS
    n_rows = n_blocks * MOE_ROWS
    experts = jnp.arange(N_EXPERTS, dtype=jnp.int32)[:, None, None]
    start_of = jnp.sum(jnp.where(e_t[None, :TOP_K] == experts, pad_starts[:, None, None], 0), axis=0)
    dest = start_of + pos_t[:TOP_K]
    fill_lo = jnp.concatenate([pad_starts + counts, pad_ends[-1:]])
    fill_hi = jnp.concatenate([pad_ends, jnp.full((1,), n_rows, jnp.int32)])
    blk_start = jnp.arange(n_blocks, dtype=jnp.int32) * MOE_ROWS
    blk_e = jnp.sum((pad_ends[None, :] <= blk_start[:, None]).astype(jnp.int32), axis=1)
    blk_e = jnp.minimum(blk_e, N_EXPERTS - 1)

    xin = _dispatch(dest[0], dest[1], fill_lo, fill_hi, x2d, n_rows)
    n_used = (pad_ends[-1:] // MOE_ROWS).astype(jnp.int32)
    yb = _experts(blk_e, n_used, xin, w_gate, w_up, w_down)
    return _combine_ln(dest[0], dest[1], yb, gate, x2d, g, b, alpha)


def kernel(x, att_w_in, att_sinks, cmp_pe_k, cmp_wk1, cmp_wk2, cmp_pe_v, cmp_wv1, cmp_wv2, att_w_o,
           ffn_w_gate, ffn_w_up, ffn_w_down, gmlp_w_in, gmlp_ln_g, gmlp_ln_b, gmlp_w_s, gmlp_b_s, gmlp_w_out,
           moe_w_router, moe_w_gate, moe_w_up, moe_w_down, ln_g, ln_b):
    bsz, seq, dm = x.shape
    depth = ln_g.shape[0]
    alpha = (2.0 * depth) ** 0.25
    assert dm == D_MODEL and seq % ROW_TILE == 0 and (bsz * seq) % COPY_TOKENS == 0
    x2d = x.reshape(bsz * seq, dm)

    def vec(p):
        return p.reshape(1, -1)

    for i in range(depth):
        j = i // 2
        g0, b0, g1, b1 = vec(ln_g[i, 0]), vec(ln_b[i, 0]), vec(ln_g[i, 1]), vec(ln_b[i, 1])
        if i % 2 == 0:
            o2d, ffn_w = _attention_heads(x2d, bsz, seq, att_w_in[j], att_sinks[j], cmp_pe_k[j], cmp_wk1[j],
                                          cmp_wk2[j], cmp_pe_v[j], cmp_wv1[j], cmp_wv2[j],
                                          (ffn_w_gate[j], ffn_w_up[j], ffn_w_down[j]))
            jn = min(j, moe_w_gate.shape[0] - 1)
            x2d, moe_w = _proj_ffn_ln(o2d, att_w_o[j].astype(BF16), x2d, g0, b0, *ffn_w, g1, b1,
                                      (moe_w_gate[jn], moe_w_up[jn], moe_w_down[jn]), alpha)
        else:
            b_s = jnp.broadcast_to(gmlp_b_s[j][:, :, None], (GMLP_GROUPS, GMLP_CHUNK, GMLP_CHUNK))
            x2d = _gmlp_ln(x2d, gmlp_w_in[j].astype(BF16), vec(gmlp_ln_g[j]), vec(gmlp_ln_b[j]), gmlp_w_s[j], b_s,
                           gmlp_w_out[j].astype(BF16), g0, b0, alpha)
            x2d = _moe_layer(x2d, moe_w_router[j], *moe_w, g1, b1, alpha)
    return x2d.reshape(bsz, seq, dm)
```

```python
import functools
import math

import jax
import jax.numpy as jnp
import numpy as np
from jax import lax
from jax.experimental import pallas as pl
from jax.experimental.pallas import tpu as pltpu

F32 = jnp.float32
BF16 = jnp.bfloat16

D_MODEL = 1024
HEAD_DIM = 64
BLOCK_Q = 128
A_HEADS = 8
A_KV_HEADS = 2
A_WINDOW = 128
B_HEADS = 8
B_KV_HEADS = 2
GROUP = 4
CMP_LEN = 32
CMP_STRIDE = 16
CMP_HIDDEN = 256
SLC_BLOCK = 64
SLC_TOPN = 8
B_WINDOW = 256
N_ATT_HEADS = A_HEADS + B_HEADS
GMLP_GROUPS = 8
GMLP_CHUNK = 128
N_EXPERTS = 8
TOP_K = 2
LN_EPS = 1e-5
LANES = 128
MXU_WIDTH = 256
BF16_SUBLANES = 16
DMA_UNROLL = 8

TOK_W = 640
KEY_W = 384
FEAT_W = 1408
GATE_ROWS = 32
SLC_CHUNK = 256

ROW_TILE = 512
MOE_ROWS = 512
COPY_TOKENS = 1024
FFN_CHUNK = 1024
EXPERT_TILE = 1792

ALIBI_SLOPES = tuple(2.0 ** (-8.0 * h / N_ATT_HEADS) for h in range(1, N_ATT_HEADS + 1))
SCALE = HEAD_DIM ** -0.5
LOG2E = 1.4426950408889634
NEG_INF = float("-inf")
RANK_FORCED = 1e30
RANK_INVALID = -1.0
MASK_BIG = 1e30
AUX_SEL = 32


def _params(n_grid, vmem_mb):
    return pltpu.CompilerParams(
        dimension_semantics=("arbitrary",) * n_grid, vmem_limit_bytes=vmem_mb * 1024 * 1024
    )


def _dot(a, b):
    return jnp.dot(a, b, preferred_element_type=F32)


def _dot_nt(a, b):
    return lax.dot_general(a, b, (((1,), (1,)), ((), ())), preferred_element_type=F32)


def _layer_norm(z, g, b):
    mu = jnp.mean(z, axis=-1, keepdims=True)
    zc = z - mu
    var = jnp.mean(zc * zc, axis=-1, keepdims=True)
    return zc * lax.rsqrt(var + LN_EPS) * g + b


def _gelu(x):
    return 0.5 * x * (1.0 + jnp.tanh(0.7978845608028654 * (x + 0.044715 * (x * x * x))))


def _silu(x):
    return x / (1.0 + jnp.exp(-x))


def _cast_plan(w, n_steps):
    cols = w.shape[-1]
    rows = w.size // cols
    n_slices = math.gcd(n_steps, rows // BF16_SUBLANES)
    view = w.reshape(n_slices, rows // n_slices, cols)
    spec = pl.BlockSpec((1,) + view.shape[1:], lambda i: (jnp.minimum(i, n_slices - 1), 0, 0))
    return view, spec


def _cast_side_job(src_refs, dst_refs):
    for src, dst in zip(src_refs, dst_refs):
        dst[...] = src[...].astype(BF16)


def _inproj_kernel(x_ref, wt_ref, wf_ref, wg_ref, c0_ref, c1_ref, c2_ref,
                   tok_ref, cmp_ref, feat_ref, gate_ref, d0_ref, d1_ref, d2_ref):
    _cast_side_job((c0_ref, c1_ref, c2_ref), (d0_ref, d1_ref, d2_ref))
    x = x_ref[...].astype(BF16)
    tok = _dot(x, wt_ref[...])
    tok_ref[...] = tok[:, :KEY_W].astype(BF16)
    cmp_ref[...] = tok[:, KEY_W:]
    rows = FEAT_W // 4
    for c in range(4):
        feat_ref[0, c * rows:(c + 1) * rows, :] = _dot_nt(wf_ref[c * rows:(c + 1) * rows, :], x).astype(BF16)
    gate_ref[0] = _dot_nt(wg_ref[...], x)


def _inproj(x2d, w_tok, w_feat, w_gate, to_cast, bsz, seq):
    n = x2d.shape[0]
    n_steps = n // ROW_TILE
    per_seq = seq // ROW_TILE
    views, cast_specs = zip(*[_cast_plan(w, n_steps) for w in to_cast])
    outs = pl.pallas_call(
        _inproj_kernel,
        grid=(n_steps,),
        in_specs=[
            pl.BlockSpec((ROW_TILE, D_MODEL), lambda i: (i, 0)),
            pl.BlockSpec((D_MODEL, TOK_W), lambda i: (0, 0)),
            pl.BlockSpec((FEAT_W, D_MODEL), lambda i: (0, 0)),
            pl.BlockSpec((GATE_ROWS, D_MODEL), lambda i: (0, 0)),
            *cast_specs,
        ],
        out_specs=[
            pl.BlockSpec((ROW_TILE, KEY_W), lambda i: (i, 0)),
            pl.BlockSpec((ROW_TILE, TOK_W - KEY_W), lambda i: (i, 0)),
            pl.BlockSpec((1, FEAT_W, ROW_TILE), lambda i: (i // per_seq, 0, i % per_seq)),
            pl.BlockSpec((1, GATE_ROWS, ROW_TILE), lambda i: (i // per_seq, 0, i % per_seq)),
            *cast_specs,
        ],
        out_shape=[
            jax.ShapeDtypeStruct((n, KEY_W), BF16),
            jax.ShapeDtypeStruct((n, TOK_W - KEY_W), F32),
            jax.ShapeDtypeStruct((bsz, FEAT_W, seq), BF16),
            jax.ShapeDtypeStruct((bsz, GATE_ROWS, seq), F32),
            *[jax.ShapeDtypeStruct(v.shape, BF16) for v in views],
        ],
        compiler_params=_params(1, 32),
        name="inproj",
    )(x2d, w_tok, w_feat, w_gate, *views)
    return outs[:4], [c.reshape(w.shape) for c, w in zip(outs[4:], to_cast)]


def _compress_hidden(x_ref, which, pe_ref, w1_ref, n_str):
    xs = jnp.concatenate(
        [x_ref[0, pl.ds(p, n_str, stride=CMP_STRIDE), :] for p in range(CMP_STRIDE)], axis=1).astype(BF16)
    a = _dot(xs, w1_ref[0, which, 0])
    b = _dot(xs, w1_ref[0, which, 1])
    b_next = pltpu.roll(b, shift=n_str - 1, axis=0)
    bias = _dot(pe_ref[which, 0].astype(BF16), w1_ref[0, which, 0]) + _dot(pe_ref[which, 1].astype(BF16),
                                                                         w1_ref[0, which, 1])
    return _gelu(a + b_next + bias[0:1]).astype(BF16)


def _compress_kernel(xk_ref, xv_ref, pe_ref, w1_ref, w2k_ref, w2vt_ref, kc_ref, vct_ref):
    n_str = kc_ref.shape[2]
    kc_ref[0, 0] = _dot(_compress_hidden(xk_ref, 0, pe_ref, w1_ref, n_str), w2k_ref[...]).astype(BF16)
    vct_ref[0, 0] = _dot_nt(w2vt_ref[...], _compress_hidden(xv_ref, 1, pe_ref, w1_ref, n_str)).astype(BF16)


def _compress(x_cmp, pe, w1, w2k, w2vt):
    bsz, seq, width = x_cmp.shape
    n_str = seq // CMP_STRIDE

    def full(shape):
        return pl.BlockSpec(shape, lambda b, h: (0,) * len(shape))

    return pl.pallas_call(
        _compress_kernel,
        grid=(bsz, B_KV_HEADS),
        in_specs=[
            pl.BlockSpec((1, seq, LANES), lambda b, h: (b, 0, 0)),
            pl.BlockSpec((1, seq, LANES), lambda b, h: (b, 0, 1)),
            full(pe.shape),
            pl.BlockSpec((1,) + w1.shape[1:], lambda b, h: (h, 0, 0, 0, 0)),
            full((CMP_HIDDEN, HEAD_DIM)),
            full((HEAD_DIM, CMP_HIDDEN)),
        ],
        out_specs=[
            pl.BlockSpec((1, 1, n_str, HEAD_DIM), lambda b, h: (b, h, 0, 0)),
            pl.BlockSpec((1, 1, HEAD_DIM, n_str), lambda b, h: (b, h, 0, 0)),
        ],
        out_shape=[
            jax.ShapeDtypeStruct((bsz, B_KV_HEADS, n_str, HEAD_DIM), BF16),
            jax.ShapeDtypeStruct((bsz, B_KV_HEADS, HEAD_DIM, n_str), BF16),
        ],
        compiler_params=_params(2, 32),
        name="nsa_compress",
    )(x_cmp, x_cmp, pe, w1, w2k, w2vt)


def _bf16_parts(x, parts=3):
    out, rest = [], np.asarray(x, np.float32)
    for _ in range(parts):
        piece = rest.astype(jnp.bfloat16).astype(np.float32)
        out.append(piece)
        rest = rest - piece
    return out


def _aux_key_lanes(seq):
    t = np.arange(seq)
    aux = np.zeros((seq, LANES), np.float32)
    aux[t, t // SLC_BLOCK] = 1.0
    aux[:, AUX_SEL:AUX_SEL + 3] = (t >> 7)[:, None]
    aux[:, AUX_SEL + 3:AUX_SEL + 6] = (t & 127)[:, None]
    return jnp.asarray(aux, BF16)


def _aux_slope_rows():
    rows = np.zeros((N_ATT_HEADS // GROUP, LANES - AUX_SEL, GROUP * BLOCK_Q), np.float32)
    for grp in range(N_ATT_HEADS // GROUP):
        for g in range(GROUP):
            slope = ALIBI_SLOPES[grp * GROUP + g] * LOG2E
            cols = slice(g * BLOCK_Q, (g + 1) * BLOCK_Q)
            for i, piece in enumerate(_bf16_parts(128.0 * slope) + _bf16_parts(slope)):
                rows[grp, i, cols] = piece
    return jnp.asarray(rows, BF16)


def _attn_kernel(sink_ref, qt_ref, ka_ref, ks_ref, kw_ref, vat_ref, vst_ref, vwt_ref, kc_ref, vct_ref, gt_ref,
                 aux_ref, slope_ref, o_ref, ot_ref, mask_a_ref, mask_w_ref, s_ref, e_ref, *, seq):
    n = pl.program_id(1)
    t0 = n * BLOCK_Q
    n_str = seq // CMP_STRIDE
    n_cmp = n_str - CMP_LEN // CMP_STRIDE + 1
    n_sel = seq // SLC_BLOCK
    top_n = min(SLC_TOPN, n_sel)
    hd = HEAD_DIM
    gq = GROUP * BLOCK_Q
    ch = SLC_CHUNK
    span_a = A_WINDOW + BLOCK_Q
    span_w = B_WINDOW + BLOCK_Q

    lane = lax.broadcasted_iota(jnp.int32, (1, gq), 1)
    q_loc = lane & (BLOCK_Q - 1)
    lane_head = lane >> (BLOCK_Q.bit_length() - 1)
    t_q = t0 + q_loc

    def head_row(vals):
        return jnp.where(lane_head == 0, vals[0], jnp.where(lane_head == 1, vals[1],
                                                           jnp.where(lane_head == 2, vals[2], vals[3])))

    def q_group(first_head):
        return jnp.concatenate(
            [qt_ref[0, (first_head + g) * hd:(first_head + g + 1) * hd, :] for g in range(GROUP)], axis=1)

    def score_rhs(grp, kvh, qg, sel_rows):
        zero = jnp.zeros((hd, gq), BF16)
        q_rows = [qg, zero] if kvh == 0 else [zero, qg]
        return jnp.concatenate(q_rows + [sel_rows, slope_ref[grp]], axis=0)

    def score_lhs(k_ref, start, span):
        return jnp.concatenate([k_ref[0, pl.ds(start, span), :], aux_ref[pl.ds(start, span), :]], axis=1)

    def band_start(span):
        return pl.multiple_of(jnp.maximum(t0 - (span - BLOCK_Q), 0), BLOCK_Q)

    def band_mask(span, window):
        ik = lax.broadcasted_iota(jnp.int32, (span, gq), 0)
        dist = (t0 - band_start(span)) + q_loc - ik
        return jnp.where((dist >= 0) & (dist < window), 0.0, -MASK_BIG)

    @pl.when(n <= span_w // BLOCK_Q - 1)
    def _():
        mask_a_ref[...] = band_mask(span_a, A_WINDOW)
        mask_w_ref[...] = band_mask(span_w, B_WINDOW)

    no_sel = jnp.zeros((AUX_SEL, gq), BF16)
    n_grp = N_ATT_HEADS // GROUP
    slopes = [head_row([ALIBI_SLOPES[grp * GROUP + g] for g in range(GROUP)]) for grp in range(n_grp)]
    qgs = [q_group(grp * GROUP) for grp in range(n_grp)]

    def softmax_cols(s, sinks=None):
        m = jnp.max(s, axis=0, keepdims=True)
        if sinks is not None:
            m = jnp.maximum(m, sinks)
        e = jnp.exp2(s - m)
        den = jnp.sum(e, axis=0, keepdims=True)
        if sinks is not None:
            den = den + jnp.exp2(sinks - m)
        return e.astype(BF16), den

    ks_a = band_start(span_a)
    ks_w = band_start(span_w)
    lhs_a = score_lhs(ka_ref, ks_a, span_a)
    lhs_w = score_lhs(kw_ref, ks_w, span_w)
    s_a = [_dot(lhs_a, score_rhs(kvh, kvh, qgs[kvh], no_sel)) + mask_a_ref[...] for kvh in range(A_KV_HEADS)]
    s_w = [_dot(lhs_w, score_rhs(A_KV_HEADS + kvh, kvh, qgs[A_KV_HEADS + kvh], no_sel)) + mask_w_ref[...]
           for kvh in range(B_KV_HEADS)]

    c_idx = lax.broadcasted_iota(jnp.int32, (n_str, gq), 0)
    dist_c = t_q - (c_idx * CMP_STRIDE + CMP_LEN - 1)
    valid_c = (dist_c >= 0) & (c_idx < n_cmp)
    dist_cf = dist_c.astype(F32)
    s_c = [jnp.where(valid_c, _dot(kc_ref[0, kvh], qgs[A_KV_HEADS + kvh])
                     - (slopes[A_KV_HEADS + kvh] * LOG2E) * dist_cf, NEG_INF) for kvh in range(B_KV_HEADS)]

    oj = lax.broadcasted_iota(jnp.int32, (n_sel, n_str), 0) * SLC_BLOCK
    oc = lax.broadcasted_iota(jnp.int32, (n_sel, n_str), 1) * CMP_STRIDE
    overlap_t = ((oc < oj + SLC_BLOCK) & (oc + CMP_LEN > oj)).astype(BF16)
    j_idx = lax.broadcasted_iota(jnp.int32, (n_sel, BLOCK_Q), 0)
    t_blk = (t0 + lax.broadcasted_iota(jnp.int32, (1, BLOCK_Q), 1)) >> (SLC_BLOCK.bit_length() - 1)
    valid_j = j_idx <= t_blk
    forced_j = (j_idx == 0) | (j_idx == t_blk) | (j_idx == t_blk - 1)
    e_c, den_c, rhs_s = [], [], []
    for kvh in range(B_KV_HEADS):
        m = jnp.max(s_c[kvh], axis=0, keepdims=True)
        m = jnp.where(m == NEG_INF, 0.0, m)
        e = jnp.exp2(s_c[kvh] - m)
        den = jnp.sum(e, axis=0, keepdims=True)
        den = jnp.where(den > 0.0, den, 1.0)
        e_c.append(e.astype(BF16))
        den_c.append(den)
        p = e / den
        psum = p[:, 0:BLOCK_Q]
        for g in range(1, GROUP):
            psum = psum + p[:, g * BLOCK_Q:(g + 1) * BLOCK_Q]
        p_hi = psum.astype(BF16)
        p_lo = (psum - p_hi.astype(F32)).astype(BF16)
        imp = _dot(overlap_t, p_hi) + _dot(overlap_t, p_lo)
        rank = jnp.where(forced_j, RANK_FORCED, jnp.where(valid_j, imp, RANK_INVALID))
        cnt = jnp.zeros((n_sel, BLOCK_Q), jnp.int32)
        for jp in range(n_sel):
            row = rank[jp:jp + 1, :]
            before = (row > rank) | ((row == rank) & (j_idx > jp))
            cnt = cnt + before.astype(jnp.int32)
        sel_bias = jnp.where(cnt < top_n, 0.0, -MASK_BIG)
        if n_sel < AUX_SEL:
            sel_bias = jnp.concatenate([sel_bias, jnp.zeros((AUX_SEL - n_sel, BLOCK_Q), F32)], axis=0)
        sel_rows = jnp.concatenate([sel_bias.astype(BF16)] * GROUP, axis=1)
        rhs_s.append(score_rhs(A_KV_HEADS + kvh, kvh, qgs[A_KV_HEADS + kvh], sel_rows))

    t_qf = t_q.astype(F32)
    sinks = [(head_row([sink_ref[kvh * GROUP + g] for g in range(GROUP)]) + slopes[kvh] * t_qf) * LOG2E
             for kvh in range(A_KV_HEADS)]
    ed_a = [softmax_cols(s_a[kvh], sinks[kvh]) for kvh in range(A_KV_HEADS)]
    ed_w = [softmax_cols(s_w[kvh]) for kvh in range(B_KV_HEADS)]

    def rows(kvh):
        return slice(kvh * hd, (kvh + 1) * hd)

    o_a = [_dot(vat_ref[0, rows(kvh), pl.ds(ks_a, span_a)], ed_a[kvh][0]) / ed_a[kvh][1] for kvh in range(A_KV_HEADS)]
    o_w = [_dot(vwt_ref[0, rows(kvh), pl.ds(ks_w, span_w)], ed_w[kvh][0]) / ed_w[kvh][1] for kvh in range(B_KV_HEADS)]
    o_c = [_dot(vct_ref[0, kvh], e_c[kvh]) / den_c[kvh] for kvh in range(B_KV_HEADS)]
    for kvh in range(A_KV_HEADS):
        for g in range(GROUP):
            h = kvh * GROUP + g
            ot_ref[h * hd:(h + 1) * hd, :] = o_a[kvh][:, g * BLOCK_Q:(g + 1) * BLOCK_Q]

    dloc = q_loc - lax.broadcasted_iota(jnp.int32, (ch, gq), 0)
    n_full = t0 // ch

    def slc_scores(c):
        lhs = score_lhs(ks_ref, pl.multiple_of(c * ch, ch), ch)
        for kvh in range(B_KV_HEADS):
            s_ref[c & 1, kvh] = _dot(lhs, rhs_s[kvh])

    def slc_values(c, state, kvh):
        m, l, acc, alpha = state
        vt = vst_ref[0, rows(kvh), pl.ds(pl.multiple_of(jnp.maximum(c, 0) * ch, ch), ch)]
        return m, l, alpha * acc + _dot(vt, e_ref[kvh]), alpha

    def slc_softmax(c, state, kvh, causal=None):
        m, l, acc, _ = state
        s = s_ref[c & 1, kvh]
        if causal is not None:
            s = jnp.where(causal, s, -MASK_BIG)
        m_new = jnp.maximum(m, jnp.max(s, axis=0, keepdims=True))
        alpha = jnp.exp2(m - m_new)
        e = jnp.exp2(s - m_new)
        e_ref[kvh] = e.astype(BF16)
        return m_new, alpha * l + jnp.sum(e, axis=0, keepdims=True), acc, alpha

    def slc_step(c, states):
        states = tuple(slc_values(c - 1, states[kvh], kvh) for kvh in range(B_KV_HEADS))
        states = tuple(slc_softmax(c, states[kvh], kvh) for kvh in range(B_KV_HEADS))
        slc_scores(c + 1)
        return states

    e_ref[...] = jnp.zeros_like(e_ref)
    init = (jnp.full((1, gq), NEG_INF, F32), jnp.zeros((1, gq), F32), jnp.zeros((hd, gq), F32),
            jnp.ones((1, gq), F32))
    slc_scores(jnp.int32(0))
    states = lax.fori_loop(0, n_full, slc_step, (init,) * B_KV_HEADS)
    causal = dloc >= n_full * ch - t0
    gt = jax.nn.sigmoid(gt_ref[0])
    for kvh in range(B_KV_HEADS):
        state = slc_values(n_full - 1, states[kvh], kvh)
        state = slc_softmax(n_full, state, kvh, causal)
        _, l_fin, acc_fin, _ = slc_values(n_full, state, kvh)
        o_slc = acc_fin / l_fin
        for g in range(GROUP):
            hb = kvh * GROUP + g
            h = A_HEADS + hb
            cols = slice(g * BLOCK_Q, (g + 1) * BLOCK_Q)
            ot_ref[h * hd:(h + 1) * hd, :] = (
                gt[hb:hb + 1] * o_c[kvh][:, cols]
                + gt[B_HEADS + hb:B_HEADS + hb + 1] * o_slc[:, cols]
                + gt[2 * B_HEADS + hb:2 * B_HEADS + hb + 1] * o_w[kvh][:, cols]
            )

    o_ref[0] = ot_ref[...].T.astype(BF16)


def _attention(sinks, tok, feat, kc, vct, gates_t, seq):
    bsz = tok.shape[0]
    n_str = seq // CMP_STRIDE
    kv_w = A_KV_HEADS * HEAD_DIM
    q_rows = N_ATT_HEADS * HEAD_DIM
    gq = GROUP * BLOCK_Q
    assert seq // SLC_BLOCK <= AUX_SEL and seq <= 128 * 256

    def k_spec(col_block):
        return pl.BlockSpec((1, seq, kv_w), lambda b, n: (b, 0, col_block))

    def vt_spec(row_block):
        return pl.BlockSpec((1, kv_w, seq), lambda b, n: (b, q_rows // kv_w + row_block, 0))

    return pl.pallas_call(
        functools.partial(_attn_kernel, seq=seq),
        grid=(bsz, seq // BLOCK_Q),
        in_specs=[
            pl.BlockSpec(memory_space=pltpu.SMEM),
            pl.BlockSpec((1, q_rows, BLOCK_Q), lambda b, n: (b, 0, n)),
            k_spec(0), k_spec(1), k_spec(2),
            vt_spec(0), vt_spec(1), vt_spec(2),
            pl.BlockSpec((1, B_KV_HEADS, n_str, HEAD_DIM), lambda b, n: (b, 0, 0, 0)),
            pl.BlockSpec((1, B_KV_HEADS, HEAD_DIM, n_str), lambda b, n: (b, 0, 0, 0)),
            pl.BlockSpec((1, GATE_ROWS, BLOCK_Q), lambda b, n: (b, 0, n)),
            pl.BlockSpec((seq, LANES), lambda b, n: (0, 0)),
            pl.BlockSpec((N_ATT_HEADS // GROUP, LANES - AUX_SEL, gq), lambda b, n: (0, 0, 0)),
        ],
        out_specs=pl.BlockSpec((1, BLOCK_Q, q_rows), lambda b, n: (b, n, 0)),
        out_shape=jax.ShapeDtypeStruct((bsz, seq, q_rows), BF16),
        scratch_shapes=[
            pltpu.VMEM((q_rows, BLOCK_Q), F32),
            pltpu.VMEM((A_WINDOW + BLOCK_Q, gq), F32),
            pltpu.VMEM((B_WINDOW + BLOCK_Q, gq), F32),
            pltpu.VMEM((2, B_KV_HEADS, SLC_CHUNK, gq), F32),
            pltpu.VMEM((B_KV_HEADS, SLC_CHUNK, gq), BF16),
        ],
        compiler_params=_params(2, 32),
        name="hybrid_attention",
    )(sinks, feat, tok, tok, tok, feat, feat, feat, kc, vct, gates_t, _aux_key_lanes(seq), _aux_slope_rows())


def _proj_ffn_ln_kernel(o_ref, wo_ref, x0_ref, g0_ref, b0_ref, wg_ref, wu_ref, wd_ref, g_ref, b_ref,
                        c0_ref, c1_ref, c2_ref, y_ref, d0_ref, d1_ref, d2_ref, *, alpha):
    _cast_side_job((c0_ref, c1_ref, c2_ref), (d0_ref, d1_ref, d2_ref))
    x = _layer_norm(alpha * x0_ref[...] + _dot(o_ref[...], wo_ref[...]), g0_ref[...], b0_ref[...])
    xb = x.astype(BF16)
    d_ff = wg_ref.shape[1]
    y = None
    for lo in range(0, d_ff, FFN_CHUNK):
        hi = min(lo + FFN_CHUNK, d_ff)
        h = (_silu(_dot(xb, wg_ref[:, lo:hi])) * _dot(xb, wu_ref[:, lo:hi])).astype(BF16)
        part = _dot(h, wd_ref[lo:hi, :])
        y = part if y is None else y + part
    y_ref[...] = _layer_norm(alpha * x + y, g_ref[...], b_ref[...])


def _proj_ffn_ln(o2d, wo, x2d, g0, b0, wg, wu, wd, g, b, to_cast, alpha):
    n, k = o2d.shape
    d_ff = wg.shape[1]
    n_steps = n // ROW_TILE
    row = pl.BlockSpec((ROW_TILE, D_MODEL), lambda i: (i, 0))
    vec = pl.BlockSpec((1, D_MODEL), lambda i: (0, 0))
    once = pl.Buffered(1)
    views, cast_specs = zip(*[_cast_plan(w, n_steps) for w in to_cast])
    outs = pl.pallas_call(
        functools.partial(_proj_ffn_ln_kernel, alpha=alpha),
        grid=(n_steps,),
        in_specs=[
            pl.BlockSpec((ROW_TILE, k), lambda i: (i, 0)),
            pl.BlockSpec((k, D_MODEL), lambda i: (0, 0), pipeline_mode=once),
            row, vec, vec,
            pl.BlockSpec((D_MODEL, d_ff), lambda i: (0, 0), pipeline_mode=once),
            pl.BlockSpec((D_MODEL, d_ff), lambda i: (0, 0), pipeline_mode=once),
            pl.BlockSpec((d_ff, D_MODEL), lambda i: (0, 0), pipeline_mode=once),
            vec, vec,
            *cast_specs,
        ],
        out_specs=[row, *cast_specs],
        out_shape=[jax.ShapeDtypeStruct((n, D_MODEL), F32), *[jax.ShapeDtypeStruct(v.shape, BF16) for v in views]],
        compiler_params=_params(1, 60),
        name="out_proj_swiglu_ln",
    )(o2d, wo, x2d, g0, b0, wg, wu, wd, g, b, *views)
    return outs[0], [c.reshape(w.shape) for c, w in zip(outs[1:], to_cast)]


def _expert_kernel(be_ref, used_ref, x_ref, wg_ref, wu_ref, wd_ref, y_ref, xb_ref):
    del be_ref
    f = pl.program_id(1)
    in_use = pl.program_id(0) < used_ref[0]

    def hidden(x):
        return (_silu(_dot(x, wg_ref[0])) * _dot(x, wu_ref[0])).astype(BF16)

    def down(h, first):
        for lo in range(0, D_MODEL, MXU_WIDTH):
            part = _dot(h, wd_ref[0, :, lo:lo + MXU_WIDTH])
            if first:
                y_ref[:, lo:lo + MXU_WIDTH] = part
            else:
                y_ref[:, lo:lo + MXU_WIDTH] += part

    @pl.when(in_use & (f == 0))
    def _():
        xb = x_ref[...].astype(BF16)
        xb_ref[...] = xb
        down(hidden(xb), True)

    @pl.when(in_use & (f != 0))
    def _():
        down(hidden(xb_ref[...]), False)

    @pl.when(jnp.logical_not(in_use) & (f == 0))
    def _():
        y_ref[...] = jnp.zeros_like(y_ref)


def _experts(blk_e, n_used, xin, wg, wu, wd):
    n_rows = blk_e.shape[0] * MOE_ROWS
    d_ff = wg.shape[2]
    n_f = d_ff // EXPERT_TILE

    def row_blk(i, used):
        return jnp.minimum(i, used[0] - 1)

    def hid_blk(i, f, used):
        return jnp.where(i < used[0], f, n_f - 1)

    return pl.pallas_call(
        _expert_kernel,
        grid_spec=pltpu.PrefetchScalarGridSpec(
            num_scalar_prefetch=2,
            grid=(n_rows // MOE_ROWS, n_f),
            in_specs=[
                pl.BlockSpec((MOE_ROWS, D_MODEL), lambda i, f, be, used: (row_blk(i, used), 0)),
                pl.BlockSpec((1, D_MODEL, EXPERT_TILE), lambda i, f, be, used: (be[i], 0, hid_blk(i, f, used))),
                pl.BlockSpec((1, D_MODEL, EXPERT_TILE), lambda i, f, be, used: (be[i], 0, hid_blk(i, f, used))),
                pl.BlockSpec((1, EXPERT_TILE, D_MODEL), lambda i, f, be, used: (be[i], hid_blk(i, f, used), 0)),
            ],
            out_specs=pl.BlockSpec((MOE_ROWS, D_MODEL), lambda i, f, be, used: (i, 0)),
            scratch_shapes=[pltpu.VMEM((MOE_ROWS, D_MODEL), BF16)],
        ),
        out_shape=jax.ShapeDtypeStruct((n_rows, D_MODEL), F32),
        compiler_params=_params(2, 52),
        name="expert_swiglu",
    )(blk_e, n_used, xin, wg, wu, wd)


def _gmlp_kernel(x_ref, win_ref, lng_ref, lnb_ref, ws_ref, bs_ref, wout_ref, g_ref, b_ref, y_ref,
                 u_ref, vn_ref, gated_ref, *, alpha):
    x = x_ref[...]
    xb = x.astype(BF16)
    u_ref[...] = _gelu(_dot(xb, win_ref[:, :D_MODEL]))
    v = _gelu(_dot(xb, win_ref[:, D_MODEL:]))
    vn_ref[...] = _layer_norm(v, lng_ref[...], lnb_ref[...]).astype(BF16)
    cs = GMLP_CHUNK
    lower = lax.broadcasted_iota(jnp.int32, (cs, cs), 0) >= lax.broadcasted_iota(jnp.int32, (cs, cs), 1)
    for grp in range(GMLP_GROUPS):
        w = jnp.where(lower, ws_ref[grp], 0.0).astype(BF16)
        bias = bs_ref[grp]
        cols = slice(grp * cs, (grp + 1) * cs)
        for c in range(x.shape[0] // cs):
            rows = slice(c * cs, (c + 1) * cs)
            mixed = _dot(w, vn_ref[rows, cols]) + bias
            gated_ref[rows, cols] = (u_ref[rows, cols] * mixed).astype(BF16)
    y = _dot(gated_ref[...], wout_ref[...])
    y_ref[...] = _layer_norm(alpha * x + y, g_ref[...], b_ref[...])


def _gmlp_ln(x2d, w_in, ln_g, ln_b, w_s, b_s, w_out, g, b, alpha):
    n = x2d.shape[0]
    row = pl.BlockSpec((ROW_TILE, D_MODEL), lambda i: (i, 0))
    vec = pl.BlockSpec((1, D_MODEL), lambda i: (0, 0))
    grp = pl.BlockSpec((GMLP_GROUPS, GMLP_CHUNK, GMLP_CHUNK), lambda i: (0, 0, 0))
    return pl.pallas_call(
        functools.partial(_gmlp_kernel, alpha=alpha),
        grid=(n // ROW_TILE,),
        in_specs=[
            row,
            pl.BlockSpec((D_MODEL, 2 * D_MODEL), lambda i: (0, 0)),
            vec, vec, grp, grp,
            pl.BlockSpec((D_MODEL, D_MODEL), lambda i: (0, 0)),
            vec, vec,
        ],
        out_specs=row,
        out_shape=jax.ShapeDtypeStruct((n, D_MODEL), F32),
        scratch_shapes=[
            pltpu.VMEM((ROW_TILE, D_MODEL), F32),
            pltpu.VMEM((ROW_TILE, D_MODEL), BF16),
            pltpu.VMEM((ROW_TILE, D_MODEL), BF16),
        ],
        compiler_params=_params(1, 40),
        name="gmlp_ln",
    )(x2d, w_in, ln_g, ln_b, w_s, b_s, w_out, g, b)


def _route_kernel(x_ref, w_ref, gate_ref, et_ref, post_ref, cnt_ref, run_ref):
    tm = x_ref.shape[0]

    @pl.when(pl.program_id(0) == 0)
    def _():
        run_ref[...] = jnp.zeros_like(run_ref)

    logits = lax.dot_general(w_ref[...], x_ref[...], (((1,), (1,)), ((), ())),
                             preferred_element_type=F32, precision=lax.Precision.HIGHEST)
    row = lax.broadcasted_iota(jnp.int32, (N_EXPERTS, tm), 0)
    v1 = jnp.max(logits, axis=0, keepdims=True)
    i1 = jnp.min(jnp.where(logits == v1, row, N_EXPERTS), axis=0, keepdims=True)
    rest = jnp.where(row == i1, NEG_INF, logits)
    v2 = jnp.max(rest, axis=0, keepdims=True)
    i2 = jnp.min(jnp.where(rest == v2, row, N_EXPERTS), axis=0, keepdims=True)
    d = jnp.exp(v2 - v1)
    g1 = 1.0 / (1.0 + d)
    g2 = d / (1.0 + d)

    hit1 = row == i1
    hit2 = row == i2
    onehot = (hit1 | hit2).astype(BF16)
    earlier = (lax.broadcasted_iota(jnp.int32, (tm, tm), 0) < lax.broadcasted_iota(jnp.int32, (tm, tm), 1)).astype(BF16)
    before = _dot(onehot, earlier) + run_ref[:, 0:1]
    pos1 = jnp.sum(jnp.where(hit1, before, 0.0), axis=0, keepdims=True).astype(jnp.int32)
    pos2 = jnp.sum(jnp.where(hit2, before, 0.0), axis=0, keepdims=True).astype(jnp.int32)
    run_ref[...] += jnp.sum(onehot.astype(F32), axis=1, keepdims=True)

    et_ref[...] = jnp.where(row == 0, i1, jnp.where(row == 1, i2, 0))
    post_ref[...] = jnp.where(row == 0, pos1, jnp.where(row == 1, pos2, 0))
    cnt_ref[...] = run_ref[...]
    wide = lax.broadcasted_iota(jnp.int32, (LANES, tm), 0)
    gate_ref[...] = jnp.where(wide == 0, g1, jnp.where(wide == 1, g2, 0.0)).T


def _route(x2d, w_router_t):
    n = x2d.shape[0]
    tm = ROW_TILE
    per_tok = pl.BlockSpec((N_EXPERTS, tm), lambda i: (0, i))
    return pl.pallas_call(
        _route_kernel,
        grid=(n // tm,),
        in_specs=[
            pl.BlockSpec((tm, D_MODEL), lambda i: (i, 0)),
            pl.BlockSpec((N_EXPERTS, D_MODEL), lambda i: (0, 0)),
        ],
        out_specs=[
            pl.BlockSpec((tm, LANES), lambda i: (i, 0)),
            per_tok, per_tok,
            pl.BlockSpec((N_EXPERTS, LANES), lambda i: (0, 0)),
        ],
        out_shape=[
            jax.ShapeDtypeStruct((n, LANES), F32),
            jax.ShapeDtypeStruct((N_EXPERTS, n), jnp.int32),
            jax.ShapeDtypeStruct((N_EXPERTS, n), jnp.int32),
            jax.ShapeDtypeStruct((N_EXPERTS, LANES), F32),
        ],
        scratch_shapes=[pltpu.VMEM((N_EXPERTS, LANES), F32)],
        compiler_params=_params(1, 24),
        name="moe_router",
    )(x2d, w_router_t)


def _dispatch_kernel(d0_ref, d1_ref, lo_ref, hi_ref, x_ref, xin_hbm, zero_ref, sem_rows, sem_zero):
    step = pl.program_id(0)
    tm = x_ref.shape[0]
    base = step * tm

    def row_copy(r, dest_ref):
        return pltpu.make_async_copy(x_ref.at[pl.ds(r, 1)], xin_hbm.at[pl.ds(dest_ref[base + r], 1)], sem_rows)

    def issue(r8, carry):
        for j in range(DMA_UNROLL):
            r = r8 * DMA_UNROLL + j
            row_copy(r, d0_ref).start(priority=j % 2)
            row_copy(r, d1_ref).start(priority=(j + 1) % 2)
        return carry

    lax.fori_loop(0, tm // DMA_UNROLL, issue, 0)

    @pl.when(step == pl.num_programs(0) - 1)
    def _():
        zero_ref[...] = jnp.zeros_like(zero_ref)
        for k in range(N_EXPERTS + 1):
            def fill(r, carry):
                pltpu.make_async_copy(zero_ref, xin_hbm.at[pl.ds(r, 1)], sem_zero).start()
                return carry

            def drain(r, carry):
                pltpu.make_async_copy(zero_ref, xin_hbm.at[pl.ds(r, 1)], sem_zero).wait()
                return carry

            lax.fori_loop(lo_ref[k], hi_ref[k], fill, 0)
            lax.fori_loop(lo_ref[k], hi_ref[k], drain, 0)

    pltpu.make_async_copy(x_ref, xin_hbm.at[pl.ds(0, tm)], sem_rows).wait()
    pltpu.make_async_copy(x_ref, xin_hbm.at[pl.ds(0, tm)], sem_rows).wait()


def _dispatch(d0, d1, fill_lo, fill_hi, x2d, n_rows):
    n = x2d.shape[0]
    tm = COPY_TOKENS
    return pl.pallas_call(
        _dispatch_kernel,
        grid_spec=pltpu.PrefetchScalarGridSpec(
            num_scalar_prefetch=4,
            grid=(n // tm,),
            in_specs=[pl.BlockSpec((tm, D_MODEL), lambda i, *_: (i, 0))],
            out_specs=pl.BlockSpec(memory_space=pl.ANY),
            scratch_shapes=[pltpu.VMEM((1, D_MODEL), F32), pltpu.SemaphoreType.DMA(()), pltpu.SemaphoreType.DMA(())],
        ),
        out_shape=jax.ShapeDtypeStruct((n_rows, D_MODEL), F32),
        compiler_params=_params(1, 16),
        name="moe_dispatch",
    )(d0, d1, fill_lo, fill_hi, x2d)


def _combine_kernel(d0_ref, d1_ref, y_hbm, gate_ref, x_ref, g_ref, b_ref, o_ref, buf, sem, *, alpha):
    step = pl.program_id(0)
    tm = x_ref.shape[0]
    slot = step & 1

    def fetch(tile, into):
        base = tile * tm

        def issue(r8, carry):
            for j in range(DMA_UNROLL):
                r = r8 * DMA_UNROLL + j
                pltpu.make_async_copy(y_hbm.at[pl.ds(d0_ref[base + r], 1)], buf.at[into, 0, pl.ds(r, 1)],
                                      sem.at[into, 0]).start(priority=0)
                pltpu.make_async_copy(y_hbm.at[pl.ds(d1_ref[base + r], 1)], buf.at[into, 1, pl.ds(r, 1)],
                                      sem.at[into, 1]).start(priority=1)
            return carry

        lax.fori_loop(0, tm // DMA_UNROLL, issue, 0)

    @pl.when(step == 0)
    def _():
        fetch(step, slot)

    @pl.when(step + 1 < pl.num_programs(0))
    def _():
        fetch(step + 1, 1 - slot)

    for k in range(TOP_K):
        pltpu.make_async_copy(y_hbm.at[pl.ds(0, tm)], buf.at[slot, k], sem.at[slot, k]).wait()
    gate = gate_ref[...]
    y = buf[slot, 0] * gate[:, 0:1] + buf[slot, 1] * gate[:, 1:2]
    o_ref[...] = _layer_norm(alpha * x_ref[...] + y, g_ref[...], b_ref[...])


def _combine_ln(d0, d1, yb, gate, x2d, g, b, alpha):
    n = x2d.shape[0]
    tm = ROW_TILE
    row = pl.BlockSpec((tm, D_MODEL), lambda i, a, c: (i, 0))
    vec = pl.BlockSpec((1, D_MODEL), lambda i, a, c: (0, 0))
    return pl.pallas_call(
        functools.partial(_combine_kernel, alpha=alpha),
        grid_spec=pltpu.PrefetchScalarGridSpec(
            num_scalar_prefetch=2,
            grid=(n // tm,),
            in_specs=[
                pl.BlockSpec(memory_space=pl.ANY),
                pl.BlockSpec((tm, LANES), lambda i, a, c: (i, 0)),
                row, vec, vec,
            ],
            out_specs=row,
            scratch_shapes=[
                pltpu.VMEM((2, TOP_K, tm, D_MODEL), F32),
                pltpu.SemaphoreType.DMA((2, TOP_K)),
            ],
        ),
        out_shape=jax.ShapeDtypeStruct((n, D_MODEL), F32),
        compiler_params=_params(1, 40),
        name="moe_combine_ln",
    )(d0, d1, yb, gate, x2d, g, b)


def _attention_heads(x2d, bsz, seq, w_in, sinks, pe_k, wk1, wk2, pe_v, wv1, wv2, to_cast):
    aq, akv = A_HEADS * HEAD_DIM, A_KV_HEADS * HEAD_DIM
    bq, bkv = B_HEADS * HEAD_DIM, B_KV_HEADS * HEAD_DIM
    bounds = [0]
    for width in (aq, akv, akv, bq, bkv, bkv, bkv, bkv, bkv, bkv, 3 * B_HEADS):
        bounds.append(bounds[-1] + width)
    qa, ka, va, qb, kc, vc, ks, vs, kw, vw, wg = [w_in[:, bounds[i]:bounds[i + 1]] for i in range(11)]
    w_tok = jnp.concatenate([ka, ks, kw, kc, vc], axis=1).astype(BF16)
    w_feat = jnp.concatenate([qa * (SCALE * LOG2E), qb * (SCALE * LOG2E), va, vs, vw], axis=1).T.astype(BF16)
    wg = wg.reshape(D_MODEL, B_HEADS, 3).transpose(2, 1, 0).reshape(3 * B_HEADS, D_MODEL)
    wg = jnp.pad(wg, ((0, GATE_ROWS - 3 * B_HEADS), (0, 0))).astype(BF16)
    (tok, x_cmp, feat, gates_t), cast = _inproj(x2d, w_tok, w_feat, wg, to_cast, bsz, seq)

    halves = CMP_LEN // CMP_STRIDE
    w1 = jnp.stack([wk1, wv1]).reshape(2, halves, CMP_STRIDE, 1, HEAD_DIM, CMP_HIDDEN)
    own_head = jnp.eye(B_KV_HEADS, dtype=F32)[:, None, None, None, :, None, None]
    w1 = (w1[None] * own_head).reshape(B_KV_HEADS, 2, halves, CMP_STRIDE * bkv, CMP_HIDDEN).astype(BF16)
    pe = jnp.stack([pe_k, pe_v]).reshape(2, halves, CMP_STRIDE, 1, HEAD_DIM)
    pe = jnp.broadcast_to(pe, (2, halves, CMP_STRIDE, B_KV_HEADS, HEAD_DIM)).reshape(2, halves, 1, CMP_STRIDE * bkv)
    pe = jnp.broadcast_to(pe, (2, halves, 8, CMP_STRIDE * bkv))
    kcmp, vcmp_t = _compress(x_cmp.reshape(bsz, seq, 2 * bkv), pe, w1, wk2.astype(BF16), wv2.T.astype(BF16))

    o = _attention(sinks, tok.reshape(bsz, seq, KEY_W), feat, kcmp, vcmp_t, gates_t, seq)
    return o.reshape(bsz * seq, N_ATT_HEADS * HEAD_DIM), cast


def _moe_layer(x2d, w_router, w_gate, w_up, w_down, g, b, alpha):
    n_tok = x2d.shape[0]
    gate, e_t, pos_t, cnt = _route(x2d, w_router.T)
    counts = cnt[:, 0].astype(jnp.int32)
    padded = (counts + MOE_ROWS - 1) // MOE_ROWS * MOE_ROWS
    pad_ends = jnp.cumsum(padded)
    pad_starts = pad_ends - padded
    n_blocks = n_tok * TOP_K // MOE_ROWS + N_EXPERTS
    n_rows = n_blocks * MOE_ROWS
    experts = jnp.arange(N_EXPERTS, dtype=jnp.int32)[:, None, None]
    start_of = jnp.sum(jnp.where(e_t[None, :TOP_K] == experts, pad_starts[:, None, None], 0), axis=0)
    dest = start_of + pos_t[:TOP_K]
    fill_lo = jnp.concatenate([pad_starts + counts, pad_ends[-1:]])
    fill_hi = jnp.concatenate([pad_ends, jnp.full((1,), n_rows, jnp.int32)])
    blk_start = jnp.arange(n_blocks, dtype=jnp.int32) * MOE_ROWS
    blk_e = jnp.sum((pad_ends[None, :] <= blk_start[:, None]).astype(jnp.int32), axis=1)
    blk_e = jnp.minimum(blk_e, N_EXPERTS - 1)

    xin = _dispatch(dest[0], dest[1], fill_lo, fill_hi, x2d, n_rows)
    n_used = (pad_ends[-1:] // MOE_ROWS).astype(jnp.int32)
    yb = _experts(blk_e, n_used, xin, w_gate, w_up, w_down)
    return _combine_ln(dest[0], dest[1], yb, gate, x2d, g, b, alpha)


def kernel(x, att_w_in, att_sinks, cmp_pe_k, cmp_wk1, cmp_wk2, cmp_pe_v, cmp_wv1, cmp_wv2, att_w_o,
           ffn_w_gate, ffn_w_up, ffn_w_down, gmlp_w_in, gmlp_ln_g, gmlp_ln_b, gmlp_w_s, gmlp_b_s, gmlp_w_out,
           moe_w_router, moe_w_gate, moe_w_up, moe_w_down, ln_g, ln_b):
    bsz, seq, dm = x.shape
    depth = ln_g.shape[0]
    alpha = (2.0 * depth) ** 0.25
    assert dm == D_MODEL and seq % ROW_TILE == 0 and (bsz * seq) % COPY_TOKENS == 0
    x2d = x.reshape(bsz * seq, dm)

    def vec(p):
        return p.reshape(1, -1)

    for i in range(depth):
        j = i // 2
        g0, b0, g1, b1 = vec(ln_g[i, 0]), vec(ln_b[i, 0]), vec(ln_g[i, 1]), vec(ln_b[i, 1])
        if i % 2 == 0:
            o2d, ffn_w = _attention_heads(x2d, bsz, seq, att_w_in[j], att_sinks[j], cmp_pe_k[j], cmp_wk1[j],
                                          cmp_wk2[j], cmp_pe_v[j], cmp_wv1[j], cmp_wv2[j],
                                          (ffn_w_gate[j], ffn_w_up[j], ffn_w_down[j]))
            jn = min(j, moe_w_gate.shape[0] - 1)
            x2d, moe_w = _proj_ffn_ln(o2d, att_w_o[j].astype(BF16), x2d, g0, b0, *ffn_w, g1, b1,
                                      (moe_w_gate[jn], moe_w_up[jn], moe_w_down[jn]), alpha)
        else:
            b_s = jnp.broadcast_to(gmlp_b_s[j][:, :, None], (GMLP_GROUPS, GMLP_CHUNK, GMLP_CHUNK))
            x2d = _gmlp_ln(x2d, gmlp_w_in[j].astype(BF16), vec(gmlp_ln_g[j]), vec(gmlp_ln_b[j]), gmlp_w_s[j], b_s,
                           gmlp_w_out[j].astype(BF16), g0, b0, alpha)
            x2d = _moe_layer(x2d, moe_w_router[j], *moe_w, g1, b1, alpha)
    return x2d.reshape(bsz, seq, dm)
```

```python
import functools
import math

import jax
import jax.numpy as jnp
import numpy as np
from jax import lax
from jax.experimental import pallas as pl
from jax.experimental.pallas import tpu as pltpu

F32 = jnp.float32
BF16 = jnp.bfloat16

D_MODEL = 1024
HEAD_DIM = 64
BLOCK_Q = 128
A_HEADS = 8
A_KV_HEADS = 2
A_WINDOW = 128
B_HEADS = 8
B_KV_HEADS = 2
GROUP = 4
CMP_LEN = 32
CMP_STRIDE = 16
CMP_HIDDEN = 256
SLC_BLOCK = 64
SLC_TOPN = 8
B_WINDOW = 256
N_ATT_HEADS = A_HEADS + B_HEADS
GMLP_GROUPS = 8
GMLP_CHUNK = 128
N_EXPERTS = 8
TOP_K = 2
LN_EPS = 1e-5
LANES = 128
MXU_WIDTH = 256
BF16_SUBLANES = 16
DMA_UNROLL = 8

TOK_W = 640
KEY_W = 384
FEAT_W = 1408
GATE_ROWS = 32
SLC_CHUNK = 256

ROW_TILE = 512
MOE_ROWS = 512
COPY_TOKENS = 1024
FFN_CHUNK = 1024
EXPERT_TILE = 1792

ALIBI_SLOPES = tuple(2.0 ** (-8.0 * h / N_ATT_HEADS) for h in range(1, N_ATT_HEADS + 1))
SCALE = HEAD_DIM ** -0.5
LOG2E = 1.4426950408889634
NEG_INF = float("-inf")
RANK_FORCED = 1e30
RANK_INVALID = -1.0
MASK_BIG = 1e30
AUX_SEL = 32


def _params(n_grid, vmem_mb):
    return pltpu.CompilerParams(
        dimension_semantics=("arbitrary",) * n_grid, vmem_limit_bytes=vmem_mb * 1024 * 1024
    )


def _dot(a, b):
    return jnp.dot(a, b, preferred_element_type=F32)


def _dot_nt(a, b):
    return lax.dot_general(a, b, (((1,), (1,)), ((), ())), preferred_element_type=F32)


def _layer_norm(z, g, b):
    mu = jnp.mean(z, axis=-1, keepdims=True)
    zc = z - mu
    var = jnp.mean(zc * zc, axis=-1, keepdims=True)
    return zc * lax.rsqrt(var + LN_EPS) * g + b


def _gelu(x):
    return 0.5 * x * (1.0 + jnp.tanh(0.7978845608028654 * (x + 0.044715 * (x * x * x))))


def _silu(x):
    return x / (1.0 + jnp.exp(-x))


def _cast_plan(w, n_steps):
    cols = w.shape[-1]
    rows = w.size // cols
    n_slices = math.gcd(n_steps, rows // BF16_SUBLANES)
    view = w.reshape(n_slices, rows // n_slices, cols)
    spec = pl.BlockSpec((1,) + view.shape[1:], lambda i: (jnp.minimum(i, n_slices - 1), 0, 0))
    return view, spec


def _cast_side_job(src_refs, dst_refs):
    for src, dst in zip(src_refs, dst_refs):
        dst[...] = src[...].astype(BF16)


def _inproj_kernel(x_ref, wt_ref, wf_ref, wg_ref, c0_ref, c1_ref, c2_ref,
                   tok_ref, cmp_ref, feat_ref, gate_ref, d0_ref, d1_ref, d2_ref):
    _cast_side_job((c0_ref, c1_ref, c2_ref), (d0_ref, d1_ref, d2_ref))
    x = x_ref[...].astype(BF16)
    tok = _dot(x, wt_ref[...])
    tok_ref[...] = tok[:, :KEY_W].astype(BF16)
    cmp_ref[...] = tok[:, KEY_W:]
    rows = FEAT_W // 4
    for c in range(4):
        feat_ref[0, c * rows:(c + 1) * rows, :] = _dot_nt(wf_ref[c * rows:(c + 1) * rows, :], x).astype(BF16)
    gate_ref[0] = _dot_nt(wg_ref[...], x)


def _inproj(x2d, w_tok, w_feat, w_gate, to_cast, bsz, seq):
    n = x2d.shape[0]
    n_steps = n // ROW_TILE
    per_seq = seq // ROW_TILE
    views, cast_specs = zip(*[_cast_plan(w, n_steps) for w in to_cast])
    outs = pl.pallas_call(
        _inproj_kernel,
        grid=(n_steps,),
        in_specs=[
            pl.BlockSpec((ROW_TILE, D_MODEL), lambda i: (i, 0)),
            pl.BlockSpec((D_MODEL, TOK_W), lambda i: (0, 0)),
            pl.BlockSpec((FEAT_W, D_MODEL), lambda i: (0, 0)),
            pl.BlockSpec((GATE_ROWS, D_MODEL), lambda i: (0, 0)),
            *cast_specs,
        ],
        out_specs=[
            pl.BlockSpec((ROW_TILE, KEY_W), lambda i: (i, 0)),
            pl.BlockSpec((ROW_TILE, TOK_W - KEY_W), lambda i: (i, 0)),
            pl.BlockSpec((1, FEAT_W, ROW_TILE), lambda i: (i // per_seq, 0, i % per_seq)),
            pl.BlockSpec((1, GATE_ROWS, ROW_TILE), lambda i: (i // per_seq, 0, i % per_seq)),
            *cast_specs,
        ],
        out_shape=[
            jax.ShapeDtypeStruct((n, KEY_W), BF16),
            jax.ShapeDtypeStruct((n, TOK_W - KEY_W), F32),
            jax.ShapeDtypeStruct((bsz, FEAT_W, seq), BF16),
            jax.ShapeDtypeStruct((bsz, GATE_ROWS, seq), F32),
            *[jax.ShapeDtypeStruct(v.shape, BF16) for v in views],
        ],
        compiler_params=_params(1, 32),
        name="inproj",
    )(x2d, w_tok, w_feat, w_gate, *views)
    return outs[:4], [c.reshape(w.shape) for c, w in zip(outs[4:], to_cast)]


def _compress_hidden(x_ref, which, pe_ref, w1_ref, n_str):
    xs = jnp.concatenate(
        [x_ref[0, pl.ds(p, n_str, stride=CMP_STRIDE), :] for p in range(CMP_STRIDE)], axis=1).astype(BF16)
    a = _dot(xs, w1_ref[0, which, 0])
    b = _dot(xs, w1_ref[0, which, 1])
    b_next = pltpu.roll(b, shift=n_str - 1, axis=0)
    bias = _dot(pe_ref[which, 0].astype(BF16), w1_ref[0, which, 0]) + _dot(pe_ref[which, 1].astype(BF16),
                                                                         w1_ref[0, which, 1])
    return _gelu(a + b_next + bias[0:1]).astype(BF16)


def _compress_kernel(xk_ref, xv_ref, pe_ref, w1_ref, w2k_ref, w2vt_ref, kc_ref, vct_ref):
    n_str = kc_ref.shape[2]
    kc_ref[0, 0] = _dot(_compress_hidden(xk_ref, 0, pe_ref, w1_ref, n_str), w2k_ref[...]).astype(BF16)
    vct_ref[0, 0] = _dot_nt(w2vt_ref[...], _compress_hidden(xv_ref, 1, pe_ref, w1_ref, n_str)).astype(BF16)


def _compress(x_cmp, pe, w1, w2k, w2vt):
    bsz, seq, width = x_cmp.shape
    n_str = seq // CMP_STRIDE

    def full(shape):
        return pl.BlockSpec(shape, lambda b, h: (0,) * len(shape))

    return pl.pallas_call(
        _compress_kernel,
        grid=(bsz, B_KV_HEADS),
        in_specs=[
            pl.BlockSpec((1, seq, LANES), lambda b, h: (b, 0, 0)),
            pl.BlockSpec((1, seq, LANES), lambda b, h: (b, 0, 1)),
            full(pe.shape),
            pl.BlockSpec((1,) + w1.shape[1:], lambda b, h: (h, 0, 0, 0, 0)),
            full((CMP_HIDDEN, HEAD_DIM)),
            full((HEAD_DIM, CMP_HIDDEN)),
        ],
        out_specs=[
            pl.BlockSpec((1, 1, n_str, HEAD_DIM), lambda b, h: (b, h, 0, 0)),
            pl.BlockSpec((1, 1, HEAD_DIM, n_str), lambda b, h: (b, h, 0, 0)),
        ],
        out_shape=[
            jax.ShapeDtypeStruct((bsz, B_KV_HEADS, n_str, HEAD_DIM), BF16),
            jax.ShapeDtypeStruct((bsz, B_KV_HEADS, HEAD_DIM, n_str), BF16),
        ],
        compiler_params=_params(2, 32),
        name="nsa_compress",
    )(x_cmp, x_cmp, pe, w1, w2k, w2vt)


def _bf16_parts(x, parts=3):
    out, rest = [], np.asarray(x, np.float32)
    for _ in range(parts):
        piece = rest.astype(jnp.bfloat16).astype(np.float32)
        out.append(piece)
        rest = rest - piece
    return out


def _aux_key_lanes(seq):
    t = np.arange(seq)
    aux = np.zeros((seq, LANES), np.float32)
    aux[t, t // SLC_BLOCK] = 1.0
    aux[:, AUX_SEL:AUX_SEL + 3] = (t >> 7)[:, None]
    aux[:, AUX_SEL + 3:AUX_SEL + 6] = (t & 127)[:, None]
    return jnp.asarray(aux, BF16)


def _aux_slope_rows():
    rows = np.zeros((N_ATT_HEADS // GROUP, LANES - AUX_SEL, GROUP * BLOCK_Q), np.float32)
    for grp in range(N_ATT_HEADS // GROUP):
        for g in range(GROUP):
            slope = ALIBI_SLOPES[grp * GROUP + g] * LOG2E
            cols = slice(g * BLOCK_Q, (g + 1) * BLOCK_Q)
            for i, piece in enumerate(_bf16_parts(128.0 * slope) + _bf16_parts(slope)):
                rows[grp, i, cols] = piece
    return jnp.asarray(rows, BF16)


def _attn_kernel(sink_ref, qt_ref, ka_ref, ks_ref, kw_ref, vat_ref, vst_ref, vwt_ref, kc_ref, vct_ref, gt_ref,
                 aux_ref, slope_ref, o_ref, ot_ref, mask_a_ref, mask_w_ref, s_ref, e_ref, *, seq):
    n = pl.program_id(1)
    t0 = n * BLOCK_Q
    n_str = seq // CMP_STRIDE
    n_cmp = n_str - CMP_LEN // CMP_STRIDE + 1
    n_sel = seq // SLC_BLOCK
    top_n = min(SLC_TOPN, n_sel)
    hd = HEAD_DIM
    gq = GROUP * BLOCK_Q
    ch = SLC_CHUNK
    span_a = A_WINDOW + BLOCK_Q
    span_w = B_WINDOW + BLOCK_Q

    lane = lax.broadcasted_iota(jnp.int32, (1, gq), 1)
    q_loc = lane & (BLOCK_Q - 1)
    lane_head = lane >> (BLOCK_Q.bit_length() - 1)
    t_q = t0 + q_loc

    def head_row(vals):
        return jnp.where(lane_head == 0, vals[0], jnp.where(lane_head == 1, vals[1],
                                                           jnp.where(lane_head == 2, vals[2], vals[3])))

    def q_group(first_head):
        return jnp.concatenate(
            [qt_ref[0, (first_head + g) * hd:(first_head + g + 1) * hd, :] for g in range(GROUP)], axis=1)

    def score_rhs(grp, kvh, qg, sel_rows):
        zero = jnp.zeros((hd, gq), BF16)
        q_rows = [qg, zero] if kvh == 0 else [zero, qg]
        return jnp.concatenate(q_rows + [sel_rows, slope_ref[grp]], axis=0)

    def score_lhs(k_ref, start, span):
        return jnp.concatenate([k_ref[0, pl.ds(start, span), :], aux_ref[pl.ds(start, span), :]], axis=1)

    def band_start(span):
        return pl.multiple_of(jnp.maximum(t0 - (span - BLOCK_Q), 0), BLOCK_Q)

    def band_mask(span, window):
        ik = lax.broadcasted_iota(jnp.int32, (span, gq), 0)
        dist = (t0 - band_start(span)) + q_loc - ik
        return jnp.where((dist >= 0) & (dist < window), 0.0, -MASK_BIG)

    @pl.when(n <= span_w // BLOCK_Q - 1)
    def _():
        mask_a_ref[...] = band_mask(span_a, A_WINDOW)
        mask_w_ref[...] = band_mask(span_w, B_WINDOW)

    no_sel = jnp.zeros((AUX_SEL, gq), BF16)
    n_grp = N_ATT_HEADS // GROUP
    slopes = [head_row([ALIBI_SLOPES[grp * GROUP + g] for g in range(GROUP)]) for grp in range(n_grp)]
    qgs = [q_group(grp * GROUP) for grp in range(n_grp)]

    def softmax_cols(s, sinks=None):
        m = jnp.max(s, axis=0, keepdims=True)
        if sinks is not None:
            m = jnp.maximum(m, sinks)
        e = jnp.exp2(s - m)
        den = jnp.sum(e, axis=0, keepdims=True)
        if sinks is not None:
            den = den + jnp.exp2(sinks - m)
        return e.astype(BF16), den

    ks_a = band_start(span_a)
    ks_w = band_start(span_w)
    lhs_a = score_lhs(ka_ref, ks_a, span_a)
    lhs_w = score_lhs(kw_ref, ks_w, span_w)
    s_a = [_dot(lhs_a, score_rhs(kvh, kvh, qgs[kvh], no_sel)) + mask_a_ref[...] for kvh in range(A_KV_HEADS)]
    s_w = [_dot(lhs_w, score_rhs(A_KV_HEADS + kvh, kvh, qgs[A_KV_HEADS + kvh], no_sel)) + mask_w_ref[...]
           for kvh in range(B_KV_HEADS)]

    c_idx = lax.broadcasted_iota(jnp.int32, (n_str, gq), 0)
    dist_c = t_q - (c_idx * CMP_STRIDE + CMP_LEN - 1)
    valid_c = (dist_c >= 0) & (c_idx < n_cmp)
    dist_cf = dist_c.astype(F32)
    s_c = [jnp.where(valid_c, _dot(kc_ref[0, kvh], qgs[A_KV_HEADS + kvh])
                     - (slopes[A_KV_HEADS + kvh] * LOG2E) * dist_cf, NEG_INF) for kvh in range(B_KV_HEADS)]

    oj = lax.broadcasted_iota(jnp.int32, (n_sel, n_str), 0) * SLC_BLOCK
    oc = lax.broadcasted_iota(jnp.int32, (n_sel, n_str), 1) * CMP_STRIDE
    overlap_t = ((oc < oj + SLC_BLOCK) & (oc + CMP_LEN > oj)).astype(BF16)
    j_idx = lax.broadcasted_iota(jnp.int32, (n_sel, BLOCK_Q), 0)
    t_blk = (t0 + lax.broadcasted_iota(jnp.int32, (1, BLOCK_Q), 1)) >> (SLC_BLOCK.bit_length() - 1)
    valid_j = j_idx <= t_blk
    forced_j = (j_idx == 0) | (j_idx == t_blk) | (j_idx == t_blk - 1)
    e_c, den_c, rhs_s = [], [], []
    for kvh in range(B_KV_HEADS):
        m = jnp.max(s_c[kvh], axis=0, keepdims=True)
        m = jnp.where(m == NEG_INF, 0.0, m)
        e = jnp.exp2(s_c[kvh] - m)
        den = jnp.sum(e, axis=0, keepdims=True)
        den = jnp.where(den > 0.0, den, 1.0)
        e_c.append(e.astype(BF16))
        den_c.append(den)
        p = e / den
        psum = p[:, 0:BLOCK_Q]
        for g in range(1, GROUP):
            psum = psum + p[:, g * BLOCK_Q:(g + 1) * BLOCK_Q]
        p_hi = psum.astype(BF16)
        p_lo = (psum - p_hi.astype(F32)).astype(BF16)
        imp = _dot(overlap_t, p_hi) + _dot(overlap_t, p_lo)
        rank = jnp.where(forced_j, RANK_FORCED, jnp.where(valid_j, imp, RANK_INVALID))
        cnt = jnp.zeros((n_sel, BLOCK_Q), jnp.int32)
        for jp in range(n_sel):
            row = rank[jp:jp + 1, :]
            before = (row > rank) | ((row == rank) & (j_idx > jp))
            cnt = cnt + before.astype(jnp.int32)
        sel_bias = jnp.where(cnt < top_n, 0.0, -MASK_BIG)
        if n_sel < AUX_SEL:
            sel_bias = jnp.concatenate([sel_bias, jnp.zeros((AUX_SEL - n_sel, BLOCK_Q), F32)], axis=0)
        sel_rows = jnp.concatenate([sel_bias.astype(BF16)] * GROUP, axis=1)
        rhs_s.append(score_rhs(A_KV_HEADS + kvh, kvh, qgs[A_KV_HEADS + kvh], sel_rows))

    t_qf = t_q.astype(F32)
    sinks = [(head_row([sink_ref[kvh * GROUP + g] for g in range(GROUP)]) + slopes[kvh] * t_qf) * LOG2E
             for kvh in range(A_KV_HEADS)]
    ed_a = [softmax_cols(s_a[kvh], sinks[kvh]) for kvh in range(A_KV_HEADS)]
    ed_w = [softmax_cols(s_w[kvh]) for kvh in range(B_KV_HEADS)]

    def rows(kvh):
        return slice(kvh * hd, (kvh + 1) * hd)

    o_a = [_dot(vat_ref[0, rows(kvh), pl.ds(ks_a, span_a)], ed_a[kvh][0]) / ed_a[kvh][1] for kvh in range(A_KV_HEADS)]
    o_w = [_dot(vwt_ref[0, rows(kvh), pl.ds(ks_w, span_w)], ed_w[kvh][0]) / ed_w[kvh][1] for kvh in range(B_KV_HEADS)]
    o_c = [_dot(vct_ref[0, kvh], e_c[kvh]) / den_c[kvh] for kvh in range(B_KV_HEADS)]
    for kvh in range(A_KV_HEADS):
        for g in range(GROUP):
            h = kvh * GROUP + g
            ot_ref[h * hd:(h + 1) * hd, :] = o_a[kvh][:, g * BLOCK_Q:(g + 1) * BLOCK_Q]

    dloc = q_loc - lax.broadcasted_iota(jnp.int32, (ch, gq), 0)
    n_full = t0 // ch

    def slc_scores(c):
        lhs = score_lhs(ks_ref, pl.multiple_of(c * ch, ch), ch)
        for kvh in range(B_KV_HEADS):
            s_ref[c & 1, kvh] = _dot(lhs, rhs_s[kvh])

    def slc_values(c, state, kvh):
        m, l, acc, alpha = state
        vt = vst_ref[0, rows(kvh), pl.ds(pl.multiple_of(jnp.maximum(c, 0) * ch, ch), ch)]
        return m, l, alpha * acc + _dot(vt, e_ref[kvh]), alpha

    def slc_softmax(c, state, kvh, causal=None):
        m, l, acc, _ = state
        s = s_ref[c & 1, kvh]
        if causal is not None:
            s = jnp.where(causal, s, -MASK_BIG)
        m_new = jnp.maximum(m, jnp.max(s, axis=0, keepdims=True))
        alpha = jnp.exp2(m - m_new)
        e = jnp.exp2(s - m_new)
        e_ref[kvh] = e.astype(BF16)
        return m_new, alpha * l + jnp.sum(e, axis=0, keepdims=True), acc, alpha

    def slc_step(c, states):
        states = tuple(slc_values(c - 1, states[kvh], kvh) for kvh in range(B_KV_HEADS))
        states = tuple(slc_softmax(c, states[kvh], kvh) for kvh in range(B_KV_HEADS))
        slc_scores(c + 1)
        return states

    e_ref[...] = jnp.zeros_like(e_ref)
    init = (jnp.full((1, gq), NEG_INF, F32), jnp.zeros((1, gq), F32), jnp.zeros((hd, gq), F32),
            jnp.ones((1, gq), F32))
    slc_scores(jnp.int32(0))
    states = lax.fori_loop(0, n_full, slc_step, (init,) * B_KV_HEADS)
    causal = dloc >= n_full * ch - t0
    gt = jax.nn.sigmoid(gt_ref[0])
    for kvh in range(B_KV_HEADS):
        state = slc_values(n_full - 1, states[kvh], kvh)
        state = slc_softmax(n_full, state, kvh, causal)
        _, l_fin, acc_fin, _ = slc_values(n_full, state, kvh)
        o_slc = acc_fin / l_fin
        for g in range(GROUP):
            hb = kvh * GROUP + g
            h = A_HEADS + hb
            cols = slice(g * BLOCK_Q, (g + 1) * BLOCK_Q)
            ot_ref[h * hd:(h + 1) * hd, :] = (
                gt[hb:hb + 1] * o_c[kvh][:, cols]
                + gt[B_HEADS + hb:B_HEADS + hb + 1] * o_slc[:, cols]
                + gt[2 * B_HEADS + hb:2 * B_HEADS + hb + 1] * o_w[kvh][:, cols]
            )

    o_ref[0] = ot_ref[...].T.astype(BF16)


def _attention(sinks, tok, feat, kc, vct, gates_t, seq):
    bsz = tok.shape[0]
    n_str = seq // CMP_STRIDE
    kv_w = A_KV_HEADS * HEAD_DIM
    q_rows = N_ATT_HEADS * HEAD_DIM
    gq = GROUP * BLOCK_Q
    assert seq // SLC_BLOCK <= AUX_SEL and seq <= 128 * 256

    def k_spec(col_block):
        return pl.BlockSpec((1, seq, kv_w), lambda b, n: (b, 0, col_block))

    def vt_spec(row_block):
        return pl.BlockSpec((1, kv_w, seq), lambda b, n: (b, q_rows // kv_w + row_block, 0))

    return pl.pallas_call(
        functools.partial(_attn_kernel, seq=seq),
        grid=(bsz, seq // BLOCK_Q),
        in_specs=[
            pl.BlockSpec(memory_space=pltpu.SMEM),
            pl.BlockSpec((1, q_rows, BLOCK_Q), lambda b, n: (b, 0, n)),
            k_spec(0), k_spec(1), k_spec(2),
            vt_spec(0), vt_spec(1), vt_spec(2),
            pl.BlockSpec((1, B_KV_HEADS, n_str, HEAD_DIM), lambda b, n: (b, 0, 0, 0)),
            pl.BlockSpec((1, B_KV_HEADS, HEAD_DIM, n_str), lambda b, n: (b, 0, 0, 0)),
            pl.BlockSpec((1, GATE_ROWS, BLOCK_Q), lambda b, n: (b, 0, n)),
            pl.BlockSpec((seq, LANES), lambda b, n: (0, 0)),
            pl.BlockSpec((N_ATT_HEADS // GROUP, LANES - AUX_SEL, gq), lambda b, n: (0, 0, 0)),
        ],
        out_specs=pl.BlockSpec((1, BLOCK_Q, q_rows), lambda b, n: (b, n, 0)),
        out_shape=jax.ShapeDtypeStruct((bsz, seq, q_rows), BF16),
        scratch_shapes=[
            pltpu.VMEM((q_rows, BLOCK_Q), F32),
            pltpu.VMEM((A_WINDOW + BLOCK_Q, gq), F32),
            pltpu.VMEM((B_WINDOW + BLOCK_Q, gq), F32),
            pltpu.VMEM((2, B_KV_HEADS, SLC_CHUNK, gq), F32),
            pltpu.VMEM((B_KV_HEADS, SLC_CHUNK, gq), BF16),
        ],
        compiler_params=_params(2, 32),
        name="hybrid_attention",
    )(sinks, feat, tok, tok, tok, feat, feat, feat, kc, vct, gates_t, _aux_key_lanes(seq), _aux_slope_rows())


def _proj_ffn_ln_kernel(o_ref, wo_ref, x0_ref, g0_ref, b0_ref, wg_ref, wu_ref, wd_ref, g_ref, b_ref,
                        c0_ref, c1_ref, c2_ref, y_ref, d0_ref, d1_ref, d2_ref, *, alpha):
    _cast_side_job((c0_ref, c1_ref, c2_ref), (d0_ref, d1_ref, d2_ref))
    x = _layer_norm(alpha * x0_ref[...] + _dot(o_ref[...], wo_ref[...]), g0_ref[...], b0_ref[...])
    xb = x.astype(BF16)
    d_ff = wg_ref.shape[1]
    y = None
    for lo in range(0, d_ff, FFN_CHUNK):
        hi = min(lo + FFN_CHUNK, d_ff)
        h = (_silu(_dot(xb, wg_ref[:, lo:hi])) * _dot(xb, wu_ref[:, lo:hi])).astype(BF16)
        part = _dot(h, wd_ref[lo:hi, :])
        y = part if y is None else y + part
    y_ref[...] = _layer_norm(alpha * x + y, g_ref[...], b_ref[...])


def _proj_ffn_ln(o2d, wo, x2d, g0, b0, wg, wu, wd, g, b, to_cast, alpha):
    n, k = o2d.shape
    d_ff = wg.shape[1]
    n_steps = n // ROW_TILE
    row = pl.BlockSpec((ROW_TILE, D_MODEL), lambda i: (i, 0))
    vec = pl.BlockSpec((1, D_MODEL), lambda i: (0, 0))
    once = pl.Buffered(1)
    views, cast_specs = zip(*[_cast_plan(w, n_steps) for w in to_cast])
    outs = pl.pallas_call(
        functools.partial(_proj_ffn_ln_kernel, alpha=alpha),
        grid=(n_steps,),
        in_specs=[
            pl.BlockSpec((ROW_TILE, k), lambda i: (i, 0)),
            pl.BlockSpec((k, D_MODEL), lambda i: (0, 0), pipeline_mode=once),
            row, vec, vec,
            pl.BlockSpec((D_MODEL, d_ff), lambda i: (0, 0), pipeline_mode=once),
            pl.BlockSpec((D_MODEL, d_ff), lambda i: (0, 0), pipeline_mode=once),
            pl.BlockSpec((d_ff, D_MODEL), lambda i: (0, 0), pipeline_mode=once),
            vec, vec,
            *cast_specs,
        ],
        out_specs=[row, *cast_specs],
        out_shape=[jax.ShapeDtypeStruct((n, D_MODEL), F32), *[jax.ShapeDtypeStruct(v.shape, BF16) for v in views]],
        compiler_params=_params(1, 60),
        name="out_proj_swiglu_ln",
    )(o2d, wo, x2d, g0, b0, wg, wu, wd, g, b, *views)
    return outs[0], [c.reshape(w.shape) for c, w in zip(outs[1:], to_cast)]


def _expert_kernel(be_ref, used_ref, x_ref, wg_ref, wu_ref, wd_ref, y_ref, xb_ref):
    del be_ref
    f = pl.program_id(1)
    in_use = pl.program_id(0) < used_ref[0]

    def hidden(x):
        return (_silu(_dot(x, wg_ref[0])) * _dot(x, wu_ref[0])).astype(BF16)

    def down(h, first):
        for lo in range(0, D_MODEL, MXU_WIDTH):
            part = _dot(h, wd_ref[0, :, lo:lo + MXU_WIDTH])
            if first:
                y_ref[:, lo:lo + MXU_WIDTH] = part
            else:
                y_ref[:, lo:lo + MXU_WIDTH] += part

    @pl.when(in_use & (f == 0))
    def _():
        xb = x_ref[...].astype(BF16)
        xb_ref[...] = xb
        down(hidden(xb), True)

    @pl.when(in_use & (f != 0))
    def _():
        down(hidden(xb_ref[...]), False)

    @pl.when(jnp.logical_not(in_use) & (f == 0))
    def _():
        y_ref[...] = jnp.zeros_like(y_ref)


def _experts(blk_e, n_used, xin, wg, wu, wd):
    n_rows = blk_e.shape[0] * MOE_ROWS
    d_ff = wg.shape[2]
    n_f = d_ff // EXPERT_TILE

    def row_blk(i, used):
        return jnp.minimum(i, used[0] - 1)

    def hid_blk(i, f, used):
        return jnp.where(i < used[0], f, n_f - 1)

    return pl.pallas_call(
        _expert_kernel,
        grid_spec=pltpu.PrefetchScalarGridSpec(
            num_scalar_prefetch=2,
            grid=(n_rows // MOE_ROWS, n_f),
            in_specs=[
                pl.BlockSpec((MOE_ROWS, D_MODEL), lambda i, f, be, used: (row_blk(i, used), 0)),
                pl.BlockSpec((1, D_MODEL, EXPERT_TILE), lambda i, f, be, used: (be[i], 0, hid_blk(i, f, used))),
                pl.BlockSpec((1, D_MODEL, EXPERT_TILE), lambda i, f, be, used: (be[i], 0, hid_blk(i, f, used))),
                pl.BlockSpec((1, EXPERT_TILE, D_MODEL), lambda i, f, be, used: (be[i], hid_blk(i, f, used), 0)),
            ],
            out_specs=pl.BlockSpec((MOE_ROWS, D_MODEL), lambda i, f, be, used: (i, 0)),
            scratch_shapes=[pltpu.VMEM((MOE_ROWS, D_MODEL), BF16)],
        ),
        out_shape=jax.ShapeDtypeStruct((n_rows, D_MODEL), F32),
        compiler_params=_params(2, 52),
        name="expert_swiglu",
    )(blk_e, n_used, xin, wg, wu, wd)


def _gmlp_kernel(x_ref, win_ref, lng_ref, lnb_ref, ws_ref, bs_ref, wout_ref, g_ref, b_ref, y_ref,
                 u_ref, vn_ref, gated_ref, *, alpha):
    x = x_ref[...]
    xb = x.astype(BF16)
    u_ref[...] = _gelu(_dot(xb, win_ref[:, :D_MODEL]))
    v = _gelu(_dot(xb, win_ref[:, D_MODEL:]))
    vn_ref[...] = _layer_norm(v, lng_ref[...], lnb_ref[...]).astype(BF16)
    cs = GMLP_CHUNK
    lower = lax.broadcasted_iota(jnp.int32, (cs, cs), 0) >= lax.broadcasted_iota(jnp.int32, (cs, cs), 1)
    for grp in range(GMLP_GROUPS):
        w = jnp.where(lower, ws_ref[grp], 0.0).astype(BF16)
        bias = bs_ref[grp]
        cols = slice(grp * cs, (grp + 1) * cs)
        for c in range(x.shape[0] // cs):
            rows = slice(c * cs, (c + 1) * cs)
            mixed = _dot(w, vn_ref[rows, cols]) + bias
            gated_ref[rows, cols] = (u_ref[rows, cols] * mixed).astype(BF16)
    y = _dot(gated_ref[...], wout_ref[...])
    y_ref[...] = _layer_norm(alpha * x + y, g_ref[...], b_ref[...])


def _gmlp_ln(x2d, w_in, ln_g, ln_b, w_s, b_s, w_out, g, b, alpha):
    n = x2d.shape[0]
    row = pl.BlockSpec((ROW_TILE, D_MODEL), lambda i: (i, 0))
    vec = pl.BlockSpec((1, D_MODEL), lambda i: (0, 0))
    grp = pl.BlockSpec((GMLP_GROUPS, GMLP_CHUNK, GMLP_CHUNK), lambda i: (0, 0, 0))
    return pl.pallas_call(
        functools.partial(_gmlp_kernel, alpha=alpha),
        grid=(n // ROW_TILE,),
        in_specs=[
            row,
            pl.BlockSpec((D_MODEL, 2 * D_MODEL), lambda i: (0, 0)),
            vec, vec, grp, grp,
            pl.BlockSpec((D_MODEL, D_MODEL), lambda i: (0, 0)),
            vec, vec,
        ],
        out_specs=row,
        out_shape=jax.ShapeDtypeStruct((n, D_MODEL), F32),
        scratch_shapes=[
            pltpu.VMEM((ROW_TILE, D_MODEL), F32),
            pltpu.VMEM((ROW_TILE, D_MODEL), BF16),
            pltpu.VMEM((ROW_TILE, D_MODEL), BF16),
        ],
        compiler_params=_params(1, 40),
        name="gmlp_ln",
    )(x2d, w_in, ln_g, ln_b, w_s, b_s, w_out, g, b)


def _route_kernel(x_ref, w_ref, gate_ref, et_ref, post_ref, cnt_ref, run_ref):
    tm = x_ref.shape[0]

    @pl.when(pl.program_id(0) == 0)
    def _():
        run_ref[...] = jnp.zeros_like(run_ref)

    x = x_ref[...]
    w = w_ref[...]
    x_hi, w_hi = x.astype(BF16), w.astype(BF16)
    x_lo = (x - x_hi.astype(F32)).astype(BF16)
    w_lo = (w - w_hi.astype(F32)).astype(BF16)
    logits = _dot_nt(w_hi, x_hi) + (_dot_nt(w_hi, x_lo) + _dot_nt(w_lo, x_hi))
    row = lax.broadcasted_iota(jnp.int32, (N_EXPERTS, tm), 0)
    v1 = jnp.max(logits, axis=0, keepdims=True)
    i1 = jnp.min(jnp.where(logits == v1, row, N_EXPERTS), axis=0, keepdims=True)
    rest = jnp.where(row == i1, NEG_INF, logits)
    v2 = jnp.max(rest, axis=0, keepdims=True)
    i2 = jnp.min(jnp.where(rest == v2, row, N_EXPERTS), axis=0, keepdims=True)
    d = jnp.exp(v2 - v1)
    g1 = 1.0 / (1.0 + d)
    g2 = d / (1.0 + d)

    hit1 = row == i1
    hit2 = row == i2
    onehot = (hit1 | hit2).astype(BF16)
    earlier = (lax.broadcasted_iota(jnp.int32, (tm, tm), 0) < lax.broadcasted_iota(jnp.int32, (tm, tm), 1)).astype(BF16)
    before = _dot(onehot, earlier) + run_ref[:, 0:1]
    pos1 = jnp.sum(jnp.where(hit1, before, 0.0), axis=0, keepdims=True).astype(jnp.int32)
    pos2 = jnp.sum(jnp.where(hit2, before, 0.0), axis=0, keepdims=True).astype(jnp.int32)
    run_ref[...] += jnp.sum(onehot.astype(F32), axis=1, keepdims=True)

    et_ref[...] = jnp.where(row == 0, i1, jnp.where(row == 1, i2, 0))
    post_ref[...] = jnp.where(row == 0, pos1, jnp.where(row == 1, pos2, 0))
    cnt_ref[...] = run_ref[...]
    wide = lax.broadcasted_iota(jnp.int32, (LANES, tm), 0)
    gate_ref[...] = jnp.where(wide == 0, g1, jnp.where(wide == 1, g2, 0.0)).T


def _route(x2d, w_router_t):
    n = x2d.shape[0]
    tm = ROW_TILE
    per_tok = pl.BlockSpec((N_EXPERTS, tm), lambda i: (0, i))
    return pl.pallas_call(
        _route_kernel,
        grid=(n // tm,),
        in_specs=[
            pl.BlockSpec((tm, D_MODEL), lambda i: (i, 0)),
            pl.BlockSpec((N_EXPERTS, D_MODEL), lambda i: (0, 0)),
        ],
        out_specs=[
            pl.BlockSpec((tm, LANES), lambda i: (i, 0)),
            per_tok, per_tok,
            pl.BlockSpec((N_EXPERTS, LANES), lambda i: (0, 0)),
        ],
        out_shape=[
            jax.ShapeDtypeStruct((n, LANES), F32),
            jax.ShapeDtypeStruct((N_EXPERTS, n), jnp.int32),
            jax.ShapeDtypeStruct((N_EXPERTS, n), jnp.int32),
            jax.ShapeDtypeStruct((N_EXPERTS, LANES), F32),
        ],
        scratch_shapes=[pltpu.VMEM((N_EXPERTS, LANES), F32)],
        compiler_params=_params(1, 24),
        name="moe_router",
    )(x2d, w_router_t)


def _dispatch_kernel(d0_ref, d1_ref, lo_ref, hi_ref, x_ref, xin_hbm, zero_ref, sem_rows, sem_zero):
    step = pl.program_id(0)
    tm = x_ref.shape[0]
    base = step * tm

    def row_copy(r, dest_ref):
        return pltpu.make_async_copy(x_ref.at[pl.ds(r, 1)], xin_hbm.at[pl.ds(dest_ref[base + r], 1)], sem_rows)

    def issue(r8, carry):
        for j in range(DMA_UNROLL):
            r = r8 * DMA_UNROLL + j
            row_copy(r, d0_ref).start(priority=j % 2)
            row_copy(r, d1_ref).start(priority=(j + 1) % 2)
        return carry

    lax.fori_loop(0, tm // DMA_UNROLL, issue, 0)

    @pl.when(step == pl.num_programs(0) - 1)
    def _():
        zero_ref[...] = jnp.zeros_like(zero_ref)
        for k in range(N_EXPERTS + 1):
            def fill(r, carry):
                pltpu.make_async_copy(zero_ref, xin_hbm.at[pl.ds(r, 1)], sem_zero).start()
                return carry

            def drain(r, carry):
                pltpu.make_async_copy(zero_ref, xin_hbm.at[pl.ds(r, 1)], sem_zero).wait()
                return carry

            lax.fori_loop(lo_ref[k], hi_ref[k], fill, 0)
            lax.fori_loop(lo_ref[k], hi_ref[k], drain, 0)

    pltpu.make_async_copy(x_ref, xin_hbm.at[pl.ds(0, tm)], sem_rows).wait()
    pltpu.make_async_copy(x_ref, xin_hbm.at[pl.ds(0, tm)], sem_rows).wait()


def _dispatch(d0, d1, fill_lo, fill_hi, x2d, n_rows):
    n = x2d.shape[0]
    tm = COPY_TOKENS
    return pl.pallas_call(
        _dispatch_kernel,
        grid_spec=pltpu.PrefetchScalarGridSpec(
            num_scalar_prefetch=4,
            grid=(n // tm,),
            in_specs=[pl.BlockSpec((tm, D_MODEL), lambda i, *_: (i, 0))],
            out_specs=pl.BlockSpec(memory_space=pl.ANY),
            scratch_shapes=[pltpu.VMEM((1, D_MODEL), F32), pltpu.SemaphoreType.DMA(()), pltpu.SemaphoreType.DMA(())],
        ),
        out_shape=jax.ShapeDtypeStruct((n_rows, D_MODEL), F32),
        compiler_params=_params(1, 16),
        name="moe_dispatch",
    )(d0, d1, fill_lo, fill_hi, x2d)


def _combine_kernel(d0_ref, d1_ref, y_hbm, gate_ref, x_ref, g_ref, b_ref, o_ref, buf0, buf1, sem, *, alpha):
    tm = x_ref.shape[0]
    base = pl.program_id(0) * tm

    def issue(r8, carry):
        for j in range(DMA_UNROLL):
            r = r8 * DMA_UNROLL + j
            pltpu.make_async_copy(y_hbm.at[pl.ds(d0_ref[base + r], 1)], buf0.at[pl.ds(r, 1)], sem.at[0]).start(priority=0)
            pltpu.make_async_copy(y_hbm.at[pl.ds(d1_ref[base + r], 1)], buf1.at[pl.ds(r, 1)], sem.at[1]).start(priority=1)
        return carry

    lax.fori_loop(0, tm // DMA_UNROLL, issue, 0)
    pltpu.make_async_copy(y_hbm.at[pl.ds(0, tm)], buf0, sem.at[0]).wait()
    pltpu.make_async_copy(y_hbm.at[pl.ds(0, tm)], buf1, sem.at[1]).wait()
    gate = gate_ref[...]
    y = buf0[...] * gate[:, 0:1] + buf1[...] * gate[:, 1:2]
    o_ref[...] = _layer_norm(alpha * x_ref[...] + y, g_ref[...], b_ref[...])


def _combine_ln(d0, d1, yb, gate, x2d, g, b, alpha):
    n = x2d.shape[0]
    tm = COPY_TOKENS
    row = pl.BlockSpec((tm, D_MODEL), lambda i, a, c: (i, 0))
    vec = pl.BlockSpec((1, D_MODEL), lambda i, a, c: (0, 0))
    return pl.pallas_call(
        functools.partial(_combine_kernel, alpha=alpha),
        grid_spec=pltpu.PrefetchScalarGridSpec(
            num_scalar_prefetch=2,
            grid=(n // tm,),
            in_specs=[
                pl.BlockSpec(memory_space=pl.ANY),
                pl.BlockSpec((tm, LANES), lambda i, a, c: (i, 0)),
                row, vec, vec,
            ],
            out_specs=row,
            scratch_shapes=[
                pltpu.VMEM((tm, D_MODEL), F32),
                pltpu.VMEM((tm, D_MODEL), F32),
                pltpu.SemaphoreType.DMA((2,)),
            ],
        ),
        out_shape=jax.ShapeDtypeStruct((n, D_MODEL), F32),
        compiler_params=_params(1, 40),
        name="moe_combine_ln",
    )(d0, d1, yb, gate, x2d, g, b)


def _attention_heads(x2d, bsz, seq, w_in, sinks, pe_k, wk1, wk2, pe_v, wv1, wv2, to_cast):
    aq, akv = A_HEADS * HEAD_DIM, A_KV_HEADS * HEAD_DIM
    bq, bkv = B_HEADS * HEAD_DIM, B_KV_HEADS * HEAD_DIM
    bounds = [0]
    for width in (aq, akv, akv, bq, bkv, bkv, bkv, bkv, bkv, bkv, 3 * B_HEADS):
        bounds.append(bounds[-1] + width)
    qa, ka, va, qb, kc, vc, ks, vs, kw, vw, wg = [w_in[:, bounds[i]:bounds[i + 1]] for i in range(11)]
    w_tok = jnp.concatenate([ka, ks, kw, kc, vc], axis=1).astype(BF16)
    w_feat = jnp.concatenate([qa * (SCALE * LOG2E), qb * (SCALE * LOG2E), va, vs, vw], axis=1).T.astype(BF16)
    wg = wg.reshape(D_MODEL, B_HEADS, 3).transpose(2, 1, 0).reshape(3 * B_HEADS, D_MODEL)
    wg = jnp.pad(wg, ((0, GATE_ROWS - 3 * B_HEADS), (0, 0))).astype(BF16)
    (tok, x_cmp, feat, gates_t), cast = _inproj(x2d, w_tok, w_feat, wg, to_cast, bsz, seq)

    halves = CMP_LEN // CMP_STRIDE
    w1 = jnp.stack([wk1, wv1]).reshape(2, halves, CMP_STRIDE, 1, HEAD_DIM, CMP_HIDDEN)
    own_head = jnp.eye(B_KV_HEADS, dtype=F32)[:, None, None, None, :, None, None]
    w1 = (w1[None] * own_head).reshape(B_KV_HEADS, 2, halves, CMP_STRIDE * bkv, CMP_HIDDEN).astype(BF16)
    pe = jnp.stack([pe_k, pe_v]).reshape(2, halves, CMP_STRIDE, 1, HEAD_DIM)
    pe = jnp.broadcast_to(pe, (2, halves, CMP_STRIDE, B_KV_HEADS, HEAD_DIM)).reshape(2, halves, 1, CMP_STRIDE * bkv)
    pe = jnp.broadcast_to(pe, (2, halves, 8, CMP_STRIDE * bkv))
    kcmp, vcmp_t = _compress(x_cmp.reshape(bsz, seq, 2 * bkv), pe, w1, wk2.astype(BF16), wv2.T.astype(BF16))

    o = _attention(sinks, tok.reshape(bsz, seq, KEY_W), feat, kcmp, vcmp_t, gates_t, seq)
    return o.reshape(bsz * seq, N_ATT_HEADS * HEAD_DIM), cast


def _moe_layer(x2d, w_router, w_gate, w_up, w_down, g, b, alpha):
    n_tok = x2d.shape[0]
    gate, e_t, pos_t, cnt = _route(x2d, w_router.T)
    counts = cnt[:, 0].astype(jnp.int32)
    padded = (counts + MOE_ROWS - 1) // MOE_ROWS * MOE_ROWS
    pad_ends = jnp.cumsum(padded)
    pad_starts = pad_ends - padded
    n_blocks = n_tok * TOP_K // MOE_ROWS + N_EXPERTS
    n_rows = n_blocks * MOE_ROWS
    experts = jnp.arange(N_EXPERTS, dtype=jnp.int32)[:, None, None]
    start_of = jnp.sum(jnp.where(e_t[None, :TOP_K] == experts, pad_starts[:, None, None], 0), axis=0)
    dest = start_of + pos_t[:TOP_K]
    fill_lo = jnp.concatenate([pad_starts + counts, pad_ends[-1:]])
    fill_hi = jnp.concatenate([pad_ends, jnp.full((1,), n_rows, jnp.int32)])
    blk_start = jnp.arange(n_blocks, dtype=jnp.int32) * MOE_ROWS
    blk_e = jnp.sum((pad_ends[None, :] <= blk_start[:, None]).astype(jnp.int32), axis=1)
    blk_e = jnp.minimum(blk_e, N_EXPERTS - 1)

    xin = _dispatch(dest[0], dest[1], fill_lo, fill_hi, x2d, n_rows)
    n_used = (pad_ends[-1:] // MOE_ROWS).astype(jnp.int32)
    yb = _experts(blk_e, n_used, xin, w_gate, w_up, w_down)
    return _combine_ln(dest[0], dest[1], yb, gate, x2d, g, b, alpha)


def kernel(x, att_w_in, att_sinks, cmp_pe_k, cmp_wk1, cmp_wk2, cmp_pe_v, cmp_wv1, cmp_wv2, att_w_o,
           ffn_w_gate, ffn_w_up, ffn_w_down, gmlp_w_in, gmlp_ln_g, gmlp_ln_b, gmlp_w_s, gmlp_b_s, gmlp_w_out,
           moe_w_router, moe_w_gate, moe_w_up, moe_w_down, ln_g, ln_b):
    bsz, seq, dm = x.shape
    depth = ln_g.shape[0]
    alpha = (2.0 * depth) ** 0.25
    assert dm == D_MODEL and seq % ROW_TILE == 0 and (bsz * seq) % COPY_TOKENS == 0
    x2d = x.reshape(bsz * seq, dm)

    def vec(p):
        return p.reshape(1, -1)

    for i in range(depth):
        j = i // 2
        g0, b0, g1, b1 = vec(ln_g[i, 0]), vec(ln_b[i, 0]), vec(ln_g[i, 1]), vec(ln_b[i, 1])
        if i % 2 == 0:
            o2d, ffn_w = _attention_heads(x2d, bsz, seq, att_w_in[j], att_sinks[j], cmp_pe_k[j], cmp_wk1[j],
                                          cmp_wk2[j], cmp_pe_v[j], cmp_wv1[j], cmp_wv2[j],
                                          (ffn_w_gate[j], ffn_w_up[j], ffn_w_down[j]))
            jn = min(j, moe_w_gate.shape[0] - 1)
            x2d, moe_w = _proj_ffn_ln(o2d, att_w_o[j].astype(BF16), x2d, g0, b0, *ffn_w, g1, b1,
                                      (moe_w_gate[jn], moe_w_up[jn], moe_w_down[jn]), alpha)
        else:
            b_s = jnp.broadcast_to(gmlp_b_s[j][:, :, None], (GMLP_GROUPS, GMLP_CHUNK, GMLP_CHUNK))
            x2d = _gmlp_ln(x2d, gmlp_w_in[j].astype(BF16), vec(gmlp_ln_g[j]), vec(gmlp_ln_b[j]), gmlp_w_s[j], b_s,
                           gmlp_w_out[j].astype(BF16), g0, b0, alpha)
            x2d = _moe_layer(x2d, moe_w_router[j], *moe_w, g1, b1, alpha)
    return x2d.reshape(bsz, seq, dm)
```
